```python
import jax, jax.numpy as jnp
from jax import lax
import numpy as np

D_MODEL = 1024
BATCH = 8
SEQ = 2048
DEPTH = 1
DEC_BATCH = 128
DEC_SEQ = 8
PAST_LEN = 16384
PAGE_SIZE = 128

D_CONV = D_MODEL // 2
CONV_WIDTH = 3
D_POOL = D_MODEL // 2
POOL_WINDOWS = (2, 4, 8, 16)
N_POOL_GROUPS = len(POOL_WINDOWS)
POOL_GROUP = D_POOL // N_POOL_GROUPS
POOL_HIST = max(POOL_WINDOWS) - 1
D_IN_PROJ = 3 * D_CONV + D_POOL + 2 * D_MODEL
SPLITS = [D_CONV, 2 * D_CONV, 3 * D_CONV, 3 * D_CONV + D_POOL, 3 * D_CONV + D_POOL + D_MODEL]
N_EXPERTS = 32
TOP_K = 4
D_EXPERT = D_MODEL
SWIGLU_LIMIT = 7.0
SWIGLU_ALPHA = 1.702
MOE_BLOCK = 128
LN_EPS = 1e-5
DEEPNORM_ALPHA = (2 * DEPTH) ** 0.25
DEEPNORM_BETA = (8 * DEPTH) ** -0.25

kernel_name = "hybrid_conv_pool_moe_decode_step"


def layer_norm(x, g, b):
    xf = x.astype(jnp.float32)
    mu = jnp.mean(xf, axis=-1, keepdims=True)
    xc = xf - mu
    var = jnp.mean(xc * xc, axis=-1, keepdims=True)
    y = xc * lax.rsqrt(var + LN_EPS) * g.astype(jnp.float32) + b.astype(jnp.float32)
    return y.astype(x.dtype)


def multiscale_pool(pc, T, start):
    s = jnp.cumsum(pc.astype(jnp.float32), axis=1)
    s = jnp.pad(s, ((0, 0), (1, 0), (0, 0)))
    H = POOL_HIST
    pos = start + jnp.arange(T)
    outs = []
    for g, w in enumerate(POOL_WINDOWS):
        sl = slice(g * POOL_GROUP, (g + 1) * POOL_GROUP)
        win = s[:, H + 1:H + 1 + T, sl] - s[:, H + 1 - w:H + 1 - w + T, sl]
        cnt = jnp.minimum(w, pos + 1).astype(jnp.float32)[None, :, None]
        outs.append(win / cnt - pc[:, H:, sl].astype(jnp.float32))
    return jnp.concatenate(outs, axis=-1).astype(pc.dtype)


def moe_ffn(xf, w_router, b_router, w_up, b_up, w_down, b_down):
    T, D = xf.shape
    logits = (xf @ w_router + b_router).astype(jnp.float32)
    top_vals, top_idx = lax.top_k(logits, TOP_K)
    gates = jax.nn.softmax(top_vals, axis=-1)
    tk = T * TOP_K
    flat_e = top_idx.reshape(-1)
    order = jnp.argsort(flat_e)
    sorted_e = flat_e[order]
    sorted_tok = order // TOP_K
    counts = jnp.bincount(flat_e, length=N_EXPERTS)
    group_start = jnp.cumsum(counts) - counts
    padded = (counts + MOE_BLOCK - 1) // MOE_BLOCK * MOE_BLOCK
    pad_end = jnp.cumsum(padded)
    pad_start = pad_end - padded
    dest = pad_start[sorted_e] + jnp.arange(tk) - group_start[sorted_e]
    n_blocks = -(-tk // MOE_BLOCK) + N_EXPERTS
    block_e = jnp.minimum(
        jnp.searchsorted(pad_end, jnp.arange(n_blocks) * MOE_BLOCK, side='right'), N_EXPERTS - 1)
    x_pad = jnp.zeros((n_blocks * MOE_BLOCK, D), xf.dtype).at[dest].set(xf[sorted_tok])

    def expert_block(args):
        xb, e = args
        h = xb @ w_up[e] + b_up[e]
        glu = jnp.minimum(h[:, :D_EXPERT], SWIGLU_LIMIT)
        lin = jnp.clip(h[:, D_EXPERT:], -SWIGLU_LIMIT, SWIGLU_LIMIT)
        act = glu * jax.nn.sigmoid(SWIGLU_ALPHA * glu) * (lin + 1)
        return act @ w_down[e] + b_down[e]

    y_pad = lax.map(expert_block, (x_pad.reshape(n_blocks, MOE_BLOCK, D), block_e))
    y_pad = y_pad.reshape(n_blocks * MOE_BLOCK, D)
    contrib = y_pad[dest].astype(jnp.float32) * gates.reshape(-1)[order][:, None]
    return jnp.zeros((T, D), jnp.float32).at[sorted_tok].add(contrib).astype(xf.dtype)


def decoder_layer(x, c, conv_hist, pool_hist, start, w_ada, b_ada, w_in, conv_w, w_out_a,
                  w_pool, ls_pool, w_out_b, w_o, ln1_g, ln1_b, w_router, b_router,
                  w_up, b_up, w_down, b_down, ln2_g, ln2_b):
    Bn, T, D = x.shape
    mod = jax.nn.silu(c) @ w_ada + b_ada
    shift1, scale1, gate1, shift2, scale2, gate2 = jnp.split(mod[:, None, :], 6, axis=-1)

    u = x * (1 + scale1) + shift1
    proj = u @ w_in
    b_g, c_g, xa, xp, g_a, g_b = jnp.split(proj, SPLITS, axis=-1)

    z = c_g * xa
    zc = jnp.concatenate([conv_hist.astype(z.dtype), z], axis=1)
    conv = conv_w[0] * zc[:, :T] + conv_w[1] * zc[:, 1:T + 1] + conv_w[2] * zc[:, 2:T + 2]
    y_a = (b_g * conv) @ w_out_a

    pc = jnp.concatenate([pool_hist.astype(xp.dtype), xp], axis=1)
    diff = multiscale_pool(pc, T, start)
    yg = jnp.einsum('btgc,gcd->btgd', diff.reshape(Bn, T, N_POOL_GROUPS, POOL_GROUP), w_pool)
    y_b = (yg.reshape(Bn, T, D_POOL) * ls_pool) @ w_out_b

    merged = jax.nn.sigmoid(g_a) * y_a + jax.nn.sigmoid(g_b) * y_b
    h = layer_norm(DEEPNORM_ALPHA * x + gate1 * (merged @ w_o), ln1_g, ln1_b)

    v = h * (1 + scale2) + shift2
    f = moe_ffn(v.reshape(Bn * T, D), w_router, b_router, w_up, b_up, w_down, b_down)
    y = layer_norm(DEEPNORM_ALPHA * h + gate2 * f.reshape(Bn, T, D), ln2_g, ln2_b)
    return y, zc[:, -(CONV_WIDTH - 1):], pc[:, -POOL_HIST:]


def setup_inputs(seed: int = 0) -> dict:
    key = jax.random.key(seed)
    ks = jax.random.split(key, 26)

    def nrm(k, shape, scale):
        return jax.random.normal(k, shape, jnp.float32) * scale

    L, D, E, F = DEPTH, D_MODEL, N_EXPERTS, D_EXPERT
    return {
        "x_prompt": nrm(ks[0], (BATCH, SEQ, D), 1.0),
        "x_sample": nrm(ks[1], (DEC_BATCH, DEC_SEQ, D), 1.0),
        "c_prompt": nrm(ks[2], (BATCH, D), 1.0),
        "c_sample": nrm(ks[3], (DEC_BATCH, D), 1.0),
        "state_conv": nrm(ks[4], (L, DEC_BATCH, CONV_WIDTH - 1, D_CONV), 1.0),
        "state_pool": nrm(ks[5], (L, DEC_BATCH, POOL_HIST, D_POOL), 1.0),
        "w_ada": nrm(ks[6], (L, D, 6 * D), 0.5 * D ** -0.5),
        "b_ada": nrm(ks[7], (L, 6 * D), 0.02),
        "w_in": nrm(ks[8], (L, D, D_IN_PROJ), D ** -0.5),
        "conv_w": nrm(ks[9], (L, CONV_WIDTH, D_CONV), CONV_WIDTH ** -0.5),
        "w_out_a": nrm(ks[10], (L, D_CONV, D), D_CONV ** -0.5),
        "w_pool": nrm(ks[11], (L, N_POOL_GROUPS, POOL_GROUP, POOL_GROUP), POOL_GROUP ** -0.5),
        "ls_pool": 1.0 + nrm(ks[12], (L, D_POOL), 0.02),
        "w_out_b": nrm(ks[13], (L, D_POOL, D), D_POOL ** -0.5),
        "w_o": nrm(ks[14], (L, D, D), DEEPNORM_BETA * D ** -0.5),
        "ln1_g": 1.0 + nrm(ks[15], (L, D), 0.02),
        "ln1_b": nrm(ks[16], (L, D), 0.02),
        "w_router": nrm(ks[17], (L, D, E), D ** -0.5),
        "b_router": nrm(ks[18], (L, E), 0.01),
        "w_up": nrm(ks[19], (L, E, D, 2 * F), D ** -0.5),
        "b_up": nrm(ks[20], (L, E, 2 * F), 0.02),
        "w_down": nrm(ks[21], (L, E, F, D), DEEPNORM_BETA * F ** -0.5),
        "b_down": nrm(ks[22], (L, E, D), 0.02),
        "ln2_g": 1.0 + nrm(ks[23], (L, D), 0.02),
        "ln2_b": nrm(ks[24], (L, D), 0.02),
    }


def reference(x_prompt, x_sample, c_prompt, c_sample, state_conv, state_pool, w_ada, b_ada,
              w_in, conv_w, w_out_a, w_pool, ls_pool, w_out_b, w_o, ln1_g, ln1_b,
              w_router, b_router, w_up, b_up, w_down, b_down, ln2_g, ln2_b):
    yp, ys = x_prompt, x_sample
    conv_p, pool_p, conv_s, pool_s = [], [], [], []
    for l in range(DEPTH):
        lw = (w_ada[l], b_ada[l], w_in[l], conv_w[l], w_out_a[l], w_pool[l], ls_pool[l],
              w_out_b[l], w_o[l], ln1_g[l], ln1_b[l], w_router[l], b_router[l],
              w_up[l], b_up[l], w_down[l], b_down[l], ln2_g[l], ln2_b[l])
        zero_conv = jnp.zeros((yp.shape[0], CONV_WIDTH - 1, D_CONV), yp.dtype)
        zero_pool = jnp.zeros((yp.shape[0], POOL_HIST, D_POOL), yp.dtype)
        yp, cp, pp = decoder_layer(yp, c_prompt, zero_conv, zero_pool, 0, *lw)
        ys, cs, ps = decoder_layer(ys, c_sample, state_conv[l], state_pool[l], PAST_LEN, *lw)
        conv_p.append(cp)
        pool_p.append(pp)
        conv_s.append(cs)
        pool_s.append(ps)
    return (yp, ys, jnp.stack(conv_p), jnp.stack(pool_p), jnp.stack(conv_s), jnp.stack(pool_s))
```

```python
import functools

import jax
import jax.numpy as jnp
from jax import lax
from jax.experimental import pallas as pl
from jax.experimental.pallas import tpu as pltpu

D = 1024
C = 512
N_GROUPS = 4
GROUP = C // N_GROUPS
CONV_HIST = 2
POOL_HIST = 15
E = 32
TOP_K = 4
F = 1024
SWIGLU_LIMIT = 7.0
SWIGLU_ALPHA = 1.702
LN_EPS = 1e-5
DEPTH = 1
ALPHA = (2 * DEPTH) ** 0.25
PAST_LEN = 16384

LANES = 128
SUBLANES = 8
ROW_TILE = 512
SAMPLE_ROW_TILE = 256
BM = 256
COMBINE_TILE = 256
VMEM_LIMIT = 56 * 1024 * 1024

_f32 = jnp.float32
_bf16 = jnp.bfloat16


def _dot(a, b):
    return jnp.dot(a, b, preferred_element_type=_f32)


def _dot_exact(a, b):
    return lax.dot_general(a, b, (((1,), (0,)), ((), ())),
                           precision=lax.Precision.HIGHEST, preferred_element_type=_f32)


def _per_seq(x, m, op):
    g = m.shape[0]
    if g == 1:
        return op(x, m)
    r, n = x.shape
    return op(x.reshape(r // g, g, n), m[None]).reshape(r, n)


def _layer_norm(x, g, b):
    mu = jnp.mean(x, axis=-1, keepdims=True)
    xc = x - mu
    var = jnp.mean(xc * xc, axis=-1, keepdims=True)
    return xc * lax.rsqrt(var + LN_EPS) * g + b


def _hist_steps(needed, g):
    return -(-needed * g // SUBLANES) * SUBLANES // g


def _ada_kernel(c_ref, w_ref, b_ref, o_ref):
    c = c_ref[...]
    o_ref[...] = _dot_exact(c * jax.nn.sigmoid(c), w_ref[...]) + b_ref[...]


def _ada(c, w_ada, b_ada):
    rows = c.shape[0]
    cols = w_ada.shape[1]
    bn = 1536
    return pl.pallas_call(
        _ada_kernel,
        out_shape=jax.ShapeDtypeStruct((rows, cols), _f32),
        grid=(cols // bn,),
        in_specs=[pl.BlockSpec((rows, D), lambda j: (0, 0)),
                  pl.BlockSpec((D, bn), lambda j: (0, j)),
                  pl.BlockSpec((1, bn), lambda j: (0, j))],
        out_specs=pl.BlockSpec((rows, bn), lambda j: (0, j)),
        compiler_params=pltpu.CompilerParams(vmem_limit_bytes=VMEM_LIMIT),
        name="ada",
    )(c, w_ada, b_ada)


def _mixer_kernel(g, tiles_per_seq, start_pos,
                  x_ref, mod_ref, hc_ref, hp_ref,
                  win_ref, cw_ref, woa_ref, wpool_ref, ls_ref, wob_ref, wo_ref, g1_ref, b1_ref,
                  wr_ref, br_ref,
                  h_ref, v_ref, lg_ref, nc_ref, np_ref, zbuf, pbuf):
    r = x_ref.shape[0]
    hrc = hc_ref.shape[0]
    hrp = hp_ref.shape[0]
    j = pl.program_id(0) % tiles_per_seq

    @pl.when(j == 0)
    def _():
        zbuf[pl.ds(0, hrc), :] = hc_ref[...]
        pbuf[pl.ds(0, hrp), :] = hp_ref[...]

    @pl.when(j != 0)
    def _():
        zt = zbuf[pl.ds(r, hrc), :]
        pt = pbuf[pl.ds(r, hrp), :]
        zbuf[pl.ds(0, hrc), :] = zt
        pbuf[pl.ds(0, hrp), :] = pt

    m = mod_ref[...]
    shift1, scale1, gate1 = m[:, 0:D], m[:, D:2 * D], m[:, 2 * D:3 * D]
    shift2, scale2 = m[:, 3 * D:4 * D], m[:, 4 * D:5 * D]

    x = x_ref[...]
    u = _per_seq(_per_seq(x, 1.0 + scale1, jnp.multiply), shift1, jnp.add).astype(_bf16)

    z = _dot(u, win_ref[:, C:2 * C]) * _dot(u, win_ref[:, 2 * C:3 * C])
    zbuf[pl.ds(hrc, r), :] = z
    cw = cw_ref[...]
    conv = (cw[0:1] * zbuf[pl.ds(hrc - 2 * g, r), :] + cw[1:2] * zbuf[pl.ds(hrc - g, r), :]
            + cw[2:3] * z)
    y_a = _dot((_dot(u, win_ref[:, 0:C]) * conv).astype(_bf16), woa_ref[...])

    xp = _dot(u, win_ref[:, 3 * C:4 * C])
    pbuf[pl.ds(hrp, r), :] = xp
    pos = start_pos + j * (r // g) + lax.broadcasted_iota(jnp.int32, (r, 1), 0) // g
    acc = xp
    yg = []
    for grp in range(N_GROUPS):
        lo = grp * GROUP
        wdw = 2 ** (grp + 1)
        for back in range(wdw // 2, wdw):
            sh = pbuf[pl.ds(hrp - back * g, r), lo:C]
            acc = jnp.concatenate([acc[:, 0:lo], acc[:, lo:C] + sh], axis=1) if lo else acc + sh
        cnt = jnp.minimum(wdw, pos + 1).astype(_f32)
        diff = acc[:, lo:lo + GROUP] / cnt - xp[:, lo:lo + GROUP]
        yg.append(_dot(diff.astype(_bf16), wpool_ref[grp]))
    y_b = _dot((jnp.concatenate(yg, axis=1) * ls_ref[...]).astype(_bf16), wob_ref[...])

    g_a = _dot(u, win_ref[:, 4 * C:4 * C + D])
    g_b = _dot(u, win_ref[:, 4 * C + D:4 * C + 2 * D])
    merged = jax.nn.sigmoid(g_a) * y_a + jax.nn.sigmoid(g_b) * y_b
    o = _dot(merged.astype(_bf16), wo_ref[...])
    h = _layer_norm(ALPHA * x + _per_seq(o, gate1, jnp.multiply), g1_ref[...], b1_ref[...])
    v = _per_seq(_per_seq(h, 1.0 + scale2, jnp.multiply), shift2, jnp.add)
    h_ref[...] = h
    v_ref[...] = v
    lg_ref[...] = _dot_exact(v, wr_ref[...]) + br_ref[...]
    nc_ref[...] = zbuf[pl.ds(r, hrc), :]
    np_ref[...] = pbuf[pl.ds(r, hrp), :]


def _mixer(x2, mod3, hc, hp, weights, row_tile, start_pos):
    n = x2.shape[0]
    n_mod, g, _ = mod3.shape
    hrc, hrp = hc.shape[0] // n_mod, hp.shape[0] // n_mod
    tiles_per_seq = n // n_mod // row_tile
    once = dict(pipeline_mode=pl.Buffered(1)) if n_mod == 1 else {}

    def full(a):
        nd = a.ndim
        return pl.BlockSpec(a.shape, lambda i: (0,) * nd)

    def seq_block(rows, **kw):
        return pl.BlockSpec((rows, C), lambda i: (i // tiles_per_seq, 0), **kw)

    def row_block(cols):
        return pl.BlockSpec((row_tile, cols), lambda i: (i, 0))

    return pl.pallas_call(
        functools.partial(_mixer_kernel, g, tiles_per_seq, start_pos),
        out_shape=[
            jax.ShapeDtypeStruct((n, D), _f32),
            jax.ShapeDtypeStruct((n, D), _f32),
            jax.ShapeDtypeStruct((n, LANES), _f32),
            jax.ShapeDtypeStruct(hc.shape, _f32),
            jax.ShapeDtypeStruct(hp.shape, _f32),
        ],
        grid=(n // row_tile,),
        in_specs=[row_block(D),
                  pl.BlockSpec((None, g, 6 * D), lambda i: (i // tiles_per_seq, 0, 0), **once),
                  seq_block(hrc, **once), seq_block(hrp, **once)] + [full(a) for a in weights],
        out_specs=[row_block(D), row_block(D), row_block(LANES), seq_block(hrc), seq_block(hrp)],
        scratch_shapes=[pltpu.VMEM((hrc + row_tile, C), _f32), pltpu.VMEM((hrp + row_tile, C), _f32)],
        compiler_params=pltpu.CompilerParams(vmem_limit_bytes=VMEM_LIMIT),
        name="mixer",
    )(x2, mod3, hc, hp, *weights)


def _plan_kernel(n_blocks_pad, lgp_ref, lgs_ref, dest_ref, gate_ref, be_ref, nv_ref, idx_s, rank_s):
    t = ROW_TILE
    e_iota = lax.broadcasted_iota(jnp.int32, (E, t), 0)
    tri = (lax.broadcasted_iota(jnp.int32, (t, t), 0)
           < lax.broadcasted_iota(jnp.int32, (t, t), 1)).astype(_f32).astype(_bf16)
    zeros_rest = jnp.zeros((LANES - TOP_K, t), _f32)

    def tile_body(lg_ref, off, i, carry):
        lt = lg_ref[i].T[0:E, :]
        vals, idxs = [], []
        for _ in range(TOP_K):
            mx = jnp.max(lt, axis=0, keepdims=True)
            ix = jnp.min(jnp.where(lt == mx, e_iota, E), axis=0, keepdims=True)
            vals.append(mx)
            idxs.append(ix)
            lt = jnp.where(e_iota == ix, -jnp.inf, lt)
        ex = [jnp.exp(vk - vals[0]) for vk in vals]
        den = ex[0] + ex[1] + ex[2] + ex[3]
        gates = [ek / den for ek in ex]
        gate_ref[off + i] = jnp.concatenate(gates + [zeros_rest], axis=0).T

        ohs = [(e_iota == ix) for ix in idxs]
        oh = (ohs[0] | ohs[1] | ohs[2] | ohs[3]).astype(_f32)
        before = _dot(oh.astype(_bf16), tri) + carry
        ranks = [jnp.sum(jnp.where(o, before, 0.0), axis=0, keepdims=True) for o in ohs]
        idx_s[off + i] = jnp.concatenate(idxs + idxs, axis=0)
        rank_s[off + i] = jnp.concatenate(ranks + ranks, axis=0).astype(jnp.int32)
        return carry + jnp.sum(oh, axis=1, keepdims=True)

    n_p, n_s = lgp_ref.shape[0], lgs_ref.shape[0]
    counts = lax.fori_loop(0, n_p, functools.partial(tile_body, lgp_ref, 0), jnp.zeros((E, 1), _f32))
    counts = lax.fori_loop(0, n_s, functools.partial(tile_body, lgs_ref, n_p), counts)
    padded = jnp.ceil(counts / BM) * BM
    low = (lax.broadcasted_iota(jnp.int32, (E, E), 1)
           <= lax.broadcasted_iota(jnp.int32, (E, E), 0)).astype(_f32)
    pad_end = _dot_exact(low, jnp.broadcast_to(padded, (E, LANES)))[:, 0:1]
    pad_start = pad_end - padded

    def dest_body(i, c):
        ix = idx_s[i]
        rk = rank_s[i]
        rows = []
        for k in range(TOP_K):
            st = jnp.sum(jnp.where(e_iota == ix[k:k + 1], pad_start, 0.0), axis=0, keepdims=True)
            rows.append(st.astype(jnp.int32) + rk[k:k + 1])
        dest_ref[i] = jnp.concatenate(rows + rows, axis=0)
        return c

    lax.fori_loop(0, n_p + n_s, dest_body, 0)

    b_start = (lax.broadcasted_iota(jnp.int32, (1, n_blocks_pad), 1) * BM).astype(_f32)
    be = jnp.minimum(jnp.sum((pad_end <= b_start).astype(jnp.int32), axis=0, keepdims=True), E - 1)
    sel = lax.broadcasted_iota(jnp.int32, (E, n_blocks_pad), 0) == be
    used_end = jnp.sum(jnp.where(sel, pad_start + counts, 0.0), axis=0, keepdims=True)
    be_ref[...] = be
    nv_ref[...] = jnp.clip(used_end - b_start, 0.0, float(BM)).astype(jnp.int32)


def _plan(logits_p, logits_s, n_blocks):
    n_tiles = (logits_p.shape[0] + logits_s.shape[0]) // ROW_TILE
    n_blocks_pad = -(-n_blocks // LANES) * LANES
    return pl.pallas_call(
        functools.partial(_plan_kernel, n_blocks_pad),
        out_shape=[
            jax.ShapeDtypeStruct((n_tiles, 2 * TOP_K, ROW_TILE), jnp.int32),
            jax.ShapeDtypeStruct((n_tiles, ROW_TILE, LANES), _f32),
            jax.ShapeDtypeStruct((1, n_blocks_pad), jnp.int32),
            jax.ShapeDtypeStruct((1, n_blocks_pad), jnp.int32),
        ],
        scratch_shapes=[pltpu.VMEM((n_tiles, 2 * TOP_K, ROW_TILE), jnp.int32),
                        pltpu.VMEM((n_tiles, 2 * TOP_K, ROW_TILE), jnp.int32)],
        compiler_params=pltpu.CompilerParams(vmem_limit_bytes=VMEM_LIMIT),
        name="plan",
    )(logits_p.reshape(-1, ROW_TILE, LANES), logits_s.reshape(-1, ROW_TILE, LANES))


def _row_copy(src, src_row, dst, dst_row, sem):
    return pltpu.make_async_copy(src.at[pl.ds(src_row, 1), :], dst.at[pl.ds(dst_row, 1), :], sem)


def _dispatch_kernel(n_blocks, n_ptiles, nv_ref, dest_ref, vp_ref, vs_ref, xpad_ref, zero_buf, sem, zsem):
    t = COMBINE_TILE
    step = pl.program_id(0)

    def zero_copy(b):
        return pltpu.make_async_copy(zero_buf, xpad_ref.at[pl.ds(b * BM, BM), :], zsem)

    @pl.when(step == 0)
    def _():
        zero_buf[...] = jnp.zeros_like(zero_buf)

        def start(b, c):
            @pl.when(nv_ref[b] < BM)
            def _():
                zero_copy(b).start()
            return c

        def wait(b, c):
            @pl.when(nv_ref[b] < BM)
            def _():
                zero_copy(b).wait()
            return c

        lax.fori_loop(0, n_blocks, start, 0)
        lax.fori_loop(0, n_blocks, wait, 0)

    def scatter(v_ref):
        def issue(i, c):
            for k in range(TOP_K):
                _row_copy(v_ref, i, xpad_ref, dest_ref[0, 0, k * t + i], sem).start()
            return c

        lax.fori_loop(0, t, issue, 0)
        for k in range(TOP_K):
            pltpu.make_async_copy(v_ref, xpad_ref.at[pl.ds(0, t), :], sem).wait()

    @pl.when(step < n_ptiles)
    def _():
        scatter(vp_ref)

    @pl.when(step >= n_ptiles)
    def _():
        scatter(vs_ref)


def _dispatch(nvalid, dest_tiles, v_p, v_s, n_blocks):
    t = COMBINE_TILE
    n_ptiles, n_stiles = v_p.shape[0] // t, v_s.shape[0] // t
    return pl.pallas_call(
        functools.partial(_dispatch_kernel, n_blocks, n_ptiles),
        out_shape=jax.ShapeDtypeStruct((n_blocks * BM, D), _f32),
        grid_spec=pltpu.PrefetchScalarGridSpec(
            num_scalar_prefetch=1,
            grid=(n_ptiles + n_stiles,),
            in_specs=[pl.BlockSpec((1, 1, TOP_K * t), lambda i, nv: (i, 0, 0), memory_space=pltpu.SMEM),
                      pl.BlockSpec((t, D), lambda i, nv: (jnp.minimum(i, n_ptiles - 1), 0)),
                      pl.BlockSpec((t, D), lambda i, nv: (jnp.maximum(i - n_ptiles, 0), 0))],
            out_specs=pl.BlockSpec(memory_space=pl.ANY),
            scratch_shapes=[pltpu.VMEM((BM, D), _f32), pltpu.SemaphoreType.DMA, pltpu.SemaphoreType.DMA],
        ),
        compiler_params=pltpu.CompilerParams(vmem_limit_bytes=VMEM_LIMIT),
        name="dispatch",
    )(nvalid, dest_tiles, v_p, v_s)


def _experts_kernel(be_ref, nv_ref, x_ref, wu_ref, bu_ref, wd_ref, bd_ref, y_ref, wu_bf, wd_bf):
    b = pl.program_id(0)
    prev = be_ref[jnp.maximum(b - 1, 0)]

    @pl.when((b == 0) | (be_ref[b] != prev))
    def _():
        wu_bf[...] = wu_ref[...].astype(_bf16)
        wd_bf[...] = wd_ref[...].astype(_bf16)

    nvalid = nv_ref[b]

    @pl.when(nvalid > 0)
    def _():
        x = x_ref[...].astype(_bf16)
        hcat = _dot(x, wu_bf[...]) + bu_ref[...]
        glu = jnp.minimum(hcat[:, 0:F], SWIGLU_LIMIT)
        lin = jnp.clip(hcat[:, F:2 * F], -SWIGLU_LIMIT, SWIGLU_LIMIT)
        act = glu * jax.nn.sigmoid(SWIGLU_ALPHA * glu) * (lin + 1.0)
        y_ref[...] = _dot(act.astype(_bf16), wd_bf[...]) + bd_ref[...]

    @pl.when(nvalid == 0)
    def _():
        y_ref[...] = jnp.zeros_like(y_ref)


def _experts(block_e, nvalid, xpad, w_up, b_up, w_down, b_down):
    n_blocks = xpad.shape[0] // BM
    return pl.pallas_call(
        _experts_kernel,
        out_shape=jax.ShapeDtypeStruct(xpad.shape, _f32),
        grid_spec=pltpu.PrefetchScalarGridSpec(
            num_scalar_prefetch=2,
            grid=(n_blocks,),
            in_specs=[pl.BlockSpec((BM, D), lambda b, be, nv: (b, 0)),
                      pl.BlockSpec((None, D, 2 * F), lambda b, be, nv: (be[b], 0, 0)),
                      pl.BlockSpec((None, 1, 2 * F), lambda b, be, nv: (be[b], 0, 0)),
                      pl.BlockSpec((None, F, D), lambda b, be, nv: (be[b], 0, 0)),
                      pl.BlockSpec((None, 1, D), lambda b, be, nv: (be[b], 0, 0))],
            out_specs=pl.BlockSpec((BM, D), lambda b, be, nv: (b, 0)),
            scratch_shapes=[pltpu.VMEM((D, 2 * F), _bf16), pltpu.VMEM((F, D), _bf16)],
        ),
        compiler_params=pltpu.CompilerParams(vmem_limit_bytes=VMEM_LIMIT),
        name="experts",
    )(block_e, nvalid, xpad, w_up, b_up, w_down, b_down)


def _combine_kernel(dest_ref, h_ref, mod_ref, gate_ref, g2_ref, b2_ref, ypad_ref, o_ref, ybuf, sem):
    t = COMBINE_TILE

    def issue(i, c):
        for k in range(TOP_K):
            _row_copy(ypad_ref, dest_ref[0, 0, k * t + i], ybuf.at[k], i, sem).start()
        return c

    lax.fori_loop(0, t, issue, 0)
    for k in range(TOP_K):
        pltpu.make_async_copy(ypad_ref.at[pl.ds(0, t), :], ybuf.at[k], sem).wait()

    gates = gate_ref[...]
    f = gates[:, 0:1] * ybuf[0]
    for k in range(1, TOP_K):
        f = f + gates[:, k:k + 1] * ybuf[k]
    gate2 = mod_ref[...][:, 5 * D:6 * D]
    pre = ALPHA * h_ref[...] + _per_seq(f, gate2, jnp.multiply)
    o_ref[...] = _layer_norm(pre, g2_ref[...], b2_ref[...])


def _combine(dest_tiles, h, mod3, gates, ln2_g, ln2_b, ypad, row0, rows_per_mod):
    t = COMBINE_TILE
    off = row0 // t
    g = mod3.shape[1]
    tiles_per_mod = rows_per_mod // t
    return pl.pallas_call(
        _combine_kernel,
        out_shape=jax.ShapeDtypeStruct(h.shape, _f32),
        grid=(h.shape[0] // t,),
        in_specs=[pl.BlockSpec((1, 1, TOP_K * t), lambda i: (i + off, 0, 0), memory_space=pltpu.SMEM),
                  pl.BlockSpec((t, D), lambda i: (i, 0)),
                  pl.BlockSpec((None, g, 6 * D), lambda i: (i // tiles_per_mod, 0, 0)),
                  pl.BlockSpec((t, LANES), lambda i: (i + off, 0)),
                  pl.BlockSpec((1, D), lambda i: (0, 0)),
                  pl.BlockSpec((1, D), lambda i: (0, 0)),
                  pl.BlockSpec(memory_space=pl.ANY)],
        out_specs=pl.BlockSpec((t, D), lambda i: (i, 0)),
        scratch_shapes=[pltpu.VMEM((TOP_K, t, D), _f32), pltpu.SemaphoreType.DMA],
        compiler_params=pltpu.CompilerParams(vmem_limit_bytes=VMEM_LIMIT),
        name="combine",
    )(dest_tiles, h, mod3, gates, ln2_g, ln2_b, ypad)


def _time_major(a):
    return a.transpose(1, 0, 2)


def kernel(x_prompt, x_sample, c_prompt, c_sample, state_conv, state_pool, w_ada, b_ada, w_in,
           conv_w, w_out_a, w_pool, ls_pool, w_out_b, w_o, ln1_g, ln1_b, w_router, b_router,
           w_up, b_up, w_down, b_down, ln2_g, ln2_b):
    n_seq_p, seq, _ = x_prompt.shape
    n_seq_s, dec_seq, _ = x_sample.shape
    n_p, n_s = n_seq_p * seq, n_seq_s * dec_seq
    n = n_p + n_s
    n_blocks = TOP_K * n // BM + E
    l = 0

    mod = _ada(jnp.concatenate([c_prompt, c_sample], axis=0), w_ada[l], b_ada[l][None])
    mod_p = mod[:n_seq_p][:, None, :]
    mod_s = mod[n_seq_p:][None]

    weights = (
        w_in[l].astype(_bf16), conv_w[l], w_out_a[l].astype(_bf16), w_pool[l].astype(_bf16),
        ls_pool[l][None], w_out_b[l].astype(_bf16), w_o[l].astype(_bf16), ln1_g[l][None], ln1_b[l][None],
        jnp.pad(w_router[l], ((0, 0), (0, LANES - E))), jnp.pad(b_router[l], (0, LANES - E))[None],
    )
    hc_p, hp_p = _hist_steps(CONV_HIST, 1), _hist_steps(POOL_HIST, 1)
    h_p, v_p, lg_p, nc_p, np_p = _mixer(
        x_prompt.reshape(n_p, D), mod_p, jnp.zeros((n_seq_p * hc_p, C), _f32),
        jnp.zeros((n_seq_p * hp_p, C), _f32), weights, ROW_TILE, 0)
    hc_s, hp_s = _hist_steps(CONV_HIST, n_seq_s), _hist_steps(POOL_HIST, n_seq_s)
    hist_c = jnp.pad(_time_major(state_conv[l]), ((hc_s - CONV_HIST, 0), (0, 0), (0, 0)))
    hist_p = jnp.pad(_time_major(state_pool[l]), ((hp_s - POOL_HIST, 0), (0, 0), (0, 0)))
    h_s, v_s, lg_s, nc_s, np_s = _mixer(
        _time_major(x_sample).reshape(n_s, D), mod_s, hist_c.reshape(hc_s * n_seq_s, C),
        hist_p.reshape(hp_s * n_seq_s, C), weights, SAMPLE_ROW_TILE, PAST_LEN)

    dest8, gates3, block_e, nvalid = _plan(lg_p, lg_s, n_blocks)
    block_e, nvalid = block_e[0, :n_blocks], nvalid[0, :n_blocks]
    sub = ROW_TILE // COMBINE_TILE
    dest_tiles = (dest8[:, :TOP_K, :].reshape(-1, TOP_K, sub, COMBINE_TILE).transpose(0, 2, 1, 3)
                  .reshape(n // COMBINE_TILE, 1, TOP_K * COMBINE_TILE))
    gates = gates3.reshape(n, LANES)

    xpad = _dispatch(nvalid, dest_tiles, v_p, v_s, n_blocks)
    ypad = _experts(block_e, nvalid, xpad, w_up[l], b_up[l][:, None, :], w_down[l], b_down[l][:, None, :])

    g2, b2 = ln2_g[l][None], ln2_b[l][None]
    y_p = _combine(dest_tiles, h_p, mod_p, gates, g2, b2, ypad, 0, seq)
    y_s = _combine(dest_tiles, h_s, mod_s, gates, g2, b2, ypad, n_p, n_s)

    y_prompt = y_p.reshape(n_seq_p, seq, D)
    y_sample = _time_major(y_s.reshape(dec_seq, n_seq_s, D))
    new_conv_p = nc_p.reshape(n_seq_p, hc_p, C)[:, hc_p - CONV_HIST:][None]
    new_pool_p = np_p.reshape(n_seq_p, hp_p, C)[:, hp_p - POOL_HIST:][None]
    new_conv_s = _time_major(nc_s.reshape(hc_s, n_seq_s, C)[hc_s - CONV_HIST:])[None]
    new_pool_s = _time_major(np_s.reshape(hp_s, n_seq_s, C)[hp_s - POOL_HIST:])[None]
    return (y_prompt, y_sample, new_conv_p, new_pool_p, new_conv_s, new_pool_s)
```

```python
import functools

import jax
import jax.numpy as jnp
from jax import lax
from jax.experimental import pallas as pl
from jax.experimental.pallas import tpu as pltpu

D = 1024
C = 512
N_GROUPS = 4
GROUP = C // N_GROUPS
CONV_HIST = 2
POOL_HIST = 15
E = 32
TOP_K = 4
F = 1024
SWIGLU_LIMIT = 7.0
SWIGLU_ALPHA = 1.702
LN_EPS = 1e-5
DEPTH = 1
ALPHA = (2 * DEPTH) ** 0.25
PAST_LEN = 16384

LANES = 128
SUBLANES = 8
ROW_TILE = 512
SAMPLE_ROW_TILE = 256
BM = 256
COMBINE_TILE = 256
VMEM_LIMIT = 56 * 1024 * 1024

_f32 = jnp.float32
_bf16 = jnp.bfloat16


def _dot(a, b):
    return jnp.dot(a, b, preferred_element_type=_f32)


def _dot_exact(a, b):
    return lax.dot_general(a, b, (((1,), (0,)), ((), ())),
                           precision=lax.Precision.HIGHEST, preferred_element_type=_f32)


def _dot_split(a, b):
    a_hi, b_hi = a.astype(_bf16), b.astype(_bf16)
    a_lo = (a - a_hi.astype(_f32)).astype(_bf16)
    b_lo = (b - b_hi.astype(_f32)).astype(_bf16)
    return _dot(a_hi, b_hi) + _dot(a_lo, b_hi) + _dot(a_hi, b_lo)


def _per_seq(x, m, op):
    g = m.shape[0]
    if g == 1:
        return op(x, m)
    r, n = x.shape
    return op(x.reshape(r // g, g, n), m[None]).reshape(r, n)


def _layer_norm(x, g, b):
    mu = jnp.mean(x, axis=-1, keepdims=True)
    xc = x - mu
    var = jnp.mean(xc * xc, axis=-1, keepdims=True)
    return xc * lax.rsqrt(var + LN_EPS) * g + b


def _hist_steps(needed, g):
    return -(-needed * g // SUBLANES) * SUBLANES // g


def _ada_kernel(c_ref, w_ref, b_ref, o_ref):
    c = c_ref[...]
    o_ref[...] = _dot_exact(c * jax.nn.sigmoid(c), w_ref[...]) + b_ref[...]


def _ada(c, w_ada, b_ada):
    rows = c.shape[0]
    cols = w_ada.shape[1]
    bn = 1536
    return pl.pallas_call(
        _ada_kernel,
        out_shape=jax.ShapeDtypeStruct((rows, cols), _f32),
        grid=(cols // bn,),
        in_specs=[pl.BlockSpec((rows, D), lambda j: (0, 0)),
                  pl.BlockSpec((D, bn), lambda j: (0, j)),
                  pl.BlockSpec((1, bn), lambda j: (0, j))],
        out_specs=pl.BlockSpec((rows, bn), lambda j: (0, j)),
        compiler_params=pltpu.CompilerParams(vmem_limit_bytes=VMEM_LIMIT),
        name="ada",
    )(c, w_ada, b_ada)


def _mixer_kernel(g, tiles_per_seq, start_pos,
                  x_ref, mod_ref, hc_ref, hp_ref,
                  win_ref, cw_ref, woa_ref, wpool_ref, ls_ref, wob_ref, wo_ref, g1_ref, b1_ref,
                  wr_ref, br_ref,
                  h_ref, v_ref, lg_ref, nc_ref, np_ref, zbuf, pbuf):
    r = x_ref.shape[0]
    hrc = hc_ref.shape[0]
    hrp = hp_ref.shape[0]
    j = pl.program_id(0) % tiles_per_seq

    @pl.when(j == 0)
    def _():
        zbuf[pl.ds(0, hrc), :] = hc_ref[...]
        pbuf[pl.ds(0, hrp), :] = hp_ref[...]

    @pl.when(j != 0)
    def _():
        zt = zbuf[pl.ds(r, hrc), :]
        pt = pbuf[pl.ds(r, hrp), :]
        zbuf[pl.ds(0, hrc), :] = zt
        pbuf[pl.ds(0, hrp), :] = pt

    m = mod_ref[...]
    shift1, scale1, gate1 = m[:, 0:D], m[:, D:2 * D], m[:, 2 * D:3 * D]
    shift2, scale2 = m[:, 3 * D:4 * D], m[:, 4 * D:5 * D]

    x = x_ref[...]
    u = _per_seq(_per_seq(x, 1.0 + scale1, jnp.multiply), shift1, jnp.add).astype(_bf16)

    z = _dot(u, win_ref[:, C:2 * C]) * _dot(u, win_ref[:, 2 * C:3 * C])
    zbuf[pl.ds(hrc, r), :] = z
    cw = cw_ref[...]
    conv = (cw[0:1] * zbuf[pl.ds(hrc - 2 * g, r), :] + cw[1:2] * zbuf[pl.ds(hrc - g, r), :]
            + cw[2:3] * z)
    y_a = _dot((_dot(u, win_ref[:, 0:C]) * conv).astype(_bf16), woa_ref[...])

    xp = _dot(u, win_ref[:, 3 * C:4 * C])
    pbuf[pl.ds(hrp, r), :] = xp
    pos = start_pos + j * (r // g) + lax.broadcasted_iota(jnp.int32, (r, 1), 0) // g
    acc = xp
    yg = []
    for grp in range(N_GROUPS):
        lo = grp * GROUP
        wdw = 2 ** (grp + 1)
        for back in range(wdw // 2, wdw):
            sh = pbuf[pl.ds(hrp - back * g, r), lo:C]
            acc = jnp.concatenate([acc[:, 0:lo], acc[:, lo:C] + sh], axis=1) if lo else acc + sh
        cnt = jnp.minimum(wdw, pos + 1).astype(_f32)
        diff = acc[:, lo:lo + GROUP] / cnt - xp[:, lo:lo + GROUP]
        yg.append(_dot(diff.astype(_bf16), wpool_ref[grp]))
    y_b = _dot((jnp.concatenate(yg, axis=1) * ls_ref[...]).astype(_bf16), wob_ref[...])

    g_a = _dot(u, win_ref[:, 4 * C:4 * C + D])
    g_b = _dot(u, win_ref[:, 4 * C + D:4 * C + 2 * D])
    merged = jax.nn.sigmoid(g_a) * y_a + jax.nn.sigmoid(g_b) * y_b
    o = _dot(merged.astype(_bf16), wo_ref[...])
    h = _layer_norm(ALPHA * x + _per_seq(o, gate1, jnp.multiply), g1_ref[...], b1_ref[...])
    v = _per_seq(_per_seq(h, 1.0 + scale2, jnp.multiply), shift2, jnp.add)
    h_ref[...] = h
    v_ref[...] = v
    lg_ref[...] = _dot_split(v, wr_ref[...]) + br_ref[...]
    nc_ref[...] = zbuf[pl.ds(r, hrc), :]
    np_ref[...] = pbuf[pl.ds(r, hrp), :]


def _mixer(x2, mod3, hc, hp, weights, row_tile, start_pos):
    n = x2.shape[0]
    n_mod, g, _ = mod3.shape
    hrc, hrp = hc.shape[0] // n_mod, hp.shape[0] // n_mod
    tiles_per_seq = n // n_mod // row_tile
    once = dict(pipeline_mode=pl.Buffered(1)) if n_mod == 1 else {}

    def full(a):
        nd = a.ndim
        return pl.BlockSpec(a.shape, lambda i: (0,) * nd)

    def seq_block(rows, **kw):
        return pl.BlockSpec((rows, C), lambda i: (i // tiles_per_seq, 0), **kw)

    def row_block(cols):
        return pl.BlockSpec((row_tile, cols), lambda i: (i, 0))

    return pl.pallas_call(
        functools.partial(_mixer_kernel, g, tiles_per_seq, start_pos),
        out_shape=[
            jax.ShapeDtypeStruct((n, D), _f32),
            jax.ShapeDtypeStruct((n, D), _f32),
            jax.ShapeDtypeStruct((n, LANES), _f32),
            jax.ShapeDtypeStruct(hc.shape, _f32),
            jax.ShapeDtypeStruct(hp.shape, _f32),
        ],
        grid=(n // row_tile,),
        in_specs=[row_block(D),
                  pl.BlockSpec((None, g, 6 * D), lambda i: (i // tiles_per_seq, 0, 0), **once),
                  seq_block(hrc, **once), seq_block(hrp, **once)] + [full(a) for a in weights],
        out_specs=[row_block(D), row_block(D), row_block(LANES), seq_block(hrc), seq_block(hrp)],
        scratch_shapes=[pltpu.VMEM((hrc + row_tile, C), _f32), pltpu.VMEM((hrp + row_tile, C), _f32)],
        compiler_params=pltpu.CompilerParams(vmem_limit_bytes=VMEM_LIMIT),
        name="mixer",
    )(x2, mod3, hc, hp, *weights)


def _plan_kernel(n_blocks_pad, lgp_ref, lgs_ref, dest_ref, gate_ref, be_ref, nv_ref, nx_ref, idx_s, rank_s):
    t = ROW_TILE
    e_iota = lax.broadcasted_iota(jnp.int32, (E, t), 0)
    tri = (lax.broadcasted_iota(jnp.int32, (t, t), 0)
           < lax.broadcasted_iota(jnp.int32, (t, t), 1)).astype(_f32).astype(_bf16)
    zeros_rest = jnp.zeros((LANES - TOP_K, t), _f32)

    def tile_body(lg_ref, off, i, carry):
        lt = lg_ref[i].T[0:E, :]
        vals, idxs = [], []
        for _ in range(TOP_K):
            mx = jnp.max(lt, axis=0, keepdims=True)
            ix = jnp.min(jnp.where(lt == mx, e_iota, E), axis=0, keepdims=True)
            vals.append(mx)
            idxs.append(ix)
            lt = jnp.where(e_iota == ix, -jnp.inf, lt)
        ex = [jnp.exp(vk - vals[0]) for vk in vals]
        den = ex[0] + ex[1] + ex[2] + ex[3]
        gates = [ek / den for ek in ex]
        gate_ref[off + i] = jnp.concatenate(gates + [zeros_rest], axis=0).T

        ohs = [(e_iota == ix) for ix in idxs]
        oh = (ohs[0] | ohs[1] | ohs[2] | ohs[3]).astype(_f32)
        before = _dot(oh.astype(_bf16), tri) + carry
        ranks = [jnp.sum(jnp.where(o, before, 0.0), axis=0, keepdims=True) for o in ohs]
        idx_s[off + i] = jnp.concatenate(idxs + idxs, axis=0)
        rank_s[off + i] = jnp.concatenate(ranks + ranks, axis=0).astype(jnp.int32)
        return carry + jnp.sum(oh, axis=1, keepdims=True)

    n_p, n_s = lgp_ref.shape[0], lgs_ref.shape[0]
    counts = lax.fori_loop(0, n_p, functools.partial(tile_body, lgp_ref, 0), jnp.zeros((E, 1), _f32))
    counts = lax.fori_loop(0, n_s, functools.partial(tile_body, lgs_ref, n_p), counts)
    padded = jnp.ceil(counts / BM) * BM
    low = (lax.broadcasted_iota(jnp.int32, (E, E), 1)
           <= lax.broadcasted_iota(jnp.int32, (E, E), 0)).astype(_f32)
    pad_end = _dot_exact(low, jnp.broadcast_to(padded, (E, LANES)))[:, 0:1]
    pad_start = pad_end - padded

    def dest_body(i, c):
        ix = idx_s[i]
        rk = rank_s[i]
        rows = []
        for k in range(TOP_K):
            st = jnp.sum(jnp.where(e_iota == ix[k:k + 1], pad_start, 0.0), axis=0, keepdims=True)
            rows.append(st.astype(jnp.int32) + rk[k:k + 1])
        dest_ref[i] = jnp.concatenate(rows + rows, axis=0)
        return c

    lax.fori_loop(0, n_p + n_s, dest_body, 0)

    b_start = (lax.broadcasted_iota(jnp.int32, (1, n_blocks_pad), 1) * BM).astype(_f32)
    be = jnp.minimum(jnp.sum((pad_end <= b_start).astype(jnp.int32), axis=0, keepdims=True), E - 1)
    sel = lax.broadcasted_iota(jnp.int32, (E, n_blocks_pad), 0) == be
    used_end = jnp.sum(jnp.where(sel, pad_start + counts, 0.0), axis=0, keepdims=True)
    be_ref[...] = be
    nv_ref[...] = jnp.clip(used_end - b_start, 0.0, float(BM)).astype(jnp.int32)
    eb = lax.broadcasted_iota(jnp.int32, (E, n_blocks_pad), 0)
    later = jnp.min(jnp.where((eb > be) & (counts > 0.0), eb, E), axis=0, keepdims=True)
    nx_ref[...] = jnp.where(later == E, be, later)


def _plan(logits_p, logits_s, n_blocks):
    n_tiles = (logits_p.shape[0] + logits_s.shape[0]) // ROW_TILE
    n_blocks_pad = -(-n_blocks // LANES) * LANES
    return pl.pallas_call(
        functools.partial(_plan_kernel, n_blocks_pad),
        out_shape=[
            jax.ShapeDtypeStruct((n_tiles, 2 * TOP_K, ROW_TILE), jnp.int32),
            jax.ShapeDtypeStruct((n_tiles, ROW_TILE, LANES), _f32),
            jax.ShapeDtypeStruct((1, n_blocks_pad), jnp.int32),
            jax.ShapeDtypeStruct((1, n_blocks_pad), jnp.int32),
            jax.ShapeDtypeStruct((1, n_blocks_pad), jnp.int32),
        ],
        scratch_shapes=[pltpu.VMEM((n_tiles, 2 * TOP_K, ROW_TILE), jnp.int32),
                        pltpu.VMEM((n_tiles, 2 * TOP_K, ROW_TILE), jnp.int32)],
        compiler_params=pltpu.CompilerParams(vmem_limit_bytes=VMEM_LIMIT),
        name="plan",
    )(logits_p.reshape(-1, ROW_TILE, LANES), logits_s.reshape(-1, ROW_TILE, LANES))


def _row_copy(src, src_row, dst, dst_row, sem):
    return pltpu.make_async_copy(src.at[pl.ds(src_row, 1), :], dst.at[pl.ds(dst_row, 1), :], sem)


def _dispatch_kernel(n_blocks, n_ptiles, nv_ref, dest_ref, vp_ref, vs_ref, xpad_ref, zero_buf, sem, zsem):
    t = COMBINE_TILE
    step = pl.program_id(0)

    def zero_copy(b):
        return pltpu.make_async_copy(zero_buf, xpad_ref.at[pl.ds(b * BM, BM), :], zsem)

    @pl.when(step == 0)
    def _():
        zero_buf[...] = jnp.zeros_like(zero_buf)

        def start(b, c):
            @pl.when(nv_ref[b] < BM)
            def _():
                zero_copy(b).start()
            return c

        def wait(b, c):
            @pl.when(nv_ref[b] < BM)
            def _():
                zero_copy(b).wait()
            return c

        lax.fori_loop(0, n_blocks, start, 0)
        lax.fori_loop(0, n_blocks, wait, 0)

    def scatter(v_ref):
        def issue(i, c):
            for k in range(TOP_K):
                _row_copy(v_ref, i, xpad_ref, dest_ref[0, 0, k * t + i], sem).start()
            return c

        lax.fori_loop(0, t, issue, 0)
        for k in range(TOP_K):
            pltpu.make_async_copy(v_ref, xpad_ref.at[pl.ds(0, t), :], sem).wait()

    @pl.when(step < n_ptiles)
    def _():
        scatter(vp_ref)

    @pl.when(step >= n_ptiles)
    def _():
        scatter(vs_ref)


def _dispatch(nvalid, dest_tiles, v_p, v_s, n_blocks):
    t = COMBINE_TILE
    n_ptiles, n_stiles = v_p.shape[0] // t, v_s.shape[0] // t
    return pl.pallas_call(
        functools.partial(_dispatch_kernel, n_blocks, n_ptiles),
        out_shape=jax.ShapeDtypeStruct((n_blocks * BM, D), _f32),
        grid_spec=pltpu.PrefetchScalarGridSpec(
            num_scalar_prefetch=1,
            grid=(n_ptiles + n_stiles,),
            in_specs=[pl.BlockSpec((1, 1, TOP_K * t), lambda i, nv: (i, 0, 0), memory_space=pltpu.SMEM),
                      pl.BlockSpec((t, D), lambda i, nv: (jnp.minimum(i, n_ptiles - 1), 0)),
                      pl.BlockSpec((t, D), lambda i, nv: (jnp.maximum(i - n_ptiles, 0), 0))],
            out_specs=pl.BlockSpec(memory_space=pl.ANY),
            scratch_shapes=[pltpu.VMEM((BM, D), _f32), pltpu.SemaphoreType.DMA, pltpu.SemaphoreType.DMA],
        ),
        compiler_params=pltpu.CompilerParams(vmem_limit_bytes=VMEM_LIMIT),
        name="dispatch",
    )(nvalid, dest_tiles, v_p, v_s)


def _experts_kernel(be_ref, nv_ref, nx_ref, x_ref, wu_hbm, bu_ref, wd_hbm, bd_ref, y_ref,
                    wu_st, wd_st, wu_bf, wd_bf, sems):
    b = pl.program_id(0)
    e = be_ref[b]
    nvalid = nv_ref[b]

    def fetch(expert):
        return (pltpu.make_async_copy(wu_hbm.at[expert], wu_st, sems.at[0]),
                pltpu.make_async_copy(wd_hbm.at[expert], wd_st, sems.at[1]))

    @pl.when((nvalid > 0) & ((b == 0) | (e != be_ref[jnp.maximum(b - 1, 0)])))
    def _():
        @pl.when(b == 0)
        def _():
            for c in fetch(e):
                c.start()

        for c in fetch(e):
            c.wait()
        wu_bf[...] = wu_st[...].astype(_bf16)
        wd_bf[...] = wd_st[...].astype(_bf16)

        @pl.when(nx_ref[b] != e)
        def _():
            for c in fetch(nx_ref[b]):
                c.start()

    @pl.when(nvalid > 0)
    def _():
        x = x_ref[...].astype(_bf16)
        hcat = _dot(x, wu_bf[...]) + bu_ref[...]
        glu = jnp.minimum(hcat[:, 0:F], SWIGLU_LIMIT)
        lin = jnp.clip(hcat[:, F:2 * F], -SWIGLU_LIMIT, SWIGLU_LIMIT)
        act = glu * jax.nn.sigmoid(SWIGLU_ALPHA * glu) * (lin + 1.0)
        y_ref[...] = _dot(act.astype(_bf16), wd_bf[...]) + bd_ref[...]

    @pl.when(nvalid == 0)
    def _():
        y_ref[...] = jnp.zeros_like(y_ref)


def _experts(block_e, nvalid, next_e, xpad, w_up, b_up, w_down, b_down):
    n_blocks = xpad.shape[0] // BM
    return pl.pallas_call(
        _experts_kernel,
        out_shape=jax.ShapeDtypeStruct(xpad.shape, _f32),
        grid_spec=pltpu.PrefetchScalarGridSpec(
            num_scalar_prefetch=3,
            grid=(n_blocks,),
            in_specs=[pl.BlockSpec((BM, D), lambda b, be, nv, nx: (b, 0)),
                      pl.BlockSpec(memory_space=pl.ANY),
                      pl.BlockSpec((None, 1, 2 * F), lambda b, be, nv, nx: (be[b], 0, 0)),
                      pl.BlockSpec(memory_space=pl.ANY),
                      pl.BlockSpec((None, 1, D), lambda b, be, nv, nx: (be[b], 0, 0))],
            out_specs=pl.BlockSpec((BM, D), lambda b, be, nv, nx: (b, 0)),
            scratch_shapes=[pltpu.VMEM((D, 2 * F), _f32), pltpu.VMEM((F, D), _f32),
                            pltpu.VMEM((D, 2 * F), _bf16), pltpu.VMEM((F, D), _bf16),
                            pltpu.SemaphoreType.DMA((2,))],
        ),
        compiler_params=pltpu.CompilerParams(vmem_limit_bytes=VMEM_LIMIT),
        name="experts",
    )(block_e, nvalid, next_e, xpad, w_up, b_up, w_down, b_down)


def _combine_kernel(dest_ref, h_ref, mod_ref, gate_ref, g2_ref, b2_ref, ypad_ref, o_ref, ybuf, sem):
    t = COMBINE_TILE

    def issue(i, c):
        for k in range(TOP_K):
            _row_copy(ypad_ref, dest_ref[0, 0, k * t + i], ybuf.at[k], i, sem).start()
        return c

    lax.fori_loop(0, t, issue, 0)
    for k in range(TOP_K):
        pltpu.make_async_copy(ypad_ref.at[pl.ds(0, t), :], ybuf.at[k], sem).wait()

    gates = gate_ref[...]
    f = gates[:, 0:1] * ybuf[0]
    for k in range(1, TOP_K):
        f = f + gates[:, k:k + 1] * ybuf[k]
    gate2 = mod_ref[...][:, 5 * D:6 * D]
    pre = ALPHA * h_ref[...] + _per_seq(f, gate2, jnp.multiply)
    o_ref[...] = _layer_norm(pre, g2_ref[...], b2_ref[...])


def _combine(dest_tiles, h, mod3, gates, ln2_g, ln2_b, ypad, row0, rows_per_mod):
    t = COMBINE_TILE
    off = row0 // t
    g = mod3.shape[1]
    tiles_per_mod = rows_per_mod // t
    return pl.pallas_call(
        _combine_kernel,
        out_shape=jax.ShapeDtypeStruct(h.shape, _f32),
        grid=(h.shape[0] // t,),
        in_specs=[pl.BlockSpec((1, 1, TOP_K * t), lambda i: (i + off, 0, 0), memory_space=pltpu.SMEM),
                  pl.BlockSpec((t, D), lambda i: (i, 0)),
                  pl.BlockSpec((None, g, 6 * D), lambda i: (i // tiles_per_mod, 0, 0)),
                  pl.BlockSpec((t, LANES), lambda i: (i + off, 0)),
                  pl.BlockSpec((1, D), lambda i: (0, 0)),
                  pl.BlockSpec((1, D), lambda i: (0, 0)),
                  pl.BlockSpec(memory_space=pl.ANY)],
        out_specs=pl.BlockSpec((t, D), lambda i: (i, 0)),
        scratch_shapes=[pltpu.VMEM((TOP_K, t, D), _f32), pltpu.SemaphoreType.DMA],
        compiler_params=pltpu.CompilerParams(vmem_limit_bytes=VMEM_LIMIT),
        name="combine",
    )(dest_tiles, h, mod3, gates, ln2_g, ln2_b, ypad)


def _time_major(a):
    return a.transpose(1, 0, 2)


def kernel(x_prompt, x_sample, c_prompt, c_sample, state_conv, state_pool, w_ada, b_ada, w_in,
           conv_w, w_out_a, w_pool, ls_pool, w_out_b, w_o, ln1_g, ln1_b, w_router, b_router,
           w_up, b_up, w_down, b_down, ln2_g, ln2_b):
    n_seq_p, seq, _ = x_prompt.shape
    n_seq_s, dec_seq, _ = x_sample.shape
    n_p, n_s = n_seq_p * seq, n_seq_s * dec_seq
    n = n_p + n_s
    n_blocks = TOP_K * n // BM + E
    l = 0

    mod = _ada(jnp.concatenate([c_prompt, c_sample], axis=0), w_ada[l], b_ada[l][None])
    mod_p = mod[:n_seq_p][:, None, :]
    mod_s = mod[n_seq_p:][None]

    weights = (
        w_in[l].astype(_bf16), conv_w[l], w_out_a[l].astype(_bf16), w_pool[l].astype(_bf16),
        ls_pool[l][None], w_out_b[l].astype(_bf16), w_o[l].astype(_bf16), ln1_g[l][None], ln1_b[l][None],
        jnp.pad(w_router[l], ((0, 0), (0, LANES - E))), jnp.pad(b_router[l], (0, LANES - E))[None],
    )
    hc_p, hp_p = _hist_steps(CONV_HIST, 1), _hist_steps(POOL_HIST, 1)
    h_p, v_p, lg_p, nc_p, np_p = _mixer(
        x_prompt.reshape(n_p, D), mod_p, jnp.zeros((n_seq_p * hc_p, C), _f32),
        jnp.zeros((n_seq_p * hp_p, C), _f32), weights, ROW_TILE, 0)
    hc_s, hp_s = _hist_steps(CONV_HIST, n_seq_s), _hist_steps(POOL_HIST, n_seq_s)
    hist_c = jnp.pad(_time_major(state_conv[l]), ((hc_s - CONV_HIST, 0), (0, 0), (0, 0)))
    hist_p = jnp.pad(_time_major(state_pool[l]), ((hp_s - POOL_HIST, 0), (0, 0), (0, 0)))
    h_s, v_s, lg_s, nc_s, np_s = _mixer(
        _time_major(x_sample).reshape(n_s, D), mod_s, hist_c.reshape(hc_s * n_seq_s, C),
        hist_p.reshape(hp_s * n_seq_s, C), weights, SAMPLE_ROW_TILE, PAST_LEN)

    dest8, gates3, block_e, nvalid, next_e = _plan(lg_p, lg_s, n_blocks)
    block_e, nvalid, next_e = block_e[0, :n_blocks], nvalid[0, :n_blocks], next_e[0, :n_blocks]
    sub = ROW_TILE // COMBINE_TILE
    dest_tiles = (dest8[:, :TOP_K, :].reshape(-1, TOP_K, sub, COMBINE_TILE).transpose(0, 2, 1, 3)
                  .reshape(n // COMBINE_TILE, 1, TOP_K * COMBINE_TILE))
    gates = gates3.reshape(n, LANES)

    xpad = _dispatch(nvalid, dest_tiles, v_p, v_s, n_blocks)
    ypad = _experts(block_e, nvalid, next_e, xpad, w_up[l], b_up[l][:, None, :], w_down[l], b_down[l][:, None, :])

    g2, b2 = ln2_g[l][None], ln2_b[l][None]
    y_p = _combine(dest_tiles, h_p, mod_p, gates, g2, b2, ypad, 0, seq)
    y_s = _combine(dest_tiles, h_s, mod_s, gates, g2, b2, ypad, n_p, n_s)

    y_prompt = y_p.reshape(n_seq_p, seq, D)
    y_sample = _time_major(y_s.reshape(dec_seq, n_seq_s, D))
    new_conv_p = nc_p.reshape(n_seq_p, hc_p, C)[:, hc_p - CONV_HIST:][None]
    new_pool_p = np_p.reshape(n_seq_p, hp_p, C)[:, hp_p - POOL_HIST:][None]
    new_conv_s = _time_major(nc_s.reshape(hc_s, n_seq_s, C)[hc_s - CONV_HIST:])[None]
    new_pool_s = _time_major(np_s.reshape(hp_s, n_seq_s, C)[hp_s - POOL_HIST:])[None]
    return (y_prompt, y_sample, new_conv_p, new_pool_p, new_conv_s, new_pool_s)
```

```python
import functools

import jax
import jax.numpy as jnp
from jax import lax
from jax.experimental import pallas as pl
from jax.experimental.pallas import tpu as pltpu
from jax.experimental.pallas import tpu_sc as plsc

D = 1024
C = 512
N_GROUPS = 4
GROUP = C // N_GROUPS
CONV_HIST = 2
POOL_HIST = 15
E = 32
TOP_K = 4
F = 1024
SWIGLU_LIMIT = 7.0
SWIGLU_ALPHA = 1.702
LN_EPS = 1e-5
DEPTH = 1
ALPHA = (2 * DEPTH) ** 0.25
PAST_LEN = 16384

LANES = 128
SUBLANES = 8
ROW_TILE = 512
SAMPLE_ROW_TILE = 256
BM = 256
COMBINE_TILE = 256
SC_WINDOW = 128
VMEM_LIMIT = 56 * 1024 * 1024

_f32 = jnp.float32
_bf16 = jnp.bfloat16


def _dot(a, b):
    return jnp.dot(a, b, preferred_element_type=_f32)


def _dot_exact(a, b):
    return lax.dot_general(a, b, (((1,), (0,)), ((), ())),
                           precision=lax.Precision.HIGHEST, preferred_element_type=_f32)


def _dot_split(a, b):
    a_hi, b_hi = a.astype(_bf16), b.astype(_bf16)
    a_lo = (a - a_hi.astype(_f32)).astype(_bf16)
    b_lo = (b - b_hi.astype(_f32)).astype(_bf16)
    return _dot(a_hi, b_hi) + _dot(a_lo, b_hi) + _dot(a_hi, b_lo)


def _pack_rows(x):
    w = x.shape[1] // 2
    hi = lax.bitcast_convert_type(x[:, :w].astype(_bf16).astype(_f32), jnp.int32)
    lo = lax.bitcast_convert_type(x[:, w:].astype(_bf16).astype(_f32), jnp.int32)
    return hi | lax.shift_right_logical(lo, 16)


def _unpack_rows(p):
    hi = lax.bitcast_convert_type(p & jnp.int32(-65536), _f32)
    lo = lax.bitcast_convert_type(lax.shift_left(p, 16), _f32)
    return jnp.concatenate([hi, lo], axis=1).astype(_bf16)


def _per_seq(x, m, op):
    g = m.shape[0]
    if g == 1:
        return op(x, m)
    r, n = x.shape
    return op(x.reshape(r // g, g, n), m[None]).reshape(r, n)


def _layer_norm(x, g, b):
    mu = jnp.mean(x, axis=-1, keepdims=True)
    xc = x - mu
    var = jnp.mean(xc * xc, axis=-1, keepdims=True)
    return xc * lax.rsqrt(var + LN_EPS) * g + b


def _hist_steps(needed, g):
    return -(-needed * g // SUBLANES) * SUBLANES // g


def _ada_kernel(c_ref, w_ref, b_ref, o_ref):
    c = c_ref[...]
    o_ref[...] = _dot_exact(c * jax.nn.sigmoid(c), w_ref[...]) + b_ref[...]


def _ada(c, w_ada, b_ada):
    rows = c.shape[0]
    cols = w_ada.shape[1]
    bn = 1536
    return pl.pallas_call(
        _ada_kernel,
        out_shape=jax.ShapeDtypeStruct((rows, cols), _f32),
        grid=(cols // bn,),
        in_specs=[pl.BlockSpec((rows, D), lambda j: (0, 0)),
                  pl.BlockSpec((D, bn), lambda j: (0, j)),
                  pl.BlockSpec((1, bn), lambda j: (0, j))],
        out_specs=pl.BlockSpec((rows, bn), lambda j: (0, j)),
        compiler_params=pltpu.CompilerParams(vmem_limit_bytes=VMEM_LIMIT),
        name="ada",
    )(c, w_ada, b_ada)


def _mixer_kernel(g, tiles_per_seq, start_pos,
                  x_ref, mod_ref, hc_ref, hp_ref,
                  win_ref, cw_ref, woa_ref, wpool_ref, ls_ref, wob_ref, wo_ref, g1_ref, b1_ref,
                  wr_ref, br_ref,
                  h_ref, v_ref, lg_ref, nc_ref, np_ref, zbuf, pbuf):
    r = x_ref.shape[0]
    hrc = hc_ref.shape[0]
    hrp = hp_ref.shape[0]
    j = pl.program_id(0) % tiles_per_seq

    @pl.when(j == 0)
    def _():
        zbuf[pl.ds(0, hrc), :] = hc_ref[...]
        pbuf[pl.ds(0, hrp), :] = hp_ref[...]

    @pl.when(j != 0)
    def _():
        zt = zbuf[pl.ds(r, hrc), :]
        pt = pbuf[pl.ds(r, hrp), :]
        zbuf[pl.ds(0, hrc), :] = zt
        pbuf[pl.ds(0, hrp), :] = pt

    m = mod_ref[...]
    shift1, scale1, gate1 = m[:, 0:D], m[:, D:2 * D], m[:, 2 * D:3 * D]
    shift2, scale2 = m[:, 3 * D:4 * D], m[:, 4 * D:5 * D]

    x = x_ref[...]
    u = _per_seq(_per_seq(x, 1.0 + scale1, jnp.multiply), shift1, jnp.add).astype(_bf16)

    z = _dot(u, win_ref[:, C:2 * C]) * _dot(u, win_ref[:, 2 * C:3 * C])
    zbuf[pl.ds(hrc, r), :] = z
    cw = cw_ref[...]
    conv = (cw[0:1] * zbuf[pl.ds(hrc - 2 * g, r), :] + cw[1:2] * zbuf[pl.ds(hrc - g, r), :]
            + cw[2:3] * z)
    y_a = _dot((_dot(u, win_ref[:, 0:C]) * conv).astype(_bf16), woa_ref[...])

    xp = _dot(u, win_ref[:, 3 * C:4 * C])
    pbuf[pl.ds(hrp, r), :] = xp
    pos = start_pos + j * (r // g) + lax.broadcasted_iota(jnp.int32, (r, 1), 0) // g
    acc = xp
    yg = []
    for grp in range(N_GROUPS):
        lo = grp * GROUP
        wdw = 2 ** (grp + 1)
        for back in range(wdw // 2, wdw):
            sh = pbuf[pl.ds(hrp - back * g, r), lo:C]
            acc = jnp.concatenate([acc[:, 0:lo], acc[:, lo:C] + sh], axis=1) if lo else acc + sh
        cnt = jnp.minimum(wdw, pos + 1).astype(_f32)
        diff = acc[:, lo:lo + GROUP] / cnt - xp[:, lo:lo + GROUP]
        yg.append(_dot(diff.astype(_bf16), wpool_ref[grp]))
    y_b = _dot((jnp.concatenate(yg, axis=1) * ls_ref[...]).astype(_bf16), wob_ref[...])

    g_a = _dot(u, win_ref[:, 4 * C:4 * C + D])
    g_b = _dot(u, win_ref[:, 4 * C + D:4 * C + 2 * D])
    merged = jax.nn.sigmoid(g_a) * y_a + jax.nn.sigmoid(g_b) * y_b
    o = _dot(merged.astype(_bf16), wo_ref[...])
    h = _layer_norm(ALPHA * x + _per_seq(o, gate1, jnp.multiply), g1_ref[...], b1_ref[...])
    v = _per_seq(_per_seq(h, 1.0 + scale2, jnp.multiply), shift2, jnp.add)
    h_ref[...] = h
    v_ref[...] = _pack_rows(v)
    lg_ref[...] = _dot_split(v, wr_ref[...]) + br_ref[...]
    nc_ref[...] = zbuf[pl.ds(r, hrc), :]
    np_ref[...] = pbuf[pl.ds(r, hrp), :]


def _mixer(x2, mod3, hc, hp, weights, row_tile, start_pos):
    n = x2.shape[0]
    n_mod, g, _ = mod3.shape
    hrc, hrp = hc.shape[0] // n_mod, hp.shape[0] // n_mod
    tiles_per_seq = n // n_mod // row_tile
    once = dict(pipeline_mode=pl.Buffered(1)) if n_mod == 1 else {}

    def full(a):
        nd = a.ndim
        return pl.BlockSpec(a.shape, lambda i: (0,) * nd)

    def seq_block(rows, **kw):
        return pl.BlockSpec((rows, C), lambda i: (i // tiles_per_seq, 0), **kw)

    def row_block(cols):
        return pl.BlockSpec((row_tile, cols), lambda i: (i, 0))

    return pl.pallas_call(
        functools.partial(_mixer_kernel, g, tiles_per_seq, start_pos),
        out_shape=[
            jax.ShapeDtypeStruct((n, D), _f32),
            jax.ShapeDtypeStruct((n, D // 2), jnp.int32),
            jax.ShapeDtypeStruct((n, LANES), _f32),
            jax.ShapeDtypeStruct(hc.shape, _f32),
            jax.ShapeDtypeStruct(hp.shape, _f32),
        ],
        grid=(n // row_tile,),
        in_specs=[row_block(D),
                  pl.BlockSpec((None, g, 6 * D), lambda i: (i // tiles_per_seq, 0, 0), **once),
                  seq_block(hrc, **once), seq_block(hrp, **once)] + [full(a) for a in weights],
        out_specs=[row_block(D), row_block(D // 2), row_block(LANES), seq_block(hrc), seq_block(hrp)],
        scratch_shapes=[pltpu.VMEM((hrc + row_tile, C), _f32), pltpu.VMEM((hrp + row_tile, C), _f32)],
        compiler_params=pltpu.CompilerParams(vmem_limit_bytes=VMEM_LIMIT),
        name="mixer",
    )(x2, mod3, hc, hp, *weights)


def _plan_kernel(n_blocks_pad, lgp_ref, lgs_ref, dest_ref, gate_ref, be_ref, nv_ref, nx_ref, idx_s, rank_s):
    t = ROW_TILE
    e_iota = lax.broadcasted_iota(jnp.int32, (E, t), 0)
    tri = (lax.broadcasted_iota(jnp.int32, (t, t), 0)
           < lax.broadcasted_iota(jnp.int32, (t, t), 1)).astype(_f32).astype(_bf16)
    zeros_rest = jnp.zeros((LANES - TOP_K, t), _f32)

    def tile_body(lg_ref, off, i, carry):
        lt = lg_ref[i].T[0:E, :]
        vals, idxs = [], []
        for _ in range(TOP_K):
            mx = jnp.max(lt, axis=0, keepdims=True)
            ix = jnp.min(jnp.where(lt == mx, e_iota, E), axis=0, keepdims=True)
            vals.append(mx)
            idxs.append(ix)
            lt = jnp.where(e_iota == ix, -jnp.inf, lt)
        ex = [jnp.exp(vk - vals[0]) for vk in vals]
        den = ex[0] + ex[1] + ex[2] + ex[3]
        gates = [ek / den for ek in ex]
        gate_ref[off + i] = jnp.concatenate(gates + [zeros_rest], axis=0).T

        ohs = [(e_iota == ix) for ix in idxs]
        oh = (ohs[0] | ohs[1] | ohs[2] | ohs[3]).astype(_f32)
        before = _dot(oh.astype(_bf16), tri) + carry
        ranks = [jnp.sum(jnp.where(o, before, 0.0), axis=0, keepdims=True) for o in ohs]
        idx_s[off + i] = jnp.concatenate(idxs + idxs, axis=0)
        rank_s[off + i] = jnp.concatenate(ranks + ranks, axis=0).astype(jnp.int32)
        return carry + jnp.sum(oh, axis=1, keepdims=True)

    n_p, n_s = lgp_ref.shape[0], lgs_ref.shape[0]
    counts = lax.fori_loop(0, n_p, functools.partial(tile_body, lgp_ref, 0), jnp.zeros((E, 1), _f32))
    counts = lax.fori_loop(0, n_s, functools.partial(tile_body, lgs_ref, n_p), counts)
    padded = jnp.ceil(counts / BM) * BM
    low = (lax.broadcasted_iota(jnp.int32, (E, E), 1)
           <= lax.broadcasted_iota(jnp.int32, (E, E), 0)).astype(_f32)
    pad_end = _dot_exact(low, jnp.broadcast_to(padded, (E, LANES)))[:, 0:1]
    pad_start = pad_end - padded

    def dest_body(i, c):
        ix = idx_s[i]
        rk = rank_s[i]
        rows = []
        for k in range(TOP_K):
            st = jnp.sum(jnp.where(e_iota == ix[k:k + 1], pad_start, 0.0), axis=0, keepdims=True)
            rows.append(st.astype(jnp.int32) + rk[k:k + 1])
        dest_ref[i] = jnp.concatenate(rows + rows, axis=0)
        return c

    lax.fori_loop(0, n_p + n_s, dest_body, 0)

    b_start = (lax.broadcasted_iota(jnp.int32, (1, n_blocks_pad), 1) * BM).astype(_f32)
    be = jnp.minimum(jnp.sum((pad_end <= b_start).astype(jnp.int32), axis=0, keepdims=True), E - 1)
    sel = lax.broadcasted_iota(jnp.int32, (E, n_blocks_pad), 0) == be
    used_end = jnp.sum(jnp.where(sel, pad_start + counts, 0.0), axis=0, keepdims=True)
    be_ref[...] = be
    nv_ref[...] = jnp.clip(used_end - b_start, 0.0, float(BM)).astype(jnp.int32)
    eb = lax.broadcasted_iota(jnp.int32, (E, n_blocks_pad), 0)
    later = jnp.min(jnp.where((eb > be) & (counts > 0.0), eb, E), axis=0, keepdims=True)
    nx_ref[...] = jnp.where(later == E, be, later)


def _plan(logits_p, logits_s, n_blocks):
    n_tiles = (logits_p.shape[0] + logits_s.shape[0]) // ROW_TILE
    n_blocks_pad = -(-n_blocks // LANES) * LANES
    return pl.pallas_call(
        functools.partial(_plan_kernel, n_blocks_pad),
        out_shape=[
            jax.ShapeDtypeStruct((n_tiles, 2 * TOP_K, ROW_TILE), jnp.int32),
            jax.ShapeDtypeStruct((n_tiles, ROW_TILE, LANES), _f32),
            jax.ShapeDtypeStruct((1, n_blocks_pad), jnp.int32),
            jax.ShapeDtypeStruct((1, n_blocks_pad), jnp.int32),
            jax.ShapeDtypeStruct((1, n_blocks_pad), jnp.int32),
        ],
        scratch_shapes=[pltpu.VMEM((n_tiles, 2 * TOP_K, ROW_TILE), jnp.int32),
                        pltpu.VMEM((n_tiles, 2 * TOP_K, ROW_TILE), jnp.int32)],
        compiler_params=pltpu.CompilerParams(vmem_limit_bytes=VMEM_LIMIT),
        name="plan",
    )(logits_p.reshape(-1, ROW_TILE, LANES), logits_s.reshape(-1, ROW_TILE, LANES))


def _row_copy(src, src_row, dst, dst_row, sem):
    return pltpu.make_async_copy(src.at[pl.ds(src_row, 1), :], dst.at[pl.ds(dst_row, 1), :], sem)


def _dispatch(v_p, v_s, dests, n_rows_out):
    n_p, n_s = v_p.shape[0], v_s.shape[0]
    width = v_p.shape[1]
    w = SC_WINDOW
    n_pw, n_windows = n_p // w, (n_p + n_s) // w
    mesh = plsc.VectorSubcoreMesh(core_axis_name="core", subcore_axis_name="subcore")
    n_workers = mesh.num_cores * mesh.num_subcores

    @functools.partial(
        pl.kernel, mesh=mesh, name="dispatch",
        out_type=jax.ShapeDtypeStruct((n_rows_out, width), jnp.int32),
        scratch_types=[pltpu.VMEM((w, width), jnp.int32)] + [pltpu.VMEM((w,), jnp.int32)] * TOP_K
        + [pltpu.SemaphoreType.DMA])
    def scatter_rows(vp_hbm, vs_hbm, d0_hbm, d1_hbm, d2_hbm, d3_hbm, o_hbm, rows, i0, i1, i2, i3, sem):
        worker = lax.axis_index("subcore") * mesh.num_cores + lax.axis_index("core")
        idx = (i0, i1, i2, i3)

        def scatter_window(c):
            t0 = pl.multiple_of(c * w, w)
            for d_hbm, iv in zip((d0_hbm, d1_hbm, d2_hbm, d3_hbm), idx):
                pltpu.sync_copy(d_hbm.at[pl.ds(t0, w)], iv)
            copies = [pltpu.async_copy(rows, o_hbm.at[iv], sem) for iv in idx]
            for cp in copies:
                cp.wait()

        for j in range(-(-n_windows // n_workers)):
            c = j * n_workers + worker

            @pl.when(c < n_pw)
            def _():
                pltpu.sync_copy(vp_hbm.at[pl.ds(pl.multiple_of(c * w, w), w)], rows)
                scatter_window(c)

            @pl.when((c >= n_pw) & (c < n_windows))
            def _():
                pltpu.sync_copy(vs_hbm.at[pl.ds(pl.multiple_of((c - n_pw) * w, w), w)], rows)
                scatter_window(c)

    return scatter_rows(v_p, v_s, *dests)


def _experts_kernel(be_ref, nv_ref, nx_ref, x_ref, wu_hbm, bu_ref, wd_hbm, bd_ref, y_ref,
                    wu_st, wd_st, wu_bf, wd_bf, sems):
    b = pl.program_id(0)
    e = be_ref[b]
    nvalid = nv_ref[b]

    def fetch(expert):
        return (pltpu.make_async_copy(wu_hbm.at[expert], wu_st, sems.at[0]),
                pltpu.make_async_copy(wd_hbm.at[expert], wd_st, sems.at[1]))

    @pl.when((nvalid > 0) & ((b == 0) | (e != be_ref[jnp.maximum(b - 1, 0)])))
    def _():
        @pl.when(b == 0)
        def _():
            for c in fetch(e):
                c.start()

        for c in fetch(e):
            c.wait()
        wu_bf[...] = wu_st[...].astype(_bf16)
        wd_bf[...] = wd_st[...].astype(_bf16)

        @pl.when(nx_ref[b] != e)
        def _():
            for c in fetch(nx_ref[b]):
                c.start()

    @pl.when(nvalid > 0)
    def _():
        rows = lax.broadcasted_iota(jnp.int32, (BM, 1), 0)
        x = _unpack_rows(jnp.where(rows < nvalid, x_ref[...], 0))
        hcat = _dot(x, wu_bf[...]) + bu_ref[...]
        glu = jnp.minimum(hcat[:, 0:F], SWIGLU_LIMIT)
        lin = jnp.clip(hcat[:, F:2 * F], -SWIGLU_LIMIT, SWIGLU_LIMIT)
        act = glu * jax.nn.sigmoid(SWIGLU_ALPHA * glu) * (lin + 1.0)
        y_ref[...] = _dot(act.astype(_bf16), wd_bf[...]) + bd_ref[...]

    @pl.when(nvalid == 0)
    def _():
        y_ref[...] = jnp.zeros_like(y_ref)


def _experts(block_e, nvalid, next_e, xpad, w_up, b_up, w_down, b_down):
    n_blocks = xpad.shape[0] // BM
    return pl.pallas_call(
        _experts_kernel,
        out_shape=jax.ShapeDtypeStruct((xpad.shape[0], D), _f32),
        grid_spec=pltpu.PrefetchScalarGridSpec(
            num_scalar_prefetch=3,
            grid=(n_blocks,),
            in_specs=[pl.BlockSpec((BM, D // 2), lambda b, be, nv, nx: (b, 0)),
                      pl.BlockSpec(memory_space=pl.ANY),
                      pl.BlockSpec((None, 1, 2 * F), lambda b, be, nv, nx: (be[b], 0, 0)),
                      pl.BlockSpec(memory_space=pl.ANY),
                      pl.BlockSpec((None, 1, D), lambda b, be, nv, nx: (be[b], 0, 0))],
            out_specs=pl.BlockSpec((BM, D), lambda b, be, nv, nx: (b, 0)),
            scratch_shapes=[pltpu.VMEM((D, 2 * F), _f32), pltpu.VMEM((F, D), _f32),
                            pltpu.VMEM((D, 2 * F), _bf16), pltpu.VMEM((F, D), _bf16),
                            pltpu.SemaphoreType.DMA((2,))],
        ),
        compiler_params=pltpu.CompilerParams(vmem_limit_bytes=VMEM_LIMIT),
        name="experts",
    )(block_e, nvalid, next_e, xpad, w_up, b_up, w_down, b_down)


def _combine_kernel(dest_ref, h_ref, mod_ref, gate_ref, g2_ref, b2_ref, ypad_ref, o_ref, ybuf, sem):
    t = COMBINE_TILE

    def issue(i, c):
        for k in range(TOP_K):
            _row_copy(ypad_ref, dest_ref[0, 0, k * t + i], ybuf.at[k], i, sem).start()
        return c

    lax.fori_loop(0, t, issue, 0)
    for k in range(TOP_K):
        pltpu.make_async_copy(ypad_ref.at[pl.ds(0, t), :], ybuf.at[k], sem).wait()

    gates = gate_ref[...]
    f = gates[:, 0:1] * ybuf[0]
    for k in range(1, TOP_K):
        f = f + gates[:, k:k + 1] * ybuf[k]
    gate2 = mod_ref[...][:, 5 * D:6 * D]
    pre = ALPHA * h_ref[...] + _per_seq(f, gate2, jnp.multiply)
    o_ref[...] = _layer_norm(pre, g2_ref[...], b2_ref[...])


def _combine(dest_tiles, h, mod3, gates, ln2_g, ln2_b, ypad, row0, rows_per_mod):
    t = COMBINE_TILE
    off = row0 // t
    g = mod3.shape[1]
    tiles_per_mod = rows_per_mod // t
    return pl.pallas_call(
        _combine_kernel,
        out_shape=jax.ShapeDtypeStruct(h.shape, _f32),
        grid=(h.shape[0] // t,),
        in_specs=[pl.BlockSpec((1, 1, TOP_K * t), lambda i: (i + off, 0, 0), memory_space=pltpu.SMEM),
                  pl.BlockSpec((t, D), lambda i: (i, 0)),
                  pl.BlockSpec((None, g, 6 * D), lambda i: (i // tiles_per_mod, 0, 0)),
                  pl.BlockSpec((t, LANES), lambda i: (i + off, 0)),
                  pl.BlockSpec((1, D), lambda i: (0, 0)),
                  pl.BlockSpec((1, D), lambda i: (0, 0)),
                  pl.BlockSpec(memory_space=pl.ANY)],
        out_specs=pl.BlockSpec((t, D), lambda i: (i, 0)),
        scratch_shapes=[pltpu.VMEM((TOP_K, t, D), _f32), pltpu.SemaphoreType.DMA],
        compiler_params=pltpu.CompilerParams(vmem_limit_bytes=VMEM_LIMIT),
        name="combine",
    )(dest_tiles, h, mod3, gates, ln2_g, ln2_b, ypad)


def _time_major(a):
    return a.transpose(1, 0, 2)


def kernel(x_prompt, x_sample, c_prompt, c_sample, state_conv, state_pool, w_ada, b_ada, w_in,
           conv_w, w_out_a, w_pool, ls_pool, w_out_b, w_o, ln1_g, ln1_b, w_router, b_router,
           w_up, b_up, w_down, b_down, ln2_g, ln2_b):
    n_seq_p, seq, _ = x_prompt.shape
    n_seq_s, dec_seq, _ = x_sample.shape
    n_p, n_s = n_seq_p * seq, n_seq_s * dec_seq
    n = n_p + n_s
    n_blocks = TOP_K * n // BM + E
    l = 0

    mod = _ada(jnp.concatenate([c_prompt, c_sample], axis=0), w_ada[l], b_ada[l][None])
    mod_p = mod[:n_seq_p][:, None, :]
    mod_s = mod[n_seq_p:][None]

    weights = (
        w_in[l].astype(_bf16), conv_w[l], w_out_a[l].astype(_bf16), w_pool[l].astype(_bf16),
        ls_pool[l][None], w_out_b[l].astype(_bf16), w_o[l].astype(_bf16), ln1_g[l][None], ln1_b[l][None],
        jnp.pad(w_router[l], ((0, 0), (0, LANES - E))), jnp.pad(b_router[l], (0, LANES - E))[None],
    )
    hc_p, hp_p = _hist_steps(CONV_HIST, 1), _hist_steps(POOL_HIST, 1)
    h_p, v_p, lg_p, nc_p, np_p = _mixer(
        x_prompt.reshape(n_p, D), mod_p, jnp.zeros((n_seq_p * hc_p, C), _f32),
        jnp.zeros((n_seq_p * hp_p, C), _f32), weights, ROW_TILE, 0)
    hc_s, hp_s = _hist_steps(CONV_HIST, n_seq_s), _hist_steps(POOL_HIST, n_seq_s)
    hist_c = jnp.pad(_time_major(state_conv[l]), ((hc_s - CONV_HIST, 0), (0, 0), (0, 0)))
    hist_p = jnp.pad(_time_major(state_pool[l]), ((hp_s - POOL_HIST, 0), (0, 0), (0, 0)))
    h_s, v_s, lg_s, nc_s, np_s = _mixer(
        _time_major(x_sample).reshape(n_s, D), mod_s, hist_c.reshape(hc_s * n_seq_s, C),
        hist_p.reshape(hp_s * n_seq_s, C), weights, SAMPLE_ROW_TILE, PAST_LEN)

    dest8, gates3, block_e, nvalid, next_e = _plan(lg_p, lg_s, n_blocks)
    block_e, nvalid, next_e = block_e[0, :n_blocks], nvalid[0, :n_blocks], next_e[0, :n_blocks]
    sub = ROW_TILE // COMBINE_TILE
    dest_tiles = (dest8[:, :TOP_K, :].reshape(-1, TOP_K, sub, COMBINE_TILE).transpose(0, 2, 1, 3)
                  .reshape(n // COMBINE_TILE, 1, TOP_K * COMBINE_TILE))
    gates = gates3.reshape(n, LANES)

    dests = [dest8[:, k, :].reshape(n) for k in range(TOP_K)]
    xpad = _dispatch(v_p, v_s, dests, n_blocks * BM)
    ypad = _experts(block_e, nvalid, next_e, xpad, w_up[l], b_up[l][:, None, :], w_down[l], b_down[l][:, None, :])

    g2, b2 = ln2_g[l][None], ln2_b[l][None]
    y_p = _combine(dest_tiles, h_p, mod_p, gates, g2, b2, ypad, 0, seq)
    y_s = _combine(dest_tiles, h_s, mod_s, gates, g2, b2, ypad, n_p, n_s)

    y_prompt = y_p.reshape(n_seq_p, seq, D)
    y_sample = _time_major(y_s.reshape(dec_seq, n_seq_s, D))
    new_conv_p = nc_p.reshape(n_seq_p, hc_p, C)[:, hc_p - CONV_HIST:][None]
    new_pool_p = np_p.reshape(n_seq_p, hp_p, C)[:, hp_p - POOL_HIST:][None]
    new_conv_s = _time_major(nc_s.reshape(hc_s, n_seq_s, C)[hc_s - CONV_HIST:])[None]
    new_pool_s = _time_major(np_s.reshape(hp_s, n_seq_s, C)[hp_s - POOL_HIST:])[None]
    return (y_prompt, y_sample, new_conv_p, new_pool_p, new_conv_s, new_pool_s)
```

```python
import functools

import jax
import jax.numpy as jnp
from jax import lax
from jax.experimental import pallas as pl
from jax.experimental.pallas import tpu as pltpu
from jax.experimental.pallas import tpu_sc as plsc

D = 1024
C = 512
N_GROUPS = 4
GROUP = C // N_GROUPS
CONV_HIST = 2
POOL_HIST = 15
E = 32
TOP_K = 4
F = 1024
SWIGLU_LIMIT = 7.0
SWIGLU_ALPHA = 1.702
LN_EPS = 1e-5
DEPTH = 1
ALPHA = (2 * DEPTH) ** 0.25
PAST_LEN = 16384

LANES = 128
SUBLANES = 8
ROW_TILE = 512
SAMPLE_ROW_TILE = 256
BM = 256
COMBINE_TILE = 256
SC_WINDOW = 128
VMEM_LIMIT = 56 * 1024 * 1024

_f32 = jnp.float32
_bf16 = jnp.bfloat16


def _dot(a, b):
    return jnp.dot(a, b, preferred_element_type=_f32)


def _dot_exact(a, b):
    return lax.dot_general(a, b, (((1,), (0,)), ((), ())),
                           precision=lax.Precision.HIGHEST, preferred_element_type=_f32)


def _dot_split(a, b):
    a_hi, b_hi = a.astype(_bf16), b.astype(_bf16)
    a_lo = (a - a_hi.astype(_f32)).astype(_bf16)
    b_lo = (b - b_hi.astype(_f32)).astype(_bf16)
    return _dot(a_hi, b_hi) + _dot(a_lo, b_hi) + _dot(a_hi, b_lo)


def _pack_rows(x):
    w = x.shape[1] // 2
    hi = lax.bitcast_convert_type(x[:, :w].astype(_bf16).astype(_f32), jnp.int32)
    lo = lax.bitcast_convert_type(x[:, w:].astype(_bf16).astype(_f32), jnp.int32)
    return hi | lax.shift_right_logical(lo, 16)


def _unpack_rows(p):
    hi = lax.bitcast_convert_type(p & jnp.int32(-65536), _f32)
    lo = lax.bitcast_convert_type(lax.shift_left(p, 16), _f32)
    return jnp.concatenate([hi, lo], axis=1).astype(_bf16)


def _per_seq(x, m, op):
    g = m.shape[0]
    if g == 1:
        return op(x, m)
    r, n = x.shape
    return op(x.reshape(r // g, g, n), m[None]).reshape(r, n)


def _layer_norm(x, g, b):
    mu = jnp.mean(x, axis=-1, keepdims=True)
    xc = x - mu
    var = jnp.mean(xc * xc, axis=-1, keepdims=True)
    return xc * lax.rsqrt(var + LN_EPS) * g + b


def _hist_steps(needed, g):
    return -(-needed * g // SUBLANES) * SUBLANES // g


def _ada_kernel(c_ref, w_ref, b_ref, o_ref):
    c = c_ref[...]
    o_ref[...] = _dot_exact(c * jax.nn.sigmoid(c), w_ref[...]) + b_ref[...]


def _ada(c, w_ada, b_ada):
    rows = c.shape[0]
    cols = w_ada.shape[1]
    bn = 1536
    return pl.pallas_call(
        _ada_kernel,
        out_shape=jax.ShapeDtypeStruct((rows, cols), _f32),
        grid=(cols // bn,),
        in_specs=[pl.BlockSpec((rows, D), lambda j: (0, 0)),
                  pl.BlockSpec((D, bn), lambda j: (0, j)),
                  pl.BlockSpec((1, bn), lambda j: (0, j))],
        out_specs=pl.BlockSpec((rows, bn), lambda j: (0, j)),
        compiler_params=pltpu.CompilerParams(vmem_limit_bytes=VMEM_LIMIT),
        name="ada",
    )(c, w_ada, b_ada)


def _mixer_kernel(g, tiles_per_seq, start_pos,
                  x_ref, mod_ref, hc_ref, hp_ref,
                  win_ref, cw_ref, woa_ref, wpool_ref, ls_ref, wob_ref, wo_ref, g1_ref, b1_ref,
                  wr_ref, br_ref,
                  h_ref, v_ref, lg_ref, nc_ref, np_ref, zbuf, pbuf):
    r = x_ref.shape[0]
    hrc = hc_ref.shape[0]
    hrp = hp_ref.shape[0]
    j = pl.program_id(0) % tiles_per_seq

    @pl.when(j == 0)
    def _():
        zbuf[pl.ds(0, hrc), :] = hc_ref[...]
        pbuf[pl.ds(0, hrp), :] = hp_ref[...]

    @pl.when(j != 0)
    def _():
        zt = zbuf[pl.ds(r, hrc), :]
        pt = pbuf[pl.ds(r, hrp), :]
        zbuf[pl.ds(0, hrc), :] = zt
        pbuf[pl.ds(0, hrp), :] = pt

    m = mod_ref[...]
    shift1, scale1, gate1 = m[:, 0:D], m[:, D:2 * D], m[:, 2 * D:3 * D]
    shift2, scale2 = m[:, 3 * D:4 * D], m[:, 4 * D:5 * D]

    x = x_ref[...]
    u = _per_seq(_per_seq(x, 1.0 + scale1, jnp.multiply), shift1, jnp.add).astype(_bf16)

    z = _dot(u, win_ref[:, C:2 * C]) * _dot(u, win_ref[:, 2 * C:3 * C])
    zbuf[pl.ds(hrc, r), :] = z
    cw = cw_ref[...]
    conv = (cw[0:1] * zbuf[pl.ds(hrc - 2 * g, r), :] + cw[1:2] * zbuf[pl.ds(hrc - g, r), :]
            + cw[2:3] * z)
    y_a = _dot((_dot(u, win_ref[:, 0:C]) * conv).astype(_bf16), woa_ref[...])

    xp = _dot(u, win_ref[:, 3 * C:4 * C])
    pbuf[pl.ds(hrp, r), :] = xp
    pos = start_pos + j * (r // g) + lax.broadcasted_iota(jnp.int32, (r, 1), 0) // g
    acc = xp
    yg = []
    for grp in range(N_GROUPS):
        lo = grp * GROUP
        wdw = 2 ** (grp + 1)
        for back in range(wdw // 2, wdw):
            sh = pbuf[pl.ds(hrp - back * g, r), lo:C]
            acc = jnp.concatenate([acc[:, 0:lo], acc[:, lo:C] + sh], axis=1) if lo else acc + sh
        cnt = jnp.minimum(wdw, pos + 1).astype(_f32)
        diff = acc[:, lo:lo + GROUP] / cnt - xp[:, lo:lo + GROUP]
        yg.append(_dot(diff.astype(_bf16), wpool_ref[grp]))
    y_b = _dot((jnp.concatenate(yg, axis=1) * ls_ref[...]).astype(_bf16), wob_ref[...])

    g_a = _dot(u, win_ref[:, 4 * C:4 * C + D])
    g_b = _dot(u, win_ref[:, 4 * C + D:4 * C + 2 * D])
    merged = jax.nn.sigmoid(g_a) * y_a + jax.nn.sigmoid(g_b) * y_b
    o = _dot(merged.astype(_bf16), wo_ref[...])
    h = _layer_norm(ALPHA * x + _per_seq(o, gate1, jnp.multiply), g1_ref[...], b1_ref[...])
    v = _per_seq(_per_seq(h, 1.0 + scale2, jnp.multiply), shift2, jnp.add)
    h_ref[...] = h
    v_ref[...] = _pack_rows(v)
    lg_ref[...] = _dot_split(v, wr_ref[...]) + br_ref[...]
    nc_ref[...] = zbuf[pl.ds(r, hrc), :]
    np_ref[...] = pbuf[pl.ds(r, hrp), :]


def _mixer(x2, mod3, hc, hp, weights, row_tile, start_pos):
    n = x2.shape[0]
    n_mod, g, _ = mod3.shape
    hrc, hrp = hc.shape[0] // n_mod, hp.shape[0] // n_mod
    tiles_per_seq = n // n_mod // row_tile
    once = dict(pipeline_mode=pl.Buffered(1)) if n_mod == 1 else {}

    def full(a):
        nd = a.ndim
        return pl.BlockSpec(a.shape, lambda i: (0,) * nd)

    def seq_block(rows, **kw):
        return pl.BlockSpec((rows, C), lambda i: (i // tiles_per_seq, 0), **kw)

    def row_block(cols):
        return pl.BlockSpec((row_tile, cols), lambda i: (i, 0))

    return pl.pallas_call(
        functools.partial(_mixer_kernel, g, tiles_per_seq, start_pos),
        out_shape=[
            jax.ShapeDtypeStruct((n, D), _f32),
            jax.ShapeDtypeStruct((n, D // 2), jnp.int32),
            jax.ShapeDtypeStruct((n, LANES), _f32),
            jax.ShapeDtypeStruct(hc.shape, _f32),
            jax.ShapeDtypeStruct(hp.shape, _f32),
        ],
        grid=(n // row_tile,),
        in_specs=[row_block(D),
                  pl.BlockSpec((None, g, 6 * D), lambda i: (i // tiles_per_seq, 0, 0), **once),
                  seq_block(hrc, **once), seq_block(hrp, **once)] + [full(a) for a in weights],
        out_specs=[row_block(D), row_block(D // 2), row_block(LANES), seq_block(hrc), seq_block(hrp)],
        scratch_shapes=[pltpu.VMEM((hrc + row_tile, C), _f32), pltpu.VMEM((hrp + row_tile, C), _f32)],
        compiler_params=pltpu.CompilerParams(vmem_limit_bytes=VMEM_LIMIT),
        name="mixer",
    )(x2, mod3, hc, hp, *weights)


def _plan_kernel(n_blocks_pad, lgp_ref, lgs_ref, dest_ref, gate_ref, be_ref, nv_ref, nx_ref, idx_s, rank_s):
    t = ROW_TILE
    e_iota = lax.broadcasted_iota(jnp.int32, (E, t), 0)
    tri = (lax.broadcasted_iota(jnp.int32, (t, t), 0)
           < lax.broadcasted_iota(jnp.int32, (t, t), 1)).astype(_f32).astype(_bf16)
    zeros_rest = jnp.zeros((LANES - TOP_K, t), _f32)

    def tile_body(lg_ref, off, i, carry):
        lt = lg_ref[i].T[0:E, :]
        vals, idxs = [], []
        for _ in range(TOP_K):
            mx = jnp.max(lt, axis=0, keepdims=True)
            ix = jnp.min(jnp.where(lt == mx, e_iota, E), axis=0, keepdims=True)
            vals.append(mx)
            idxs.append(ix)
            lt = jnp.where(e_iota == ix, -jnp.inf, lt)
        ex = [jnp.exp(vk - vals[0]) for vk in vals]
        den = ex[0] + ex[1] + ex[2] + ex[3]
        gates = [ek / den for ek in ex]
        gate_ref[off + i] = jnp.concatenate(gates + [zeros_rest], axis=0).T

        ohs = [(e_iota == ix) for ix in idxs]
        oh = (ohs[0] | ohs[1] | ohs[2] | ohs[3]).astype(_f32)
        before = _dot(oh.astype(_bf16), tri) + carry
        ranks = [jnp.sum(jnp.where(o, before, 0.0), axis=0, keepdims=True) for o in ohs]
        idx_s[off + i] = jnp.concatenate(idxs + idxs, axis=0)
        rank_s[off + i] = jnp.concatenate(ranks + ranks, axis=0).astype(jnp.int32)
        return carry + jnp.sum(oh, axis=1, keepdims=True)

    n_p, n_s = lgp_ref.shape[0], lgs_ref.shape[0]
    counts = lax.fori_loop(0, n_p, functools.partial(tile_body, lgp_ref, 0), jnp.zeros((E, 1), _f32))
    counts = lax.fori_loop(0, n_s, functools.partial(tile_body, lgs_ref, n_p), counts)
    padded = jnp.ceil(counts / BM) * BM
    low = (lax.broadcasted_iota(jnp.int32, (E, E), 1)
           <= lax.broadcasted_iota(jnp.int32, (E, E), 0)).astype(_f32)
    pad_end = _dot_exact(low, jnp.broadcast_to(padded, (E, LANES)))[:, 0:1]
    pad_start = pad_end - padded

    def dest_body(i, c):
        ix = idx_s[i]
        rk = rank_s[i]
        rows = []
        for k in range(TOP_K):
            st = jnp.sum(jnp.where(e_iota == ix[k:k + 1], pad_start, 0.0), axis=0, keepdims=True)
            rows.append(st.astype(jnp.int32) + rk[k:k + 1])
        dest_ref[i] = jnp.concatenate(rows + rows, axis=0)
        return c

    lax.fori_loop(0, n_p + n_s, dest_body, 0)

    b_start = (lax.broadcasted_iota(jnp.int32, (1, n_blocks_pad), 1) * BM).astype(_f32)
    be = jnp.minimum(jnp.sum((pad_end <= b_start).astype(jnp.int32), axis=0, keepdims=True), E - 1)
    sel = lax.broadcasted_iota(jnp.int32, (E, n_blocks_pad), 0) == be
    used_end = jnp.sum(jnp.where(sel, pad_start + counts, 0.0), axis=0, keepdims=True)
    be_ref[...] = be
    nv_ref[...] = jnp.clip(used_end - b_start, 0.0, float(BM)).astype(jnp.int32)
    eb = lax.broadcasted_iota(jnp.int32, (E, n_blocks_pad), 0)
    later = jnp.min(jnp.where((eb > be) & (counts > 0.0), eb, E), axis=0, keepdims=True)
    nx_ref[...] = jnp.where(later == E, be, later)


def _plan(logits_p, logits_s, n_blocks):
    n_tiles = (logits_p.shape[0] + logits_s.shape[0]) // ROW_TILE
    n_blocks_pad = -(-n_blocks // LANES) * LANES
    return pl.pallas_call(
        functools.partial(_plan_kernel, n_blocks_pad),
        out_shape=[
            jax.ShapeDtypeStruct((n_tiles, 2 * TOP_K, ROW_TILE), jnp.int32),
            jax.ShapeDtypeStruct((n_tiles, ROW_TILE, LANES), _f32),
            jax.ShapeDtypeStruct((1, n_blocks_pad), jnp.int32),
            jax.ShapeDtypeStruct((1, n_blocks_pad), jnp.int32),
            jax.ShapeDtypeStruct((1, n_blocks_pad), jnp.int32),
        ],
        scratch_shapes=[pltpu.VMEM((n_tiles, 2 * TOP_K, ROW_TILE), jnp.int32),
                        pltpu.VMEM((n_tiles, 2 * TOP_K, ROW_TILE), jnp.int32)],
        compiler_params=pltpu.CompilerParams(vmem_limit_bytes=VMEM_LIMIT),
        name="plan",
    )(logits_p.reshape(-1, ROW_TILE, LANES), logits_s.reshape(-1, ROW_TILE, LANES))


def _dispatch(v_p, v_s, dests, n_rows_out):
    n_p, n_s = v_p.shape[0], v_s.shape[0]
    width = v_p.shape[1]
    w = SC_WINDOW
    n_pw, n_windows = n_p // w, (n_p + n_s) // w
    mesh = plsc.VectorSubcoreMesh(core_axis_name="core", subcore_axis_name="subcore")
    n_workers = mesh.num_cores * mesh.num_subcores

    @functools.partial(
        pl.kernel, mesh=mesh, name="dispatch",
        out_type=jax.ShapeDtypeStruct((n_rows_out, width), jnp.int32),
        scratch_types=[pltpu.VMEM((w, width), jnp.int32)] + [pltpu.VMEM((w,), jnp.int32)] * TOP_K
        + [pltpu.SemaphoreType.DMA])
    def scatter_rows(vp_hbm, vs_hbm, d0_hbm, d1_hbm, d2_hbm, d3_hbm, o_hbm, rows, i0, i1, i2, i3, sem):
        worker = lax.axis_index("subcore") * mesh.num_cores + lax.axis_index("core")
        idx = (i0, i1, i2, i3)

        def scatter_window(c):
            t0 = pl.multiple_of(c * w, w)
            for d_hbm, iv in zip((d0_hbm, d1_hbm, d2_hbm, d3_hbm), idx):
                pltpu.sync_copy(d_hbm.at[pl.ds(t0, w)], iv)
            copies = [pltpu.async_copy(rows, o_hbm.at[iv], sem) for iv in idx]
            for cp in copies:
                cp.wait()

        for j in range(-(-n_windows // n_workers)):
            c = j * n_workers + worker

            @pl.when(c < n_pw)
            def _():
                pltpu.sync_copy(vp_hbm.at[pl.ds(pl.multiple_of(c * w, w), w)], rows)
                scatter_window(c)

            @pl.when((c >= n_pw) & (c < n_windows))
            def _():
                pltpu.sync_copy(vs_hbm.at[pl.ds(pl.multiple_of((c - n_pw) * w, w), w)], rows)
                scatter_window(c)

    return scatter_rows(v_p, v_s, *dests)


def _experts_kernel(be_ref, nv_ref, nx_ref, x_ref, wu_hbm, bu_ref, wd_hbm, bd_ref, y_ref,
                    wu_st, wd_st, wu_bf, wd_bf, sems):
    b = pl.program_id(0)
    e = be_ref[b]
    nvalid = nv_ref[b]

    def fetch(expert):
        return (pltpu.make_async_copy(wu_hbm.at[expert], wu_st, sems.at[0]),
                pltpu.make_async_copy(wd_hbm.at[expert], wd_st, sems.at[1]))

    @pl.when((nvalid > 0) & ((b == 0) | (e != be_ref[jnp.maximum(b - 1, 0)])))
    def _():
        @pl.when(b == 0)
        def _():
            for c in fetch(e):
                c.start()

        for c in fetch(e):
            c.wait()
        wu_bf[...] = wu_st[...].astype(_bf16)
        wd_bf[...] = wd_st[...].astype(_bf16)

        @pl.when(nx_ref[b] != e)
        def _():
            for c in fetch(nx_ref[b]):
                c.start()

    @pl.when(nvalid > 0)
    def _():
        rows = lax.broadcasted_iota(jnp.int32, (BM, 1), 0)
        x = _unpack_rows(jnp.where(rows < nvalid, x_ref[...], 0))
        hcat = _dot(x, wu_bf[...]) + bu_ref[...]
        glu = jnp.minimum(hcat[:, 0:F], SWIGLU_LIMIT)
        lin = jnp.clip(hcat[:, F:2 * F], -SWIGLU_LIMIT, SWIGLU_LIMIT)
        act = glu * jax.nn.sigmoid(SWIGLU_ALPHA * glu) * (lin + 1.0)
        y_ref[...] = _pack_rows(_dot(act.astype(_bf16), wd_bf[...]) + bd_ref[...])

    @pl.when(nvalid == 0)
    def _():
        y_ref[...] = jnp.zeros_like(y_ref)


def _experts(block_e, nvalid, next_e, xpad, w_up, b_up, w_down, b_down):
    n_blocks = xpad.shape[0] // BM
    return pl.pallas_call(
        _experts_kernel,
        out_shape=jax.ShapeDtypeStruct(xpad.shape, jnp.int32),
        grid_spec=pltpu.PrefetchScalarGridSpec(
            num_scalar_prefetch=3,
            grid=(n_blocks,),
            in_specs=[pl.BlockSpec((BM, D // 2), lambda b, be, nv, nx: (b, 0)),
                      pl.BlockSpec(memory_space=pl.ANY),
                      pl.BlockSpec((None, 1, 2 * F), lambda b, be, nv, nx: (be[b], 0, 0)),
                      pl.BlockSpec(memory_space=pl.ANY),
                      pl.BlockSpec((None, 1, D), lambda b, be, nv, nx: (be[b], 0, 0))],
            out_specs=pl.BlockSpec((BM, D // 2), lambda b, be, nv, nx: (b, 0)),
            scratch_shapes=[pltpu.VMEM((D, 2 * F), _f32), pltpu.VMEM((F, D), _f32),
                            pltpu.VMEM((D, 2 * F), _bf16), pltpu.VMEM((F, D), _bf16),
                            pltpu.SemaphoreType.DMA((2,))],
        ),
        compiler_params=pltpu.CompilerParams(vmem_limit_bytes=VMEM_LIMIT),
        name="experts",
    )(block_e, nvalid, next_e, xpad, w_up, b_up, w_down, b_down)


def _gather_rows(ypad, dest_all):
    n_out = dest_all.shape[0]
    width = ypad.shape[1]
    w = SC_WINDOW
    mesh = plsc.VectorSubcoreMesh(core_axis_name="core", subcore_axis_name="subcore")
    n_workers = mesh.num_cores * mesh.num_subcores
    n_windows = n_out // w

    @functools.partial(
        pl.kernel, mesh=mesh, name="gather_rows",
        out_type=jax.ShapeDtypeStruct((n_out, width), jnp.int32),
        scratch_types=[pltpu.VMEM((w, width), jnp.int32), pltpu.VMEM((w,), jnp.int32), pltpu.SemaphoreType.DMA])
    def gather_rows(y_hbm, d_hbm, o_hbm, rows, iv, sem):
        worker = lax.axis_index("subcore") * mesh.num_cores + lax.axis_index("core")

        @pl.loop(0, n_windows // n_workers)
        def _(j):
            r0 = pl.multiple_of((j * n_workers + worker) * w, w)
            pltpu.sync_copy(d_hbm.at[pl.ds(r0, w)], iv)
            pltpu.async_copy(y_hbm.at[iv], rows, sem).wait()
            pltpu.sync_copy(rows, o_hbm.at[pl.ds(r0, w)])

    assert n_windows % n_workers == 0
    return gather_rows(ypad, dest_all)


def _combine_kernel(h_ref, mod_ref, gate_ref, g2_ref, b2_ref, y_ref, o_ref):
    gates = gate_ref[...]
    half = D // 2
    f_hi = jnp.zeros((h_ref.shape[0], half), _f32)
    f_lo = jnp.zeros((h_ref.shape[0], half), _f32)
    for k in range(TOP_K):
        p = y_ref[k]
        gk = gates[:, k:k + 1]
        f_hi = f_hi + gk * lax.bitcast_convert_type(p & jnp.int32(-65536), _f32)
        f_lo = f_lo + gk * lax.bitcast_convert_type(lax.shift_left(p, 16), _f32)
    f = jnp.concatenate([f_hi, f_lo], axis=1)
    gate2 = mod_ref[...][:, 5 * D:6 * D]
    pre = ALPHA * h_ref[...] + _per_seq(f, gate2, jnp.multiply)
    o_ref[...] = _layer_norm(pre, g2_ref[...], b2_ref[...])


def _combine(h, mod3, gates, ln2_g, ln2_b, y4, row0, rows_per_mod):
    t = COMBINE_TILE
    off = row0 // t
    g = mod3.shape[1]
    tiles_per_mod = rows_per_mod // t
    return pl.pallas_call(
        _combine_kernel,
        out_shape=jax.ShapeDtypeStruct(h.shape, _f32),
        grid=(h.shape[0] // t,),
        in_specs=[pl.BlockSpec((t, D), lambda i: (i, 0)),
                  pl.BlockSpec((None, g, 6 * D), lambda i: (i // tiles_per_mod, 0, 0)),
                  pl.BlockSpec((t, LANES), lambda i: (i + off, 0)),
                  pl.BlockSpec((1, D), lambda i: (0, 0)),
                  pl.BlockSpec((1, D), lambda i: (0, 0)),
                  pl.BlockSpec((TOP_K, t, D // 2), lambda i: (0, i + off, 0))],
        out_specs=pl.BlockSpec((t, D), lambda i: (i, 0)),
        compiler_params=pltpu.CompilerParams(vmem_limit_bytes=VMEM_LIMIT),
        name="combine",
    )(h, mod3, gates, ln2_g, ln2_b, y4)


def _time_major(a):
    return a.transpose(1, 0, 2)


def kernel(x_prompt, x_sample, c_prompt, c_sample, state_conv, state_pool, w_ada, b_ada, w_in,
           conv_w, w_out_a, w_pool, ls_pool, w_out_b, w_o, ln1_g, ln1_b, w_router, b_router,
           w_up, b_up, w_down, b_down, ln2_g, ln2_b):
    n_seq_p, seq, _ = x_prompt.shape
    n_seq_s, dec_seq, _ = x_sample.shape
    n_p, n_s = n_seq_p * seq, n_seq_s * dec_seq
    n = n_p + n_s
    n_blocks = TOP_K * n // BM + E
    l = 0

    mod = _ada(jnp.concatenate([c_prompt, c_sample], axis=0), w_ada[l], b_ada[l][None])
    mod_p = mod[:n_seq_p][:, None, :]
    mod_s = mod[n_seq_p:][None]

    weights = (
        w_in[l].astype(_bf16), conv_w[l], w_out_a[l].astype(_bf16), w_pool[l].astype(_bf16),
        ls_pool[l][None], w_out_b[l].astype(_bf16), w_o[l].astype(_bf16), ln1_g[l][None], ln1_b[l][None],
        jnp.pad(w_router[l], ((0, 0), (0, LANES - E))), jnp.pad(b_router[l], (0, LANES - E))[None],
    )
    hc_p, hp_p = _hist_steps(CONV_HIST, 1), _hist_steps(POOL_HIST, 1)
    h_p, v_p, lg_p, nc_p, np_p = _mixer(
        x_prompt.reshape(n_p, D), mod_p, jnp.zeros((n_seq_p * hc_p, C), _f32),
        jnp.zeros((n_seq_p * hp_p, C), _f32), weights, ROW_TILE, 0)
    hc_s, hp_s = _hist_steps(CONV_HIST, n_seq_s), _hist_steps(POOL_HIST, n_seq_s)
    hist_c = jnp.pad(_time_major(state_conv[l]), ((hc_s - CONV_HIST, 0), (0, 0), (0, 0)))
    hist_p = jnp.pad(_time_major(state_pool[l]), ((hp_s - POOL_HIST, 0), (0, 0), (0, 0)))
    h_s, v_s, lg_s, nc_s, np_s = _mixer(
        _time_major(x_sample).reshape(n_s, D), mod_s, hist_c.reshape(hc_s * n_seq_s, C),
        hist_p.reshape(hp_s * n_seq_s, C), weights, SAMPLE_ROW_TILE, PAST_LEN)

    dest8, gates3, block_e, nvalid, next_e = _plan(lg_p, lg_s, n_blocks)
    block_e, nvalid, next_e = block_e[0, :n_blocks], nvalid[0, :n_blocks], next_e[0, :n_blocks]
    gates = gates3.reshape(n, LANES)

    dests = [dest8[:, k, :].reshape(n) for k in range(TOP_K)]
    xpad = _dispatch(v_p, v_s, dests, n_blocks * BM)
    ypad = _experts(block_e, nvalid, next_e, xpad, w_up[l], b_up[l][:, None, :], w_down[l], b_down[l][:, None, :])
    y4 = _gather_rows(ypad, jnp.concatenate(dests)).reshape(TOP_K, n, D // 2)

    g2, b2 = ln2_g[l][None], ln2_b[l][None]
    y_p = _combine(h_p, mod_p, gates, g2, b2, y4, 0, seq)
    y_s = _combine(h_s, mod_s, gates, g2, b2, y4, n_p, n_s)

    y_prompt = y_p.reshape(n_seq_p, seq, D)
    y_sample = _time_major(y_s.reshape(dec_seq, n_seq_s, D))
    new_conv_p = nc_p.reshape(n_seq_p, hc_p, C)[:, hc_p - CONV_HIST:][None]
    new_pool_p = np_p.reshape(n_seq_p, hp_p, C)[:, hp_p - POOL_HIST:][None]
    new_conv_s = _time_major(nc_s.reshape(hc_s, n_seq_s, C)[hc_s - CONV_HIST:])[None]
    new_pool_s = _time_major(np_s.reshape(hp_s, n_seq_s, C)[hp_s - POOL_HIST:])[None]
    return (y_prompt, y_sample, new_conv_p, new_pool_p, new_conv_s, new_pool_s)
```

```python
import functools

import jax
import jax.numpy as jnp
from jax import lax
from jax.experimental import pallas as pl
from jax.experimental.pallas import tpu as pltpu
from jax.experimental.pallas import tpu_sc as plsc

D = 1024
C = 512
N_GROUPS = 4
GROUP = C // N_GROUPS
CONV_HIST = 2
POOL_HIST = 15
E = 32
TOP_K = 4
F = 1024
SWIGLU_LIMIT = 7.0
SWIGLU_ALPHA = 1.702
LN_EPS = 1e-5
DEPTH = 1
ALPHA = (2 * DEPTH) ** 0.25
PAST_LEN = 16384

LANES = 128
SUBLANES = 8
ROW_TILE = 512
SAMPLE_ROW_TILE = 256
BM = 256
COMBINE_TILE = 256
SC_WINDOW = 128
VMEM_LIMIT = 56 * 1024 * 1024

_f32 = jnp.float32
_bf16 = jnp.bfloat16


def _dot(a, b):
    return jnp.dot(a, b, preferred_element_type=_f32)


def _dot_exact(a, b):
    return lax.dot_general(a, b, (((1,), (0,)), ((), ())),
                           precision=lax.Precision.HIGHEST, preferred_element_type=_f32)


def _dot_split(a, b):
    a_hi, b_hi = a.astype(_bf16), b.astype(_bf16)
    a_lo = (a - a_hi.astype(_f32)).astype(_bf16)
    b_lo = (b - b_hi.astype(_f32)).astype(_bf16)
    return _dot(a_hi, b_hi) + _dot(a_lo, b_hi) + _dot(a_hi, b_lo)


def _pack_rows(x):
    w = x.shape[1] // 2
    hi = lax.bitcast_convert_type(x[:, :w].astype(_bf16).astype(_f32), jnp.int32)
    lo = lax.bitcast_convert_type(x[:, w:].astype(_bf16).astype(_f32), jnp.int32)
    return hi | lax.shift_right_logical(lo, 16)


def _unpack_rows(p):
    hi = lax.bitcast_convert_type(p & jnp.int32(-65536), _f32)
    lo = lax.bitcast_convert_type(lax.shift_left(p, 16), _f32)
    return jnp.concatenate([hi, lo], axis=1).astype(_bf16)


def _per_seq(x, m, op):
    g = m.shape[0]
    if g == 1:
        return op(x, m)
    r, n = x.shape
    return op(x.reshape(r // g, g, n), m[None]).reshape(r, n)


def _layer_norm(x, g, b):
    mu = jnp.mean(x, axis=-1, keepdims=True)
    xc = x - mu
    var = jnp.mean(xc * xc, axis=-1, keepdims=True)
    return xc * lax.rsqrt(var + LN_EPS) * g + b


def _hist_steps(needed, g):
    return -(-needed * g // SUBLANES) * SUBLANES // g


def _ada_kernel(c_ref, w_ref, b_ref, o_ref):
    c = c_ref[...]
    o_ref[...] = _dot_exact(c * jax.nn.sigmoid(c), w_ref[...]) + b_ref[...]


def _ada(c, w_ada, b_ada):
    rows = c.shape[0]
    cols = w_ada.shape[1]
    bn = 1536
    return pl.pallas_call(
        _ada_kernel,
        out_shape=jax.ShapeDtypeStruct((rows, cols), _f32),
        grid=(cols // bn,),
        in_specs=[pl.BlockSpec((rows, D), lambda j: (0, 0)),
                  pl.BlockSpec((D, bn), lambda j: (0, j)),
                  pl.BlockSpec((1, bn), lambda j: (0, j))],
        out_specs=pl.BlockSpec((rows, bn), lambda j: (0, j)),
        compiler_params=pltpu.CompilerParams(vmem_limit_bytes=VMEM_LIMIT),
        name="ada",
    )(c, w_ada, b_ada)


def _mixer_kernel(g, tiles_per_seq, start_pos,
                  x_ref, mod_ref, hc_ref, hp_ref,
                  win_ref, cw_ref, woa_ref, wpool_ref, ls_ref, wob_ref, wo_ref, g1_ref, b1_ref,
                  wr_ref, br_ref,
                  h_ref, v_ref, lg_ref, nc_ref, np_ref, zbuf, pbuf):
    r = x_ref.shape[0]
    hrc = hc_ref.shape[0]
    hrp = hp_ref.shape[0]
    j = pl.program_id(0) % tiles_per_seq

    @pl.when(j == 0)
    def _():
        zbuf[pl.ds(0, hrc), :] = hc_ref[...]
        pbuf[pl.ds(0, hrp), :] = hp_ref[...]

    @pl.when(j != 0)
    def _():
        zt = zbuf[pl.ds(r, hrc), :]
        pt = pbuf[pl.ds(r, hrp), :]
        zbuf[pl.ds(0, hrc), :] = zt
        pbuf[pl.ds(0, hrp), :] = pt

    m = mod_ref[...]
    shift1, scale1, gate1 = m[:, 0:D], m[:, D:2 * D], m[:, 2 * D:3 * D]
    shift2, scale2 = m[:, 3 * D:4 * D], m[:, 4 * D:5 * D]

    x = x_ref[...]
    u = _per_seq(_per_seq(x, 1.0 + scale1, jnp.multiply), shift1, jnp.add).astype(_bf16)

    z = _dot(u, win_ref[:, C:2 * C]) * _dot(u, win_ref[:, 2 * C:3 * C])
    zbuf[pl.ds(hrc, r), :] = z
    cw = cw_ref[...]
    conv = (cw[0:1] * zbuf[pl.ds(hrc - 2 * g, r), :] + cw[1:2] * zbuf[pl.ds(hrc - g, r), :]
            + cw[2:3] * z)
    y_a = _dot((_dot(u, win_ref[:, 0:C]) * conv).astype(_bf16), woa_ref[...])

    xp = _dot(u, win_ref[:, 3 * C:4 * C])
    pbuf[pl.ds(hrp, r), :] = xp
    pos = start_pos + j * (r // g) + lax.broadcasted_iota(jnp.int32, (r, 1), 0) // g
    acc = xp
    yg = []
    for grp in range(N_GROUPS):
        lo = grp * GROUP
        wdw = 2 ** (grp + 1)
        for back in range(wdw // 2, wdw):
            sh = pbuf[pl.ds(hrp - back * g, r), lo:C]
            acc = jnp.concatenate([acc[:, 0:lo], acc[:, lo:C] + sh], axis=1) if lo else acc + sh
        cnt = jnp.minimum(wdw, pos + 1).astype(_f32)
        diff = acc[:, lo:lo + GROUP] / cnt - xp[:, lo:lo + GROUP]
        yg.append(_dot(diff.astype(_bf16), wpool_ref[grp]))
    y_b = _dot((jnp.concatenate(yg, axis=1) * ls_ref[...]).astype(_bf16), wob_ref[...])

    g_a = _dot(u, win_ref[:, 4 * C:4 * C + D])
    g_b = _dot(u, win_ref[:, 4 * C + D:4 * C + 2 * D])
    merged = jax.nn.sigmoid(g_a) * y_a + jax.nn.sigmoid(g_b) * y_b
    o = _dot(merged.astype(_bf16), wo_ref[...])
    h = _layer_norm(ALPHA * x + _per_seq(o, gate1, jnp.multiply), g1_ref[...], b1_ref[...])
    v = _per_seq(_per_seq(h, 1.0 + scale2, jnp.multiply), shift2, jnp.add)
    h_ref[...] = h
    v_ref[...] = _pack_rows(v)
    lg_ref[...] = _dot_split(v, wr_ref[...]) + br_ref[...]
    nc_ref[...] = zbuf[pl.ds(r, hrc), :]
    np_ref[...] = pbuf[pl.ds(r, hrp), :]


def _mixer(x2, mod3, hc, hp, weights, row_tile, start_pos):
    n = x2.shape[0]
    n_mod, g, _ = mod3.shape
    hrc, hrp = hc.shape[0] // n_mod, hp.shape[0] // n_mod
    tiles_per_seq = n // n_mod // row_tile
    once = dict(pipeline_mode=pl.Buffered(1)) if n_mod == 1 else {}

    def full(a):
        nd = a.ndim
        return pl.BlockSpec(a.shape, lambda i: (0,) * nd)

    def seq_block(rows, **kw):
        return pl.BlockSpec((rows, C), lambda i: (i // tiles_per_seq, 0), **kw)

    def row_block(cols):
        return pl.BlockSpec((row_tile, cols), lambda i: (i, 0))

    return pl.pallas_call(
        functools.partial(_mixer_kernel, g, tiles_per_seq, start_pos),
        out_shape=[
            jax.ShapeDtypeStruct((n, D), _f32),
            jax.ShapeDtypeStruct((n, D // 2), jnp.int32),
            jax.ShapeDtypeStruct((n, LANES), _f32),
            jax.ShapeDtypeStruct(hc.shape, _f32),
            jax.ShapeDtypeStruct(hp.shape, _f32),
        ],
        grid=(n // row_tile,),
        in_specs=[row_block(D),
                  pl.BlockSpec((None, g, 6 * D), lambda i: (i // tiles_per_seq, 0, 0), **once),
                  seq_block(hrc, **once), seq_block(hrp, **once)] + [full(a) for a in weights],
        out_specs=[row_block(D), row_block(D // 2), row_block(LANES), seq_block(hrc), seq_block(hrp)],
        scratch_shapes=[pltpu.VMEM((hrc + row_tile, C), _f32), pltpu.VMEM((hrp + row_tile, C), _f32)],
        compiler_params=pltpu.CompilerParams(vmem_limit_bytes=VMEM_LIMIT),
        name="mixer",
    )(x2, mod3, hc, hp, *weights)


def _plan_kernel(lgp_ref, lgs_ref, dest_ref, gate_ref, meta_ref, idx_s, rank_s):
    t = ROW_TILE
    e_iota = lax.broadcasted_iota(jnp.int32, (E, t), 0)
    tri = (lax.broadcasted_iota(jnp.int32, (t, t), 0)
           < lax.broadcasted_iota(jnp.int32, (t, t), 1)).astype(_f32).astype(_bf16)
    zeros_rest = jnp.zeros((LANES - TOP_K, t), _f32)

    def tile_body(lg_ref, off, i, carry):
        lt = lg_ref[i].T[0:E, :]
        vals, idxs = [], []
        for _ in range(TOP_K):
            mx = jnp.max(lt, axis=0, keepdims=True)
            ix = jnp.min(jnp.where(lt == mx, e_iota, E), axis=0, keepdims=True)
            vals.append(mx)
            idxs.append(ix)
            lt = jnp.where(e_iota == ix, -jnp.inf, lt)
        ex = [jnp.exp(vk - vals[0]) for vk in vals]
        den = ex[0] + ex[1] + ex[2] + ex[3]
        gates = [ek / den for ek in ex]
        gate_ref[off + i] = jnp.concatenate(gates + [zeros_rest], axis=0).T

        ohs = [(e_iota == ix) for ix in idxs]
        oh = (ohs[0] | ohs[1] | ohs[2] | ohs[3]).astype(_f32)
        before = _dot(oh.astype(_bf16), tri) + carry
        ranks = [jnp.sum(jnp.where(o, before, 0.0), axis=0, keepdims=True) for o in ohs]
        idx_s[off + i] = jnp.concatenate(idxs + idxs, axis=0)
        rank_s[off + i] = jnp.concatenate(ranks + ranks, axis=0).astype(jnp.int32)
        return carry + jnp.sum(oh, axis=1, keepdims=True)

    n_p, n_s = lgp_ref.shape[0], lgs_ref.shape[0]
    counts = lax.fori_loop(0, n_p, functools.partial(tile_body, lgp_ref, 0), jnp.zeros((E, 1), _f32))
    counts = lax.fori_loop(0, n_s, functools.partial(tile_body, lgs_ref, n_p), counts)
    padded = jnp.ceil(counts / BM) * BM
    low = (lax.broadcasted_iota(jnp.int32, (E, E), 1)
           <= lax.broadcasted_iota(jnp.int32, (E, E), 0)).astype(_f32)
    pad_end = _dot_exact(low, jnp.broadcast_to(padded, (E, LANES)))[:, 0:1]
    pad_start = pad_end - padded

    def dest_body(i, c):
        ix = idx_s[i]
        rk = rank_s[i]
        rows = []
        for k in range(TOP_K):
            st = jnp.sum(jnp.where(e_iota == ix[k:k + 1], pad_start, 0.0), axis=0, keepdims=True)
            rows.append(st.astype(jnp.int32) + rk[k:k + 1])
        dest_ref[i] = jnp.concatenate(rows + rows, axis=0)
        return c

    lax.fori_loop(0, n_p + n_s, dest_body, 0)

    on_lane = (lax.broadcasted_iota(jnp.int32, (E, LANES), 0)
               == lax.broadcasted_iota(jnp.int32, (E, LANES), 1))

    def to_lanes(col):
        return jnp.sum(jnp.where(on_lane, col, 0.0), axis=0, keepdims=True).astype(jnp.int32)

    meta_ref[...] = jnp.concatenate(
        [to_lanes(counts), to_lanes(pad_start / BM), to_lanes(padded / BM),
         jnp.zeros((SUBLANES - 3, LANES), jnp.int32)], axis=0)


def _plan(logits_p, logits_s):
    n_tiles = (logits_p.shape[0] + logits_s.shape[0]) // ROW_TILE
    return pl.pallas_call(
        _plan_kernel,
        out_shape=[
            jax.ShapeDtypeStruct((n_tiles, 2 * TOP_K, ROW_TILE), jnp.int32),
            jax.ShapeDtypeStruct((n_tiles, ROW_TILE, LANES), _f32),
            jax.ShapeDtypeStruct((SUBLANES, LANES), jnp.int32),
        ],
        scratch_shapes=[pltpu.VMEM((n_tiles, 2 * TOP_K, ROW_TILE), jnp.int32),
                        pltpu.VMEM((n_tiles, 2 * TOP_K, ROW_TILE), jnp.int32)],
        compiler_params=pltpu.CompilerParams(vmem_limit_bytes=VMEM_LIMIT),
        name="plan",
    )(logits_p.reshape(-1, ROW_TILE, LANES), logits_s.reshape(-1, ROW_TILE, LANES))


def _dispatch(v_p, v_s, dests, n_rows_out):
    n_p, n_s = v_p.shape[0], v_s.shape[0]
    width = v_p.shape[1]
    w = SC_WINDOW
    n_pw, n_windows = n_p // w, (n_p + n_s) // w
    mesh = plsc.VectorSubcoreMesh(core_axis_name="core", subcore_axis_name="subcore")
    n_workers = mesh.num_cores * mesh.num_subcores

    @functools.partial(
        pl.kernel, mesh=mesh, name="dispatch",
        out_type=jax.ShapeDtypeStruct((n_rows_out, width), jnp.int32),
        scratch_types=[pltpu.VMEM((w, width), jnp.int32)] + [pltpu.VMEM((w,), jnp.int32)] * TOP_K
        + [pltpu.SemaphoreType.DMA])
    def scatter_rows(vp_hbm, vs_hbm, d0_hbm, d1_hbm, d2_hbm, d3_hbm, o_hbm, rows, i0, i1, i2, i3, sem):
        worker = lax.axis_index("subcore") * mesh.num_cores + lax.axis_index("core")
        idx = (i0, i1, i2, i3)

        def scatter_window(c):
            t0 = pl.multiple_of(c * w, w)
            for d_hbm, iv in zip((d0_hbm, d1_hbm, d2_hbm, d3_hbm), idx):
                pltpu.sync_copy(d_hbm.at[pl.ds(t0, w)], iv)
            copies = [pltpu.async_copy(rows, o_hbm.at[iv], sem) for iv in idx]
            for cp in copies:
                cp.wait()

        for j in range(-(-n_windows // n_workers)):
            c = j * n_workers + worker

            @pl.when(c < n_pw)
            def _():
                pltpu.sync_copy(vp_hbm.at[pl.ds(pl.multiple_of(c * w, w), w)], rows)
                scatter_window(c)

            @pl.when((c >= n_pw) & (c < n_windows))
            def _():
                pltpu.sync_copy(vs_hbm.at[pl.ds(pl.multiple_of((c - n_pw) * w, w), w)], rows)
                scatter_window(c)

    return scatter_rows(v_p, v_s, *dests)


def _experts_kernel(cnt_ref, first_ref, nblk_ref, x_hbm, wu_hbm, bu_ref, wd_hbm, bd_ref, y_hbm,
                    xbuf, ybuf, wu_st, wd_st, wu_bf, wd_bf, xsem, ysem, wsem):
    total = first_ref[E - 1] + nblk_ref[E - 1]

    def w_fetch(e):
        return (pltpu.make_async_copy(wu_hbm.at[e], wu_st, wsem.at[0]),
                pltpu.make_async_copy(wd_hbm.at[e], wd_st, wsem.at[1]))

    def x_fetch(b, slot):
        return pltpu.make_async_copy(x_hbm.at[pl.ds(b * BM, BM), :], xbuf.at[slot], xsem.at[slot])

    def y_store(b, slot):
        return pltpu.make_async_copy(ybuf.at[slot], y_hbm.at[pl.ds(b * BM, BM), :], ysem.at[slot])

    for c in w_fetch(0):
        c.start()
    x_fetch(0, 0).start()

    def expert_body(e, carry):
        for c in w_fetch(e):
            c.wait()
        nblk = nblk_ref[e]

        @pl.when(nblk > 0)
        def _():
            wu_bf[...] = wu_st[...].astype(_bf16)
            wd_bf[...] = wd_st[...].astype(_bf16)

        @pl.when(e + 1 < E)
        def _():
            for c in w_fetch(e + 1):
                c.start()

        b_up = bu_ref[e]
        b_down = bd_ref[e]

        def block_body(j, c):
            b = first_ref[e] + j
            slot = b % 2
            x_fetch(b, slot).wait()

            @pl.when(b + 1 < total)
            def _():
                x_fetch(b + 1, 1 - slot).start()

            rows = lax.broadcasted_iota(jnp.int32, (BM, 1), 0)
            x = _unpack_rows(jnp.where(rows < cnt_ref[e] - j * BM, xbuf[slot], 0))
            hcat = _dot(x, wu_bf[...]) + b_up
            glu = jnp.minimum(hcat[:, 0:F], SWIGLU_LIMIT)
            lin = jnp.clip(hcat[:, F:2 * F], -SWIGLU_LIMIT, SWIGLU_LIMIT)
            act = glu * jax.nn.sigmoid(SWIGLU_ALPHA * glu) * (lin + 1.0)
            y = _pack_rows(_dot(act.astype(_bf16), wd_bf[...]) + b_down)

            @pl.when(b >= 2)
            def _():
                y_store(b - 2, slot).wait()

            ybuf[slot] = y
            y_store(b, slot).start()
            return c

        lax.fori_loop(0, nblk, block_body, 0)
        return carry

    lax.fori_loop(0, E, expert_body, 0)

    @pl.when(total >= 2)
    def _():
        y_store(total - 2, total % 2).wait()

    y_store(total - 1, (total - 1) % 2).wait()


def _experts(cnt, first, nblk, xpad, w_up, b_up, w_down, b_down):
    def full(a):
        nd = a.ndim
        return pl.BlockSpec(a.shape, lambda i, *_: (0,) * nd)

    return pl.pallas_call(
        _experts_kernel,
        out_shape=jax.ShapeDtypeStruct(xpad.shape, jnp.int32),
        grid_spec=pltpu.PrefetchScalarGridSpec(
            num_scalar_prefetch=3,
            grid=(1,),
            in_specs=[pl.BlockSpec(memory_space=pl.ANY),
                      pl.BlockSpec(memory_space=pl.ANY), full(b_up),
                      pl.BlockSpec(memory_space=pl.ANY), full(b_down)],
            out_specs=pl.BlockSpec(memory_space=pl.ANY),
            scratch_shapes=[pltpu.VMEM((2, BM, D // 2), jnp.int32), pltpu.VMEM((2, BM, D // 2), jnp.int32),
                            pltpu.VMEM((D, 2 * F), _f32), pltpu.VMEM((F, D), _f32),
                            pltpu.VMEM((D, 2 * F), _bf16), pltpu.VMEM((F, D), _bf16),
                            pltpu.SemaphoreType.DMA((2,)), pltpu.SemaphoreType.DMA((2,)),
                            pltpu.SemaphoreType.DMA((2,))],
        ),
        compiler_params=pltpu.CompilerParams(vmem_limit_bytes=VMEM_LIMIT),
        name="experts",
    )(cnt, first, nblk, xpad, w_up, b_up, w_down, b_down)


def _gather_rows(ypad, dest_all):
    n_out = dest_all.shape[0]
    width = ypad.shape[1]
    w = SC_WINDOW
    mesh = plsc.VectorSubcoreMesh(core_axis_name="core", subcore_axis_name="subcore")
    n_workers = mesh.num_cores * mesh.num_subcores
    n_windows = n_out // w

    @functools.partial(
        pl.kernel, mesh=mesh, name="gather_rows",
        out_type=jax.ShapeDtypeStruct((n_out, width), jnp.int32),
        scratch_types=[pltpu.VMEM((w, width), jnp.int32), pltpu.VMEM((w,), jnp.int32), pltpu.SemaphoreType.DMA])
    def gather_rows(y_hbm, d_hbm, o_hbm, rows, iv, sem):
        worker = lax.axis_index("subcore") * mesh.num_cores + lax.axis_index("core")

        @pl.loop(0, n_windows // n_workers)
        def _(j):
            r0 = pl.multiple_of((j * n_workers + worker) * w, w)
            pltpu.sync_copy(d_hbm.at[pl.ds(r0, w)], iv)
            pltpu.async_copy(y_hbm.at[iv], rows, sem).wait()
            pltpu.sync_copy(rows, o_hbm.at[pl.ds(r0, w)])

    assert n_windows % n_workers == 0
    return gather_rows(ypad, dest_all)


def _combine_kernel(h_ref, mod_ref, gate_ref, g2_ref, b2_ref, y_ref, o_ref):
    gates = gate_ref[...]
    half = D // 2
    f_hi = jnp.zeros((h_ref.shape[0], half), _f32)
    f_lo = jnp.zeros((h_ref.shape[0], half), _f32)
    for k in range(TOP_K):
        p = y_ref[k]
        gk = gates[:, k:k + 1]
        f_hi = f_hi + gk * lax.bitcast_convert_type(p & jnp.int32(-65536), _f32)
        f_lo = f_lo + gk * lax.bitcast_convert_type(lax.shift_left(p, 16), _f32)
    f = jnp.concatenate([f_hi, f_lo], axis=1)
    gate2 = mod_ref[...][:, 5 * D:6 * D]
    pre = ALPHA * h_ref[...] + _per_seq(f, gate2, jnp.multiply)
    o_ref[...] = _layer_norm(pre, g2_ref[...], b2_ref[...])


def _combine(h, mod3, gates, ln2_g, ln2_b, y4, row0, rows_per_mod):
    t = COMBINE_TILE
    off = row0 // t
    g = mod3.shape[1]
    tiles_per_mod = rows_per_mod // t
    return pl.pallas_call(
        _combine_kernel,
        out_shape=jax.ShapeDtypeStruct(h.shape, _f32),
        grid=(h.shape[0] // t,),
        in_specs=[pl.BlockSpec((t, D), lambda i: (i, 0)),
                  pl.BlockSpec((None, g, 6 * D), lambda i: (i // tiles_per_mod, 0, 0)),
                  pl.BlockSpec((t, LANES), lambda i: (i + off, 0)),
                  pl.BlockSpec((1, D), lambda i: (0, 0)),
                  pl.BlockSpec((1, D), lambda i: (0, 0)),
                  pl.BlockSpec((TOP_K, t, D // 2), lambda i: (0, i + off, 0))],
        out_specs=pl.BlockSpec((t, D), lambda i: (i, 0)),
        compiler_params=pltpu.CompilerParams(vmem_limit_bytes=VMEM_LIMIT),
        name="combine",
    )(h, mod3, gates, ln2_g, ln2_b, y4)


def _time_major(a):
    return a.transpose(1, 0, 2)


def kernel(x_prompt, x_sample, c_prompt, c_sample, state_conv, state_pool, w_ada, b_ada, w_in,
           conv_w, w_out_a, w_pool, ls_pool, w_out_b, w_o, ln1_g, ln1_b, w_router, b_router,
           w_up, b_up, w_down, b_down, ln2_g, ln2_b):
    n_seq_p, seq, _ = x_prompt.shape
    n_seq_s, dec_seq, _ = x_sample.shape
    n_p, n_s = n_seq_p * seq, n_seq_s * dec_seq
    n = n_p + n_s
    n_blocks = TOP_K * n // BM + E
    l = 0

    mod = _ada(jnp.concatenate([c_prompt, c_sample], axis=0), w_ada[l], b_ada[l][None])
    mod_p = mod[:n_seq_p][:, None, :]
    mod_s = mod[n_seq_p:][None]

    weights = (
        w_in[l].astype(_bf16), conv_w[l], w_out_a[l].astype(_bf16), w_pool[l].astype(_bf16),
        ls_pool[l][None], w_out_b[l].astype(_bf16), w_o[l].astype(_bf16), ln1_g[l][None], ln1_b[l][None],
        jnp.pad(w_router[l], ((0, 0), (0, LANES - E))), jnp.pad(b_router[l], (0, LANES - E))[None],
    )
    hc_p, hp_p = _hist_steps(CONV_HIST, 1), _hist_steps(POOL_HIST, 1)
    h_p, v_p, lg_p, nc_p, np_p = _mixer(
        x_prompt.reshape(n_p, D), mod_p, jnp.zeros((n_seq_p * hc_p, C), _f32),
        jnp.zeros((n_seq_p * hp_p, C), _f32), weights, ROW_TILE, 0)
    hc_s, hp_s = _hist_steps(CONV_HIST, n_seq_s), _hist_steps(POOL_HIST, n_seq_s)
    hist_c = jnp.pad(_time_major(state_conv[l]), ((hc_s - CONV_HIST, 0), (0, 0), (0, 0)))
    hist_p = jnp.pad(_time_major(state_pool[l]), ((hp_s - POOL_HIST, 0), (0, 0), (0, 0)))
    h_s, v_s, lg_s, nc_s, np_s = _mixer(
        _time_major(x_sample).reshape(n_s, D), mod_s, hist_c.reshape(hc_s * n_seq_s, C),
        hist_p.reshape(hp_s * n_seq_s, C), weights, SAMPLE_ROW_TILE, PAST_LEN)

    dest8, gates3, meta = _plan(lg_p, lg_s)
    cnt, first, nblk = meta[0, :E], meta[1, :E], meta[2, :E]
    gates = gates3.reshape(n, LANES)

    dests = [dest8[:, k, :].reshape(n) for k in range(TOP_K)]
    xpad = _dispatch(v_p, v_s, dests, n_blocks * BM)
    ypad = _experts(cnt, first, nblk, xpad, w_up[l], b_up[l][:, None, :], w_down[l], b_down[l][:, None, :])
    y4 = _gather_rows(ypad, jnp.concatenate(dests)).reshape(TOP_K, n, D // 2)

    g2, b2 = ln2_g[l][None], ln2_b[l][None]
    y_p = _combine(h_p, mod_p, gates, g2, b2, y4, 0, seq)
    y_s = _combine(h_s, mod_s, gates, g2, b2, y4, n_p, n_s)

    y_prompt = y_p.reshape(n_seq_p, seq, D)
    y_sample = _time_major(y_s.reshape(dec_seq, n_seq_s, D))
    new_conv_p = nc_p.reshape(n_seq_p, hc_p, C)[:, hc_p - CONV_HIST:][None]
    new_pool_p = np_p.reshape(n_seq_p, hp_p, C)[:, hp_p - POOL_HIST:][None]
    new_conv_s = _time_major(nc_s.reshape(hc_s, n_seq_s, C)[hc_s - CONV_HIST:])[None]
    new_pool_s = _time_major(np_s.reshape(hp_s, n_seq_s, C)[hp_s - POOL_HIST:])[None]
    return (y_prompt, y_sample, new_conv_p, new_pool_p, new_conv_s, new_pool_s)
```

```python
import functools

import jax
import jax.numpy as jnp
from jax import lax
from jax.experimental import pallas as pl
from jax.experimental.pallas import tpu as pltpu
from jax.experimental.pallas import tpu_sc as plsc

D = 1024
C = 512
N_GROUPS = 4
GROUP = C // N_GROUPS
CONV_HIST = 2
POOL_HIST = 15
E = 32
TOP_K = 4
F = 1024
SWIGLU_LIMIT = 7.0
SWIGLU_ALPHA = 1.702
LN_EPS = 1e-5
DEPTH = 1
ALPHA = (2 * DEPTH) ** 0.25
PAST_LEN = 16384

LANES = 128
SUBLANES = 8
ROW_TILE = 512
SAMPLE_ROW_TILE = 256
BM = 512
COMBINE_TILE = 256
SC_WINDOW = 128
VMEM_LIMIT = 56 * 1024 * 1024

_f32 = jnp.float32
_bf16 = jnp.bfloat16


def _dot(a, b):
    return jnp.dot(a, b, preferred_element_type=_f32)


def _dot_exact(a, b):
    return lax.dot_general(a, b, (((1,), (0,)), ((), ())),
                           precision=lax.Precision.HIGHEST, preferred_element_type=_f32)


def _dot_split(a, b):
    a_hi, b_hi = a.astype(_bf16), b.astype(_bf16)
    a_lo = (a - a_hi.astype(_f32)).astype(_bf16)
    b_lo = (b - b_hi.astype(_f32)).astype(_bf16)
    return _dot(a_hi, b_hi) + _dot(a_lo, b_hi) + _dot(a_hi, b_lo)


def _pack_rows(x):
    w = x.shape[1] // 2
    hi = lax.bitcast_convert_type(x[:, :w].astype(_bf16).astype(_f32), jnp.int32)
    lo = lax.bitcast_convert_type(x[:, w:].astype(_bf16).astype(_f32), jnp.int32)
    return hi | lax.shift_right_logical(lo, 16)


def _unpack_rows(p):
    hi = lax.bitcast_convert_type(p & jnp.int32(-65536), _f32)
    lo = lax.bitcast_convert_type(lax.shift_left(p, 16), _f32)
    return jnp.concatenate([hi, lo], axis=1).astype(_bf16)


def _per_seq(x, m, op):
    g = m.shape[0]
    if g == 1:
        return op(x, m)
    r, n = x.shape
    return op(x.reshape(r // g, g, n), m[None]).reshape(r, n)


def _layer_norm(x, g, b):
    mu = jnp.mean(x, axis=-1, keepdims=True)
    xc = x - mu
    var = jnp.mean(xc * xc, axis=-1, keepdims=True)
    return xc * lax.rsqrt(var + LN_EPS) * g + b


def _hist_steps(needed, g):
    return -(-needed * g // SUBLANES) * SUBLANES // g


def _ada_kernel(c_ref, w_ref, b_ref, o_ref):
    c = c_ref[...]
    o_ref[...] = _dot_exact(c * jax.nn.sigmoid(c), w_ref[...]) + b_ref[...]


def _ada(c, w_ada, b_ada):
    rows = c.shape[0]
    cols = w_ada.shape[1]
    bn = 1536
    return pl.pallas_call(
        _ada_kernel,
        out_shape=jax.ShapeDtypeStruct((rows, cols), _f32),
        grid=(cols // bn,),
        in_specs=[pl.BlockSpec((rows, D), lambda j: (0, 0)),
                  pl.BlockSpec((D, bn), lambda j: (0, j)),
                  pl.BlockSpec((1, bn), lambda j: (0, j))],
        out_specs=pl.BlockSpec((rows, bn), lambda j: (0, j)),
        compiler_params=pltpu.CompilerParams(vmem_limit_bytes=VMEM_LIMIT),
        name="ada",
    )(c, w_ada, b_ada)


def _mixer_kernel(g, tiles_per_seq, start_pos,
                  x_ref, mod_ref, hc_ref, hp_ref,
                  win_ref, cw_ref, woa_ref, wpool_ref, ls_ref, wob_ref, wo_ref, g1_ref, b1_ref,
                  wr_ref, br_ref,
                  h_ref, v_ref, lg_ref, nc_ref, np_ref, zbuf, pbuf):
    r = x_ref.shape[0]
    hrc = hc_ref.shape[0]
    hrp = hp_ref.shape[0]
    j = pl.program_id(0) % tiles_per_seq

    @pl.when(j == 0)
    def _():
        zbuf[pl.ds(0, hrc), :] = hc_ref[...]
        pbuf[pl.ds(0, hrp), :] = hp_ref[...]

    @pl.when(j != 0)
    def _():
        zt = zbuf[pl.ds(r, hrc), :]
        pt = pbuf[pl.ds(r, hrp), :]
        zbuf[pl.ds(0, hrc), :] = zt
        pbuf[pl.ds(0, hrp), :] = pt

    m = mod_ref[...]
    shift1, scale1, gate1 = m[:, 0:D], m[:, D:2 * D], m[:, 2 * D:3 * D]
    shift2, scale2 = m[:, 3 * D:4 * D], m[:, 4 * D:5 * D]

    x = x_ref[...]
    u = _per_seq(_per_seq(x, 1.0 + scale1, jnp.multiply), shift1, jnp.add).astype(_bf16)

    z = _dot(u, win_ref[:, C:2 * C]) * _dot(u, win_ref[:, 2 * C:3 * C])
    zbuf[pl.ds(hrc, r), :] = z
    cw = cw_ref[...]
    conv = (cw[0:1] * zbuf[pl.ds(hrc - 2 * g, r), :] + cw[1:2] * zbuf[pl.ds(hrc - g, r), :]
            + cw[2:3] * z)
    y_a = _dot((_dot(u, win_ref[:, 0:C]) * conv).astype(_bf16), woa_ref[...])

    xp = _dot(u, win_ref[:, 3 * C:4 * C])
    pbuf[pl.ds(hrp, r), :] = xp
    pos = start_pos + j * (r // g) + lax.broadcasted_iota(jnp.int32, (r, 1), 0) // g
    acc = xp
    yg = []
    for grp in range(N_GROUPS):
        lo = grp * GROUP
        wdw = 2 ** (grp + 1)
        for back in range(wdw // 2, wdw):
            sh = pbuf[pl.ds(hrp - back * g, r), lo:C]
            acc = jnp.concatenate([acc[:, 0:lo], acc[:, lo:C] + sh], axis=1) if lo else acc + sh
        cnt = jnp.minimum(wdw, pos + 1).astype(_f32)
        diff = acc[:, lo:lo + GROUP] / cnt - xp[:, lo:lo + GROUP]
        yg.append(_dot(diff.astype(_bf16), wpool_ref[grp]))
    y_b = _dot((jnp.concatenate(yg, axis=1) * ls_ref[...]).astype(_bf16), wob_ref[...])

    g_a = _dot(u, win_ref[:, 4 * C:4 * C + D])
    g_b = _dot(u, win_ref[:, 4 * C + D:4 * C + 2 * D])
    merged = jax.nn.sigmoid(g_a) * y_a + jax.nn.sigmoid(g_b) * y_b
    o = _dot(merged.astype(_bf16), wo_ref[...])
    h = _layer_norm(ALPHA * x + _per_seq(o, gate1, jnp.multiply), g1_ref[...], b1_ref[...])
    v = _per_seq(_per_seq(h, 1.0 + scale2, jnp.multiply), shift2, jnp.add)
    h_ref[...] = h
    v_ref[...] = _pack_rows(v)
    lg_ref[...] = _dot_split(v, wr_ref[...]) + br_ref[...]
    nc_ref[...] = zbuf[pl.ds(r, hrc), :]
    np_ref[...] = pbuf[pl.ds(r, hrp), :]


def _mixer(x2, mod3, hc, hp, weights, row_tile, start_pos):
    n = x2.shape[0]
    n_mod, g, _ = mod3.shape
    hrc, hrp = hc.shape[0] // n_mod, hp.shape[0] // n_mod
    tiles_per_seq = n // n_mod // row_tile
    once = dict(pipeline_mode=pl.Buffered(1)) if n_mod == 1 else {}

    def full(a):
        nd = a.ndim
        return pl.BlockSpec(a.shape, lambda i: (0,) * nd)

    def seq_block(rows, **kw):
        return pl.BlockSpec((rows, C), lambda i: (i // tiles_per_seq, 0), **kw)

    def row_block(cols):
        return pl.BlockSpec((row_tile, cols), lambda i: (i, 0))

    return pl.pallas_call(
        functools.partial(_mixer_kernel, g, tiles_per_seq, start_pos),
        out_shape=[
            jax.ShapeDtypeStruct((n, D), _f32),
            jax.ShapeDtypeStruct((n, D // 2), jnp.int32),
            jax.ShapeDtypeStruct((n, LANES), _f32),
            jax.ShapeDtypeStruct(hc.shape, _f32),
            jax.ShapeDtypeStruct(hp.shape, _f32),
        ],
        grid=(n // row_tile,),
        in_specs=[row_block(D),
                  pl.BlockSpec((None, g, 6 * D), lambda i: (i // tiles_per_seq, 0, 0), **once),
                  seq_block(hrc, **once), seq_block(hrp, **once)] + [full(a) for a in weights],
        out_specs=[row_block(D), row_block(D // 2), row_block(LANES), seq_block(hrc), seq_block(hrp)],
        scratch_shapes=[pltpu.VMEM((hrc + row_tile, C), _f32), pltpu.VMEM((hrp + row_tile, C), _f32)],
        compiler_params=pltpu.CompilerParams(vmem_limit_bytes=VMEM_LIMIT),
        name="mixer",
    )(x2, mod3, hc, hp, *weights)


def _plan_kernel(lgp_ref, lgs_ref, dest_ref, gate_ref, meta_ref, idx_s, rank_s):
    t = ROW_TILE
    e_iota = lax.broadcasted_iota(jnp.int32, (E, t), 0)
    tri = (lax.broadcasted_iota(jnp.int32, (t, t), 0)
           < lax.broadcasted_iota(jnp.int32, (t, t), 1)).astype(_f32).astype(_bf16)
    zeros_rest = jnp.zeros((LANES - TOP_K, t), _f32)

    def tile_body(lg_ref, off, i, carry):
        lt = lg_ref[i].T[0:E, :]
        vals, idxs = [], []
        for _ in range(TOP_K):
            mx = jnp.max(lt, axis=0, keepdims=True)
            ix = jnp.min(jnp.where(lt == mx, e_iota, E), axis=0, keepdims=True)
            vals.append(mx)
            idxs.append(ix)
            lt = jnp.where(e_iota == ix, -jnp.inf, lt)
        ex = [jnp.exp(vk - vals[0]) for vk in vals]
        den = ex[0] + ex[1] + ex[2] + ex[3]
        gates = [ek / den for ek in ex]
        gate_ref[off + i] = jnp.concatenate(gates + [zeros_rest], axis=0).T

        ohs = [(e_iota == ix) for ix in idxs]
        oh = (ohs[0] | ohs[1] | ohs[2] | ohs[3]).astype(_f32)
        before = _dot(oh.astype(_bf16), tri) + carry
        ranks = [jnp.sum(jnp.where(o, before, 0.0), axis=0, keepdims=True) for o in ohs]
        idx_s[off + i] = jnp.concatenate(idxs + idxs, axis=0)
        rank_s[off + i] = jnp.concatenate(ranks + ranks, axis=0).astype(jnp.int32)
        return carry + jnp.sum(oh, axis=1, keepdims=True)

    n_p, n_s = lgp_ref.shape[0], lgs_ref.shape[0]
    counts = lax.fori_loop(0, n_p, functools.partial(tile_body, lgp_ref, 0), jnp.zeros((E, 1), _f32))
    counts = lax.fori_loop(0, n_s, functools.partial(tile_body, lgs_ref, n_p), counts)
    padded = jnp.ceil(counts / BM) * BM
    low = (lax.broadcasted_iota(jnp.int32, (E, E), 1)
           <= lax.broadcasted_iota(jnp.int32, (E, E), 0)).astype(_f32)
    pad_end = _dot_exact(low, jnp.broadcast_to(padded, (E, LANES)))[:, 0:1]
    pad_start = pad_end - padded

    def dest_body(i, c):
        ix = idx_s[i]
        rk = rank_s[i]
        rows = []
        for k in range(TOP_K):
            st = jnp.sum(jnp.where(e_iota == ix[k:k + 1], pad_start, 0.0), axis=0, keepdims=True)
            rows.append(st.astype(jnp.int32) + rk[k:k + 1])
        dest_ref[i] = jnp.concatenate(rows + rows, axis=0)
        return c

    lax.fori_loop(0, n_p + n_s, dest_body, 0)

    on_lane = (lax.broadcasted_iota(jnp.int32, (E, LANES), 0)
               == lax.broadcasted_iota(jnp.int32, (E, LANES), 1))

    def to_lanes(col):
        return jnp.sum(jnp.where(on_lane, col, 0.0), axis=0, keepdims=True).astype(jnp.int32)

    meta_ref[...] = jnp.concatenate(
        [to_lanes(counts), to_lanes(pad_start / BM), to_lanes(padded / BM),
         jnp.zeros((SUBLANES - 3, LANES), jnp.int32)], axis=0)


def _plan(logits_p, logits_s):
    n_tiles = (logits_p.shape[0] + logits_s.shape[0]) // ROW_TILE
    return pl.pallas_call(
        _plan_kernel,
        out_shape=[
            jax.ShapeDtypeStruct((n_tiles, 2 * TOP_K, ROW_TILE), jnp.int32),
            jax.ShapeDtypeStruct((n_tiles, ROW_TILE, LANES), _f32),
            jax.ShapeDtypeStruct((SUBLANES, LANES), jnp.int32),
        ],
        scratch_shapes=[pltpu.VMEM((n_tiles, 2 * TOP_K, ROW_TILE), jnp.int32),
                        pltpu.VMEM((n_tiles, 2 * TOP_K, ROW_TILE), jnp.int32)],
        compiler_params=pltpu.CompilerParams(vmem_limit_bytes=VMEM_LIMIT),
        name="plan",
    )(logits_p.reshape(-1, ROW_TILE, LANES), logits_s.reshape(-1, ROW_TILE, LANES))


def _dispatch(v_p, v_s, dests, n_rows_out):
    n_p, n_s = v_p.shape[0], v_s.shape[0]
    width = v_p.shape[1]
    w = SC_WINDOW
    n_pw, n_windows = n_p // w, (n_p + n_s) // w
    mesh = plsc.VectorSubcoreMesh(core_axis_name="core", subcore_axis_name="subcore")
    n_workers = mesh.num_cores * mesh.num_subcores

    @functools.partial(
        pl.kernel, mesh=mesh, name="dispatch",
        out_type=jax.ShapeDtypeStruct((n_rows_out, width), jnp.int32),
        scratch_types=[pltpu.VMEM((w, width), jnp.int32)] + [pltpu.VMEM((w,), jnp.int32)] * TOP_K
        + [pltpu.SemaphoreType.DMA])
    def scatter_rows(vp_hbm, vs_hbm, d0_hbm, d1_hbm, d2_hbm, d3_hbm, o_hbm, rows, i0, i1, i2, i3, sem):
        worker = lax.axis_index("subcore") * mesh.num_cores + lax.axis_index("core")
        idx = (i0, i1, i2, i3)

        def scatter_window(c):
            t0 = pl.multiple_of(c * w, w)
            for d_hbm, iv in zip((d0_hbm, d1_hbm, d2_hbm, d3_hbm), idx):
                pltpu.sync_copy(d_hbm.at[pl.ds(t0, w)], iv)
            copies = [pltpu.async_copy(rows, o_hbm.at[iv], sem) for iv in idx]
            for cp in copies:
                cp.wait()

        for j in range(-(-n_windows // n_workers)):
            c = j * n_workers + worker

            @pl.when(c < n_pw)
            def _():
                pltpu.sync_copy(vp_hbm.at[pl.ds(pl.multiple_of(c * w, w), w)], rows)
                scatter_window(c)

            @pl.when((c >= n_pw) & (c < n_windows))
            def _():
                pltpu.sync_copy(vs_hbm.at[pl.ds(pl.multiple_of((c - n_pw) * w, w), w)], rows)
                scatter_window(c)

    return scatter_rows(v_p, v_s, *dests)


def _experts_kernel(cnt_ref, first_ref, nblk_ref, x_hbm, wu_hbm, bu_ref, wd_hbm, bd_ref, y_hbm,
                    xbuf, ybuf, wu_st, wd_st, wu_bf, wd_bf, xsem, ysem, wsem):
    total = first_ref[E - 1] + nblk_ref[E - 1]

    def w_fetch(e):
        return (pltpu.make_async_copy(wu_hbm.at[e], wu_st, wsem.at[0]),
                pltpu.make_async_copy(wd_hbm.at[e], wd_st, wsem.at[1]))

    def x_fetch(b, slot):
        return pltpu.make_async_copy(x_hbm.at[pl.ds(b * BM, BM), :], xbuf.at[slot], xsem.at[slot])

    def y_store(b, slot):
        return pltpu.make_async_copy(ybuf.at[slot], y_hbm.at[pl.ds(b * BM, BM), :], ysem.at[slot])

    for c in w_fetch(0):
        c.start()
    x_fetch(0, 0).start()

    def expert_body(e, carry):
        for c in w_fetch(e):
            c.wait()
        nblk = nblk_ref[e]

        @pl.when(nblk > 0)
        def _():
            wu_bf[...] = wu_st[...].astype(_bf16)
            wd_bf[...] = wd_st[...].astype(_bf16)

        @pl.when(e + 1 < E)
        def _():
            for c in w_fetch(e + 1):
                c.start()

        b_up = bu_ref[e]
        b_down = bd_ref[e]

        def block_body(j, c):
            b = first_ref[e] + j
            slot = b % 2
            x_fetch(b, slot).wait()

            @pl.when(b + 1 < total)
            def _():
                x_fetch(b + 1, 1 - slot).start()

            rows = lax.broadcasted_iota(jnp.int32, (BM, 1), 0)
            x = _unpack_rows(jnp.where(rows < cnt_ref[e] - j * BM, xbuf[slot], 0))
            hcat = _dot(x, wu_bf[...]) + b_up
            glu = jnp.minimum(hcat[:, 0:F], SWIGLU_LIMIT)
            lin = jnp.clip(hcat[:, F:2 * F], -SWIGLU_LIMIT, SWIGLU_LIMIT)
            act = glu * jax.nn.sigmoid(SWIGLU_ALPHA * glu) * (lin + 1.0)
            y = _pack_rows(_dot(act.astype(_bf16), wd_bf[...]) + b_down)

            @pl.when(b >= 2)
            def _():
                y_store(b - 2, slot).wait()

            ybuf[slot] = y
            y_store(b, slot).start()
            return c

        lax.fori_loop(0, nblk, block_body, 0)
        return carry

    lax.fori_loop(0, E, expert_body, 0)

    @pl.when(total >= 2)
    def _():
        y_store(total - 2, total % 2).wait()

    y_store(total - 1, (total - 1) % 2).wait()


def _experts(cnt, first, nblk, xpad, w_up, b_up, w_down, b_down):
    def full(a):
        nd = a.ndim
        return pl.BlockSpec(a.shape, lambda i, *_: (0,) * nd)

    return pl.pallas_call(
        _experts_kernel,
        out_shape=jax.ShapeDtypeStruct(xpad.shape, jnp.int32),
        grid_spec=pltpu.PrefetchScalarGridSpec(
            num_scalar_prefetch=3,
            grid=(1,),
            in_specs=[pl.BlockSpec(memory_space=pl.ANY),
                      pl.BlockSpec(memory_space=pl.ANY), full(b_up),
                      pl.BlockSpec(memory_space=pl.ANY), full(b_down)],
            out_specs=pl.BlockSpec(memory_space=pl.ANY),
            scratch_shapes=[pltpu.VMEM((2, BM, D // 2), jnp.int32), pltpu.VMEM((2, BM, D // 2), jnp.int32),
                            pltpu.VMEM((D, 2 * F), _f32), pltpu.VMEM((F, D), _f32),
                            pltpu.VMEM((D, 2 * F), _bf16), pltpu.VMEM((F, D), _bf16),
                            pltpu.SemaphoreType.DMA((2,)), pltpu.SemaphoreType.DMA((2,)),
                            pltpu.SemaphoreType.DMA((2,))],
        ),
        compiler_params=pltpu.CompilerParams(vmem_limit_bytes=VMEM_LIMIT),
        name="experts",
    )(cnt, first, nblk, xpad, w_up, b_up, w_down, b_down)


def _gather_rows(ypad, dest_all):
    n_out = dest_all.shape[0]
    width = ypad.shape[1]
    w = SC_WINDOW
    mesh = plsc.VectorSubcoreMesh(core_axis_name="core", subcore_axis_name="subcore")
    n_workers = mesh.num_cores * mesh.num_subcores
    n_windows = n_out // w

    @functools.partial(
        pl.kernel, mesh=mesh, name="gather_rows",
        out_type=jax.ShapeDtypeStruct((n_out, width), jnp.int32),
        scratch_types=[pltpu.VMEM((w, width), jnp.int32), pltpu.VMEM((w,), jnp.int32), pltpu.SemaphoreType.DMA])
    def gather_rows(y_hbm, d_hbm, o_hbm, rows, iv, sem):
        worker = lax.axis_index("subcore") * mesh.num_cores + lax.axis_index("core")

        @pl.loop(0, n_windows // n_workers)
        def _(j):
            r0 = pl.multiple_of((j * n_workers + worker) * w, w)
            pltpu.sync_copy(d_hbm.at[pl.ds(r0, w)], iv)
            pltpu.async_copy(y_hbm.at[iv], rows, sem).wait()
            pltpu.sync_copy(rows, o_hbm.at[pl.ds(r0, w)])

    assert n_windows % n_workers == 0
    return gather_rows(ypad, dest_all)


def _combine_kernel(h_ref, mod_ref, gate_ref, g2_ref, b2_ref, y_ref, o_ref):
    gates = gate_ref[...]
    half = D // 2
    f_hi = jnp.zeros((h_ref.shape[0], half), _f32)
    f_lo = jnp.zeros((h_ref.shape[0], half), _f32)
    for k in range(TOP_K):
        p = y_ref[k]
        gk = gates[:, k:k + 1]
        f_hi = f_hi + gk * lax.bitcast_convert_type(p & jnp.int32(-65536), _f32)
        f_lo = f_lo + gk * lax.bitcast_convert_type(lax.shift_left(p, 16), _f32)
    f = jnp.concatenate([f_hi, f_lo], axis=1)
    gate2 = mod_ref[...][:, 5 * D:6 * D]
    pre = ALPHA * h_ref[...] + _per_seq(f, gate2, jnp.multiply)
    o_ref[...] = _layer_norm(pre, g2_ref[...], b2_ref[...])


def _combine(h, mod3, gates, ln2_g, ln2_b, y4, row0, rows_per_mod):
    t = COMBINE_TILE
    off = row0 // t
    g = mod3.shape[1]
    tiles_per_mod = rows_per_mod // t
    return pl.pallas_call(
        _combine_kernel,
        out_shape=jax.ShapeDtypeStruct(h.shape, _f32),
        grid=(h.shape[0] // t,),
        in_specs=[pl.BlockSpec((t, D), lambda i: (i, 0)),
                  pl.BlockSpec((None, g, 6 * D), lambda i: (i // tiles_per_mod, 0, 0)),
                  pl.BlockSpec((t, LANES), lambda i: (i + off, 0)),
                  pl.BlockSpec((1, D), lambda i: (0, 0)),
                  pl.BlockSpec((1, D), lambda i: (0, 0)),
                  pl.BlockSpec((TOP_K, t, D // 2), lambda i: (0, i + off, 0))],
        out_specs=pl.BlockSpec((t, D), lambda i: (i, 0)),
        compiler_params=pltpu.CompilerParams(vmem_limit_bytes=VMEM_LIMIT),
        name="combine",
    )(h, mod3, gates, ln2_g, ln2_b, y4)


def _time_major(a):
    return a.transpose(1, 0, 2)


def kernel(x_prompt, x_sample, c_prompt, c_sample, state_conv, state_pool, w_ada, b_ada, w_in,
           conv_w, w_out_a, w_pool, ls_pool, w_out_b, w_o, ln1_g, ln1_b, w_router, b_router,
           w_up, b_up, w_down, b_down, ln2_g, ln2_b):
    n_seq_p, seq, _ = x_prompt.shape
    n_seq_s, dec_seq, _ = x_sample.shape
    n_p, n_s = n_seq_p * seq, n_seq_s * dec_seq
    n = n_p + n_s
    n_blocks = TOP_K * n // BM + E
    l = 0

    mod = _ada(jnp.concatenate([c_prompt, c_sample], axis=0), w_ada[l], b_ada[l][None])
    mod_p = mod[:n_seq_p][:, None, :]
    mod_s = mod[n_seq_p:][None]

    weights = (
        w_in[l].astype(_bf16), conv_w[l], w_out_a[l].astype(_bf16), w_pool[l].astype(_bf16),
        ls_pool[l][None], w_out_b[l].astype(_bf16), w_o[l].astype(_bf16), ln1_g[l][None], ln1_b[l][None],
        jnp.pad(w_router[l], ((0, 0), (0, LANES - E))), jnp.pad(b_router[l], (0, LANES - E))[None],
    )
    hc_p, hp_p = _hist_steps(CONV_HIST, 1), _hist_steps(POOL_HIST, 1)
    h_p, v_p, lg_p, nc_p, np_p = _mixer(
        x_prompt.reshape(n_p, D), mod_p, jnp.zeros((n_seq_p * hc_p, C), _f32),
        jnp.zeros((n_seq_p * hp_p, C), _f32), weights, ROW_TILE, 0)
    hc_s, hp_s = _hist_steps(CONV_HIST, n_seq_s), _hist_steps(POOL_HIST, n_seq_s)
    hist_c = jnp.pad(_time_major(state_conv[l]), ((hc_s - CONV_HIST, 0), (0, 0), (0, 0)))
    hist_p = jnp.pad(_time_major(state_pool[l]), ((hp_s - POOL_HIST, 0), (0, 0), (0, 0)))
    h_s, v_s, lg_s, nc_s, np_s = _mixer(
        _time_major(x_sample).reshape(n_s, D), mod_s, hist_c.reshape(hc_s * n_seq_s, C),
        hist_p.reshape(hp_s * n_seq_s, C), weights, SAMPLE_ROW_TILE, PAST_LEN)

    dest8, gates3, meta = _plan(lg_p, lg_s)
    cnt, first, nblk = meta[0, :E], meta[1, :E], meta[2, :E]
    gates = gates3.reshape(n, LANES)

    dests = [dest8[:, k, :].reshape(n) for k in range(TOP_K)]
    xpad = _dispatch(v_p, v_s, dests, n_blocks * BM)
    ypad = _experts(cnt, first, nblk, xpad, w_up[l], b_up[l][:, None, :], w_down[l], b_down[l][:, None, :])
    y4 = _gather_rows(ypad, jnp.concatenate(dests)).reshape(TOP_K, n, D // 2)

    g2, b2 = ln2_g[l][None], ln2_b[l][None]
    y_p = _combine(h_p, mod_p, gates, g2, b2, y4, 0, seq)
    y_s = _combine(h_s, mod_s, gates, g2, b2, y4, n_p, n_s)

    y_prompt = y_p.reshape(n_seq_p, seq, D)
    y_sample = _time_major(y_s.reshape(dec_seq, n_seq_s, D))
    new_conv_p = nc_p.reshape(n_seq_p, hc_p, C)[:, hc_p - CONV_HIST:][None]
    new_pool_p = np_p.reshape(n_seq_p, hp_p, C)[:, hp_p - POOL_HIST:][None]
    new_conv_s = _time_major(nc_s.reshape(hc_s, n_seq_s, C)[hc_s - CONV_HIST:])[None]
    new_pool_s = _time_major(np_s.reshape(hp_s, n_seq_s, C)[hp_s - POOL_HIST:])[None]
    return (y_prompt, y_sample, new_conv_p, new_pool_p, new_conv_s, new_pool_s)
```

```python
import functools

import jax
import jax.numpy as jnp
from jax import lax
from jax.experimental import pallas as pl
from jax.experimental.pallas import tpu as pltpu
from jax.experimental.pallas import tpu_sc as plsc

D = 1024
C = 512
N_GROUPS = 4
GROUP = C // N_GROUPS
CONV_HIST = 2
POOL_HIST = 15
E = 32
TOP_K = 4
F = 1024
SWIGLU_LIMIT = 7.0
SWIGLU_ALPHA = 1.702
LN_EPS = 1e-5
DEPTH = 1
ALPHA = (2 * DEPTH) ** 0.25
PAST_LEN = 16384

LANES = 128
SUBLANES = 8
ROW_TILE = 512
PROMPT_ROW_TILE = 1024
PROMPT_SUB_TILES = 2
SAMPLE_ROW_TILE = 256
BM = 512
COMBINE_TILE = 256
COMBINE_CHUNKS = 4
SC_WINDOW = 128
VMEM_LIMIT = 56 * 1024 * 1024

_f32 = jnp.float32
_bf16 = jnp.bfloat16


def _dot(a, b):
    return jnp.dot(a, b, preferred_element_type=_f32)


def _dot_exact(a, b):
    return lax.dot_general(a, b, (((1,), (0,)), ((), ())),
                           precision=lax.Precision.HIGHEST, preferred_element_type=_f32)


def _dot_split(a, b):
    a_hi, b_hi = a.astype(_bf16), b.astype(_bf16)
    a_lo = (a - a_hi.astype(_f32)).astype(_bf16)
    b_lo = (b - b_hi.astype(_f32)).astype(_bf16)
    return _dot(a_hi, b_hi) + _dot(a_lo, b_hi) + _dot(a_hi, b_lo)


def _pack_rows(x):
    w = x.shape[1] // 2
    hi = lax.bitcast_convert_type(x[:, :w].astype(_bf16).astype(_f32), jnp.int32)
    lo = lax.bitcast_convert_type(x[:, w:].astype(_bf16).astype(_f32), jnp.int32)
    return hi | lax.shift_right_logical(lo, 16)


def _unpack_rows(p):
    hi = lax.bitcast_convert_type(p & jnp.int32(-65536), _f32)
    lo = lax.bitcast_convert_type(lax.shift_left(p, 16), _f32)
    return jnp.concatenate([hi, lo], axis=1).astype(_bf16)


def _per_seq(x, m, op):
    g = m.shape[0]
    if g == 1:
        return op(x, m)
    r, n = x.shape
    return op(x.reshape(r // g, g, n), m[None]).reshape(r, n)


def _layer_norm(x, g, b):
    mu = jnp.mean(x, axis=-1, keepdims=True)
    xc = x - mu
    var = jnp.mean(xc * xc, axis=-1, keepdims=True)
    return xc * lax.rsqrt(var + LN_EPS) * g + b


def _hist_steps(needed, g):
    return -(-needed * g // SUBLANES) * SUBLANES // g


def _ada_kernel(c_ref, w_ref, b_ref, o_ref):
    c = c_ref[...]
    o_ref[...] = _dot_exact(c * jax.nn.sigmoid(c), w_ref[...]) + b_ref[...]


def _ada(c, w_ada, b_ada):
    rows = c.shape[0]
    cols = w_ada.shape[1]
    bn = 1536
    return pl.pallas_call(
        _ada_kernel,
        out_shape=jax.ShapeDtypeStruct((rows, cols), _f32),
        grid=(cols // bn,),
        in_specs=[pl.BlockSpec((rows, D), lambda j: (0, 0)),
                  pl.BlockSpec((D, bn), lambda j: (0, j)),
                  pl.BlockSpec((1, bn), lambda j: (0, j))],
        out_specs=pl.BlockSpec((rows, bn), lambda j: (0, j)),
        compiler_params=pltpu.CompilerParams(vmem_limit_bytes=VMEM_LIMIT),
        name="ada",
    )(c, w_ada, b_ada)


def _mixer_kernel(g, tiles_per_seq, start_pos, n_sub,
                  x_ref, mod_ref, hc_ref, hp_ref,
                  win_ref, cw_ref, woa_ref, wpool_ref, ls_ref, wob_ref, wo_ref, g1_ref, b1_ref,
                  wr_ref, br_ref,
                  h_ref, v_ref, lg_ref, nc_ref, np_ref, zbuf, pbuf):
    r = x_ref.shape[0]
    hrc = hc_ref.shape[0]
    hrp = hp_ref.shape[0]
    j = pl.program_id(0) % tiles_per_seq

    @pl.when(j == 0)
    def _():
        zbuf[pl.ds(0, hrc), :] = hc_ref[...]
        pbuf[pl.ds(0, hrp), :] = hp_ref[...]

    @pl.when(j != 0)
    def _():
        zt = zbuf[pl.ds(r, hrc), :]
        pt = pbuf[pl.ds(r, hrp), :]
        zbuf[pl.ds(0, hrc), :] = zt
        pbuf[pl.ds(0, hrp), :] = pt

    m = mod_ref[...]
    shift1, scale1, gate1 = m[:, 0:D], m[:, D:2 * D], m[:, 2 * D:3 * D]
    shift2, scale2 = m[:, 3 * D:4 * D], m[:, 4 * D:5 * D]

    rs = r // n_sub
    for s in range(n_sub):
        rows = pl.ds(s * rs, rs)
        x = x_ref[rows, :]
        u = _per_seq(_per_seq(x, 1.0 + scale1, jnp.multiply), shift1, jnp.add).astype(_bf16)

        z = _dot(u, win_ref[:, C:2 * C]) * _dot(u, win_ref[:, 2 * C:3 * C])
        zrow = hrc + s * rs
        zbuf[pl.ds(zrow, rs), :] = z
        cw = cw_ref[...]
        conv = (cw[0:1] * zbuf[pl.ds(zrow - 2 * g, rs), :] + cw[1:2] * zbuf[pl.ds(zrow - g, rs), :]
                + cw[2:3] * z)
        y_a = _dot((_dot(u, win_ref[:, 0:C]) * conv).astype(_bf16), woa_ref[...])

        xp = _dot(u, win_ref[:, 3 * C:4 * C])
        prow = hrp + s * rs
        pbuf[pl.ds(prow, rs), :] = xp
        pos = (start_pos + j * (r // g) + s * (rs // g)
               + lax.broadcasted_iota(jnp.int32, (rs, 1), 0) // g)
        acc = xp
        yg = []
        for grp in range(N_GROUPS):
            lo = grp * GROUP
            wdw = 2 ** (grp + 1)
            for back in range(wdw // 2, wdw):
                sh = pbuf[pl.ds(prow - back * g, rs), lo:C]
                acc = jnp.concatenate([acc[:, 0:lo], acc[:, lo:C] + sh], axis=1) if lo else acc + sh
            cnt = jnp.minimum(wdw, pos + 1).astype(_f32)
            diff = acc[:, lo:lo + GROUP] / cnt - xp[:, lo:lo + GROUP]
            yg.append(_dot(diff.astype(_bf16), wpool_ref[grp]))
        y_b = _dot((jnp.concatenate(yg, axis=1) * ls_ref[...]).astype(_bf16), wob_ref[...])

        g_a = _dot(u, win_ref[:, 4 * C:4 * C + D])
        g_b = _dot(u, win_ref[:, 4 * C + D:4 * C + 2 * D])
        merged = jax.nn.sigmoid(g_a) * y_a + jax.nn.sigmoid(g_b) * y_b
        o = _dot(merged.astype(_bf16), wo_ref[...])
        h = _layer_norm(ALPHA * x + _per_seq(o, gate1, jnp.multiply), g1_ref[...], b1_ref[...])
        v = _per_seq(_per_seq(h, 1.0 + scale2, jnp.multiply), shift2, jnp.add)
        h_ref[rows, :] = h
        v_ref[rows, :] = _pack_rows(v)
        lg_ref[rows, :] = _dot_split(v, wr_ref[...]) + br_ref[...]
    nc_ref[...] = zbuf[pl.ds(r, hrc), :]
    np_ref[...] = pbuf[pl.ds(r, hrp), :]


def _mixer(x2, mod3, hc, hp, weights, row_tile, n_sub, start_pos):
    n = x2.shape[0]
    n_mod, g, _ = mod3.shape
    hrc, hrp = hc.shape[0] // n_mod, hp.shape[0] // n_mod
    tiles_per_seq = n // n_mod // row_tile
    once = dict(pipeline_mode=pl.Buffered(1)) if n_mod == 1 else {}

    def full(a):
        nd = a.ndim
        return pl.BlockSpec(a.shape, lambda i: (0,) * nd)

    def seq_block(rows, **kw):
        return pl.BlockSpec((rows, C), lambda i: (i // tiles_per_seq, 0), **kw)

    def row_block(cols):
        return pl.BlockSpec((row_tile, cols), lambda i: (i, 0))

    return pl.pallas_call(
        functools.partial(_mixer_kernel, g, tiles_per_seq, start_pos, n_sub),
        out_shape=[
            jax.ShapeDtypeStruct((n, D), _f32),
            jax.ShapeDtypeStruct((n, D // 2), jnp.int32),
            jax.ShapeDtypeStruct((n, LANES), _f32),
            jax.ShapeDtypeStruct(hc.shape, _f32),
            jax.ShapeDtypeStruct(hp.shape, _f32),
        ],
        grid=(n // row_tile,),
        in_specs=[row_block(D),
                  pl.BlockSpec((None, g, 6 * D), lambda i: (i // tiles_per_seq, 0, 0), **once),
                  seq_block(hrc, **once), seq_block(hrp, **once)] + [full(a) for a in weights],
        out_specs=[row_block(D), row_block(D // 2), row_block(LANES), seq_block(hrc), seq_block(hrp)],
        scratch_shapes=[pltpu.VMEM((hrc + row_tile, C), _f32), pltpu.VMEM((hrp + row_tile, C), _f32)],
        compiler_params=pltpu.CompilerParams(vmem_limit_bytes=VMEM_LIMIT),
        name="mixer",
    )(x2, mod3, hc, hp, *weights)


def _plan_kernel(lgp_ref, lgs_ref, dest_ref, gate_ref, meta_ref, idx_s, rank_s):
    t = ROW_TILE
    e_iota = lax.broadcasted_iota(jnp.int32, (E, t), 0)
    tri = (lax.broadcasted_iota(jnp.int32, (t, t), 0)
           < lax.broadcasted_iota(jnp.int32, (t, t), 1)).astype(_f32).astype(_bf16)
    zeros_rest = jnp.zeros((LANES - TOP_K, t), _f32)

    def tile_body(lg_ref, off, i, carry):
        lt = lg_ref[i].T[0:E, :]
        vals, idxs = [], []
        for _ in range(TOP_K):
            mx = jnp.max(lt, axis=0, keepdims=True)
            ix = jnp.min(jnp.where(lt == mx, e_iota, E), axis=0, keepdims=True)
            vals.append(mx)
            idxs.append(ix)
            lt = jnp.where(e_iota == ix, -jnp.inf, lt)
        ex = [jnp.exp(vk - vals[0]) for vk in vals]
        den = ex[0] + ex[1] + ex[2] + ex[3]
        gates = [ek / den for ek in ex]
        gate_ref[off + i] = jnp.concatenate(gates + [zeros_rest], axis=0).T

        ohs = [(e_iota == ix) for ix in idxs]
        oh = (ohs[0] | ohs[1] | ohs[2] | ohs[3]).astype(_f32)
        before = _dot(oh.astype(_bf16), tri) + carry
        ranks = [jnp.sum(jnp.where(o, before, 0.0), axis=0, keepdims=True) for o in ohs]
        idx_s[off + i] = jnp.concatenate(idxs + idxs, axis=0)
        rank_s[off + i] = jnp.concatenate(ranks + ranks, axis=0).astype(jnp.int32)
        return carry + jnp.sum(oh, axis=1, keepdims=True)

    n_p, n_s = lgp_ref.shape[0], lgs_ref.shape[0]
    counts = lax.fori_loop(0, n_p, functools.partial(tile_body, lgp_ref, 0), jnp.zeros((E, 1), _f32))
    counts = lax.fori_loop(0, n_s, functools.partial(tile_body, lgs_ref, n_p), counts)
    padded = jnp.ceil(counts / BM) * BM
    low = (lax.broadcasted_iota(jnp.int32, (E, E), 1)
           <= lax.broadcasted_iota(jnp.int32, (E, E), 0)).astype(_f32)
    pad_end = _dot_exact(low, jnp.broadcast_to(padded, (E, LANES)))[:, 0:1]
    pad_start = pad_end - padded

    def dest_body(i, c):
        ix = idx_s[i]
        rk = rank_s[i]
        rows = []
        for k in range(TOP_K):
            st = jnp.sum(jnp.where(e_iota == ix[k:k + 1], pad_start, 0.0), axis=0, keepdims=True)
            rows.append(st.astype(jnp.int32) + rk[k:k + 1])
        dest_ref[i] = jnp.concatenate(rows + rows, axis=0)
        return c

    lax.fori_loop(0, n_p + n_s, dest_body, 0)

    on_lane = (lax.broadcasted_iota(jnp.int32, (E, LANES), 0)
               == lax.broadcasted_iota(jnp.int32, (E, LANES), 1))

    def to_lanes(col):
        return jnp.sum(jnp.where(on_lane, col, 0.0), axis=0, keepdims=True).astype(jnp.int32)

    meta_ref[...] = jnp.concatenate(
        [to_lanes(counts), to_lanes(pad_start / BM), to_lanes(padded / BM),
         jnp.zeros((SUBLANES - 3, LANES), jnp.int32)], axis=0)


def _plan(logits_p, logits_s):
    n_tiles = (logits_p.shape[0] + logits_s.shape[0]) // ROW_TILE
    return pl.pallas_call(
        _plan_kernel,
        out_shape=[
            jax.ShapeDtypeStruct((n_tiles, 2 * TOP_K, ROW_TILE), jnp.int32),
            jax.ShapeDtypeStruct((n_tiles, ROW_TILE, LANES), _f32),
            jax.ShapeDtypeStruct((SUBLANES, LANES), jnp.int32),
        ],
        scratch_shapes=[pltpu.VMEM((n_tiles, 2 * TOP_K, ROW_TILE), jnp.int32),
                        pltpu.VMEM((n_tiles, 2 * TOP_K, ROW_TILE), jnp.int32)],
        compiler_params=pltpu.CompilerParams(vmem_limit_bytes=VMEM_LIMIT),
        name="plan",
    )(logits_p.reshape(-1, ROW_TILE, LANES), logits_s.reshape(-1, ROW_TILE, LANES))


def _dispatch(v_p, v_s, dests, n_rows_out):
    n_p, n_s = v_p.shape[0], v_s.shape[0]
    width = v_p.shape[1]
    w = SC_WINDOW
    n_pw, n_windows = n_p // w, (n_p + n_s) // w
    mesh = plsc.VectorSubcoreMesh(core_axis_name="core", subcore_axis_name="subcore")
    n_workers = mesh.num_cores * mesh.num_subcores

    @functools.partial(
        pl.kernel, mesh=mesh, name="dispatch",
        out_type=jax.ShapeDtypeStruct((n_rows_out, width), jnp.int32),
        scratch_types=[pltpu.VMEM((w, width), jnp.int32)] + [pltpu.VMEM((w,), jnp.int32)] * TOP_K
        + [pltpu.SemaphoreType.DMA])
    def scatter_rows(vp_hbm, vs_hbm, d0_hbm, d1_hbm, d2_hbm, d3_hbm, o_hbm, rows, i0, i1, i2, i3, sem):
        worker = lax.axis_index("subcore") * mesh.num_cores + lax.axis_index("core")
        idx = (i0, i1, i2, i3)

        def scatter_window(c):
            t0 = pl.multiple_of(c * w, w)
            for d_hbm, iv in zip((d0_hbm, d1_hbm, d2_hbm, d3_hbm), idx):
                pltpu.sync_copy(d_hbm.at[pl.ds(t0, w)], iv)
            copies = [pltpu.async_copy(rows, o_hbm.at[iv], sem) for iv in idx]
            for cp in copies:
                cp.wait()

        for j in range(-(-n_windows // n_workers)):
            c = j * n_workers + worker

            @pl.when(c < n_pw)
            def _():
                pltpu.sync_copy(vp_hbm.at[pl.ds(pl.multiple_of(c * w, w), w)], rows)
                scatter_window(c)

            @pl.when((c >= n_pw) & (c < n_windows))
            def _():
                pltpu.sync_copy(vs_hbm.at[pl.ds(pl.multiple_of((c - n_pw) * w, w), w)], rows)
                scatter_window(c)

    return scatter_rows(v_p, v_s, *dests)


def _experts_kernel(cnt_ref, first_ref, nblk_ref, x_hbm, wu_hbm, bu_ref, wd_hbm, bd_ref, y_hbm,
                    xbuf, ybuf, wu_st, wd_st, wu_bf, wd_bf, xsem, ysem, wsem):
    total = first_ref[E - 1] + nblk_ref[E - 1]

    def w_fetch(e):
        return (pltpu.make_async_copy(wu_hbm.at[e], wu_st, wsem.at[0]),
                pltpu.make_async_copy(wd_hbm.at[e], wd_st, wsem.at[1]))

    def x_fetch(b, slot):
        return pltpu.make_async_copy(x_hbm.at[pl.ds(b * BM, BM), :], xbuf.at[slot], xsem.at[slot])

    def y_store(b, slot):
        return pltpu.make_async_copy(ybuf.at[slot], y_hbm.at[pl.ds(b * BM, BM), :], ysem.at[slot])

    for c in w_fetch(0):
        c.start()
    x_fetch(0, 0).start()

    def expert_body(e, carry):
        for c in w_fetch(e):
            c.wait()
        nblk = nblk_ref[e]

        @pl.when(nblk > 0)
        def _():
            wu_bf[...] = wu_st[...].astype(_bf16)
            wd_bf[...] = wd_st[...].astype(_bf16)

        @pl.when(e + 1 < E)
        def _():
            for c in w_fetch(e + 1):
                c.start()

        b_up = bu_ref[e]
        b_down = bd_ref[e]

        def block_body(j, c):
            b = first_ref[e] + j
            slot = b % 2
            x_fetch(b, slot).wait()

            @pl.when(b + 1 < total)
            def _():
                x_fetch(b + 1, 1 - slot).start()

            rows = lax.broadcasted_iota(jnp.int32, (BM, 1), 0)
            x = _unpack_rows(jnp.where(rows < cnt_ref[e] - j * BM, xbuf[slot], 0))
            hcat = _dot(x, wu_bf[...]) + b_up
            glu = jnp.minimum(hcat[:, 0:F], SWIGLU_LIMIT)
            lin = jnp.clip(hcat[:, F:2 * F], -SWIGLU_LIMIT, SWIGLU_LIMIT)
            act = glu * jax.nn.sigmoid(SWIGLU_ALPHA * glu) * (lin + 1.0)
            y = _pack_rows(_dot(act.astype(_bf16), wd_bf[...]) + b_down)

            @pl.when(b >= 2)
            def _():
                y_store(b - 2, slot).wait()

            ybuf[slot] = y
            y_store(b, slot).start()
            return c

        lax.fori_loop(0, nblk, block_body, 0)
        return carry

    lax.fori_loop(0, E, expert_body, 0)

    @pl.when(total >= 2)
    def _():
        y_store(total - 2, total % 2).wait()

    y_store(total - 1, (total - 1) % 2).wait()


def _experts(cnt, first, nblk, xpad, w_up, b_up, w_down, b_down):
    def full(a):
        nd = a.ndim
        return pl.BlockSpec(a.shape, lambda i, *_: (0,) * nd)

    return pl.pallas_call(
        _experts_kernel,
        out_shape=jax.ShapeDtypeStruct(xpad.shape, jnp.int32),
        grid_spec=pltpu.PrefetchScalarGridSpec(
            num_scalar_prefetch=3,
            grid=(1,),
            in_specs=[pl.BlockSpec(memory_space=pl.ANY),
                      pl.BlockSpec(memory_space=pl.ANY), full(b_up),
                      pl.BlockSpec(memory_space=pl.ANY), full(b_down)],
            out_specs=pl.BlockSpec(memory_space=pl.ANY),
            scratch_shapes=[pltpu.VMEM((2, BM, D // 2), jnp.int32), pltpu.VMEM((2, BM, D // 2), jnp.int32),
                            pltpu.VMEM((D, 2 * F), _f32), pltpu.VMEM((F, D), _f32),
                            pltpu.VMEM((D, 2 * F), _bf16), pltpu.VMEM((F, D), _bf16),
                            pltpu.SemaphoreType.DMA((2,)), pltpu.SemaphoreType.DMA((2,)),
                            pltpu.SemaphoreType.DMA((2,))],
        ),
        compiler_params=pltpu.CompilerParams(vmem_limit_bytes=VMEM_LIMIT),
        name="experts",
    )(cnt, first, nblk, xpad, w_up, b_up, w_down, b_down)


def _gather_rows(ypad, dest_all):
    n_out = dest_all.shape[0]
    width = ypad.shape[1]
    w = SC_WINDOW
    mesh = plsc.VectorSubcoreMesh(core_axis_name="core", subcore_axis_name="subcore")
    n_workers = mesh.num_cores * mesh.num_subcores
    n_windows = n_out // w

    @functools.partial(
        pl.kernel, mesh=mesh, name="gather_rows",
        out_type=jax.ShapeDtypeStruct((n_out, width), jnp.int32),
        scratch_types=[pltpu.VMEM((w, width), jnp.int32), pltpu.VMEM((w,), jnp.int32), pltpu.SemaphoreType.DMA])
    def gather_rows(y_hbm, d_hbm, o_hbm, rows, iv, sem):
        worker = lax.axis_index("subcore") * mesh.num_cores + lax.axis_index("core")

        @pl.loop(0, n_windows // n_workers)
        def _(j):
            r0 = pl.multiple_of((j * n_workers + worker) * w, w)
            pltpu.sync_copy(d_hbm.at[pl.ds(r0, w)], iv)
            pltpu.async_copy(y_hbm.at[iv], rows, sem).wait()
            pltpu.sync_copy(rows, o_hbm.at[pl.ds(r0, w)])

    assert n_windows % n_workers == 0
    return gather_rows(ypad, dest_all)


def _combine_kernel(h_ref, mod_ref, gate_ref, g2_ref, b2_ref, y_ref, *aliased_and_out):
    o_ref = aliased_and_out[-1]
    gates = gate_ref[...]
    half = D // 2
    f_hi = jnp.zeros((h_ref.shape[0], half), _f32)
    f_lo = jnp.zeros((h_ref.shape[0], half), _f32)
    for k in range(TOP_K):
        p = y_ref[k]
        gk = gates[:, k:k + 1]
        f_hi = f_hi + gk * lax.bitcast_convert_type(p & jnp.int32(-65536), _f32)
        f_lo = f_lo + gk * lax.bitcast_convert_type(lax.shift_left(p, 16), _f32)
    f = jnp.concatenate([f_hi, f_lo], axis=1)
    gate2 = mod_ref[...][:, 5 * D:6 * D]
    pre = ALPHA * h_ref[...] + _per_seq(f, gate2, jnp.multiply)
    o_ref[...] = _layer_norm(pre, g2_ref[...], b2_ref[...])


def _combine(h, mod3, gates, ln2_g, ln2_b, y4, token0, rows_per_mod, row0, out_so_far):
    t = COMBINE_TILE
    hoff = row0 // t
    goff = (token0 + row0) // t
    g = mod3.shape[1]
    tiles_per_mod = rows_per_mod // t
    in_specs = [pl.BlockSpec((t, D), lambda i: (i + hoff, 0)),
                pl.BlockSpec((None, g, 6 * D), lambda i: ((i + hoff) // tiles_per_mod, 0, 0)),
                pl.BlockSpec((t, LANES), lambda i: (i + goff, 0)),
                pl.BlockSpec((1, D), lambda i: (0, 0)),
                pl.BlockSpec((1, D), lambda i: (0, 0)),
                pl.BlockSpec((TOP_K, t, D // 2), lambda i: (0, i, 0))]
    args = [h, mod3, gates, ln2_g, ln2_b, y4]
    aliases = {}
    if out_so_far is not None:
        in_specs.append(pl.BlockSpec(memory_space=pl.ANY))
        args.append(out_so_far)
        aliases = {len(args) - 1: 0}
    return pl.pallas_call(
        _combine_kernel,
        out_shape=jax.ShapeDtypeStruct(h.shape, _f32),
        grid=(y4.shape[1] // t,),
        in_specs=in_specs,
        out_specs=pl.BlockSpec((t, D), lambda i: (i + hoff, 0)),
        input_output_aliases=aliases,
        compiler_params=pltpu.CompilerParams(vmem_limit_bytes=VMEM_LIMIT),
        name="combine",
    )(*args)


def _time_major(a):
    return a.transpose(1, 0, 2)


def kernel(x_prompt, x_sample, c_prompt, c_sample, state_conv, state_pool, w_ada, b_ada, w_in,
           conv_w, w_out_a, w_pool, ls_pool, w_out_b, w_o, ln1_g, ln1_b, w_router, b_router,
           w_up, b_up, w_down, b_down, ln2_g, ln2_b):
    n_seq_p, seq, _ = x_prompt.shape
    n_seq_s, dec_seq, _ = x_sample.shape
    n_p, n_s = n_seq_p * seq, n_seq_s * dec_seq
    n = n_p + n_s
    n_blocks = TOP_K * n // BM + E
    l = 0

    mod = _ada(jnp.concatenate([c_prompt, c_sample], axis=0), w_ada[l], b_ada[l][None])
    mod_p = mod[:n_seq_p][:, None, :]
    mod_s = mod[n_seq_p:][None]

    weights = (
        w_in[l].astype(_bf16), conv_w[l], w_out_a[l].astype(_bf16), w_pool[l].astype(_bf16),
        ls_pool[l][None], w_out_b[l].astype(_bf16), w_o[l].astype(_bf16), ln1_g[l][None], ln1_b[l][None],
        jnp.pad(w_router[l], ((0, 0), (0, LANES - E))), jnp.pad(b_router[l], (0, LANES - E))[None],
    )
    hc_p, hp_p = _hist_steps(CONV_HIST, 1), _hist_steps(POOL_HIST, 1)
    h_p, v_p, lg_p, nc_p, np_p = _mixer(
        x_prompt.reshape(n_p, D), mod_p, jnp.zeros((n_seq_p * hc_p, C), _f32),
        jnp.zeros((n_seq_p * hp_p, C), _f32), weights, PROMPT_ROW_TILE, PROMPT_SUB_TILES, 0)
    hc_s, hp_s = _hist_steps(CONV_HIST, n_seq_s), _hist_steps(POOL_HIST, n_seq_s)
    hist_c = jnp.pad(_time_major(state_conv[l]), ((hc_s - CONV_HIST, 0), (0, 0), (0, 0)))
    hist_p = jnp.pad(_time_major(state_pool[l]), ((hp_s - POOL_HIST, 0), (0, 0), (0, 0)))
    h_s, v_s, lg_s, nc_s, np_s = _mixer(
        _time_major(x_sample).reshape(n_s, D), mod_s, hist_c.reshape(hc_s * n_seq_s, C),
        hist_p.reshape(hp_s * n_seq_s, C), weights, SAMPLE_ROW_TILE, 1, PAST_LEN)

    dest8, gates3, meta = _plan(lg_p, lg_s)
    cnt, first, nblk = meta[0, :E], meta[1, :E], meta[2, :E]
    gates = gates3.reshape(n, LANES)

    dests = [dest8[:, k, :].reshape(n) for k in range(TOP_K)]
    xpad = _dispatch(v_p, v_s, dests, n_blocks * BM)
    ypad = _experts(cnt, first, nblk, xpad, w_up[l], b_up[l][:, None, :], w_down[l], b_down[l][:, None, :])

    def gathered(t0, rows):
        idx = jnp.concatenate([dk[t0:t0 + rows] for dk in dests])
        return _gather_rows(ypad, idx).reshape(TOP_K, rows, D // 2)

    chunk = n_p // COMBINE_CHUNKS
    y4_p = [gathered(c * chunk, chunk) for c in range(COMBINE_CHUNKS)]
    y4_s = gathered(n_p, n_s)
    g2, b2 = ln2_g[l][None], ln2_b[l][None]
    y_p = None
    for c in range(COMBINE_CHUNKS):
        y_p = _combine(h_p, mod_p, gates, g2, b2, y4_p[c], 0, seq, c * chunk, y_p)
    y_s = _combine(h_s, mod_s, gates, g2, b2, y4_s, n_p, n_s, 0, None)

    y_prompt = y_p.reshape(n_seq_p, seq, D)
    y_sample = _time_major(y_s.reshape(dec_seq, n_seq_s, D))
    new_conv_p = nc_p.reshape(n_seq_p, hc_p, C)[:, hc_p - CONV_HIST:][None]
    new_pool_p = np_p.reshape(n_seq_p, hp_p, C)[:, hp_p - POOL_HIST:][None]
    new_conv_s = _time_major(nc_s.reshape(hc_s, n_seq_s, C)[hc_s - CONV_HIST:])[None]
    new_pool_s = _time_major(np_s.reshape(hp_s, n_seq_s, C)[hp_s - POOL_HIST:])[None]
    return (y_prompt, y_sample, new_conv_p, new_pool_p, new_conv_s, new_pool_s)
```

```python
import functools

import jax
import jax.numpy as jnp
from jax import lax
from jax.experimental import pallas as pl
from jax.experimental.pallas import tpu as pltpu
from jax.experimental.pallas import tpu_sc as plsc

D = 1024
C = 512
N_GROUPS = 4
GROUP = C // N_GROUPS
CONV_HIST = 2
POOL_HIST = 15
E = 32
TOP_K = 4
F = 1024
SWIGLU_LIMIT = 7.0
SWIGLU_ALPHA = 1.702
LN_EPS = 1e-5
DEPTH = 1
ALPHA = (2 * DEPTH) ** 0.25
PAST_LEN = 16384

LANES = 128
SUBLANES = 8
ROW_TILE = 512
PROMPT_ROW_TILE = 1024
PROMPT_SUB_TILES = 2
SAMPLE_ROW_TILE = 256
BM = 1024
COMBINE_TILE = 256
COMBINE_CHUNKS = 4
SC_WINDOW = 128
VMEM_LIMIT = 56 * 1024 * 1024

_f32 = jnp.float32
_bf16 = jnp.bfloat16


def _dot(a, b):
    return jnp.dot(a, b, preferred_element_type=_f32)


def _dot_exact(a, b):
    return lax.dot_general(a, b, (((1,), (0,)), ((), ())),
                           precision=lax.Precision.HIGHEST, preferred_element_type=_f32)


def _dot_split(a, b):
    a_hi, b_hi = a.astype(_bf16), b.astype(_bf16)
    a_lo = (a - a_hi.astype(_f32)).astype(_bf16)
    b_lo = (b - b_hi.astype(_f32)).astype(_bf16)
    return _dot(a_hi, b_hi) + _dot(a_lo, b_hi) + _dot(a_hi, b_lo)


def _pack_rows(x):
    w = x.shape[1] // 2
    hi = lax.bitcast_convert_type(x[:, :w].astype(_bf16).astype(_f32), jnp.int32)
    lo = lax.bitcast_convert_type(x[:, w:].astype(_bf16).astype(_f32), jnp.int32)
    return hi | lax.shift_right_logical(lo, 16)


def _unpack_rows(p):
    hi = lax.bitcast_convert_type(p & jnp.int32(-65536), _f32)
    lo = lax.bitcast_convert_type(lax.shift_left(p, 16), _f32)
    return jnp.concatenate([hi, lo], axis=1).astype(_bf16)


def _per_seq(x, m, op):
    g = m.shape[0]
    if g == 1:
        return op(x, m)
    r, n = x.shape
    return op(x.reshape(r // g, g, n), m[None]).reshape(r, n)


def _layer_norm(x, g, b):
    mu = jnp.mean(x, axis=-1, keepdims=True)
    xc = x - mu
    var = jnp.mean(xc * xc, axis=-1, keepdims=True)
    return xc * lax.rsqrt(var + LN_EPS) * g + b


def _hist_steps(needed, g):
    return -(-needed * g // SUBLANES) * SUBLANES // g


def _ada_kernel(c_ref, w_ref, b_ref, o_ref):
    c = c_ref[...]
    o_ref[...] = _dot_exact(c * jax.nn.sigmoid(c), w_ref[...]) + b_ref[...]


def _ada(c, w_ada, b_ada):
    rows = c.shape[0]
    cols = w_ada.shape[1]
    bn = 1536
    return pl.pallas_call(
        _ada_kernel,
        out_shape=jax.ShapeDtypeStruct((rows, cols), _f32),
        grid=(cols // bn,),
        in_specs=[pl.BlockSpec((rows, D), lambda j: (0, 0)),
                  pl.BlockSpec((D, bn), lambda j: (0, j)),
                  pl.BlockSpec((1, bn), lambda j: (0, j))],
        out_specs=pl.BlockSpec((rows, bn), lambda j: (0, j)),
        compiler_params=pltpu.CompilerParams(vmem_limit_bytes=VMEM_LIMIT),
        name="ada",
    )(c, w_ada, b_ada)


def _mixer_kernel(g, tiles_per_seq, start_pos, n_sub,
                  x_ref, mod_ref, hc_ref, hp_ref,
                  win_ref, cw_ref, woa_ref, wpool_ref, ls_ref, wob_ref, wo_ref, g1_ref, b1_ref,
                  wr_ref, br_ref,
                  h_ref, v_ref, lg_ref, nc_ref, np_ref, zbuf, pbuf):
    r = x_ref.shape[0]
    hrc = hc_ref.shape[0]
    hrp = hp_ref.shape[0]
    j = pl.program_id(0) % tiles_per_seq

    @pl.when(j == 0)
    def _():
        zbuf[pl.ds(0, hrc), :] = hc_ref[...]
        pbuf[pl.ds(0, hrp), :] = hp_ref[...]

    @pl.when(j != 0)
    def _():
        zt = zbuf[pl.ds(r, hrc), :]
        pt = pbuf[pl.ds(r, hrp), :]
        zbuf[pl.ds(0, hrc), :] = zt
        pbuf[pl.ds(0, hrp), :] = pt

    m = mod_ref[...]
    shift1, scale1, gate1 = m[:, 0:D], m[:, D:2 * D], m[:, 2 * D:3 * D]
    shift2, scale2 = m[:, 3 * D:4 * D], m[:, 4 * D:5 * D]

    rs = r // n_sub
    for s in range(n_sub):
        rows = pl.ds(s * rs, rs)
        x = x_ref[rows, :]
        u = _per_seq(_per_seq(x, 1.0 + scale1, jnp.multiply), shift1, jnp.add).astype(_bf16)

        z = _dot(u, win_ref[:, C:2 * C]) * _dot(u, win_ref[:, 2 * C:3 * C])
        zrow = hrc + s * rs
        zbuf[pl.ds(zrow, rs), :] = z
        cw = cw_ref[...]
        conv = (cw[0:1] * zbuf[pl.ds(zrow - 2 * g, rs), :] + cw[1:2] * zbuf[pl.ds(zrow - g, rs), :]
                + cw[2:3] * z)
        y_a = _dot((_dot(u, win_ref[:, 0:C]) * conv).astype(_bf16), woa_ref[...])

        xp = _dot(u, win_ref[:, 3 * C:4 * C])
        prow = hrp + s * rs
        pbuf[pl.ds(prow, rs), :] = xp
        pos = (start_pos + j * (r // g) + s * (rs // g)
               + lax.broadcasted_iota(jnp.int32, (rs, 1), 0) // g)
        acc = xp
        yg = []
        for grp in range(N_GROUPS):
            lo = grp * GROUP
            wdw = 2 ** (grp + 1)
            for back in range(wdw // 2, wdw):
                sh = pbuf[pl.ds(prow - back * g, rs), lo:C]
                acc = jnp.concatenate([acc[:, 0:lo], acc[:, lo:C] + sh], axis=1) if lo else acc + sh
            cnt = jnp.minimum(wdw, pos + 1).astype(_f32)
            diff = acc[:, lo:lo + GROUP] / cnt - xp[:, lo:lo + GROUP]
            yg.append(_dot(diff.astype(_bf16), wpool_ref[grp]))
        y_b = _dot((jnp.concatenate(yg, axis=1) * ls_ref[...]).astype(_bf16), wob_ref[...])

        g_a = _dot(u, win_ref[:, 4 * C:4 * C + D])
        g_b = _dot(u, win_ref[:, 4 * C + D:4 * C + 2 * D])
        merged = jax.nn.sigmoid(g_a) * y_a + jax.nn.sigmoid(g_b) * y_b
        o = _dot(merged.astype(_bf16), wo_ref[...])
        h = _layer_norm(ALPHA * x + _per_seq(o, gate1, jnp.multiply), g1_ref[...], b1_ref[...])
        v = _per_seq(_per_seq(h, 1.0 + scale2, jnp.multiply), shift2, jnp.add)
        h_ref[rows, :] = h
        v_ref[rows, :] = _pack_rows(v)
        lg_ref[rows, :] = _dot_split(v, wr_ref[...]) + br_ref[...]
    nc_ref[...] = zbuf[pl.ds(r, hrc), :]
    np_ref[...] = pbuf[pl.ds(r, hrp), :]


def _mixer(x2, mod3, hc, hp, weights, row_tile, n_sub, start_pos):
    n = x2.shape[0]
    n_mod, g, _ = mod3.shape
    hrc, hrp = hc.shape[0] // n_mod, hp.shape[0] // n_mod
    tiles_per_seq = n // n_mod // row_tile
    once = dict(pipeline_mode=pl.Buffered(1)) if n_mod == 1 else {}

    def full(a):
        nd = a.ndim
        return pl.BlockSpec(a.shape, lambda i: (0,) * nd)

    def seq_block(rows, **kw):
        return pl.BlockSpec((rows, C), lambda i: (i // tiles_per_seq, 0), **kw)

    def row_block(cols):
        return pl.BlockSpec((row_tile, cols), lambda i: (i, 0))

    return pl.pallas_call(
        functools.partial(_mixer_kernel, g, tiles_per_seq, start_pos, n_sub),
        out_shape=[
            jax.ShapeDtypeStruct((n, D), _f32),
            jax.ShapeDtypeStruct((n, D // 2), jnp.int32),
            jax.ShapeDtypeStruct((n, LANES), _f32),
            jax.ShapeDtypeStruct(hc.shape, _f32),
            jax.ShapeDtypeStruct(hp.shape, _f32),
        ],
        grid=(n // row_tile,),
        in_specs=[row_block(D),
                  pl.BlockSpec((None, g, 6 * D), lambda i: (i // tiles_per_seq, 0, 0), **once),
                  seq_block(hrc, **once), seq_block(hrp, **once)] + [full(a) for a in weights],
        out_specs=[row_block(D), row_block(D // 2), row_block(LANES), seq_block(hrc), seq_block(hrp)],
        scratch_shapes=[pltpu.VMEM((hrc + row_tile, C), _f32), pltpu.VMEM((hrp + row_tile, C), _f32)],
        compiler_params=pltpu.CompilerParams(vmem_limit_bytes=VMEM_LIMIT),
        name="mixer",
    )(x2, mod3, hc, hp, *weights)


def _plan_kernel(lgp_ref, lgs_ref, dest_ref, gate_ref, meta_ref, idx_s, rank_s):
    t = ROW_TILE
    e_iota = lax.broadcasted_iota(jnp.int32, (E, t), 0)
    tri = (lax.broadcasted_iota(jnp.int32, (t, t), 0)
           < lax.broadcasted_iota(jnp.int32, (t, t), 1)).astype(_f32).astype(_bf16)
    zeros_rest = jnp.zeros((LANES - TOP_K, t), _f32)

    def tile_body(lg_ref, off, i, carry):
        lt = lg_ref[i].T[0:E, :]
        vals, idxs = [], []
        for _ in range(TOP_K):
            mx = jnp.max(lt, axis=0, keepdims=True)
            ix = jnp.min(jnp.where(lt == mx, e_iota, E), axis=0, keepdims=True)
            vals.append(mx)
            idxs.append(ix)
            lt = jnp.where(e_iota == ix, -jnp.inf, lt)
        ex = [jnp.exp(vk - vals[0]) for vk in vals]
        den = ex[0] + ex[1] + ex[2] + ex[3]
        gates = [ek / den for ek in ex]
        gate_ref[off + i] = jnp.concatenate(gates + [zeros_rest], axis=0).T

        ohs = [(e_iota == ix) for ix in idxs]
        oh = (ohs[0] | ohs[1] | ohs[2] | ohs[3]).astype(_f32)
        before = _dot(oh.astype(_bf16), tri) + carry
        ranks = [jnp.sum(jnp.where(o, before, 0.0), axis=0, keepdims=True) for o in ohs]
        idx_s[off + i] = jnp.concatenate(idxs + idxs, axis=0)
        rank_s[off + i] = jnp.concatenate(ranks + ranks, axis=0).astype(jnp.int32)
        return carry + jnp.sum(oh, axis=1, keepdims=True)

    n_p, n_s = lgp_ref.shape[0], lgs_ref.shape[0]
    counts = lax.fori_loop(0, n_p, functools.partial(tile_body, lgp_ref, 0), jnp.zeros((E, 1), _f32))
    counts = lax.fori_loop(0, n_s, functools.partial(tile_body, lgs_ref, n_p), counts)
    padded = jnp.ceil(counts / BM) * BM
    low = (lax.broadcasted_iota(jnp.int32, (E, E), 1)
           <= lax.broadcasted_iota(jnp.int32, (E, E), 0)).astype(_f32)
    pad_end = _dot_exact(low, jnp.broadcast_to(padded, (E, LANES)))[:, 0:1]
    pad_start = pad_end - padded

    def dest_body(i, c):
        ix = idx_s[i]
        rk = rank_s[i]
        rows = []
        for k in range(TOP_K):
            st = jnp.sum(jnp.where(e_iota == ix[k:k + 1], pad_start, 0.0), axis=0, keepdims=True)
            rows.append(st.astype(jnp.int32) + rk[k:k + 1])
        dest_ref[i] = jnp.concatenate(rows + rows, axis=0)
        return c

    lax.fori_loop(0, n_p + n_s, dest_body, 0)

    on_lane = (lax.broadcasted_iota(jnp.int32, (E, LANES), 0)
               == lax.broadcasted_iota(jnp.int32, (E, LANES), 1))

    def to_lanes(col):
        return jnp.sum(jnp.where(on_lane, col, 0.0), axis=0, keepdims=True).astype(jnp.int32)

    meta_ref[...] = jnp.concatenate(
        [to_lanes(counts), to_lanes(pad_start / BM), to_lanes(padded / BM),
         jnp.zeros((SUBLANES - 3, LANES), jnp.int32)], axis=0)


def _plan(logits_p, logits_s):
    n_tiles = (logits_p.shape[0] + logits_s.shape[0]) // ROW_TILE
    return pl.pallas_call(
        _plan_kernel,
        out_shape=[
            jax.ShapeDtypeStruct((n_tiles, 2 * TOP_K, ROW_TILE), jnp.int32),
            jax.ShapeDtypeStruct((n_tiles, ROW_TILE, LANES), _f32),
            jax.ShapeDtypeStruct((SUBLANES, LANES), jnp.int32),
        ],
        scratch_shapes=[pltpu.VMEM((n_tiles, 2 * TOP_K, ROW_TILE), jnp.int32),
                        pltpu.VMEM((n_tiles, 2 * TOP_K, ROW_TILE), jnp.int32)],
        compiler_params=pltpu.CompilerParams(vmem_limit_bytes=VMEM_LIMIT),
        name="plan",
    )(logits_p.reshape(-1, ROW_TILE, LANES), logits_s.reshape(-1, ROW_TILE, LANES))


def _dispatch(v_p, v_s, dests, n_rows_out):
    n_p, n_s = v_p.shape[0], v_s.shape[0]
    width = v_p.shape[1]
    w = SC_WINDOW
    n_pw, n_windows = n_p // w, (n_p + n_s) // w
    mesh = plsc.VectorSubcoreMesh(core_axis_name="core", subcore_axis_name="subcore")
    n_workers = mesh.num_cores * mesh.num_subcores

    @functools.partial(
        pl.kernel, mesh=mesh, name="dispatch",
        out_type=jax.ShapeDtypeStruct((n_rows_out, width), jnp.int32),
        scratch_types=[pltpu.VMEM((w, width), jnp.int32)] + [pltpu.VMEM((w,), jnp.int32)] * TOP_K
        + [pltpu.SemaphoreType.DMA])
    def scatter_rows(vp_hbm, vs_hbm, d0_hbm, d1_hbm, d2_hbm, d3_hbm, o_hbm, rows, i0, i1, i2, i3, sem):
        worker = lax.axis_index("subcore") * mesh.num_cores + lax.axis_index("core")
        idx = (i0, i1, i2, i3)

        def scatter_window(c):
            t0 = pl.multiple_of(c * w, w)
            for d_hbm, iv in zip((d0_hbm, d1_hbm, d2_hbm, d3_hbm), idx):
                pltpu.sync_copy(d_hbm.at[pl.ds(t0, w)], iv)
            copies = [pltpu.async_copy(rows, o_hbm.at[iv], sem) for iv in idx]
            for cp in copies:
                cp.wait()

        for j in range(-(-n_windows // n_workers)):
            c = j * n_workers + worker

            @pl.when(c < n_pw)
            def _():
                pltpu.sync_copy(vp_hbm.at[pl.ds(pl.multiple_of(c * w, w), w)], rows)
                scatter_window(c)

            @pl.when((c >= n_pw) & (c < n_windows))
            def _():
                pltpu.sync_copy(vs_hbm.at[pl.ds(pl.multiple_of((c - n_pw) * w, w), w)], rows)
                scatter_window(c)

    return scatter_rows(v_p, v_s, *dests)


def _experts_kernel(cnt_ref, first_ref, nblk_ref, x_hbm, wu_hbm, bu_ref, wd_hbm, bd_ref, y_hbm,
                    xbuf, ybuf, wu_st, wd_st, wu_bf, wd_bf, xsem, ysem, wsem):
    total = first_ref[E - 1] + nblk_ref[E - 1]

    def w_fetch(e):
        return (pltpu.make_async_copy(wu_hbm.at[e], wu_st, wsem.at[0]),
                pltpu.make_async_copy(wd_hbm.at[e], wd_st, wsem.at[1]))

    def x_fetch(b, slot):
        return pltpu.make_async_copy(x_hbm.at[pl.ds(b * BM, BM), :], xbuf.at[slot], xsem.at[slot])

    def y_store(b, slot):
        return pltpu.make_async_copy(ybuf.at[slot], y_hbm.at[pl.ds(b * BM, BM), :], ysem.at[slot])

    for c in w_fetch(0):
        c.start()
    x_fetch(0, 0).start()

    def expert_body(e, carry):
        for c in w_fetch(e):
            c.wait()
        nblk = nblk_ref[e]

        @pl.when(nblk > 0)
        def _():
            wu_bf[...] = wu_st[...].astype(_bf16)
            wd_bf[...] = wd_st[...].astype(_bf16)

        @pl.when(e + 1 < E)
        def _():
            for c in w_fetch(e + 1):
                c.start()

        b_up = bu_ref[e]
        b_down = bd_ref[e]

        def block_body(j, c):
            b = first_ref[e] + j
            slot = b % 2
            x_fetch(b, slot).wait()

            @pl.when(b + 1 < total)
            def _():
                x_fetch(b + 1, 1 - slot).start()

            rows = lax.broadcasted_iota(jnp.int32, (BM, 1), 0)
            x = _unpack_rows(jnp.where(rows < cnt_ref[e] - j * BM, xbuf[slot], 0))
            hcat = _dot(x, wu_bf[...]) + b_up
            glu = jnp.minimum(hcat[:, 0:F], SWIGLU_LIMIT)
            lin = jnp.clip(hcat[:, F:2 * F], -SWIGLU_LIMIT, SWIGLU_LIMIT)
            act = glu * jax.nn.sigmoid(SWIGLU_ALPHA * glu) * (lin + 1.0)
            y = _pack_rows(_dot(act.astype(_bf16), wd_bf[...]) + b_down)

            @pl.when(b >= 2)
            def _():
                y_store(b - 2, slot).wait()

            ybuf[slot] = y
            y_store(b, slot).start()
            return c

        lax.fori_loop(0, nblk, block_body, 0)
        return carry

    lax.fori_loop(0, E, expert_body, 0)

    @pl.when(total >= 2)
    def _():
        y_store(total - 2, total % 2).wait()

    y_store(total - 1, (total - 1) % 2).wait()


def _experts(cnt, first, nblk, xpad, w_up, b_up, w_down, b_down):
    def full(a):
        nd = a.ndim
        return pl.BlockSpec(a.shape, lambda i, *_: (0,) * nd)

    return pl.pallas_call(
        _experts_kernel,
        out_shape=jax.ShapeDtypeStruct(xpad.shape, jnp.int32),
        grid_spec=pltpu.PrefetchScalarGridSpec(
            num_scalar_prefetch=3,
            grid=(1,),
            in_specs=[pl.BlockSpec(memory_space=pl.ANY),
                      pl.BlockSpec(memory_space=pl.ANY), full(b_up),
                      pl.BlockSpec(memory_space=pl.ANY), full(b_down)],
            out_specs=pl.BlockSpec(memory_space=pl.ANY),
            scratch_shapes=[pltpu.VMEM((2, BM, D // 2), jnp.int32), pltpu.VMEM((2, BM, D // 2), jnp.int32),
                            pltpu.VMEM((D, 2 * F), _f32), pltpu.VMEM((F, D), _f32),
                            pltpu.VMEM((D, 2 * F), _bf16), pltpu.VMEM((F, D), _bf16),
                            pltpu.SemaphoreType.DMA((2,)), pltpu.SemaphoreType.DMA((2,)),
                            pltpu.SemaphoreType.DMA((2,))],
        ),
        compiler_params=pltpu.CompilerParams(vmem_limit_bytes=VMEM_LIMIT),
        name="experts",
    )(cnt, first, nblk, xpad, w_up, b_up, w_down, b_down)


def _gather_rows(ypad, dest_all):
    n_out = dest_all.shape[0]
    width = ypad.shape[1]
    w = SC_WINDOW
    mesh = plsc.VectorSubcoreMesh(core_axis_name="core", subcore_axis_name="subcore")
    n_workers = mesh.num_cores * mesh.num_subcores
    n_windows = n_out // w

    @functools.partial(
        pl.kernel, mesh=mesh, name="gather_rows",
        out_type=jax.ShapeDtypeStruct((n_out, width), jnp.int32),
        scratch_types=[pltpu.VMEM((w, width), jnp.int32), pltpu.VMEM((w,), jnp.int32), pltpu.SemaphoreType.DMA])
    def gather_rows(y_hbm, d_hbm, o_hbm, rows, iv, sem):
        worker = lax.axis_index("subcore") * mesh.num_cores + lax.axis_index("core")

        @pl.loop(0, n_windows // n_workers)
        def _(j):
            r0 = pl.multiple_of((j * n_workers + worker) * w, w)
            pltpu.sync_copy(d_hbm.at[pl.ds(r0, w)], iv)
            pltpu.async_copy(y_hbm.at[iv], rows, sem).wait()
            pltpu.sync_copy(rows, o_hbm.at[pl.ds(r0, w)])

    assert n_windows % n_workers == 0
    return gather_rows(ypad, dest_all)


def _combine_kernel(h_ref, mod_ref, gate_ref, g2_ref, b2_ref, y_ref, *aliased_and_out):
    o_ref = aliased_and_out[-1]
    gates = gate_ref[...]
    half = D // 2
    f_hi = jnp.zeros((h_ref.shape[0], half), _f32)
    f_lo = jnp.zeros((h_ref.shape[0], half), _f32)
    for k in range(TOP_K):
        p = y_ref[k]
        gk = gates[:, k:k + 1]
        f_hi = f_hi + gk * lax.bitcast_convert_type(p & jnp.int32(-65536), _f32)
        f_lo = f_lo + gk * lax.bitcast_convert_type(lax.shift_left(p, 16), _f32)
    f = jnp.concatenate([f_hi, f_lo], axis=1)
    gate2 = mod_ref[...][:, 5 * D:6 * D]
    pre = ALPHA * h_ref[...] + _per_seq(f, gate2, jnp.multiply)
    o_ref[...] = _layer_norm(pre, g2_ref[...], b2_ref[...])


def _combine(h, mod3, gates, ln2_g, ln2_b, y4, token0, rows_per_mod, row0, out_so_far):
    t = COMBINE_TILE
    hoff = row0 // t
    goff = (token0 + row0) // t
    g = mod3.shape[1]
    tiles_per_mod = rows_per_mod // t
    in_specs = [pl.BlockSpec((t, D), lambda i: (i + hoff, 0)),
                pl.BlockSpec((None, g, 6 * D), lambda i: ((i + hoff) // tiles_per_mod, 0, 0)),
                pl.BlockSpec((t, LANES), lambda i: (i + goff, 0)),
                pl.BlockSpec((1, D), lambda i: (0, 0)),
                pl.BlockSpec((1, D), lambda i: (0, 0)),
                pl.BlockSpec((TOP_K, t, D // 2), lambda i: (0, i, 0))]
    args = [h, mod3, gates, ln2_g, ln2_b, y4]
    aliases = {}
    if out_so_far is not None:
        in_specs.append(pl.BlockSpec(memory_space=pl.ANY))
        args.append(out_so_far)
        aliases = {len(args) - 1: 0}
    return pl.pallas_call(
        _combine_kernel,
        out_shape=jax.ShapeDtypeStruct(h.shape, _f32),
        grid=(y4.shape[1] // t,),
        in_specs=in_specs,
        out_specs=pl.BlockSpec((t, D), lambda i: (i + hoff, 0)),
        input_output_aliases=aliases,
        compiler_params=pltpu.CompilerParams(vmem_limit_bytes=VMEM_LIMIT),
        name="combine",
    )(*args)


def _time_major(a):
    return a.transpose(1, 0, 2)


def kernel(x_prompt, x_sample, c_prompt, c_sample, state_conv, state_pool, w_ada, b_ada, w_in,
           conv_w, w_out_a, w_pool, ls_pool, w_out_b, w_o, ln1_g, ln1_b, w_router, b_router,
           w_up, b_up, w_down, b_down, ln2_g, ln2_b):
    n_seq_p, seq, _ = x_prompt.shape
    n_seq_s, dec_seq, _ = x_sample.shape
    n_p, n_s = n_seq_p * seq, n_seq_s * dec_seq
    n = n_p + n_s
    n_blocks = TOP_K * n // BM + E
    l = 0

    mod = _ada(jnp.concatenate([c_prompt, c_sample], axis=0), w_ada[l], b_ada[l][None])
    mod_p = mod[:n_seq_p][:, None, :]
    mod_s = mod[n_seq_p:][None]

    weights = (
        w_in[l].astype(_bf16), conv_w[l], w_out_a[l].astype(_bf16), w_pool[l].astype(_bf16),
        ls_pool[l][None], w_out_b[l].astype(_bf16), w_o[l].astype(_bf16), ln1_g[l][None], ln1_b[l][None],
        jnp.pad(w_router[l], ((0, 0), (0, LANES - E))), jnp.pad(b_router[l], (0, LANES - E))[None],
    )
    hc_p, hp_p = _hist_steps(CONV_HIST, 1), _hist_steps(POOL_HIST, 1)
    h_p, v_p, lg_p, nc_p, np_p = _mixer(
        x_prompt.reshape(n_p, D), mod_p, jnp.zeros((n_seq_p * hc_p, C), _f32),
        jnp.zeros((n_seq_p * hp_p, C), _f32), weights, PROMPT_ROW_TILE, PROMPT_SUB_TILES, 0)
    hc_s, hp_s = _hist_steps(CONV_HIST, n_seq_s), _hist_steps(POOL_HIST, n_seq_s)
    hist_c = jnp.pad(_time_major(state_conv[l]), ((hc_s - CONV_HIST, 0), (0, 0), (0, 0)))
    hist_p = jnp.pad(_time_major(state_pool[l]), ((hp_s - POOL_HIST, 0), (0, 0), (0, 0)))
    h_s, v_s, lg_s, nc_s, np_s = _mixer(
        _time_major(x_sample).reshape(n_s, D), mod_s, hist_c.reshape(hc_s * n_seq_s, C),
        hist_p.reshape(hp_s * n_seq_s, C), weights, SAMPLE_ROW_TILE, 1, PAST_LEN)

    dest8, gates3, meta = _plan(lg_p, lg_s)
    cnt, first, nblk = meta[0, :E], meta[1, :E], meta[2, :E]
    gates = gates3.reshape(n, LANES)

    dests = [dest8[:, k, :].reshape(n) for k in range(TOP_K)]
    xpad = _dispatch(v_p, v_s, dests, n_blocks * BM)
    ypad = _experts(cnt, first, nblk, xpad, w_up[l], b_up[l][:, None, :], w_down[l], b_down[l][:, None, :])

    def gathered(t0, rows):
        idx = jnp.concatenate([dk[t0:t0 + rows] for dk in dests])
        return _gather_rows(ypad, idx).reshape(TOP_K, rows, D // 2)

    chunk = n_p // COMBINE_CHUNKS
    y4_p = [gathered(c * chunk, chunk) for c in range(COMBINE_CHUNKS)]
    y4_s = gathered(n_p, n_s)
    g2, b2 = ln2_g[l][None], ln2_b[l][None]
    y_p = None
    for c in range(COMBINE_CHUNKS):
        y_p = _combine(h_p, mod_p, gates, g2, b2, y4_p[c], 0, seq, c * chunk, y_p)
    y_s = _combine(h_s, mod_s, gates, g2, b2, y4_s, n_p, n_s, 0, None)

    y_prompt = y_p.reshape(n_seq_p, seq, D)
    y_sample = _time_major(y_s.reshape(dec_seq, n_seq_s, D))
    new_conv_p = nc_p.reshape(n_seq_p, hc_p, C)[:, hc_p - CONV_HIST:][None]
    new_pool_p = np_p.reshape(n_seq_p, hp_p, C)[:, hp_p - POOL_HIST:][None]
    new_conv_s = _time_major(nc_s.reshape(hc_s, n_seq_s, C)[hc_s - CONV_HIST:])[None]
    new_pool_s = _time_major(np_s.reshape(hp_s, n_seq_s, C)[hp_s - POOL_HIST:])[None]
    return (y_prompt, y_sample, new_conv_p, new_pool_p, new_conv_s, new_pool_s)
```

```python
import functools

import jax
import jax.numpy as jnp
from jax import lax
from jax.experimental import pallas as pl
from jax.experimental.pallas import tpu as pltpu
from jax.experimental.pallas import tpu_sc as plsc

D = 1024
C = 512
N_GROUPS = 4
GROUP = C // N_GROUPS
CONV_HIST = 2
POOL_HIST = 15
E = 32
TOP_K = 4
F = 1024
SWIGLU_LIMIT = 7.0
SWIGLU_ALPHA = 1.702
LN_EPS = 1e-5
DEPTH = 1
ALPHA = (2 * DEPTH) ** 0.25
PAST_LEN = 16384

LANES = 128
SUBLANES = 8
ROW_TILE = 512
PROMPT_ROW_TILE = 1024
PROMPT_SUB_TILES = 1
SAMPLE_ROW_TILE = 256
BM = 512
COMBINE_TILE = 256
COMBINE_CHUNKS = 4
SC_WINDOW = 128
VMEM_LIMIT = 56 * 1024 * 1024

_f32 = jnp.float32
_bf16 = jnp.bfloat16


def _dot(a, b):
    return jnp.dot(a, b, preferred_element_type=_f32)


def _dot_exact(a, b):
    return lax.dot_general(a, b, (((1,), (0,)), ((), ())),
                           precision=lax.Precision.HIGHEST, preferred_element_type=_f32)


def _dot_split(a, b):
    a_hi, b_hi = a.astype(_bf16), b.astype(_bf16)
    a_lo = (a - a_hi.astype(_f32)).astype(_bf16)
    b_lo = (b - b_hi.astype(_f32)).astype(_bf16)
    return _dot(a_hi, b_hi) + _dot(a_lo, b_hi) + _dot(a_hi, b_lo)


def _pack_rows(x):
    w = x.shape[1] // 2
    hi = lax.bitcast_convert_type(x[:, :w].astype(_bf16).astype(_f32), jnp.int32)
    lo = lax.bitcast_convert_type(x[:, w:].astype(_bf16).astype(_f32), jnp.int32)
    return hi | lax.shift_right_logical(lo, 16)


def _unpack_rows(p):
    hi = lax.bitcast_convert_type(p & jnp.int32(-65536), _f32)
    lo = lax.bitcast_convert_type(lax.shift_left(p, 16), _f32)
    return jnp.concatenate([hi, lo], axis=1).astype(_bf16)


def _per_seq(x, m, op):
    g = m.shape[0]
    if g == 1:
        return op(x, m)
    r, n = x.shape
    return op(x.reshape(r // g, g, n), m[None]).reshape(r, n)


def _layer_norm(x, g, b):
    mu = jnp.mean(x, axis=-1, keepdims=True)
    xc = x - mu
    var = jnp.mean(xc * xc, axis=-1, keepdims=True)
    return xc * lax.rsqrt(var + LN_EPS) * g + b


def _hist_steps(needed, g):
    return -(-needed * g // SUBLANES) * SUBLANES // g


def _ada_kernel(c_ref, w_ref, b_ref, o_ref):
    c = c_ref[...]
    o_ref[...] = _dot_exact(c * jax.nn.sigmoid(c), w_ref[...]) + b_ref[...]


def _ada(c, w_ada, b_ada):
    rows = c.shape[0]
    cols = w_ada.shape[1]
    bn = 1536
    return pl.pallas_call(
        _ada_kernel,
        out_shape=jax.ShapeDtypeStruct((rows, cols), _f32),
        grid=(cols // bn,),
        in_specs=[pl.BlockSpec((rows, D), lambda j: (0, 0)),
                  pl.BlockSpec((D, bn), lambda j: (0, j)),
                  pl.BlockSpec((1, bn), lambda j: (0, j))],
        out_specs=pl.BlockSpec((rows, bn), lambda j: (0, j)),
        compiler_params=pltpu.CompilerParams(vmem_limit_bytes=VMEM_LIMIT),
        name="ada",
    )(c, w_ada, b_ada)


def _mixer_kernel(g, tiles_per_seq, start_pos, n_sub,
                  x_ref, mod_ref, hc_ref, hp_ref,
                  win_ref, cw_ref, woa_ref, wpool_ref, ls_ref, wob_ref, wo_ref, g1_ref, b1_ref,
                  wr_ref, br_ref,
                  h_ref, v_ref, lg_ref, nc_ref, np_ref, zbuf, pbuf):
    r = x_ref.shape[0]
    hrc = hc_ref.shape[0]
    hrp = hp_ref.shape[0]
    j = pl.program_id(0) % tiles_per_seq

    @pl.when(j == 0)
    def _():
        zbuf[pl.ds(0, hrc), :] = hc_ref[...]
        pbuf[pl.ds(0, hrp), :] = hp_ref[...]

    @pl.when(j != 0)
    def _():
        zt = zbuf[pl.ds(r, hrc), :]
        pt = pbuf[pl.ds(r, hrp), :]
        zbuf[pl.ds(0, hrc), :] = zt
        pbuf[pl.ds(0, hrp), :] = pt

    m = mod_ref[...]
    shift1, scale1, gate1 = m[:, 0:D], m[:, D:2 * D], m[:, 2 * D:3 * D]
    shift2, scale2 = m[:, 3 * D:4 * D], m[:, 4 * D:5 * D]

    rs = r // n_sub
    for s in range(n_sub):
        rows = pl.ds(s * rs, rs)
        x = x_ref[rows, :]
        u = _per_seq(_per_seq(x, 1.0 + scale1, jnp.multiply), shift1, jnp.add).astype(_bf16)

        z = _dot(u, win_ref[:, C:2 * C]) * _dot(u, win_ref[:, 2 * C:3 * C])
        zrow = hrc + s * rs
        zbuf[pl.ds(zrow, rs), :] = z
        cw = cw_ref[...]
        conv = (cw[0:1] * zbuf[pl.ds(zrow - 2 * g, rs), :] + cw[1:2] * zbuf[pl.ds(zrow - g, rs), :]
                + cw[2:3] * z)
        y_a = _dot((_dot(u, win_ref[:, 0:C]) * conv).astype(_bf16), woa_ref[...])

        xp = _dot(u, win_ref[:, 3 * C:4 * C])
        prow = hrp + s * rs
        pbuf[pl.ds(prow, rs), :] = xp
        pos = (start_pos + j * (r // g) + s * (rs // g)
               + lax.broadcasted_iota(jnp.int32, (rs, 1), 0) // g)
        acc = xp
        yg = []
        for grp in range(N_GROUPS):
            lo = grp * GROUP
            wdw = 2 ** (grp + 1)
            for back in range(wdw // 2, wdw):
                sh = pbuf[pl.ds(prow - back * g, rs), lo:C]
                acc = jnp.concatenate([acc[:, 0:lo], acc[:, lo:C] + sh], axis=1) if lo else acc + sh
            cnt = jnp.minimum(wdw, pos + 1).astype(_f32)
            diff = acc[:, lo:lo + GROUP] / cnt - xp[:, lo:lo + GROUP]
            yg.append(_dot(diff.astype(_bf16), wpool_ref[grp]))
        y_b = _dot((jnp.concatenate(yg, axis=1) * ls_ref[...]).astype(_bf16), wob_ref[...])

        g_a = _dot(u, win_ref[:, 4 * C:4 * C + D])
        g_b = _dot(u, win_ref[:, 4 * C + D:4 * C + 2 * D])
        merged = jax.nn.sigmoid(g_a) * y_a + jax.nn.sigmoid(g_b) * y_b
        o = _dot(merged.astype(_bf16), wo_ref[...])
        h = _layer_norm(ALPHA * x + _per_seq(o, gate1, jnp.multiply), g1_ref[...], b1_ref[...])
        v = _per_seq(_per_seq(h, 1.0 + scale2, jnp.multiply), shift2, jnp.add)
        h_ref[rows, :] = h
        v_ref[rows, :] = _pack_rows(v)
        lg_ref[rows, :] = _dot_split(v, wr_ref[...]) + br_ref[...]
    nc_ref[...] = zbuf[pl.ds(r, hrc), :]
    np_ref[...] = pbuf[pl.ds(r, hrp), :]


def _mixer(x2, mod3, hc, hp, weights, row_tile, n_sub, start_pos):
    n = x2.shape[0]
    n_mod, g, _ = mod3.shape
    hrc, hrp = hc.shape[0] // n_mod, hp.shape[0] // n_mod
    tiles_per_seq = n // n_mod // row_tile
    once = dict(pipeline_mode=pl.Buffered(1)) if n_mod == 1 else {}

    def full(a):
        nd = a.ndim
        return pl.BlockSpec(a.shape, lambda i: (0,) * nd)

    def seq_block(rows, **kw):
        return pl.BlockSpec((rows, C), lambda i: (i // tiles_per_seq, 0), **kw)

    def row_block(cols):
        return pl.BlockSpec((row_tile, cols), lambda i: (i, 0))

    return pl.pallas_call(
        functools.partial(_mixer_kernel, g, tiles_per_seq, start_pos, n_sub),
        out_shape=[
            jax.ShapeDtypeStruct((n, D), _f32),
            jax.ShapeDtypeStruct((n, D // 2), jnp.int32),
            jax.ShapeDtypeStruct((n, LANES), _f32),
            jax.ShapeDtypeStruct(hc.shape, _f32),
            jax.ShapeDtypeStruct(hp.shape, _f32),
        ],
        grid=(n // row_tile,),
        in_specs=[row_block(D),
                  pl.BlockSpec((None, g, 6 * D), lambda i: (i // tiles_per_seq, 0, 0), **once),
                  seq_block(hrc, **once), seq_block(hrp, **once)] + [full(a) for a in weights],
        out_specs=[row_block(D), row_block(D // 2), row_block(LANES), seq_block(hrc), seq_block(hrp)],
        scratch_shapes=[pltpu.VMEM((hrc + row_tile, C), _f32), pltpu.VMEM((hrp + row_tile, C), _f32)],
        compiler_params=pltpu.CompilerParams(vmem_limit_bytes=VMEM_LIMIT),
        name="mixer",
    )(x2, mod3, hc, hp, *weights)


def _plan_kernel(lgp_ref, lgs_ref, dest_ref, gate_ref, meta_ref, idx_s, rank_s):
    t = ROW_TILE
    e_iota = lax.broadcasted_iota(jnp.int32, (E, t), 0)
    tri = (lax.broadcasted_iota(jnp.int32, (t, t), 0)
           < lax.broadcasted_iota(jnp.int32, (t, t), 1)).astype(_f32).astype(_bf16)
    zeros_rest = jnp.zeros((LANES - TOP_K, t), _f32)

    def tile_body(lg_ref, off, i, carry):
        lt = lg_ref[i].T[0:E, :]
        vals, idxs = [], []
        for _ in range(TOP_K):
            mx = jnp.max(lt, axis=0, keepdims=True)
            ix = jnp.min(jnp.where(lt == mx, e_iota, E), axis=0, keepdims=True)
            vals.append(mx)
            idxs.append(ix)
            lt = jnp.where(e_iota == ix, -jnp.inf, lt)
        ex = [jnp.exp(vk - vals[0]) for vk in vals]
        den = ex[0] + ex[1] + ex[2] + ex[3]
        gates = [ek / den for ek in ex]
        gate_ref[off + i] = jnp.concatenate(gates + [zeros_rest], axis=0).T

        ohs = [(e_iota == ix) for ix in idxs]
        oh = (ohs[0] | ohs[1] | ohs[2] | ohs[3]).astype(_f32)
        before = _dot(oh.astype(_bf16), tri) + carry
        ranks = [jnp.sum(jnp.where(o, before, 0.0), axis=0, keepdims=True) for o in ohs]
        idx_s[off + i] = jnp.concatenate(idxs + idxs, axis=0)
        rank_s[off + i] = jnp.concatenate(ranks + ranks, axis=0).astype(jnp.int32)
        return carry + jnp.sum(oh, axis=1, keepdims=True)

    n_p, n_s = lgp_ref.shape[0], lgs_ref.shape[0]
    counts = lax.fori_loop(0, n_p, functools.partial(tile_body, lgp_ref, 0), jnp.zeros((E, 1), _f32))
    counts = lax.fori_loop(0, n_s, functools.partial(tile_body, lgs_ref, n_p), counts)
    padded = jnp.ceil(counts / BM) * BM
    low = (lax.broadcasted_iota(jnp.int32, (E, E), 1)
           <= lax.broadcasted_iota(jnp.int32, (E, E), 0)).astype(_f32)
    pad_end = _dot_exact(low, jnp.broadcast_to(padded, (E, LANES)))[:, 0:1]
    pad_start = pad_end - padded

    def dest_body(i, c):
        ix = idx_s[i]
        rk = rank_s[i]
        rows = []
        for k in range(TOP_K):
            st = jnp.sum(jnp.where(e_iota == ix[k:k + 1], pad_start, 0.0), axis=0, keepdims=True)
            rows.append(st.astype(jnp.int32) + rk[k:k + 1])
        dest_ref[i] = jnp.concatenate(rows + rows, axis=0)
        return c

    lax.fori_loop(0, n_p + n_s, dest_body, 0)

    on_lane = (lax.broadcasted_iota(jnp.int32, (E, LANES), 0)
               == lax.broadcasted_iota(jnp.int32, (E, LANES), 1))

    def to_lanes(col):
        return jnp.sum(jnp.where(on_lane, col, 0.0), axis=0, keepdims=True).astype(jnp.int32)

    meta_ref[...] = jnp.concatenate(
        [to_lanes(counts), to_lanes(pad_start / BM), to_lanes(padded / BM),
         jnp.zeros((SUBLANES - 3, LANES), jnp.int32)], axis=0)


def _plan(logits_p, logits_s):
    n_tiles = (logits_p.shape[0] + logits_s.shape[0]) // ROW_TILE
    return pl.pallas_call(
        _plan_kernel,
        out_shape=[
            jax.ShapeDtypeStruct((n_tiles, 2 * TOP_K, ROW_TILE), jnp.int32),
            jax.ShapeDtypeStruct((n_tiles, ROW_TILE, LANES), _f32),
            jax.ShapeDtypeStruct((SUBLANES, LANES), jnp.int32),
        ],
        scratch_shapes=[pltpu.VMEM((n_tiles, 2 * TOP_K, ROW_TILE), jnp.int32),
                        pltpu.VMEM((n_tiles, 2 * TOP_K, ROW_TILE), jnp.int32)],
        compiler_params=pltpu.CompilerParams(vmem_limit_bytes=VMEM_LIMIT),
        name="plan",
    )(logits_p.reshape(-1, ROW_TILE, LANES), logits_s.reshape(-1, ROW_TILE, LANES))


def _dispatch(v_p, v_s, dests, n_rows_out):
    n_p, n_s = v_p.shape[0], v_s.shape[0]
    width = v_p.shape[1]
    w = SC_WINDOW
    n_pw, n_windows = n_p // w, (n_p + n_s) // w
    mesh = plsc.VectorSubcoreMesh(core_axis_name="core", subcore_axis_name="subcore")
    n_workers = mesh.num_cores * mesh.num_subcores

    @functools.partial(
        pl.kernel, mesh=mesh, name="dispatch",
        out_type=jax.ShapeDtypeStruct((n_rows_out, width), jnp.int32),
        scratch_types=[pltpu.VMEM((w, width), jnp.int32)] + [pltpu.VMEM((w,), jnp.int32)] * TOP_K
        + [pltpu.SemaphoreType.DMA])
    def scatter_rows(vp_hbm, vs_hbm, d0_hbm, d1_hbm, d2_hbm, d3_hbm, o_hbm, rows, i0, i1, i2, i3, sem):
        worker = lax.axis_index("subcore") * mesh.num_cores + lax.axis_index("core")
        idx = (i0, i1, i2, i3)

        def scatter_window(c):
            t0 = pl.multiple_of(c * w, w)
            for d_hbm, iv in zip((d0_hbm, d1_hbm, d2_hbm, d3_hbm), idx):
                pltpu.sync_copy(d_hbm.at[pl.ds(t0, w)], iv)
            copies = [pltpu.async_copy(rows, o_hbm.at[iv], sem) for iv in idx]
            for cp in copies:
                cp.wait()

        for j in range(-(-n_windows // n_workers)):
            c = j * n_workers + worker

            @pl.when(c < n_pw)
            def _():
                pltpu.sync_copy(vp_hbm.at[pl.ds(pl.multiple_of(c * w, w), w)], rows)
                scatter_window(c)

            @pl.when((c >= n_pw) & (c < n_windows))
            def _():
                pltpu.sync_copy(vs_hbm.at[pl.ds(pl.multiple_of((c - n_pw) * w, w), w)], rows)
                scatter_window(c)

    return scatter_rows(v_p, v_s, *dests)


def _experts_kernel(cnt_ref, first_ref, nblk_ref, x_hbm, wu_hbm, bu_ref, wd_hbm, bd_ref, y_hbm,
                    xbuf, ybuf, wu_st, wd_st, wu_bf, wd_bf, xsem, ysem, wsem):
    total = first_ref[E - 1] + nblk_ref[E - 1]

    def w_fetch(e):
        return (pltpu.make_async_copy(wu_hbm.at[e], wu_st, wsem.at[0]),
                pltpu.make_async_copy(wd_hbm.at[e], wd_st, wsem.at[1]))

    def x_fetch(b, slot):
        return pltpu.make_async_copy(x_hbm.at[pl.ds(b * BM, BM), :], xbuf.at[slot], xsem.at[slot])

    def y_store(b, slot):
        return pltpu.make_async_copy(ybuf.at[slot], y_hbm.at[pl.ds(b * BM, BM), :], ysem.at[slot])

    for c in w_fetch(0):
        c.start()
    x_fetch(0, 0).start()

    def expert_body(e, carry):
        for c in w_fetch(e):
            c.wait()
        nblk = nblk_ref[e]

        @pl.when(nblk > 0)
        def _():
            wu_bf[...] = wu_st[...].astype(_bf16)
            wd_bf[...] = wd_st[...].astype(_bf16)

        @pl.when(e + 1 < E)
        def _():
            for c in w_fetch(e + 1):
                c.start()

        b_up = bu_ref[e]
        b_down = bd_ref[e]

        def block_body(j, c):
            b = first_ref[e] + j
            slot = b % 2
            x_fetch(b, slot).wait()

            @pl.when(b + 1 < total)
            def _():
                x_fetch(b + 1, 1 - slot).start()

            rows = lax.broadcasted_iota(jnp.int32, (BM, 1), 0)
            x = _unpack_rows(jnp.where(rows < cnt_ref[e] - j * BM, xbuf[slot], 0))
            hcat = _dot(x, wu_bf[...]) + b_up
            glu = jnp.minimum(hcat[:, 0:F], SWIGLU_LIMIT)
            lin = jnp.clip(hcat[:, F:2 * F], -SWIGLU_LIMIT, SWIGLU_LIMIT)
            act = glu * jax.nn.sigmoid(SWIGLU_ALPHA * glu) * (lin + 1.0)
            y = _pack_rows(_dot(act.astype(_bf16), wd_bf[...]) + b_down)

            @pl.when(b >= 2)
            def _():
                y_store(b - 2, slot).wait()

            ybuf[slot] = y
            y_store(b, slot).start()
            return c

        lax.fori_loop(0, nblk, block_body, 0)
        return carry

    lax.fori_loop(0, E, expert_body, 0)

    @pl.when(total >= 2)
    def _():
        y_store(total - 2, total % 2).wait()

    y_store(total - 1, (total - 1) % 2).wait()


def _experts(cnt, first, nblk, xpad, w_up, b_up, w_down, b_down):
    def full(a):
        nd = a.ndim
        return pl.BlockSpec(a.shape, lambda i, *_: (0,) * nd)

    return pl.pallas_call(
        _experts_kernel,
        out_shape=jax.ShapeDtypeStruct(xpad.shape, jnp.int32),
        grid_spec=pltpu.PrefetchScalarGridSpec(
            num_scalar_prefetch=3,
            grid=(1,),
            in_specs=[pl.BlockSpec(memory_space=pl.ANY),
                      pl.BlockSpec(memory_space=pl.ANY), full(b_up),
                      pl.BlockSpec(memory_space=pl.ANY), full(b_down)],
            out_specs=pl.BlockSpec(memory_space=pl.ANY),
            scratch_shapes=[pltpu.VMEM((2, BM, D // 2), jnp.int32), pltpu.VMEM((2, BM, D // 2), jnp.int32),
                            pltpu.VMEM((D, 2 * F), _f32), pltpu.VMEM((F, D), _f32),
                            pltpu.VMEM((D, 2 * F), _bf16), pltpu.VMEM((F, D), _bf16),
                            pltpu.SemaphoreType.DMA((2,)), pltpu.SemaphoreType.DMA((2,)),
                            pltpu.SemaphoreType.DMA((2,))],
        ),
        compiler_params=pltpu.CompilerParams(vmem_limit_bytes=VMEM_LIMIT),
        name="experts",
    )(cnt, first, nblk, xpad, w_up, b_up, w_down, b_down)


def _gather_rows(ypad, dest_all):
    n_out = dest_all.shape[0]
    width = ypad.shape[1]
    w = SC_WINDOW
    mesh = plsc.VectorSubcoreMesh(core_axis_name="core", subcore_axis_name="subcore")
    n_workers = mesh.num_cores * mesh.num_subcores
    n_windows = n_out // w

    @functools.partial(
        pl.kernel, mesh=mesh, name="gather_rows",
        out_type=jax.ShapeDtypeStruct((n_out, width), jnp.int32),
        scratch_types=[pltpu.VMEM((w, width), jnp.int32), pltpu.VMEM((w,), jnp.int32), pltpu.SemaphoreType.DMA])
    def gather_rows(y_hbm, d_hbm, o_hbm, rows, iv, sem):
        worker = lax.axis_index("subcore") * mesh.num_cores + lax.axis_index("core")

        @pl.loop(0, n_windows // n_workers)
        def _(j):
            r0 = pl.multiple_of((j * n_workers + worker) * w, w)
            pltpu.sync_copy(d_hbm.at[pl.ds(r0, w)], iv)
            pltpu.async_copy(y_hbm.at[iv], rows, sem).wait()
            pltpu.sync_copy(rows, o_hbm.at[pl.ds(r0, w)])

    assert n_windows % n_workers == 0
    return gather_rows(ypad, dest_all)


def _combine_kernel(h_ref, mod_ref, gate_ref, g2_ref, b2_ref, y_ref, *aliased_and_out):
    o_ref = aliased_and_out[-1]
    gates = gate_ref[...]
    half = D // 2
    f_hi = jnp.zeros((h_ref.shape[0], half), _f32)
    f_lo = jnp.zeros((h_ref.shape[0], half), _f32)
    for k in range(TOP_K):
        p = y_ref[k]
        gk = gates[:, k:k + 1]
        f_hi = f_hi + gk * lax.bitcast_convert_type(p & jnp.int32(-65536), _f32)
        f_lo = f_lo + gk * lax.bitcast_convert_type(lax.shift_left(p, 16), _f32)
    f = jnp.concatenate([f_hi, f_lo], axis=1)
    gate2 = mod_ref[...][:, 5 * D:6 * D]
    pre = ALPHA * h_ref[...] + _per_seq(f, gate2, jnp.multiply)
    o_ref[...] = _layer_norm(pre, g2_ref[...], b2_ref[...])


def _combine(h, mod3, gates, ln2_g, ln2_b, y4, token0, rows_per_mod, row0, out_so_far):
    t = COMBINE_TILE
    hoff = row0 // t
    goff = (token0 + row0) // t
    g = mod3.shape[1]
    tiles_per_mod = rows_per_mod // t
    in_specs = [pl.BlockSpec((t, D), lambda i: (i + hoff, 0)),
                pl.BlockSpec((None, g, 6 * D), lambda i: ((i + hoff) // tiles_per_mod, 0, 0)),
                pl.BlockSpec((t, LANES), lambda i: (i + goff, 0)),
                pl.BlockSpec((1, D), lambda i: (0, 0)),
                pl.BlockSpec((1, D), lambda i: (0, 0)),
                pl.BlockSpec((TOP_K, t, D // 2), lambda i: (0, i, 0))]
    args = [h, mod3, gates, ln2_g, ln2_b, y4]
    aliases = {}
    if out_so_far is not None:
        in_specs.append(pl.BlockSpec(memory_space=pl.ANY))
        args.append(out_so_far)
        aliases = {len(args) - 1: 0}
    return pl.pallas_call(
        _combine_kernel,
        out_shape=jax.ShapeDtypeStruct(h.shape, _f32),
        grid=(y4.shape[1] // t,),
        in_specs=in_specs,
        out_specs=pl.BlockSpec((t, D), lambda i: (i + hoff, 0)),
        input_output_aliases=aliases,
        compiler_params=pltpu.CompilerParams(vmem_limit_bytes=VMEM_LIMIT),
        name="combine",
    )(*args)


def _time_major(a):
    return a.transpose(1, 0, 2)


def kernel(x_prompt, x_sample, c_prompt, c_sample, state_conv, state_pool, w_ada, b_ada, w_in,
           conv_w, w_out_a, w_pool, ls_pool, w_out_b, w_o, ln1_g, ln1_b, w_router, b_router,
           w_up, b_up, w_down, b_down, ln2_g, ln2_b):
    n_seq_p, seq, _ = x_prompt.shape
    n_seq_s, dec_seq, _ = x_sample.shape
    n_p, n_s = n_seq_p * seq, n_seq_s * dec_seq
    n = n_p + n_s
    n_blocks = TOP_K * n // BM + E
    l = 0

    mod = _ada(jnp.concatenate([c_prompt, c_sample], axis=0), w_ada[l], b_ada[l][None])
    mod_p = mod[:n_seq_p][:, None, :]
    mod_s = mod[n_seq_p:][None]

    weights = (
        w_in[l].astype(_bf16), conv_w[l], w_out_a[l].astype(_bf16), w_pool[l].astype(_bf16),
        ls_pool[l][None], w_out_b[l].astype(_bf16), w_o[l].astype(_bf16), ln1_g[l][None], ln1_b[l][None],
        jnp.pad(w_router[l], ((0, 0), (0, LANES - E))), jnp.pad(b_router[l], (0, LANES - E))[None],
    )
    hc_p, hp_p = _hist_steps(CONV_HIST, 1), _hist_steps(POOL_HIST, 1)
    h_p, v_p, lg_p, nc_p, np_p = _mixer(
        x_prompt.reshape(n_p, D), mod_p, jnp.zeros((n_seq_p * hc_p, C), _f32),
        jnp.zeros((n_seq_p * hp_p, C), _f32), weights, PROMPT_ROW_TILE, PROMPT_SUB_TILES, 0)
    hc_s, hp_s = _hist_steps(CONV_HIST, n_seq_s), _hist_steps(POOL_HIST, n_seq_s)
    hist_c = jnp.pad(_time_major(state_conv[l]), ((hc_s - CONV_HIST, 0), (0, 0), (0, 0)))
    hist_p = jnp.pad(_time_major(state_pool[l]), ((hp_s - POOL_HIST, 0), (0, 0), (0, 0)))
    h_s, v_s, lg_s, nc_s, np_s = _mixer(
        _time_major(x_sample).reshape(n_s, D), mod_s, hist_c.reshape(hc_s * n_seq_s, C),
        hist_p.reshape(hp_s * n_seq_s, C), weights, SAMPLE_ROW_TILE, 1, PAST_LEN)

    dest8, gates3, meta = _plan(lg_p, lg_s)
    cnt, first, nblk = meta[0, :E], meta[1, :E], meta[2, :E]
    gates = gates3.reshape(n, LANES)

    dests = [dest8[:, k, :].reshape(n) for k in range(TOP_K)]
    xpad = _dispatch(v_p, v_s, dests, n_blocks * BM)
    ypad = _experts(cnt, first, nblk, xpad, w_up[l], b_up[l][:, None, :], w_down[l], b_down[l][:, None, :])

    def gathered(t0, rows):
        idx = jnp.concatenate([dk[t0:t0 + rows] for dk in dests])
        return _gather_rows(ypad, idx).reshape(TOP_K, rows, D // 2)

    chunk = n_p // COMBINE_CHUNKS
    y4_p = [gathered(c * chunk, chunk) for c in range(COMBINE_CHUNKS)]
    y4_s = gathered(n_p, n_s)
    g2, b2 = ln2_g[l][None], ln2_b[l][None]
    y_p = None
    for c in range(COMBINE_CHUNKS):
        y_p = _combine(h_p, mod_p, gates, g2, b2, y4_p[c], 0, seq, c * chunk, y_p)
    y_s = _combine(h_s, mod_s, gates, g2, b2, y4_s, n_p, n_s, 0, None)

    y_prompt = y_p.reshape(n_seq_p, seq, D)
    y_sample = _time_major(y_s.reshape(dec_seq, n_seq_s, D))
    new_conv_p = nc_p.reshape(n_seq_p, hc_p, C)[:, hc_p - CONV_HIST:][None]
    new_pool_p = np_p.reshape(n_seq_p, hp_p, C)[:, hp_p - POOL_HIST:][None]
    new_conv_s = _time_major(nc_s.reshape(hc_s, n_seq_s, C)[hc_s - CONV_HIST:])[None]
    new_pool_s = _time_major(np_s.reshape(hp_s, n_seq_s, C)[hp_s - POOL_HIST:])[None]
    return (y_prompt, y_sample, new_conv_p, new_pool_p, new_conv_s, new_pool_s)
```

```python
import functools

import jax
import jax.numpy as jnp
from jax import lax
from jax.experimental import pallas as pl
from jax.experimental.pallas import tpu as pltpu
from jax.experimental.pallas import tpu_sc as plsc

D = 1024
C = 512
N_GROUPS = 4
GROUP = C // N_GROUPS
CONV_HIST = 2
POOL_HIST = 15
E = 32
TOP_K = 4
F = 1024
SWIGLU_LIMIT = 7.0
SWIGLU_ALPHA = 1.702
LN_EPS = 1e-5
DEPTH = 1
ALPHA = (2 * DEPTH) ** 0.25
PAST_LEN = 16384

LANES = 128
SUBLANES = 8
ROW_TILE = 512
PROMPT_ROW_TILE = 1024
PROMPT_SUB_TILES = 1
SAMPLE_ROW_TILE = 256
SLOT_PAD = 256
BIG_BLOCK = 1024
COMBINE_TILE = 256
COMBINE_CHUNKS = 4
SC_WINDOW = 128
VMEM_LIMIT = 56 * 1024 * 1024

_f32 = jnp.float32
_bf16 = jnp.bfloat16


def _dot(a, b):
    return jnp.dot(a, b, preferred_element_type=_f32)


def _dot_exact(a, b):
    return lax.dot_general(a, b, (((1,), (0,)), ((), ())),
                           precision=lax.Precision.HIGHEST, preferred_element_type=_f32)


def _dot_split(a, b):
    a_hi, b_hi = a.astype(_bf16), b.astype(_bf16)
    a_lo = (a - a_hi.astype(_f32)).astype(_bf16)
    b_lo = (b - b_hi.astype(_f32)).astype(_bf16)
    return _dot(a_hi, b_hi) + _dot(a_lo, b_hi) + _dot(a_hi, b_lo)


def _pack_rows(x):
    w = x.shape[1] // 2
    hi = lax.bitcast_convert_type(x[:, :w].astype(_bf16).astype(_f32), jnp.int32)
    lo = lax.bitcast_convert_type(x[:, w:].astype(_bf16).astype(_f32), jnp.int32)
    return hi | lax.shift_right_logical(lo, 16)


def _unpack_rows(p):
    hi = lax.bitcast_convert_type(p & jnp.int32(-65536), _f32)
    lo = lax.bitcast_convert_type(lax.shift_left(p, 16), _f32)
    return jnp.concatenate([hi, lo], axis=1).astype(_bf16)


def _per_seq(x, m, op):
    g = m.shape[0]
    if g == 1:
        return op(x, m)
    r, n = x.shape
    return op(x.reshape(r // g, g, n), m[None]).reshape(r, n)


def _layer_norm(x, g, b):
    mu = jnp.mean(x, axis=-1, keepdims=True)
    xc = x - mu
    var = jnp.mean(xc * xc, axis=-1, keepdims=True)
    return xc * lax.rsqrt(var + LN_EPS) * g + b


def _hist_steps(needed, g):
    return -(-needed * g // SUBLANES) * SUBLANES // g


def _ada_kernel(c_ref, w_ref, b_ref, o_ref):
    c = c_ref[...]
    o_ref[...] = _dot_exact(c * jax.nn.sigmoid(c), w_ref[...]) + b_ref[...]


def _ada(c, w_ada, b_ada):
    rows = c.shape[0]
    cols = w_ada.shape[1]
    bn = 1536
    return pl.pallas_call(
        _ada_kernel,
        out_shape=jax.ShapeDtypeStruct((rows, cols), _f32),
        grid=(cols // bn,),
        in_specs=[pl.BlockSpec((rows, D), lambda j: (0, 0)),
                  pl.BlockSpec((D, bn), lambda j: (0, j)),
                  pl.BlockSpec((1, bn), lambda j: (0, j))],
        out_specs=pl.BlockSpec((rows, bn), lambda j: (0, j)),
        compiler_params=pltpu.CompilerParams(vmem_limit_bytes=VMEM_LIMIT),
        name="ada",
    )(c, w_ada, b_ada)


def _mixer_kernel(g, tiles_per_seq, start_pos, n_sub,
                  x_ref, mod_ref, hc_ref, hp_ref,
                  win_ref, cw_ref, woa_ref, wpool_ref, ls_ref, wob_ref, wo_ref, g1_ref, b1_ref,
                  wr_ref, br_ref,
                  h_ref, v_ref, lg_ref, nc_ref, np_ref, zbuf, pbuf):
    r = x_ref.shape[0]
    hrc = hc_ref.shape[0]
    hrp = hp_ref.shape[0]
    j = pl.program_id(0) % tiles_per_seq

    @pl.when(j == 0)
    def _():
        zbuf[pl.ds(0, hrc), :] = hc_ref[...]
        pbuf[pl.ds(0, hrp), :] = hp_ref[...]

    @pl.when(j != 0)
    def _():
        zt = zbuf[pl.ds(r, hrc), :]
        pt = pbuf[pl.ds(r, hrp), :]
        zbuf[pl.ds(0, hrc), :] = zt
        pbuf[pl.ds(0, hrp), :] = pt

    m = mod_ref[...]
    shift1, scale1, gate1 = m[:, 0:D], m[:, D:2 * D], m[:, 2 * D:3 * D]
    shift2, scale2 = m[:, 3 * D:4 * D], m[:, 4 * D:5 * D]

    rs = r // n_sub
    for s in range(n_sub):
        rows = pl.ds(s * rs, rs)
        x = x_ref[rows, :]
        u = _per_seq(_per_seq(x, 1.0 + scale1, jnp.multiply), shift1, jnp.add).astype(_bf16)

        z = _dot(u, win_ref[:, C:2 * C]) * _dot(u, win_ref[:, 2 * C:3 * C])
        zrow = hrc + s * rs
        zbuf[pl.ds(zrow, rs), :] = z
        cw = cw_ref[...]
        conv = (cw[0:1] * zbuf[pl.ds(zrow - 2 * g, rs), :] + cw[1:2] * zbuf[pl.ds(zrow - g, rs), :]
                + cw[2:3] * z)
        y_a = _dot((_dot(u, win_ref[:, 0:C]) * conv).astype(_bf16), woa_ref[...])

        xp = _dot(u, win_ref[:, 3 * C:4 * C])
        prow = hrp + s * rs
        pbuf[pl.ds(prow, rs), :] = xp
        pos = (start_pos + j * (r // g) + s * (rs // g)
               + lax.broadcasted_iota(jnp.int32, (rs, 1), 0) // g)
        acc = xp
        yg = []
        for grp in range(N_GROUPS):
            lo = grp * GROUP
            wdw = 2 ** (grp + 1)
            for back in range(wdw // 2, wdw):
                sh = pbuf[pl.ds(prow - back * g, rs), lo:C]
                acc = jnp.concatenate([acc[:, 0:lo], acc[:, lo:C] + sh], axis=1) if lo else acc + sh
            cnt = jnp.minimum(wdw, pos + 1).astype(_f32)
            diff = acc[:, lo:lo + GROUP] / cnt - xp[:, lo:lo + GROUP]
            yg.append(_dot(diff.astype(_bf16), wpool_ref[grp]))
        y_b = _dot((jnp.concatenate(yg, axis=1) * ls_ref[...]).astype(_bf16), wob_ref[...])

        g_a = _dot(u, win_ref[:, 4 * C:4 * C + D])
        g_b = _dot(u, win_ref[:, 4 * C + D:4 * C + 2 * D])
        merged = jax.nn.sigmoid(g_a) * y_a + jax.nn.sigmoid(g_b) * y_b
        o = _dot(merged.astype(_bf16), wo_ref[...])
        h = _layer_norm(ALPHA * x + _per_seq(o, gate1, jnp.multiply), g1_ref[...], b1_ref[...])
        v = _per_seq(_per_seq(h, 1.0 + scale2, jnp.multiply), shift2, jnp.add)
        h_ref[rows, :] = h
        v_ref[rows, :] = _pack_rows(v)
        lg_ref[rows, :] = _dot_split(v, wr_ref[...]) + br_ref[...]
    nc_ref[...] = zbuf[pl.ds(r, hrc), :]
    np_ref[...] = pbuf[pl.ds(r, hrp), :]


def _mixer(x2, mod3, hc, hp, weights, row_tile, n_sub, start_pos):
    n = x2.shape[0]
    n_mod, g, _ = mod3.shape
    hrc, hrp = hc.shape[0] // n_mod, hp.shape[0] // n_mod
    tiles_per_seq = n // n_mod // row_tile
    once = dict(pipeline_mode=pl.Buffered(1)) if n_mod == 1 else {}

    def full(a):
        nd = a.ndim
        return pl.BlockSpec(a.shape, lambda i: (0,) * nd)

    def seq_block(rows, **kw):
        return pl.BlockSpec((rows, C), lambda i: (i // tiles_per_seq, 0), **kw)

    def row_block(cols):
        return pl.BlockSpec((row_tile, cols), lambda i: (i, 0))

    return pl.pallas_call(
        functools.partial(_mixer_kernel, g, tiles_per_seq, start_pos, n_sub),
        out_shape=[
            jax.ShapeDtypeStruct((n, D), _f32),
            jax.ShapeDtypeStruct((n, D // 2), jnp.int32),
            jax.ShapeDtypeStruct((n, LANES), _f32),
            jax.ShapeDtypeStruct(hc.shape, _f32),
            jax.ShapeDtypeStruct(hp.shape, _f32),
        ],
        grid=(n // row_tile,),
        in_specs=[row_block(D),
                  pl.BlockSpec((None, g, 6 * D), lambda i: (i // tiles_per_seq, 0, 0), **once),
                  seq_block(hrc, **once), seq_block(hrp, **once)] + [full(a) for a in weights],
        out_specs=[row_block(D), row_block(D // 2), row_block(LANES), seq_block(hrc), seq_block(hrp)],
        scratch_shapes=[pltpu.VMEM((hrc + row_tile, C), _f32), pltpu.VMEM((hrp + row_tile, C), _f32)],
        compiler_params=pltpu.CompilerParams(vmem_limit_bytes=VMEM_LIMIT),
        name="mixer",
    )(x2, mod3, hc, hp, *weights)


def _plan_kernel(lgp_ref, lgs_ref, dest_ref, gate_ref, meta_ref, idx_s, rank_s):
    t = ROW_TILE
    e_iota = lax.broadcasted_iota(jnp.int32, (E, t), 0)
    tri = (lax.broadcasted_iota(jnp.int32, (t, t), 0)
           < lax.broadcasted_iota(jnp.int32, (t, t), 1)).astype(_f32).astype(_bf16)
    zeros_rest = jnp.zeros((LANES - TOP_K, t), _f32)

    def tile_body(lg_ref, off, i, carry):
        lt = lg_ref[i].T[0:E, :]
        vals, idxs = [], []
        for _ in range(TOP_K):
            mx = jnp.max(lt, axis=0, keepdims=True)
            ix = jnp.min(jnp.where(lt == mx, e_iota, E), axis=0, keepdims=True)
            vals.append(mx)
            idxs.append(ix)
            lt = jnp.where(e_iota == ix, -jnp.inf, lt)
        ex = [jnp.exp(vk - vals[0]) for vk in vals]
        den = ex[0] + ex[1] + ex[2] + ex[3]
        gates = [ek / den for ek in ex]
        gate_ref[off + i] = jnp.concatenate(gates + [zeros_rest], axis=0).T

        ohs = [(e_iota == ix) for ix in idxs]
        oh = (ohs[0] | ohs[1] | ohs[2] | ohs[3]).astype(_f32)
        before = _dot(oh.astype(_bf16), tri) + carry
        ranks = [jnp.sum(jnp.where(o, before, 0.0), axis=0, keepdims=True) for o in ohs]
        idx_s[off + i] = jnp.concatenate(idxs + idxs, axis=0)
        rank_s[off + i] = jnp.concatenate(ranks + ranks, axis=0).astype(jnp.int32)
        return carry + jnp.sum(oh, axis=1, keepdims=True)

    n_p, n_s = lgp_ref.shape[0], lgs_ref.shape[0]
    counts = lax.fori_loop(0, n_p, functools.partial(tile_body, lgp_ref, 0), jnp.zeros((E, 1), _f32))
    counts = lax.fori_loop(0, n_s, functools.partial(tile_body, lgs_ref, n_p), counts)
    padded = jnp.ceil(counts / SLOT_PAD) * SLOT_PAD
    low = (lax.broadcasted_iota(jnp.int32, (E, E), 1)
           <= lax.broadcasted_iota(jnp.int32, (E, E), 0)).astype(_f32)
    pad_end = _dot_exact(low, jnp.broadcast_to(padded, (E, LANES)))[:, 0:1]
    pad_start = pad_end - padded

    def dest_body(i, c):
        ix = idx_s[i]
        rk = rank_s[i]
        rows = []
        for k in range(TOP_K):
            st = jnp.sum(jnp.where(e_iota == ix[k:k + 1], pad_start, 0.0), axis=0, keepdims=True)
            rows.append(st.astype(jnp.int32) + rk[k:k + 1])
        dest_ref[i] = jnp.concatenate(rows + rows, axis=0)
        return c

    lax.fori_loop(0, n_p + n_s, dest_body, 0)

    sub = lax.broadcasted_iota(jnp.int32, (E, LANES), 0)
    lane = lax.broadcasted_iota(jnp.int32, (E, LANES), 1)

    def to_lanes(col):
        return jnp.sum(jnp.where(sub == lane, col, 0.0), axis=0, keepdims=True).astype(jnp.int32)

    later = jnp.min(jnp.where((sub > lane) & (counts > 0.0), sub, E), axis=0, keepdims=True)
    meta_ref[...] = jnp.concatenate(
        [to_lanes(counts), to_lanes(pad_start), to_lanes(padded), later,
         jnp.zeros((SUBLANES - 4, LANES), jnp.int32)], axis=0)


def _plan(logits_p, logits_s):
    n_tiles = (logits_p.shape[0] + logits_s.shape[0]) // ROW_TILE
    return pl.pallas_call(
        _plan_kernel,
        out_shape=[
            jax.ShapeDtypeStruct((n_tiles, 2 * TOP_K, ROW_TILE), jnp.int32),
            jax.ShapeDtypeStruct((n_tiles, ROW_TILE, LANES), _f32),
            jax.ShapeDtypeStruct((SUBLANES, LANES), jnp.int32),
        ],
        scratch_shapes=[pltpu.VMEM((n_tiles, 2 * TOP_K, ROW_TILE), jnp.int32),
                        pltpu.VMEM((n_tiles, 2 * TOP_K, ROW_TILE), jnp.int32)],
        compiler_params=pltpu.CompilerParams(vmem_limit_bytes=VMEM_LIMIT),
        name="plan",
    )(logits_p.reshape(-1, ROW_TILE, LANES), logits_s.reshape(-1, ROW_TILE, LANES))


def _dispatch(v_p, v_s, dests, n_rows_out):
    n_p, n_s = v_p.shape[0], v_s.shape[0]
    width = v_p.shape[1]
    w = SC_WINDOW
    n_pw, n_windows = n_p // w, (n_p + n_s) // w
    mesh = plsc.VectorSubcoreMesh(core_axis_name="core", subcore_axis_name="subcore")
    n_workers = mesh.num_cores * mesh.num_subcores

    @functools.partial(
        pl.kernel, mesh=mesh, name="dispatch",
        out_type=jax.ShapeDtypeStruct((n_rows_out, width), jnp.int32),
        scratch_types=[pltpu.VMEM((w, width), jnp.int32)] + [pltpu.VMEM((w,), jnp.int32)] * TOP_K
        + [pltpu.SemaphoreType.DMA])
    def scatter_rows(vp_hbm, vs_hbm, d0_hbm, d1_hbm, d2_hbm, d3_hbm, o_hbm, rows, i0, i1, i2, i3, sem):
        worker = lax.axis_index("subcore") * mesh.num_cores + lax.axis_index("core")
        idx = (i0, i1, i2, i3)

        def scatter_window(c):
            t0 = pl.multiple_of(c * w, w)
            for d_hbm, iv in zip((d0_hbm, d1_hbm, d2_hbm, d3_hbm), idx):
                pltpu.sync_copy(d_hbm.at[pl.ds(t0, w)], iv)
            copies = [pltpu.async_copy(rows, o_hbm.at[iv], sem) for iv in idx]
            for cp in copies:
                cp.wait()

        for j in range(-(-n_windows // n_workers)):
            c = j * n_workers + worker

            @pl.when(c < n_pw)
            def _():
                pltpu.sync_copy(vp_hbm.at[pl.ds(pl.multiple_of(c * w, w), w)], rows)
                scatter_window(c)

            @pl.when((c >= n_pw) & (c < n_windows))
            def _():
                pltpu.sync_copy(vs_hbm.at[pl.ds(pl.multiple_of((c - n_pw) * w, w), w)], rows)
                scatter_window(c)

    return scatter_rows(v_p, v_s, *dests)


def _experts_kernel(cnt_ref, row0_ref, pad_ref, nxt_ref, x_hbm, wu_hbm, bu_ref, wd_hbm, bd_ref, y_hbm,
                    xbuf, ybuf, wu_st, wd_st, wu_bf, wd_bf, ysz, xsem, ysem, wsem):
    def w_fetch(e):
        return (pltpu.make_async_copy(wu_hbm.at[e], wu_st, wsem.at[0]),
                pltpu.make_async_copy(wd_hbm.at[e], wd_st, wsem.at[1]))

    def x_fetch(row, size, slot):
        rows = pl.ds(pl.multiple_of(row, SLOT_PAD), size)
        return pltpu.make_async_copy(x_hbm.at[rows, :], xbuf.at[slot, pl.ds(0, size), :], xsem.at[slot])

    def y_store(row, size, slot):
        rows = pl.ds(pl.multiple_of(row, SLOT_PAD), size)
        return pltpu.make_async_copy(ybuf.at[slot, pl.ds(0, size), :], y_hbm.at[rows, :], ysem.at[slot])

    def y_wait(slot):
        for size in (BIG_BLOCK, SLOT_PAD):
            @pl.when(ysz[slot] == size)
            def _():
                y_store(0, size, slot).wait()

    def fetch_first(e, slot):
        @pl.when(pad_ref[e] >= BIG_BLOCK)
        def _():
            x_fetch(row0_ref[e], BIG_BLOCK, slot).start()

        @pl.when((pad_ref[e] > 0) & (pad_ref[e] < BIG_BLOCK))
        def _():
            x_fetch(row0_ref[e], SLOT_PAD, slot).start()

    ysz[0] = 0
    ysz[1] = 0
    for c in w_fetch(0):
        c.start()
    fetch_first(jnp.where(pad_ref[0] > 0, 0, nxt_ref[0]), 0)

    def expert_body(e, n_done):
        for c in w_fetch(e):
            c.wait()
        padded = pad_ref[e]

        @pl.when(padded > 0)
        def _():
            wu_bf[...] = wu_st[...].astype(_bf16)
            wd_bf[...] = wd_st[...].astype(_bf16)

        @pl.when(e + 1 < E)
        def _():
            for c in w_fetch(e + 1):
                c.start()

        b_up = bu_ref[e]
        b_down = bd_ref[e]
        row0 = row0_ref[e]
        n_big = padded // BIG_BLOCK
        n_small = (padded - n_big * BIG_BLOCK) // SLOT_PAD
        small0 = row0 + n_big * BIG_BLOCK
        nxt = nxt_ref[e]

        def pass_body(size, j, n_done):
            slot = n_done % 2
            big = size == BIG_BLOCK
            row = row0 + j * BIG_BLOCK if big else small0 + j * SLOT_PAD
            x_fetch(row, size, slot).wait()

            more = j + 1 < (n_big if big else n_small)
            tail = (n_small > 0) if big else False

            @pl.when(more)
            def _():
                x_fetch(row + size, size, 1 - slot).start()

            if big:
                @pl.when(jnp.logical_not(more) & tail)
                def _():
                    x_fetch(small0, SLOT_PAD, 1 - slot).start()

            @pl.when(jnp.logical_not(more) & jnp.logical_not(tail) & (nxt < E))
            def _():
                fetch_first(jnp.minimum(nxt, E - 1), 1 - slot)

            rows = lax.broadcasted_iota(jnp.int32, (size, 1), 0)
            x = _unpack_rows(jnp.where(rows < cnt_ref[e] - (row - row0), xbuf[slot, pl.ds(0, size), :], 0))
            hcat = _dot(x, wu_bf[...]) + b_up
            glu = jnp.minimum(hcat[:, 0:F], SWIGLU_LIMIT)
            lin = jnp.clip(hcat[:, F:2 * F], -SWIGLU_LIMIT, SWIGLU_LIMIT)
            act = glu * jax.nn.sigmoid(SWIGLU_ALPHA * glu) * (lin + 1.0)
            y = _pack_rows(_dot(act.astype(_bf16), wd_bf[...]) + b_down)

            y_wait(slot)
            ybuf[slot, pl.ds(0, size), :] = y
            y_store(row, size, slot).start()
            ysz[slot] = size
            return n_done + 1

        n_done = lax.fori_loop(0, n_big, functools.partial(pass_body, BIG_BLOCK), n_done)
        return lax.fori_loop(0, n_small, functools.partial(pass_body, SLOT_PAD), n_done)

    lax.fori_loop(0, E, expert_body, 0)
    y_wait(0)
    y_wait(1)


def _experts(cnt, row0, padded, nxt, xpad, w_up, b_up, w_down, b_down):
    def full(a):
        nd = a.ndim
        return pl.BlockSpec(a.shape, lambda i, *_: (0,) * nd)

    return pl.pallas_call(
        _experts_kernel,
        out_shape=jax.ShapeDtypeStruct(xpad.shape, jnp.int32),
        grid_spec=pltpu.PrefetchScalarGridSpec(
            num_scalar_prefetch=4,
            grid=(1,),
            in_specs=[pl.BlockSpec(memory_space=pl.ANY),
                      pl.BlockSpec(memory_space=pl.ANY), full(b_up),
                      pl.BlockSpec(memory_space=pl.ANY), full(b_down)],
            out_specs=pl.BlockSpec(memory_space=pl.ANY),
            scratch_shapes=[pltpu.VMEM((2, BIG_BLOCK, D // 2), jnp.int32),
                            pltpu.VMEM((2, BIG_BLOCK, D // 2), jnp.int32),
                            pltpu.VMEM((D, 2 * F), _f32), pltpu.VMEM((F, D), _f32),
                            pltpu.VMEM((D, 2 * F), _bf16), pltpu.VMEM((F, D), _bf16),
                            pltpu.SMEM((2,), jnp.int32),
                            pltpu.SemaphoreType.DMA((2,)), pltpu.SemaphoreType.DMA((2,)),
                            pltpu.SemaphoreType.DMA((2,))],
        ),
        compiler_params=pltpu.CompilerParams(vmem_limit_bytes=VMEM_LIMIT),
        name="experts",
    )(cnt, row0, padded, nxt, xpad, w_up, b_up, w_down, b_down)


def _gather_rows(ypad, dest_all):
    n_out = dest_all.shape[0]
    width = ypad.shape[1]
    w = SC_WINDOW
    mesh = plsc.VectorSubcoreMesh(core_axis_name="core", subcore_axis_name="subcore")
    n_workers = mesh.num_cores * mesh.num_subcores
    n_windows = n_out // w

    @functools.partial(
        pl.kernel, mesh=mesh, name="gather_rows",
        out_type=jax.ShapeDtypeStruct((n_out, width), jnp.int32),
        scratch_types=[pltpu.VMEM((w, width), jnp.int32), pltpu.VMEM((w,), jnp.int32), pltpu.SemaphoreType.DMA])
    def gather_rows(y_hbm, d_hbm, o_hbm, rows, iv, sem):
        worker = lax.axis_index("subcore") * mesh.num_cores + lax.axis_index("core")

        @pl.loop(0, n_windows // n_workers)
        def _(j):
            r0 = pl.multiple_of((j * n_workers + worker) * w, w)
            pltpu.sync_copy(d_hbm.at[pl.ds(r0, w)], iv)
            pltpu.async_copy(y_hbm.at[iv], rows, sem).wait()
            pltpu.sync_copy(rows, o_hbm.at[pl.ds(r0, w)])

    assert n_windows % n_workers == 0
    return gather_rows(ypad, dest_all)


def _combine_kernel(h_ref, mod_ref, gate_ref, g2_ref, b2_ref, y_ref, *aliased_and_out):
    o_ref = aliased_and_out[-1]
    gates = gate_ref[...]
    half = D // 2
    f_hi = jnp.zeros((h_ref.shape[0], half), _f32)
    f_lo = jnp.zeros((h_ref.shape[0], half), _f32)
    for k in range(TOP_K):
        p = y_ref[k]
        gk = gates[:, k:k + 1]
        f_hi = f_hi + gk * lax.bitcast_convert_type(p & jnp.int32(-65536), _f32)
        f_lo = f_lo + gk * lax.bitcast_convert_type(lax.shift_left(p, 16), _f32)
    f = jnp.concatenate([f_hi, f_lo], axis=1)
    gate2 = mod_ref[...][:, 5 * D:6 * D]
    pre = ALPHA * h_ref[...] + _per_seq(f, gate2, jnp.multiply)
    o_ref[...] = _layer_norm(pre, g2_ref[...], b2_ref[...])


def _combine(h, mod3, gates, ln2_g, ln2_b, y4, token0, rows_per_mod, row0, out_so_far):
    t = COMBINE_TILE
    hoff = row0 // t
    goff = (token0 + row0) // t
    g = mod3.shape[1]
    tiles_per_mod = rows_per_mod // t
    in_specs = [pl.BlockSpec((t, D), lambda i: (i + hoff, 0)),
                pl.BlockSpec((None, g, 6 * D), lambda i: ((i + hoff) // tiles_per_mod, 0, 0)),
                pl.BlockSpec((t, LANES), lambda i: (i + goff, 0)),
                pl.BlockSpec((1, D), lambda i: (0, 0)),
                pl.BlockSpec((1, D), lambda i: (0, 0)),
                pl.BlockSpec((TOP_K, t, D // 2), lambda i: (0, i, 0))]
    args = [h, mod3, gates, ln2_g, ln2_b, y4]
    aliases = {}
    if out_so_far is not None:
        in_specs.append(pl.BlockSpec(memory_space=pl.ANY))
        args.append(out_so_far)
        aliases = {len(args) - 1: 0}
    return pl.pallas_call(
        _combine_kernel,
        out_shape=jax.ShapeDtypeStruct(h.shape, _f32),
        grid=(y4.shape[1] // t,),
        in_specs=in_specs,
        out_specs=pl.BlockSpec((t, D), lambda i: (i + hoff, 0)),
        input_output_aliases=aliases,
        compiler_params=pltpu.CompilerParams(vmem_limit_bytes=VMEM_LIMIT),
        name="combine",
    )(*args)


def _time_major(a):
    return a.transpose(1, 0, 2)


def kernel(x_prompt, x_sample, c_prompt, c_sample, state_conv, state_pool, w_ada, b_ada, w_in,
           conv_w, w_out_a, w_pool, ls_pool, w_out_b, w_o, ln1_g, ln1_b, w_router, b_router,
           w_up, b_up, w_down, b_down, ln2_g, ln2_b):
    n_seq_p, seq, _ = x_prompt.shape
    n_seq_s, dec_seq, _ = x_sample.shape
    n_p, n_s = n_seq_p * seq, n_seq_s * dec_seq
    n = n_p + n_s
    n_slots = TOP_K * n + E * SLOT_PAD
    l = 0

    mod = _ada(jnp.concatenate([c_prompt, c_sample], axis=0), w_ada[l], b_ada[l][None])
    mod_p = mod[:n_seq_p][:, None, :]
    mod_s = mod[n_seq_p:][None]

    weights = (
        w_in[l].astype(_bf16), conv_w[l], w_out_a[l].astype(_bf16), w_pool[l].astype(_bf16),
        ls_pool[l][None], w_out_b[l].astype(_bf16), w_o[l].astype(_bf16), ln1_g[l][None], ln1_b[l][None],
        jnp.pad(w_router[l], ((0, 0), (0, LANES - E))), jnp.pad(b_router[l], (0, LANES - E))[None],
    )
    hc_p, hp_p = _hist_steps(CONV_HIST, 1), _hist_steps(POOL_HIST, 1)
    h_p, v_p, lg_p, nc_p, np_p = _mixer(
        x_prompt.reshape(n_p, D), mod_p, jnp.zeros((n_seq_p * hc_p, C), _f32),
        jnp.zeros((n_seq_p * hp_p, C), _f32), weights, PROMPT_ROW_TILE, PROMPT_SUB_TILES, 0)
    hc_s, hp_s = _hist_steps(CONV_HIST, n_seq_s), _hist_steps(POOL_HIST, n_seq_s)
    hist_c = jnp.pad(_time_major(state_conv[l]), ((hc_s - CONV_HIST, 0), (0, 0), (0, 0)))
    hist_p = jnp.pad(_time_major(state_pool[l]), ((hp_s - POOL_HIST, 0), (0, 0), (0, 0)))
    h_s, v_s, lg_s, nc_s, np_s = _mixer(
        _time_major(x_sample).reshape(n_s, D), mod_s, hist_c.reshape(hc_s * n_seq_s, C),
        hist_p.reshape(hp_s * n_seq_s, C), weights, SAMPLE_ROW_TILE, 1, PAST_LEN)

    dest8, gates3, meta = _plan(lg_p, lg_s)
    cnt, row0, padded, nxt = (meta[i, :E] for i in range(4))
    gates = gates3.reshape(n, LANES)

    dests = [dest8[:, k, :].reshape(n) for k in range(TOP_K)]
    xpad = _dispatch(v_p, v_s, dests, n_slots)
    ypad = _experts(cnt, row0, padded, nxt, xpad, w_up[l], b_up[l][:, None, :], w_down[l], b_down[l][:, None, :])

    def gathered(t0, rows):
        idx = jnp.concatenate([dk[t0:t0 + rows] for dk in dests])
        return _gather_rows(ypad, idx).reshape(TOP_K, rows, D // 2)

    chunk = n_p // COMBINE_CHUNKS
    y4_p = [gathered(c * chunk, chunk) for c in range(COMBINE_CHUNKS)]
    y4_s = gathered(n_p, n_s)
    g2, b2 = ln2_g[l][None], ln2_b[l][None]
    y_p = None
    for c in range(COMBINE_CHUNKS):
        y_p = _combine(h_p, mod_p, gates, g2, b2, y4_p[c], 0, seq, c * chunk, y_p)
    y_s = _combine(h_s, mod_s, gates, g2, b2, y4_s, n_p, n_s, 0, None)

    y_prompt = y_p.reshape(n_seq_p, seq, D)
    y_sample = _time_major(y_s.reshape(dec_seq, n_seq_s, D))
    new_conv_p = nc_p.reshape(n_seq_p, hc_p, C)[:, hc_p - CONV_HIST:][None]
    new_pool_p = np_p.reshape(n_seq_p, hp_p, C)[:, hp_p - POOL_HIST:][None]
    new_conv_s = _time_major(nc_s.reshape(hc_s, n_seq_s, C)[hc_s - CONV_HIST:])[None]
    new_pool_s = _time_major(np_s.reshape(hp_s, n_seq_s, C)[hp_s - POOL_HIST:])[None]
    return (y_prompt, y_sample, new_conv_p, new_pool_p, new_conv_s, new_pool_s)
```

```python
import functools

import jax
import jax.numpy as jnp
from jax import lax
from jax.experimental import pallas as pl
from jax.experimental.pallas import tpu as pltpu
from jax.experimental.pallas import tpu_sc as plsc

D = 1024
C = 512
N_GROUPS = 4
GROUP = C // N_GROUPS
CONV_HIST = 2
POOL_HIST = 15
E = 32
TOP_K = 4
F = 1024
SWIGLU_LIMIT = 7.0
SWIGLU_ALPHA = 1.702
LN_EPS = 1e-5
DEPTH = 1
ALPHA = (2 * DEPTH) ** 0.25
PAST_LEN = 16384

LANES = 128
SUBLANES = 8
ROW_TILE = 512
PROMPT_ROW_TILE = 1024
PROMPT_SUB_TILES = 1
SAMPLE_ROW_TILE = 256
SLOT_PAD = 256
BIG_BLOCK = 1024
COMBINE_TILE = 256
COMBINE_CHUNKS = 4
SC_WINDOW = 128
VMEM_LIMIT = 56 * 1024 * 1024

_f32 = jnp.float32
_bf16 = jnp.bfloat16


def _dot(a, b):
    return jnp.dot(a, b, preferred_element_type=_f32)


def _dot_exact(a, b):
    return lax.dot_general(a, b, (((1,), (0,)), ((), ())),
                           precision=lax.Precision.HIGHEST, preferred_element_type=_f32)


def _dot_split(a, b):
    a_hi, b_hi = a.astype(_bf16), b.astype(_bf16)
    a_lo = (a - a_hi.astype(_f32)).astype(_bf16)
    b_lo = (b - b_hi.astype(_f32)).astype(_bf16)
    return _dot(a_hi, b_hi) + _dot(a_lo, b_hi) + _dot(a_hi, b_lo)


def _pack_rows(x):
    w = x.shape[1] // 2
    hi = lax.bitcast_convert_type(x[:, :w].astype(_bf16).astype(_f32), jnp.int32)
    lo = lax.bitcast_convert_type(x[:, w:].astype(_bf16).astype(_f32), jnp.int32)
    return hi | lax.shift_right_logical(lo, 16)


def _unpack_rows(p):
    hi = lax.bitcast_convert_type(p & jnp.int32(-65536), _f32)
    lo = lax.bitcast_convert_type(lax.shift_left(p, 16), _f32)
    return jnp.concatenate([hi, lo], axis=1).astype(_bf16)


def _sigmoid(x):
    return 0.5 * jnp.tanh(0.5 * x) + 0.5


def _per_seq(x, m, op):
    g = m.shape[0]
    if g == 1:
        return op(x, m)
    r, n = x.shape
    return op(x.reshape(r // g, g, n), m[None]).reshape(r, n)


def _layer_norm(x, g, b):
    mu = jnp.mean(x, axis=-1, keepdims=True)
    xc = x - mu
    var = jnp.mean(xc * xc, axis=-1, keepdims=True)
    return xc * lax.rsqrt(var + LN_EPS) * g + b


def _hist_steps(needed, g):
    return -(-needed * g // SUBLANES) * SUBLANES // g


def _ada_kernel(c_ref, w_ref, b_ref, o_ref):
    c = c_ref[...]
    o_ref[...] = _dot_split(c * _sigmoid(c), w_ref[...]) + b_ref[...]


def _ada(c, w_ada, b_ada):
    rows = c.shape[0]
    cols = w_ada.shape[1]
    bn = 1536
    return pl.pallas_call(
        _ada_kernel,
        out_shape=jax.ShapeDtypeStruct((rows, cols), _f32),
        grid=(cols // bn,),
        in_specs=[pl.BlockSpec((rows, D), lambda j: (0, 0)),
                  pl.BlockSpec((D, bn), lambda j: (0, j)),
                  pl.BlockSpec((1, bn), lambda j: (0, j))],
        out_specs=pl.BlockSpec((rows, bn), lambda j: (0, j)),
        compiler_params=pltpu.CompilerParams(vmem_limit_bytes=VMEM_LIMIT),
        name="ada",
    )(c, w_ada, b_ada)


def _mixer_kernel(g, tiles_per_seq, start_pos, n_sub,
                  x_ref, mod_ref, hc_ref, hp_ref,
                  win_ref, cw_ref, woa_ref, wpool_ref, ls_ref, wob_ref, wo_ref, g1_ref, b1_ref,
                  wr_ref, br_ref,
                  h_ref, v_ref, lg_ref, nc_ref, np_ref, zbuf, pbuf):
    r = x_ref.shape[0]
    hrc = hc_ref.shape[0]
    hrp = hp_ref.shape[0]
    j = pl.program_id(0) % tiles_per_seq

    @pl.when(j == 0)
    def _():
        zbuf[pl.ds(0, hrc), :] = hc_ref[...]
        pbuf[pl.ds(0, hrp), :] = hp_ref[...]

    @pl.when(j != 0)
    def _():
        zt = zbuf[pl.ds(r, hrc), :]
        pt = pbuf[pl.ds(r, hrp), :]
        zbuf[pl.ds(0, hrc), :] = zt
        pbuf[pl.ds(0, hrp), :] = pt

    m = mod_ref[...]
    shift1, scale1, gate1 = m[:, 0:D], m[:, D:2 * D], m[:, 2 * D:3 * D]
    shift2, scale2 = m[:, 3 * D:4 * D], m[:, 4 * D:5 * D]

    rs = r // n_sub
    for s in range(n_sub):
        rows = pl.ds(s * rs, rs)
        x = x_ref[rows, :]
        u = _per_seq(_per_seq(x, 1.0 + scale1, jnp.multiply), shift1, jnp.add).astype(_bf16)

        z = _dot(u, win_ref[:, C:2 * C]) * _dot(u, win_ref[:, 2 * C:3 * C])
        zrow = hrc + s * rs
        zbuf[pl.ds(zrow, rs), :] = z
        cw = cw_ref[...]
        conv = (cw[0:1] * zbuf[pl.ds(zrow - 2 * g, rs), :] + cw[1:2] * zbuf[pl.ds(zrow - g, rs), :]
                + cw[2:3] * z)
        y_a = _dot((_dot(u, win_ref[:, 0:C]) * conv).astype(_bf16), woa_ref[...])

        xp = _dot(u, win_ref[:, 3 * C:4 * C])
        prow = hrp + s * rs
        pbuf[pl.ds(prow, rs), :] = xp
        pos = (start_pos + j * (r // g) + s * (rs // g)
               + lax.broadcasted_iota(jnp.int32, (rs, 1), 0) // g)
        acc = xp
        yg = []
        for grp in range(N_GROUPS):
            lo = grp * GROUP
            wdw = 2 ** (grp + 1)
            for back in range(wdw // 2, wdw):
                sh = pbuf[pl.ds(prow - back * g, rs), lo:C]
                acc = jnp.concatenate([acc[:, 0:lo], acc[:, lo:C] + sh], axis=1) if lo else acc + sh
            inv_cnt = 1.0 / jnp.minimum(wdw, pos + 1).astype(_f32)
            diff = acc[:, lo:lo + GROUP] * inv_cnt - xp[:, lo:lo + GROUP]
            yg.append(_dot(diff.astype(_bf16), wpool_ref[grp]))
        y_b = _dot((jnp.concatenate(yg, axis=1) * ls_ref[...]).astype(_bf16), wob_ref[...])

        g_a = _dot(u, win_ref[:, 4 * C:4 * C + D])
        g_b = _dot(u, win_ref[:, 4 * C + D:4 * C + 2 * D])
        merged = _sigmoid(g_a) * y_a + _sigmoid(g_b) * y_b
        o = _dot(merged.astype(_bf16), wo_ref[...])
        h = _layer_norm(ALPHA * x + _per_seq(o, gate1, jnp.multiply), g1_ref[...], b1_ref[...])
        v = _per_seq(_per_seq(h, 1.0 + scale2, jnp.multiply), shift2, jnp.add)
        h_ref[rows, :] = h
        v_ref[rows, :] = _pack_rows(v)
        lg_ref[rows, :] = _dot_split(v, wr_ref[...]) + br_ref[...]
    nc_ref[...] = zbuf[pl.ds(r, hrc), :]
    np_ref[...] = pbuf[pl.ds(r, hrp), :]


def _mixer(x2, mod3, hc, hp, weights, row_tile, n_sub, start_pos):
    n = x2.shape[0]
    n_mod, g, _ = mod3.shape
    hrc, hrp = hc.shape[0] // n_mod, hp.shape[0] // n_mod
    tiles_per_seq = n // n_mod // row_tile
    once = dict(pipeline_mode=pl.Buffered(1)) if n_mod == 1 else {}

    def full(a):
        nd = a.ndim
        return pl.BlockSpec(a.shape, lambda i: (0,) * nd)

    def seq_block(rows, **kw):
        return pl.BlockSpec((rows, C), lambda i: (i // tiles_per_seq, 0), **kw)

    def row_block(cols):
        return pl.BlockSpec((row_tile, cols), lambda i: (i, 0))

    return pl.pallas_call(
        functools.partial(_mixer_kernel, g, tiles_per_seq, start_pos, n_sub),
        out_shape=[
            jax.ShapeDtypeStruct((n, D), _f32),
            jax.ShapeDtypeStruct((n, D // 2), jnp.int32),
            jax.ShapeDtypeStruct((n, LANES), _f32),
            jax.ShapeDtypeStruct(hc.shape, _f32),
            jax.ShapeDtypeStruct(hp.shape, _f32),
        ],
        grid=(n // row_tile,),
        in_specs=[row_block(D),
                  pl.BlockSpec((None, g, 6 * D), lambda i: (i // tiles_per_seq, 0, 0), **once),
                  seq_block(hrc, **once), seq_block(hrp, **once)] + [full(a) for a in weights],
        out_specs=[row_block(D), row_block(D // 2), row_block(LANES), seq_block(hrc), seq_block(hrp)],
        scratch_shapes=[pltpu.VMEM((hrc + row_tile, C), _f32), pltpu.VMEM((hrp + row_tile, C), _f32)],
        compiler_params=pltpu.CompilerParams(vmem_limit_bytes=VMEM_LIMIT),
        name="mixer",
    )(x2, mod3, hc, hp, *weights)


def _plan_kernel(lgp_ref, lgs_ref, dest_ref, gate_ref, meta_ref, idx_s, rank_s):
    t = ROW_TILE
    e_iota = lax.broadcasted_iota(jnp.int32, (E, t), 0)
    tri = (lax.broadcasted_iota(jnp.int32, (t, t), 0)
           < lax.broadcasted_iota(jnp.int32, (t, t), 1)).astype(_f32).astype(_bf16)
    zeros_rest = jnp.zeros((LANES - TOP_K, t), _f32)

    def tile_body(lg_ref, off, i, carry):
        lt = lg_ref[i].T[0:E, :]
        vals, idxs = [], []
        for _ in range(TOP_K):
            mx = jnp.max(lt, axis=0, keepdims=True)
            ix = jnp.min(jnp.where(lt == mx, e_iota, E), axis=0, keepdims=True)
            vals.append(mx)
            idxs.append(ix)
            lt = jnp.where(e_iota == ix, -jnp.inf, lt)
        ex = [jnp.exp(vk - vals[0]) for vk in vals]
        den = ex[0] + ex[1] + ex[2] + ex[3]
        gates = [ek / den for ek in ex]
        gate_ref[off + i] = jnp.concatenate(gates + [zeros_rest], axis=0).T

        ohs = [(e_iota == ix) for ix in idxs]
        oh = (ohs[0] | ohs[1] | ohs[2] | ohs[3]).astype(_f32)
        before = _dot(oh.astype(_bf16), tri) + carry
        ranks = [jnp.sum(jnp.where(o, before, 0.0), axis=0, keepdims=True) for o in ohs]
        idx_s[off + i] = jnp.concatenate(idxs + idxs, axis=0)
        rank_s[off + i] = jnp.concatenate(ranks + ranks, axis=0).astype(jnp.int32)
        return carry + jnp.sum(oh, axis=1, keepdims=True)

    n_p, n_s = lgp_ref.shape[0], lgs_ref.shape[0]
    counts = lax.fori_loop(0, n_p, functools.partial(tile_body, lgp_ref, 0), jnp.zeros((E, 1), _f32))
    counts = lax.fori_loop(0, n_s, functools.partial(tile_body, lgs_ref, n_p), counts)
    padded = jnp.ceil(counts / SLOT_PAD) * SLOT_PAD
    low = (lax.broadcasted_iota(jnp.int32, (E, E), 1)
           <= lax.broadcasted_iota(jnp.int32, (E, E), 0)).astype(_f32)
    pad_end = _dot_exact(low, jnp.broadcast_to(padded, (E, LANES)))[:, 0:1]
    pad_start = pad_end - padded

    def dest_body(i, c):
        ix = idx_s[i]
        rk = rank_s[i]
        rows = []
        for k in range(TOP_K):
            st = jnp.sum(jnp.where(e_iota == ix[k:k + 1], pad_start, 0.0), axis=0, keepdims=True)
            rows.append(st.astype(jnp.int32) + rk[k:k + 1])
        dest_ref[i] = jnp.concatenate(rows + rows, axis=0)
        return c

    lax.fori_loop(0, n_p + n_s, dest_body, 0)

    sub = lax.broadcasted_iota(jnp.int32, (E, LANES), 0)
    lane = lax.broadcasted_iota(jnp.int32, (E, LANES), 1)

    def to_lanes(col):
        return jnp.sum(jnp.where(sub == lane, col, 0.0), axis=0, keepdims=True).astype(jnp.int32)

    later = jnp.min(jnp.where((sub > lane) & (counts > 0.0), sub, E), axis=0, keepdims=True)
    meta_ref[...] = jnp.concatenate(
        [to_lanes(counts), to_lanes(pad_start), to_lanes(padded), later,
         jnp.zeros((SUBLANES - 4, LANES), jnp.int32)], axis=0)


def _plan(logits_p, logits_s):
    n_tiles = (logits_p.shape[0] + logits_s.shape[0]) // ROW_TILE
    return pl.pallas_call(
        _plan_kernel,
        out_shape=[
            jax.ShapeDtypeStruct((n_tiles, 2 * TOP_K, ROW_TILE), jnp.int32),
            jax.ShapeDtypeStruct((n_tiles, ROW_TILE, LANES), _f32),
            jax.ShapeDtypeStruct((SUBLANES, LANES), jnp.int32),
        ],
        scratch_shapes=[pltpu.VMEM((n_tiles, 2 * TOP_K, ROW_TILE), jnp.int32),
                        pltpu.VMEM((n_tiles, 2 * TOP_K, ROW_TILE), jnp.int32)],
        compiler_params=pltpu.CompilerParams(vmem_limit_bytes=VMEM_LIMIT),
        name="plan",
    )(logits_p.reshape(-1, ROW_TILE, LANES), logits_s.reshape(-1, ROW_TILE, LANES))


def _dispatch(v_p, v_s, dests, n_rows_out):
    n_p, n_s = v_p.shape[0], v_s.shape[0]
    width = v_p.shape[1]
    w = SC_WINDOW
    n_pw, n_windows = n_p // w, (n_p + n_s) // w
    mesh = plsc.VectorSubcoreMesh(core_axis_name="core", subcore_axis_name="subcore")
    n_workers = mesh.num_cores * mesh.num_subcores

    @functools.partial(
        pl.kernel, mesh=mesh, name="dispatch",
        out_type=jax.ShapeDtypeStruct((n_rows_out, width), jnp.int32),
        scratch_types=[pltpu.VMEM((w, width), jnp.int32)] + [pltpu.VMEM((w,), jnp.int32)] * TOP_K
        + [pltpu.SemaphoreType.DMA])
    def scatter_rows(vp_hbm, vs_hbm, d0_hbm, d1_hbm, d2_hbm, d3_hbm, o_hbm, rows, i0, i1, i2, i3, sem):
        worker = lax.axis_index("subcore") * mesh.num_cores + lax.axis_index("core")
        idx = (i0, i1, i2, i3)

        def scatter_window(c):
            t0 = pl.multiple_of(c * w, w)
            for d_hbm, iv in zip((d0_hbm, d1_hbm, d2_hbm, d3_hbm), idx):
                pltpu.sync_copy(d_hbm.at[pl.ds(t0, w)], iv)
            copies = [pltpu.async_copy(rows, o_hbm.at[iv], sem) for iv in idx]
            for cp in copies:
                cp.wait()

        for j in range(-(-n_windows // n_workers)):
            c = j * n_workers + worker

            @pl.when(c < n_pw)
            def _():
                pltpu.sync_copy(vp_hbm.at[pl.ds(pl.multiple_of(c * w, w), w)], rows)
                scatter_window(c)

            @pl.when((c >= n_pw) & (c < n_windows))
            def _():
                pltpu.sync_copy(vs_hbm.at[pl.ds(pl.multiple_of((c - n_pw) * w, w), w)], rows)
                scatter_window(c)

    return scatter_rows(v_p, v_s, *dests)


def _experts_kernel(cnt_ref, row0_ref, pad_ref, nxt_ref, x_hbm, wu_hbm, bu_ref, wd_hbm, bd_ref, y_hbm,
                    xbuf, ybuf, wu_st, wd_st, wu_bf, wd_bf, ysz, xsem, ysem, wsem):
    def w_fetch(e):
        return (pltpu.make_async_copy(wu_hbm.at[e], wu_st, wsem.at[0]),
                pltpu.make_async_copy(wd_hbm.at[e], wd_st, wsem.at[1]))

    def x_fetch(row, size, slot):
        rows = pl.ds(pl.multiple_of(row, SLOT_PAD), size)
        return pltpu.make_async_copy(x_hbm.at[rows, :], xbuf.at[slot, pl.ds(0, size), :], xsem.at[slot])

    def y_store(row, size, slot):
        rows = pl.ds(pl.multiple_of(row, SLOT_PAD), size)
        return pltpu.make_async_copy(ybuf.at[slot, pl.ds(0, size), :], y_hbm.at[rows, :], ysem.at[slot])

    def y_wait(slot):
        for size in (BIG_BLOCK, SLOT_PAD):
            @pl.when(ysz[slot] == size)
            def _():
                y_store(0, size, slot).wait()

    def fetch_first(e, slot):
        @pl.when(pad_ref[e] >= BIG_BLOCK)
        def _():
            x_fetch(row0_ref[e], BIG_BLOCK, slot).start()

        @pl.when((pad_ref[e] > 0) & (pad_ref[e] < BIG_BLOCK))
        def _():
            x_fetch(row0_ref[e], SLOT_PAD, slot).start()

    ysz[0] = 0
    ysz[1] = 0
    for c in w_fetch(0):
        c.start()
    fetch_first(jnp.where(pad_ref[0] > 0, 0, nxt_ref[0]), 0)

    def expert_body(e, n_done):
        for c in w_fetch(e):
            c.wait()
        padded = pad_ref[e]

        @pl.when(padded > 0)
        def _():
            wu_bf[...] = wu_st[...].astype(_bf16)
            wd_bf[...] = wd_st[...].astype(_bf16)

        @pl.when(e + 1 < E)
        def _():
            for c in w_fetch(e + 1):
                c.start()

        b_up = bu_ref[e]
        b_down = bd_ref[e]
        row0 = row0_ref[e]
        n_big = padded // BIG_BLOCK
        n_small = (padded - n_big * BIG_BLOCK) // SLOT_PAD
        small0 = row0 + n_big * BIG_BLOCK
        nxt = nxt_ref[e]

        def pass_body(size, j, n_done):
            slot = n_done % 2
            big = size == BIG_BLOCK
            row = row0 + j * BIG_BLOCK if big else small0 + j * SLOT_PAD
            x_fetch(row, size, slot).wait()

            more = j + 1 < (n_big if big else n_small)
            tail = (n_small > 0) if big else False

            @pl.when(more)
            def _():
                x_fetch(row + size, size, 1 - slot).start()

            if big:
                @pl.when(jnp.logical_not(more) & tail)
                def _():
                    x_fetch(small0, SLOT_PAD, 1 - slot).start()

            @pl.when(jnp.logical_not(more) & jnp.logical_not(tail) & (nxt < E))
            def _():
                fetch_first(jnp.minimum(nxt, E - 1), 1 - slot)

            rows = lax.broadcasted_iota(jnp.int32, (size, 1), 0)
            x = _unpack_rows(jnp.where(rows < cnt_ref[e] - (row - row0), xbuf[slot, pl.ds(0, size), :], 0))
            hcat = _dot(x, wu_bf[...]) + b_up
            glu = jnp.minimum(hcat[:, 0:F], SWIGLU_LIMIT)
            lin = jnp.clip(hcat[:, F:2 * F], -SWIGLU_LIMIT, SWIGLU_LIMIT)
            act = glu * _sigmoid(SWIGLU_ALPHA * glu) * (lin + 1.0)
            y = _pack_rows(_dot(act.astype(_bf16), wd_bf[...]) + b_down)

            y_wait(slot)
            ybuf[slot, pl.ds(0, size), :] = y
            y_store(row, size, slot).start()
            ysz[slot] = size
            return n_done + 1

        n_done = lax.fori_loop(0, n_big, functools.partial(pass_body, BIG_BLOCK), n_done)
        return lax.fori_loop(0, n_small, functools.partial(pass_body, SLOT_PAD), n_done)

    lax.fori_loop(0, E, expert_body, 0)
    y_wait(0)
    y_wait(1)


def _experts(cnt, row0, padded, nxt, xpad, w_up, b_up, w_down, b_down):
    def full(a):
        nd = a.ndim
        return pl.BlockSpec(a.shape, lambda i, *_: (0,) * nd)

    return pl.pallas_call(
        _experts_kernel,
        out_shape=jax.ShapeDtypeStruct(xpad.shape, jnp.int32),
        grid_spec=pltpu.PrefetchScalarGridSpec(
            num_scalar_prefetch=4,
            grid=(1,),
            in_specs=[pl.BlockSpec(memory_space=pl.ANY),
                      pl.BlockSpec(memory_space=pl.ANY), full(b_up),
                      pl.BlockSpec(memory_space=pl.ANY), full(b_down)],
            out_specs=pl.BlockSpec(memory_space=pl.ANY),
            scratch_shapes=[pltpu.VMEM((2, BIG_BLOCK, D // 2), jnp.int32),
                            pltpu.VMEM((2, BIG_BLOCK, D // 2), jnp.int32),
                            pltpu.VMEM((D, 2 * F), _f32), pltpu.VMEM((F, D), _f32),
                            pltpu.VMEM((D, 2 * F), _bf16), pltpu.VMEM((F, D), _bf16),
                            pltpu.SMEM((2,), jnp.int32),
                            pltpu.SemaphoreType.DMA((2,)), pltpu.SemaphoreType.DMA((2,)),
                            pltpu.SemaphoreType.DMA((2,))],
        ),
        compiler_params=pltpu.CompilerParams(vmem_limit_bytes=VMEM_LIMIT),
        name="experts",
    )(cnt, row0, padded, nxt, xpad, w_up, b_up, w_down, b_down)


def _gather_rows(ypad, dest_all):
    n_out = dest_all.shape[0]
    width = ypad.shape[1]
    w = SC_WINDOW
    mesh = plsc.VectorSubcoreMesh(core_axis_name="core", subcore_axis_name="subcore")
    n_workers = mesh.num_cores * mesh.num_subcores
    n_windows = n_out // w

    @functools.partial(
        pl.kernel, mesh=mesh, name="gather_rows",
        out_type=jax.ShapeDtypeStruct((n_out, width), jnp.int32),
        scratch_types=[pltpu.VMEM((w, width), jnp.int32), pltpu.VMEM((w,), jnp.int32), pltpu.SemaphoreType.DMA])
    def gather_rows(y_hbm, d_hbm, o_hbm, rows, iv, sem):
        worker = lax.axis_index("subcore") * mesh.num_cores + lax.axis_index("core")

        @pl.loop(0, n_windows // n_workers)
        def _(j):
            r0 = pl.multiple_of((j * n_workers + worker) * w, w)
            pltpu.sync_copy(d_hbm.at[pl.ds(r0, w)], iv)
            pltpu.async_copy(y_hbm.at[iv], rows, sem).wait()
            pltpu.sync_copy(rows, o_hbm.at[pl.ds(r0, w)])

    assert n_windows % n_workers == 0
    return gather_rows(ypad, dest_all)


def _combine_kernel(h_ref, mod_ref, gate_ref, g2_ref, b2_ref, y_ref, *aliased_and_out):
    o_ref = aliased_and_out[-1]
    gates = gate_ref[...]
    half = D // 2
    f_hi = jnp.zeros((h_ref.shape[0], half), _f32)
    f_lo = jnp.zeros((h_ref.shape[0], half), _f32)
    for k in range(TOP_K):
        p = y_ref[k]
        gk = gates[:, k:k + 1]
        f_hi = f_hi + gk * lax.bitcast_convert_type(p & jnp.int32(-65536), _f32)
        f_lo = f_lo + gk * lax.bitcast_convert_type(lax.shift_left(p, 16), _f32)
    f = jnp.concatenate([f_hi, f_lo], axis=1)
    gate2 = mod_ref[...][:, 5 * D:6 * D]
    pre = ALPHA * h_ref[...] + _per_seq(f, gate2, jnp.multiply)
    o_ref[...] = _layer_norm(pre, g2_ref[...], b2_ref[...])


def _combine(h, mod3, gates, ln2_g, ln2_b, y4, token0, rows_per_mod, row0, out_so_far):
    t = COMBINE_TILE
    hoff = row0 // t
    goff = (token0 + row0) // t
    g = mod3.shape[1]
    tiles_per_mod = rows_per_mod // t
    in_specs = [pl.BlockSpec((t, D), lambda i: (i + hoff, 0)),
                pl.BlockSpec((None, g, 6 * D), lambda i: ((i + hoff) // tiles_per_mod, 0, 0)),
                pl.BlockSpec((t, LANES), lambda i: (i + goff, 0)),
                pl.BlockSpec((1, D), lambda i: (0, 0)),
                pl.BlockSpec((1, D), lambda i: (0, 0)),
                pl.BlockSpec((TOP_K, t, D // 2), lambda i: (0, i, 0))]
    args = [h, mod3, gates, ln2_g, ln2_b, y4]
    aliases = {}
    if out_so_far is not None:
        in_specs.append(pl.BlockSpec(memory_space=pl.ANY))
        args.append(out_so_far)
        aliases = {len(args) - 1: 0}
    return pl.pallas_call(
        _combine_kernel,
        out_shape=jax.ShapeDtypeStruct(h.shape, _f32),
        grid=(y4.shape[1] // t,),
        in_specs=in_specs,
        out_specs=pl.BlockSpec((t, D), lambda i: (i + hoff, 0)),
        input_output_aliases=aliases,
        compiler_params=pltpu.CompilerParams(vmem_limit_bytes=VMEM_LIMIT),
        name="combine",
    )(*args)


def _time_major(a):
    return a.transpose(1, 0, 2)


def kernel(x_prompt, x_sample, c_prompt, c_sample, state_conv, state_pool, w_ada, b_ada, w_in,
           conv_w, w_out_a, w_pool, ls_pool, w_out_b, w_o, ln1_g, ln1_b, w_router, b_router,
           w_up, b_up, w_down, b_down, ln2_g, ln2_b):
    n_seq_p, seq, _ = x_prompt.shape
    n_seq_s, dec_seq, _ = x_sample.shape
    n_p, n_s = n_seq_p * seq, n_seq_s * dec_seq
    n = n_p + n_s
    n_slots = TOP_K * n + E * SLOT_PAD
    l = 0

    mod = _ada(jnp.concatenate([c_prompt, c_sample], axis=0), w_ada[l], b_ada[l][None])
    mod_p = mod[:n_seq_p][:, None, :]
    mod_s = mod[n_seq_p:][None]

    weights = (
        w_in[l].astype(_bf16), conv_w[l], w_out_a[l].astype(_bf16), w_pool[l].astype(_bf16),
        ls_pool[l][None], w_out_b[l].astype(_bf16), w_o[l].astype(_bf16), ln1_g[l][None], ln1_b[l][None],
        jnp.pad(w_router[l], ((0, 0), (0, LANES - E))), jnp.pad(b_router[l], (0, LANES - E))[None],
    )
    hc_p, hp_p = _hist_steps(CONV_HIST, 1), _hist_steps(POOL_HIST, 1)
    h_p, v_p, lg_p, nc_p, np_p = _mixer(
        x_prompt.reshape(n_p, D), mod_p, jnp.zeros((n_seq_p * hc_p, C), _f32),
        jnp.zeros((n_seq_p * hp_p, C), _f32), weights, PROMPT_ROW_TILE, PROMPT_SUB_TILES, 0)
    hc_s, hp_s = _hist_steps(CONV_HIST, n_seq_s), _hist_steps(POOL_HIST, n_seq_s)
    hist_c = jnp.pad(_time_major(state_conv[l]), ((hc_s - CONV_HIST, 0), (0, 0), (0, 0)))
    hist_p = jnp.pad(_time_major(state_pool[l]), ((hp_s - POOL_HIST, 0), (0, 0), (0, 0)))
    h_s, v_s, lg_s, nc_s, np_s = _mixer(
        _time_major(x_sample).reshape(n_s, D), mod_s, hist_c.reshape(hc_s * n_seq_s, C),
        hist_p.reshape(hp_s * n_seq_s, C), weights, SAMPLE_ROW_TILE, 1, PAST_LEN)

    dest8, gates3, meta = _plan(lg_p, lg_s)
    cnt, row0, padded, nxt = (meta[i, :E] for i in range(4))
    gates = gates3.reshape(n, LANES)

    dests = [dest8[:, k, :].reshape(n) for k in range(TOP_K)]
    xpad = _dispatch(v_p, v_s, dests, n_slots)
    ypad = _experts(cnt, row0, padded, nxt, xpad, w_up[l], b_up[l][:, None, :], w_down[l], b_down[l][:, None, :])

    def gathered(t0, rows):
        idx = jnp.concatenate([dk[t0:t0 + rows] for dk in dests])
        return _gather_rows(ypad, idx).reshape(TOP_K, rows, D // 2)

    chunk = n_p // COMBINE_CHUNKS
    y4_p = [gathered(c * chunk, chunk) for c in range(COMBINE_CHUNKS)]
    y4_s = gathered(n_p, n_s)
    g2, b2 = ln2_g[l][None], ln2_b[l][None]
    y_p = None
    for c in range(COMBINE_CHUNKS):
        y_p = _combine(h_p, mod_p, gates, g2, b2, y4_p[c], 0, seq, c * chunk, y_p)
    y_s = _combine(h_s, mod_s, gates, g2, b2, y4_s, n_p, n_s, 0, None)

    y_prompt = y_p.reshape(n_seq_p, seq, D)
    y_sample = _time_major(y_s.reshape(dec_seq, n_seq_s, D))
    new_conv_p = nc_p.reshape(n_seq_p, hc_p, C)[:, hc_p - CONV_HIST:][None]
    new_pool_p = np_p.reshape(n_seq_p, hp_p, C)[:, hp_p - POOL_HIST:][None]
    new_conv_s = _time_major(nc_s.reshape(hc_s, n_seq_s, C)[hc_s - CONV_HIST:])[None]
    new_pool_s = _time_major(np_s.reshape(hp_s, n_seq_s, C)[hp_s - POOL_HIST:])[None]
    return (y_prompt, y_sample, new_conv_p, new_pool_p, new_conv_s, new_pool_s)
```

```python
import functools

import jax
import jax.numpy as jnp
from jax import lax
from jax.experimental import pallas as pl
from jax.experimental.pallas import tpu as pltpu
from jax.experimental.pallas import tpu_sc as plsc

D = 1024
C = 512
N_GROUPS = 4
GROUP = C // N_GROUPS
CONV_HIST = 2
POOL_HIST = 15
E = 32
TOP_K = 4
F = 1024
SWIGLU_LIMIT = 7.0
SWIGLU_ALPHA = 1.702
LN_EPS = 1e-5
DEPTH = 1
ALPHA = (2 * DEPTH) ** 0.25
PAST_LEN = 16384

LANES = 128
SUBLANES = 8
ROW_TILE = 512
PROMPT_ROW_TILE = 1024
PROMPT_SUB_TILES = 1
SAMPLE_ROW_TILE = 256
SLOT_PAD = 256
BIG_BLOCK = 1024
COMBINE_TILE = 512
COMBINE_CHUNKS = 4
SC_WINDOW = 128
VMEM_LIMIT = 56 * 1024 * 1024

_f32 = jnp.float32
_bf16 = jnp.bfloat16


def _dot(a, b):
    return jnp.dot(a, b, preferred_element_type=_f32)


def _dot_exact(a, b):
    return lax.dot_general(a, b, (((1,), (0,)), ((), ())),
                           precision=lax.Precision.HIGHEST, preferred_element_type=_f32)


def _dot_split(a, b):
    a_hi, b_hi = a.astype(_bf16), b.astype(_bf16)
    a_lo = (a - a_hi.astype(_f32)).astype(_bf16)
    b_lo = (b - b_hi.astype(_f32)).astype(_bf16)
    return _dot(a_hi, b_hi) + _dot(a_lo, b_hi) + _dot(a_hi, b_lo)


def _dot_split_narrow(a, b):
    a_hi, b_hi = a.astype(_bf16), b.astype(_bf16)
    a_lo = (a - a_hi.astype(_f32)).astype(_bf16)
    b_lo = (b - b_hi.astype(_f32)).astype(_bf16)
    b2 = jnp.concatenate([b_hi, b_lo], axis=1)
    p = _dot(a_hi, b2) + _dot(a_lo, b2)
    return p[:, :LANES] + p[:, LANES:]


def _pack_rows(x):
    w = x.shape[1] // 2
    hi = lax.bitcast_convert_type(x[:, :w].astype(_bf16).astype(_f32), jnp.int32)
    lo = lax.bitcast_convert_type(x[:, w:].astype(_bf16).astype(_f32), jnp.int32)
    return hi | lax.shift_right_logical(lo, 16)


def _unpack_rows(p):
    hi = lax.bitcast_convert_type(p & jnp.int32(-65536), _f32)
    lo = lax.bitcast_convert_type(lax.shift_left(p, 16), _f32)
    return jnp.concatenate([hi, lo], axis=1).astype(_bf16)


def _sigmoid(x):
    return 0.5 * jnp.tanh(0.5 * x) + 0.5


def _per_seq(x, m, op):
    g = m.shape[0]
    if g == 1:
        return op(x, m)
    r, n = x.shape
    return op(x.reshape(r // g, g, n), m[None]).reshape(r, n)


def _layer_norm(x, g, b):
    mu = jnp.mean(x, axis=-1, keepdims=True)
    xc = x - mu
    var = jnp.mean(xc * xc, axis=-1, keepdims=True)
    return xc * lax.rsqrt(var + LN_EPS) * g + b


def _hist_steps(needed, g):
    return -(-needed * g // SUBLANES) * SUBLANES // g


def _ada_kernel(c_ref, w_ref, b_ref, o_ref):
    c = c_ref[...]
    o_ref[...] = _dot_split(c * _sigmoid(c), w_ref[...]) + b_ref[...]


def _ada(c, w_ada, b_ada):
    rows = c.shape[0]
    cols = w_ada.shape[1]
    bn = 1536
    return pl.pallas_call(
        _ada_kernel,
        out_shape=jax.ShapeDtypeStruct((rows, cols), _f32),
        grid=(cols // bn,),
        in_specs=[pl.BlockSpec((rows, D), lambda j: (0, 0)),
                  pl.BlockSpec((D, bn), lambda j: (0, j)),
                  pl.BlockSpec((1, bn), lambda j: (0, j))],
        out_specs=pl.BlockSpec((rows, bn), lambda j: (0, j)),
        compiler_params=pltpu.CompilerParams(vmem_limit_bytes=VMEM_LIMIT),
        name="ada",
    )(c, w_ada, b_ada)


def _mixer_kernel(g, tiles_per_seq, start_pos, n_sub,
                  x_ref, mod_ref, hc_ref, hp_ref,
                  win_ref, cw_ref, woa_ref, wpool_ref, ls_ref, wob_ref, wo_ref, g1_ref, b1_ref,
                  wr_ref, br_ref,
                  h_ref, v_ref, lg_ref, nc_ref, np_ref, zbuf, pbuf):
    r = x_ref.shape[0]
    hrc = hc_ref.shape[0]
    hrp = hp_ref.shape[0]
    j = pl.program_id(0) % tiles_per_seq

    @pl.when(j == 0)
    def _():
        zbuf[pl.ds(0, hrc), :] = hc_ref[...]
        pbuf[pl.ds(0, hrp), :] = hp_ref[...]

    @pl.when(j != 0)
    def _():
        zt = zbuf[pl.ds(r, hrc), :]
        pt = pbuf[pl.ds(r, hrp), :]
        zbuf[pl.ds(0, hrc), :] = zt
        pbuf[pl.ds(0, hrp), :] = pt

    m = mod_ref[...]
    shift1, scale1, gate1 = m[:, 0:D], m[:, D:2 * D], m[:, 2 * D:3 * D]
    shift2, scale2 = m[:, 3 * D:4 * D], m[:, 4 * D:5 * D]

    rs = r // n_sub
    for s in range(n_sub):
        rows = pl.ds(s * rs, rs)
        x = x_ref[rows, :]
        u = _per_seq(_per_seq(x, 1.0 + scale1, jnp.multiply), shift1, jnp.add).astype(_bf16)

        z = _dot(u, win_ref[:, C:2 * C]) * _dot(u, win_ref[:, 2 * C:3 * C])
        zrow = hrc + s * rs
        zbuf[pl.ds(zrow, rs), :] = z
        cw = cw_ref[...]
        conv = (cw[0:1] * zbuf[pl.ds(zrow - 2 * g, rs), :] + cw[1:2] * zbuf[pl.ds(zrow - g, rs), :]
                + cw[2:3] * z)
        y_a = _dot((_dot(u, win_ref[:, 0:C]) * conv).astype(_bf16), woa_ref[...])

        xp = _dot(u, win_ref[:, 3 * C:4 * C])
        prow = hrp + s * rs
        pbuf[pl.ds(prow, rs), :] = xp
        pos = (start_pos + j * (r // g) + s * (rs // g)
               + lax.broadcasted_iota(jnp.int32, (rs, 1), 0) // g)
        acc = xp
        diffs = []
        for grp in range(N_GROUPS):
            lo = grp * GROUP
            wdw = 2 ** (grp + 1)
            for back in range(wdw // 2, wdw):
                sh = pbuf[pl.ds(prow - back * g, rs), lo:C]
                acc = jnp.concatenate([acc[:, 0:lo], acc[:, lo:C] + sh], axis=1) if lo else acc + sh
            inv_cnt = 1.0 / jnp.minimum(wdw, pos + 1).astype(_f32)
            diffs.append((acc[:, lo:lo + GROUP] * inv_cnt - xp[:, lo:lo + GROUP]).astype(_bf16))
        yg = _dot(jnp.concatenate(diffs, axis=1), wpool_ref[...])
        y_b = _dot((yg * ls_ref[...]).astype(_bf16), wob_ref[...])

        g_a = _dot(u, win_ref[:, 4 * C:4 * C + D])
        g_b = _dot(u, win_ref[:, 4 * C + D:4 * C + 2 * D])
        merged = _sigmoid(g_a) * y_a + _sigmoid(g_b) * y_b
        o = _dot(merged.astype(_bf16), wo_ref[...])
        h = _layer_norm(ALPHA * x + _per_seq(o, gate1, jnp.multiply), g1_ref[...], b1_ref[...])
        v = _per_seq(_per_seq(h, 1.0 + scale2, jnp.multiply), shift2, jnp.add)
        h_ref[rows, :] = h
        v_ref[rows, :] = _pack_rows(v)
        lg_ref[rows, :] = _dot_split_narrow(v, wr_ref[...]) + br_ref[...]
    nc_ref[...] = zbuf[pl.ds(r, hrc), :]
    np_ref[...] = pbuf[pl.ds(r, hrp), :]


def _mixer(x2, mod3, hc, hp, weights, row_tile, n_sub, start_pos):
    n = x2.shape[0]
    n_mod, g, _ = mod3.shape
    hrc, hrp = hc.shape[0] // n_mod, hp.shape[0] // n_mod
    tiles_per_seq = n // n_mod // row_tile
    once = dict(pipeline_mode=pl.Buffered(1)) if n_mod == 1 else {}

    def full(a):
        nd = a.ndim
        return pl.BlockSpec(a.shape, lambda i: (0,) * nd)

    def seq_block(rows, **kw):
        return pl.BlockSpec((rows, C), lambda i: (i // tiles_per_seq, 0), **kw)

    def row_block(cols):
        return pl.BlockSpec((row_tile, cols), lambda i: (i, 0))

    return pl.pallas_call(
        functools.partial(_mixer_kernel, g, tiles_per_seq, start_pos, n_sub),
        out_shape=[
            jax.ShapeDtypeStruct((n, D), _f32),
            jax.ShapeDtypeStruct((n, D // 2), jnp.int32),
            jax.ShapeDtypeStruct((n, LANES), _f32),
            jax.ShapeDtypeStruct(hc.shape, _f32),
            jax.ShapeDtypeStruct(hp.shape, _f32),
        ],
        grid=(n // row_tile,),
        in_specs=[row_block(D),
                  pl.BlockSpec((None, g, 6 * D), lambda i: (i // tiles_per_seq, 0, 0), **once),
                  seq_block(hrc, **once), seq_block(hrp, **once)] + [full(a) for a in weights],
        out_specs=[row_block(D), row_block(D // 2), row_block(LANES), seq_block(hrc), seq_block(hrp)],
        scratch_shapes=[pltpu.VMEM((hrc + row_tile, C), _f32), pltpu.VMEM((hrp + row_tile, C), _f32)],
        compiler_params=pltpu.CompilerParams(vmem_limit_bytes=VMEM_LIMIT),
        name="mixer",
    )(x2, mod3, hc, hp, *weights)


def _plan_kernel(lgp_ref, lgs_ref, dest_ref, gate_ref, meta_ref, idx_s, rank_s):
    t = ROW_TILE
    e_iota = lax.broadcasted_iota(jnp.int32, (E, t), 0)
    tri = (lax.broadcasted_iota(jnp.int32, (t, t), 0)
           < lax.broadcasted_iota(jnp.int32, (t, t), 1)).astype(_f32).astype(_bf16)
    zeros_rest = jnp.zeros((LANES - TOP_K, t), _f32)

    def tile_body(lg_ref, off, i, carry):
        lt = lg_ref[i].T[0:E, :]
        vals, idxs = [], []
        for _ in range(TOP_K):
            mx = jnp.max(lt, axis=0, keepdims=True)
            ix = jnp.min(jnp.where(lt == mx, e_iota, E), axis=0, keepdims=True)
            vals.append(mx)
            idxs.append(ix)
            lt = jnp.where(e_iota == ix, -jnp.inf, lt)
        ex = [jnp.exp(vk - vals[0]) for vk in vals]
        den = ex[0] + ex[1] + ex[2] + ex[3]
        gates = [ek / den for ek in ex]
        gate_ref[off + i] = jnp.concatenate(gates + [zeros_rest], axis=0).T

        ohs = [(e_iota == ix) for ix in idxs]
        oh = (ohs[0] | ohs[1] | ohs[2] | ohs[3]).astype(_f32)
        before = _dot(oh.astype(_bf16), tri) + carry
        ranks = [jnp.sum(jnp.where(o, before, 0.0), axis=0, keepdims=True) for o in ohs]
        idx_s[off + i] = jnp.concatenate(idxs + idxs, axis=0)
        rank_s[off + i] = jnp.concatenate(ranks + ranks, axis=0).astype(jnp.int32)
        return carry + jnp.sum(oh, axis=1, keepdims=True)

    n_p, n_s = lgp_ref.shape[0], lgs_ref.shape[0]
    counts = lax.fori_loop(0, n_p, functools.partial(tile_body, lgp_ref, 0), jnp.zeros((E, 1), _f32))
    counts = lax.fori_loop(0, n_s, functools.partial(tile_body, lgs_ref, n_p), counts)
    padded = jnp.ceil(counts / SLOT_PAD) * SLOT_PAD
    low = (lax.broadcasted_iota(jnp.int32, (E, E), 1)
           <= lax.broadcasted_iota(jnp.int32, (E, E), 0)).astype(_f32)
    pad_end = _dot_exact(low, jnp.broadcast_to(padded, (E, LANES)))[:, 0:1]
    pad_start = pad_end - padded

    def dest_body(i, c):
        ix = idx_s[i]
        rk = rank_s[i]
        rows = []
        for k in range(TOP_K):
            st = jnp.sum(jnp.where(e_iota == ix[k:k + 1], pad_start, 0.0), axis=0, keepdims=True)
            rows.append(st.astype(jnp.int32) + rk[k:k + 1])
        dest_ref[i] = jnp.concatenate(rows + rows, axis=0)
        return c

    lax.fori_loop(0, n_p + n_s, dest_body, 0)

    sub = lax.broadcasted_iota(jnp.int32, (E, LANES), 0)
    lane = lax.broadcasted_iota(jnp.int32, (E, LANES), 1)

    def to_lanes(col):
        return jnp.sum(jnp.where(sub == lane, col, 0.0), axis=0, keepdims=True).astype(jnp.int32)

    later = jnp.min(jnp.where((sub > lane) & (counts > 0.0), sub, E), axis=0, keepdims=True)
    meta_ref[...] = jnp.concatenate(
        [to_lanes(counts), to_lanes(pad_start), to_lanes(padded), later,
         jnp.zeros((SUBLANES - 4, LANES), jnp.int32)], axis=0)


def _plan(logits_p, logits_s):
    n_tiles = (logits_p.shape[0] + logits_s.shape[0]) // ROW_TILE
    return pl.pallas_call(
        _plan_kernel,
        out_shape=[
            jax.ShapeDtypeStruct((n_tiles, 2 * TOP_K, ROW_TILE), jnp.int32),
            jax.ShapeDtypeStruct((n_tiles, ROW_TILE, LANES), _f32),
            jax.ShapeDtypeStruct((SUBLANES, LANES), jnp.int32),
        ],
        scratch_shapes=[pltpu.VMEM((n_tiles, 2 * TOP_K, ROW_TILE), jnp.int32),
                        pltpu.VMEM((n_tiles, 2 * TOP_K, ROW_TILE), jnp.int32)],
        compiler_params=pltpu.CompilerParams(vmem_limit_bytes=VMEM_LIMIT),
        name="plan",
    )(logits_p.reshape(-1, ROW_TILE, LANES), logits_s.reshape(-1, ROW_TILE, LANES))


def _dispatch(v_p, v_s, dests, n_rows_out):
    n_p, n_s = v_p.shape[0], v_s.shape[0]
    width = v_p.shape[1]
    w = SC_WINDOW
    n_pw, n_windows = n_p // w, (n_p + n_s) // w
    mesh = plsc.VectorSubcoreMesh(core_axis_name="core", subcore_axis_name="subcore")
    n_workers = mesh.num_cores * mesh.num_subcores

    @functools.partial(
        pl.kernel, mesh=mesh, name="dispatch",
        out_type=jax.ShapeDtypeStruct((n_rows_out, width), jnp.int32),
        scratch_types=[pltpu.VMEM((w, width), jnp.int32)] + [pltpu.VMEM((w,), jnp.int32)] * TOP_K
        + [pltpu.SemaphoreType.DMA])
    def scatter_rows(vp_hbm, vs_hbm, d0_hbm, d1_hbm, d2_hbm, d3_hbm, o_hbm, rows, i0, i1, i2, i3, sem):
        worker = lax.axis_index("subcore") * mesh.num_cores + lax.axis_index("core")
        idx = (i0, i1, i2, i3)

        def scatter_window(c):
            t0 = pl.multiple_of(c * w, w)
            for d_hbm, iv in zip((d0_hbm, d1_hbm, d2_hbm, d3_hbm), idx):
                pltpu.sync_copy(d_hbm.at[pl.ds(t0, w)], iv)
            copies = [pltpu.async_copy(rows, o_hbm.at[iv], sem) for iv in idx]
            for cp in copies:
                cp.wait()

        for j in range(-(-n_windows // n_workers)):
            c = j * n_workers + worker

            @pl.when(c < n_pw)
            def _():
                pltpu.sync_copy(vp_hbm.at[pl.ds(pl.multiple_of(c * w, w), w)], rows)
                scatter_window(c)

            @pl.when((c >= n_pw) & (c < n_windows))
            def _():
                pltpu.sync_copy(vs_hbm.at[pl.ds(pl.multiple_of((c - n_pw) * w, w), w)], rows)
                scatter_window(c)

    return scatter_rows(v_p, v_s, *dests)


def _experts_kernel(cnt_ref, row0_ref, pad_ref, nxt_ref, x_hbm, wu_hbm, bu_ref, wd_hbm, bd_ref, y_hbm,
                    xbuf, ybuf, wu_st, wd_st, wu_bf, wd_bf, ysz, xsem, ysem, wsem):
    def w_fetch(e):
        return (pltpu.make_async_copy(wu_hbm.at[e], wu_st, wsem.at[0]),
                pltpu.make_async_copy(wd_hbm.at[e], wd_st, wsem.at[1]))

    def x_fetch(row, size, slot):
        rows = pl.ds(pl.multiple_of(row, SLOT_PAD), size)
        return pltpu.make_async_copy(x_hbm.at[rows, :], xbuf.at[slot, pl.ds(0, size), :], xsem.at[slot])

    def y_store(row, size, slot):
        rows = pl.ds(pl.multiple_of(row, SLOT_PAD), size)
        return pltpu.make_async_copy(ybuf.at[slot, pl.ds(0, size), :], y_hbm.at[rows, :], ysem.at[slot])

    def y_wait(slot):
        for size in (BIG_BLOCK, SLOT_PAD):
            @pl.when(ysz[slot] == size)
            def _():
                y_store(0, size, slot).wait()

    def fetch_first(e, slot):
        @pl.when(pad_ref[e] >= BIG_BLOCK)
        def _():
            x_fetch(row0_ref[e], BIG_BLOCK, slot).start()

        @pl.when((pad_ref[e] > 0) & (pad_ref[e] < BIG_BLOCK))
        def _():
            x_fetch(row0_ref[e], SLOT_PAD, slot).start()

    ysz[0] = 0
    ysz[1] = 0
    for c in w_fetch(0):
        c.start()
    fetch_first(jnp.where(pad_ref[0] > 0, 0, nxt_ref[0]), 0)

    def expert_body(e, n_done):
        for c in w_fetch(e):
            c.wait()
        padded = pad_ref[e]

        @pl.when(padded > 0)
        def _():
            wu_bf[...] = wu_st[...].astype(_bf16)
            wd_bf[...] = wd_st[...].astype(_bf16)

        @pl.when(e + 1 < E)
        def _():
            for c in w_fetch(e + 1):
                c.start()

        b_up = bu_ref[e]
        b_down = bd_ref[e]
        row0 = row0_ref[e]
        n_big = padded // BIG_BLOCK
        n_small = (padded - n_big * BIG_BLOCK) // SLOT_PAD
        small0 = row0 + n_big * BIG_BLOCK
        nxt = nxt_ref[e]

        def pass_body(size, j, n_done):
            slot = n_done % 2
            big = size == BIG_BLOCK
            row = row0 + j * BIG_BLOCK if big else small0 + j * SLOT_PAD
            x_fetch(row, size, slot).wait()

            more = j + 1 < (n_big if big else n_small)
            tail = (n_small > 0) if big else False

            @pl.when(more)
            def _():
                x_fetch(row + size, size, 1 - slot).start()

            if big:
                @pl.when(jnp.logical_not(more) & tail)
                def _():
                    x_fetch(small0, SLOT_PAD, 1 - slot).start()

            @pl.when(jnp.logical_not(more) & jnp.logical_not(tail) & (nxt < E))
            def _():
                fetch_first(jnp.minimum(nxt, E - 1), 1 - slot)

            rows = lax.broadcasted_iota(jnp.int32, (size, 1), 0)
            x = _unpack_rows(jnp.where(rows < cnt_ref[e] - (row - row0), xbuf[slot, pl.ds(0, size), :], 0))
            hcat = _dot(x, wu_bf[...]) + b_up
            glu = jnp.minimum(hcat[:, 0:F], SWIGLU_LIMIT)
            lin = jnp.clip(hcat[:, F:2 * F], -SWIGLU_LIMIT, SWIGLU_LIMIT)
            act = glu * _sigmoid(SWIGLU_ALPHA * glu) * (lin + 1.0)
            y = _pack_rows(_dot(act.astype(_bf16), wd_bf[...]) + b_down)

            y_wait(slot)
            ybuf[slot, pl.ds(0, size), :] = y
            y_store(row, size, slot).start()
            ysz[slot] = size
            return n_done + 1

        n_done = lax.fori_loop(0, n_big, functools.partial(pass_body, BIG_BLOCK), n_done)
        return lax.fori_loop(0, n_small, functools.partial(pass_body, SLOT_PAD), n_done)

    lax.fori_loop(0, E, expert_body, 0)
    y_wait(0)
    y_wait(1)


def _experts(cnt, row0, padded, nxt, xpad, w_up, b_up, w_down, b_down):
    def full(a):
        nd = a.ndim
        return pl.BlockSpec(a.shape, lambda i, *_: (0,) * nd)

    return pl.pallas_call(
        _experts_kernel,
        out_shape=jax.ShapeDtypeStruct(xpad.shape, jnp.int32),
        grid_spec=pltpu.PrefetchScalarGridSpec(
            num_scalar_prefetch=4,
            grid=(1,),
            in_specs=[pl.BlockSpec(memory_space=pl.ANY),
                      pl.BlockSpec(memory_space=pl.ANY), full(b_up),
                      pl.BlockSpec(memory_space=pl.ANY), full(b_down)],
            out_specs=pl.BlockSpec(memory_space=pl.ANY),
            scratch_shapes=[pltpu.VMEM((2, BIG_BLOCK, D // 2), jnp.int32),
                            pltpu.VMEM((2, BIG_BLOCK, D // 2), jnp.int32),
                            pltpu.VMEM((D, 2 * F), _f32), pltpu.VMEM((F, D), _f32),
                            pltpu.VMEM((D, 2 * F), _bf16), pltpu.VMEM((F, D), _bf16),
                            pltpu.SMEM((2,), jnp.int32),
                            pltpu.SemaphoreType.DMA((2,)), pltpu.SemaphoreType.DMA((2,)),
                            pltpu.SemaphoreType.DMA((2,))],
        ),
        compiler_params=pltpu.CompilerParams(vmem_limit_bytes=VMEM_LIMIT),
        name="experts",
    )(cnt, row0, padded, nxt, xpad, w_up, b_up, w_down, b_down)


def _gather_rows(ypad, dest_all):
    n_out = dest_all.shape[0]
    width = ypad.shape[1]
    w = SC_WINDOW
    mesh = plsc.VectorSubcoreMesh(core_axis_name="core", subcore_axis_name="subcore")
    n_workers = mesh.num_cores * mesh.num_subcores
    n_windows = n_out // w

    @functools.partial(
        pl.kernel, mesh=mesh, name="gather_rows",
        out_type=jax.ShapeDtypeStruct((n_out, width), jnp.int32),
        scratch_types=[pltpu.VMEM((w, width), jnp.int32), pltpu.VMEM((w,), jnp.int32), pltpu.SemaphoreType.DMA])
    def gather_rows(y_hbm, d_hbm, o_hbm, rows, iv, sem):
        worker = lax.axis_index("subcore") * mesh.num_cores + lax.axis_index("core")

        @pl.loop(0, n_windows // n_workers)
        def _(j):
            r0 = pl.multiple_of((j * n_workers + worker) * w, w)
            pltpu.sync_copy(d_hbm.at[pl.ds(r0, w)], iv)
            pltpu.async_copy(y_hbm.at[iv], rows, sem).wait()
            pltpu.sync_copy(rows, o_hbm.at[pl.ds(r0, w)])

    assert n_windows % n_workers == 0
    return gather_rows(ypad, dest_all)


def _combine_kernel(h_ref, mod_ref, gate_ref, g2_ref, b2_ref, y_ref, *aliased_and_out):
    o_ref = aliased_and_out[-1]
    gates = gate_ref[...]
    half = D // 2
    f_hi = jnp.zeros((h_ref.shape[0], half), _f32)
    f_lo = jnp.zeros((h_ref.shape[0], half), _f32)
    for k in range(TOP_K):
        p = y_ref[k]
        gk = gates[:, k:k + 1]
        f_hi = f_hi + gk * lax.bitcast_convert_type(p & jnp.int32(-65536), _f32)
        f_lo = f_lo + gk * lax.bitcast_convert_type(lax.shift_left(p, 16), _f32)
    f = jnp.concatenate([f_hi, f_lo], axis=1)
    gate2 = mod_ref[...][:, 5 * D:6 * D]
    pre = ALPHA * h_ref[...] + _per_seq(f, gate2, jnp.multiply)
    o_ref[...] = _layer_norm(pre, g2_ref[...], b2_ref[...])


def _combine(h, mod3, gates, ln2_g, ln2_b, y4, token0, rows_per_mod, row0, out_so_far):
    t = COMBINE_TILE
    hoff = row0 // t
    goff = (token0 + row0) // t
    g = mod3.shape[1]
    tiles_per_mod = rows_per_mod // t
    in_specs = [pl.BlockSpec((t, D), lambda i: (i + hoff, 0)),
                pl.BlockSpec((None, g, 6 * D), lambda i: ((i + hoff) // tiles_per_mod, 0, 0)),
                pl.BlockSpec((t, LANES), lambda i: (i + goff, 0)),
                pl.BlockSpec((1, D), lambda i: (0, 0)),
                pl.BlockSpec((1, D), lambda i: (0, 0)),
                pl.BlockSpec((TOP_K, t, D // 2), lambda i: (0, i, 0))]
    args = [h, mod3, gates, ln2_g, ln2_b, y4]
    aliases = {}
    if out_so_far is not None:
        in_specs.append(pl.BlockSpec(memory_space=pl.ANY))
        args.append(out_so_far)
        aliases = {len(args) - 1: 0}
    return pl.pallas_call(
        _combine_kernel,
        out_shape=jax.ShapeDtypeStruct(h.shape, _f32),
        grid=(y4.shape[1] // t,),
        in_specs=in_specs,
        out_specs=pl.BlockSpec((t, D), lambda i: (i + hoff, 0)),
        input_output_aliases=aliases,
        compiler_params=pltpu.CompilerParams(vmem_limit_bytes=VMEM_LIMIT),
        name="combine",
    )(*args)


def _time_major(a):
    return a.transpose(1, 0, 2)


def kernel(x_prompt, x_sample, c_prompt, c_sample, state_conv, state_pool, w_ada, b_ada, w_in,
           conv_w, w_out_a, w_pool, ls_pool, w_out_b, w_o, ln1_g, ln1_b, w_router, b_router,
           w_up, b_up, w_down, b_down, ln2_g, ln2_b):
    n_seq_p, seq, _ = x_prompt.shape
    n_seq_s, dec_seq, _ = x_sample.shape
    n_p, n_s = n_seq_p * seq, n_seq_s * dec_seq
    n = n_p + n_s
    n_slots = TOP_K * n + E * SLOT_PAD
    l = 0

    mod = _ada(jnp.concatenate([c_prompt, c_sample], axis=0), w_ada[l], b_ada[l][None])
    mod_p = mod[:n_seq_p][:, None, :]
    mod_s = mod[n_seq_p:][None]

    w_pool_bd = (jnp.eye(N_GROUPS, dtype=_bf16)[:, None, :, None]
                 * w_pool[l].astype(_bf16)[:, :, None, :]).reshape(C, C)
    weights = (
        w_in[l].astype(_bf16), conv_w[l], w_out_a[l].astype(_bf16), w_pool_bd,
        ls_pool[l][None], w_out_b[l].astype(_bf16), w_o[l].astype(_bf16), ln1_g[l][None], ln1_b[l][None],
        jnp.pad(w_router[l], ((0, 0), (0, LANES - E))), jnp.pad(b_router[l], (0, LANES - E))[None],
    )
    hc_p, hp_p = _hist_steps(CONV_HIST, 1), _hist_steps(POOL_HIST, 1)
    h_p, v_p, lg_p, nc_p, np_p = _mixer(
        x_prompt.reshape(n_p, D), mod_p, jnp.zeros((n_seq_p * hc_p, C), _f32),
        jnp.zeros((n_seq_p * hp_p, C), _f32), weights, PROMPT_ROW_TILE, PROMPT_SUB_TILES, 0)
    hc_s, hp_s = _hist_steps(CONV_HIST, n_seq_s), _hist_steps(POOL_HIST, n_seq_s)
    hist_c = jnp.pad(_time_major(state_conv[l]), ((hc_s - CONV_HIST, 0), (0, 0), (0, 0)))
    hist_p = jnp.pad(_time_major(state_pool[l]), ((hp_s - POOL_HIST, 0), (0, 0), (0, 0)))
    h_s, v_s, lg_s, nc_s, np_s = _mixer(
        _time_major(x_sample).reshape(n_s, D), mod_s, hist_c.reshape(hc_s * n_seq_s, C),
        hist_p.reshape(hp_s * n_seq_s, C), weights, SAMPLE_ROW_TILE, 1, PAST_LEN)

    dest8, gates3, meta = _plan(lg_p, lg_s)
    cnt, row0, padded, nxt = (meta[i, :E] for i in range(4))
    gates = gates3.reshape(n, LANES)

    dests = [dest8[:, k, :].reshape(n) for k in range(TOP_K)]
    xpad = _dispatch(v_p, v_s, dests, n_slots)
    ypad = _experts(cnt, row0, padded, nxt, xpad, w_up[l], b_up[l][:, None, :], w_down[l], b_down[l][:, None, :])

    def gathered(t0, rows):
        idx = jnp.concatenate([dk[t0:t0 + rows] for dk in dests])
        return _gather_rows(ypad, idx).reshape(TOP_K, rows, D // 2)

    chunk = n_p // COMBINE_CHUNKS
    y4_p = [gathered(c * chunk, chunk) for c in range(COMBINE_CHUNKS)]
    y4_s = gathered(n_p, n_s)
    g2, b2 = ln2_g[l][None], ln2_b[l][None]
    y_p = None
    for c in range(COMBINE_CHUNKS):
        y_p = _combine(h_p, mod_p, gates, g2, b2, y4_p[c], 0, seq, c * chunk, y_p)
    y_s = _combine(h_s, mod_s, gates, g2, b2, y4_s, n_p, n_s, 0, None)

    y_prompt = y_p.reshape(n_seq_p, seq, D)
    y_sample = _time_major(y_s.reshape(dec_seq, n_seq_s, D))
    new_conv_p = nc_p.reshape(n_seq_p, hc_p, C)[:, hc_p - CONV_HIST:][None]
    new_pool_p = np_p.reshape(n_seq_p, hp_p, C)[:, hp_p - POOL_HIST:][None]
    new_conv_s = _time_major(nc_s.reshape(hc_s, n_seq_s, C)[hc_s - CONV_HIST:])[None]
    new_pool_s = _time_major(np_s.reshape(hp_s, n_seq_s, C)[hp_s - POOL_HIST:])[None]
    return (y_prompt, y_sample, new_conv_p, new_pool_p, new_conv_s, new_pool_s)
```

```python
import functools

import jax
import jax.numpy as jnp
from jax import lax
from jax.experimental import pallas as pl
from jax.experimental.pallas import tpu as pltpu
from jax.experimental.pallas import tpu_sc as plsc

D = 1024
C = 512
N_GROUPS = 4
GROUP = C // N_GROUPS
CONV_HIST = 2
POOL_HIST = 15
E = 32
TOP_K = 4
F = 1024
SWIGLU_LIMIT = 7.0
SWIGLU_ALPHA = 1.702
LN_EPS = 1e-5
DEPTH = 1
ALPHA = (2 * DEPTH) ** 0.25
PAST_LEN = 16384

LANES = 128
SUBLANES = 8
ROW_TILE = 512
PROMPT_ROW_TILE = 1024
PROMPT_SUB_TILES = 1
SAMPLE_ROW_TILE = 256
SLOT_PAD = 256
BIG_BLOCK = 1024
COMBINE_TILE = 512
COMBINE_CHUNKS = 4
SC_WINDOW = 128
VMEM_LIMIT = 56 * 1024 * 1024

_f32 = jnp.float32
_bf16 = jnp.bfloat16


def _dot(a, b):
    return jnp.dot(a, b, preferred_element_type=_f32)


def _dot_exact(a, b):
    return lax.dot_general(a, b, (((1,), (0,)), ((), ())),
                           precision=lax.Precision.HIGHEST, preferred_element_type=_f32)


def _dot_split(a, b):
    a_hi, b_hi = a.astype(_bf16), b.astype(_bf16)
    a_lo = (a - a_hi.astype(_f32)).astype(_bf16)
    b_lo = (b - b_hi.astype(_f32)).astype(_bf16)
    return _dot(a_hi, b_hi) + _dot(a_lo, b_hi) + _dot(a_hi, b_lo)


def _pack_rows(x):
    w = x.shape[1] // 2
    hi = lax.bitcast_convert_type(x[:, :w].astype(_bf16).astype(_f32), jnp.int32)
    lo = lax.bitcast_convert_type(x[:, w:].astype(_bf16).astype(_f32), jnp.int32)
    return hi | lax.shift_right_logical(lo, 16)


def _unpack_rows(p):
    hi = lax.bitcast_convert_type(p & jnp.int32(-65536), _f32)
    lo = lax.bitcast_convert_type(lax.shift_left(p, 16), _f32)
    return jnp.concatenate([hi, lo], axis=1).astype(_bf16)


def _sigmoid(x):
    return 0.5 * jnp.tanh(0.5 * x) + 0.5


def _per_seq(x, m, op):
    g = m.shape[0]
    if g == 1:
        return op(x, m)
    r, n = x.shape
    return op(x.reshape(r // g, g, n), m[None]).reshape(r, n)


def _layer_norm(x, g, b):
    mu = jnp.mean(x, axis=-1, keepdims=True)
    xc = x - mu
    var = jnp.mean(xc * xc, axis=-1, keepdims=True)
    return xc * lax.rsqrt(var + LN_EPS) * g + b


def _hist_steps(needed, g):
    return -(-needed * g // SUBLANES) * SUBLANES // g


def _ada_kernel(c_ref, w_ref, b_ref, o_ref):
    c = c_ref[...]
    o_ref[...] = _dot_split(c * _sigmoid(c), w_ref[...]) + b_ref[...]


def _ada(c, w_ada, b_ada):
    rows = c.shape[0]
    cols = w_ada.shape[1]
    bn = 1536
    return pl.pallas_call(
        _ada_kernel,
        out_shape=jax.ShapeDtypeStruct((rows, cols), _f32),
        grid=(cols // bn,),
        in_specs=[pl.BlockSpec((rows, D), lambda j: (0, 0)),
                  pl.BlockSpec((D, bn), lambda j: (0, j)),
                  pl.BlockSpec((1, bn), lambda j: (0, j))],
        out_specs=pl.BlockSpec((rows, bn), lambda j: (0, j)),
        compiler_params=pltpu.CompilerParams(vmem_limit_bytes=VMEM_LIMIT),
        name="ada",
    )(c, w_ada, b_ada)


def _mixer_kernel(g, tiles_per_seq, start_pos, n_sub,
                  x_ref, mod_ref, hc_ref, hp_ref,
                  win_ref, cw_ref, woa_ref, wpool_ref, ls_ref, wob_ref, wo_ref, g1_ref, b1_ref,
                  wr_ref, br_ref,
                  h_ref, v_ref, lg_ref, nc_ref, np_ref, zbuf, pbuf):
    r = x_ref.shape[0]
    hrc = hc_ref.shape[0]
    hrp = hp_ref.shape[0]
    j = pl.program_id(0) % tiles_per_seq

    @pl.when(j == 0)
    def _():
        zbuf[pl.ds(0, hrc), :] = hc_ref[...]
        pbuf[pl.ds(0, hrp), :] = hp_ref[...]

    @pl.when(j != 0)
    def _():
        zt = zbuf[pl.ds(r, hrc), :]
        pt = pbuf[pl.ds(r, hrp), :]
        zbuf[pl.ds(0, hrc), :] = zt
        pbuf[pl.ds(0, hrp), :] = pt

    m = mod_ref[...]
    shift1, scale1, gate1 = m[:, 0:D], m[:, D:2 * D], m[:, 2 * D:3 * D]
    shift2, scale2 = m[:, 3 * D:4 * D], m[:, 4 * D:5 * D]

    rs = r // n_sub
    for s in range(n_sub):
        rows = pl.ds(s * rs, rs)
        x = x_ref[rows, :]
        u = _per_seq(_per_seq(x, 1.0 + scale1, jnp.multiply), shift1, jnp.add).astype(_bf16)

        z = _dot(u, win_ref[:, C:2 * C]) * _dot(u, win_ref[:, 2 * C:3 * C])
        zrow = hrc + s * rs
        zbuf[pl.ds(zrow, rs), :] = z
        cw = cw_ref[...]
        conv = (cw[0:1] * zbuf[pl.ds(zrow - 2 * g, rs), :] + cw[1:2] * zbuf[pl.ds(zrow - g, rs), :]
                + cw[2:3] * z)
        y_a = _dot((_dot(u, win_ref[:, 0:C]) * conv).astype(_bf16), woa_ref[...])

        xp = _dot(u, win_ref[:, 3 * C:4 * C])
        prow = hrp + s * rs
        pbuf[pl.ds(prow, rs), :] = xp
        pos = (start_pos + j * (r // g) + s * (rs // g)
               + lax.broadcasted_iota(jnp.int32, (rs, 1), 0) // g)
        acc = xp
        yg = []
        for grp in range(N_GROUPS):
            lo = grp * GROUP
            wdw = 2 ** (grp + 1)
            for back in range(wdw // 2, wdw):
                sh = pbuf[pl.ds(prow - back * g, rs), lo:C]
                acc = jnp.concatenate([acc[:, 0:lo], acc[:, lo:C] + sh], axis=1) if lo else acc + sh
            inv_cnt = 1.0 / jnp.minimum(wdw, pos + 1).astype(_f32)
            diff = acc[:, lo:lo + GROUP] * inv_cnt - xp[:, lo:lo + GROUP]
            yg.append(_dot(diff.astype(_bf16), wpool_ref[grp]))
        y_b = _dot((jnp.concatenate(yg, axis=1) * ls_ref[...]).astype(_bf16), wob_ref[...])

        g_a = _dot(u, win_ref[:, 4 * C:4 * C + D])
        g_b = _dot(u, win_ref[:, 4 * C + D:4 * C + 2 * D])
        merged = _sigmoid(g_a) * y_a + _sigmoid(g_b) * y_b
        o = _dot(merged.astype(_bf16), wo_ref[...])
        h = _layer_norm(ALPHA * x + _per_seq(o, gate1, jnp.multiply), g1_ref[...], b1_ref[...])
        v = _per_seq(_per_seq(h, 1.0 + scale2, jnp.multiply), shift2, jnp.add)
        h_ref[rows, :] = h
        v_ref[rows, :] = _pack_rows(v)
        lg_ref[rows, :] = _dot_split(v, wr_ref[...]) + br_ref[...]
    nc_ref[...] = zbuf[pl.ds(r, hrc), :]
    np_ref[...] = pbuf[pl.ds(r, hrp), :]


def _mixer(x2, mod3, hc, hp, weights, row_tile, n_sub, start_pos):
    n = x2.shape[0]
    n_mod, g, _ = mod3.shape
    hrc, hrp = hc.shape[0] // n_mod, hp.shape[0] // n_mod
    tiles_per_seq = n // n_mod // row_tile
    once = dict(pipeline_mode=pl.Buffered(1)) if n_mod == 1 else {}

    def full(a):
        nd = a.ndim
        return pl.BlockSpec(a.shape, lambda i: (0,) * nd)

    def seq_block(rows, **kw):
        return pl.BlockSpec((rows, C), lambda i: (i // tiles_per_seq, 0), **kw)

    def row_block(cols):
        return pl.BlockSpec((row_tile, cols), lambda i: (i, 0))

    return pl.pallas_call(
        functools.partial(_mixer_kernel, g, tiles_per_seq, start_pos, n_sub),
        out_shape=[
            jax.ShapeDtypeStruct((n, D), _f32),
            jax.ShapeDtypeStruct((n, D // 2), jnp.int32),
            jax.ShapeDtypeStruct((n, LANES), _f32),
            jax.ShapeDtypeStruct(hc.shape, _f32),
            jax.ShapeDtypeStruct(hp.shape, _f32),
        ],
        grid=(n // row_tile,),
        in_specs=[row_block(D),
                  pl.BlockSpec((None, g, 6 * D), lambda i: (i // tiles_per_seq, 0, 0), **once),
                  seq_block(hrc, **once), seq_block(hrp, **once)] + [full(a) for a in weights],
        out_specs=[row_block(D), row_block(D // 2), row_block(LANES), seq_block(hrc), seq_block(hrp)],
        scratch_shapes=[pltpu.VMEM((hrc + row_tile, C), _f32), pltpu.VMEM((hrp + row_tile, C), _f32)],
        compiler_params=pltpu.CompilerParams(vmem_limit_bytes=VMEM_LIMIT),
        name="mixer",
    )(x2, mod3, hc, hp, *weights)


def _plan_kernel(lgp_ref, lgs_ref, dest_ref, gate_ref, meta_ref, idx_s, rank_s):
    t = ROW_TILE
    e_iota = lax.broadcasted_iota(jnp.int32, (E, t), 0)
    tri = (lax.broadcasted_iota(jnp.int32, (t, t), 0)
           < lax.broadcasted_iota(jnp.int32, (t, t), 1)).astype(_f32).astype(_bf16)
    zeros_rest = jnp.zeros((LANES - TOP_K, t), _f32)

    def tile_body(lg_ref, off, i, carry):
        lt = lg_ref[i].T[0:E, :]
        vals, idxs = [], []
        for _ in range(TOP_K):
            mx = jnp.max(lt, axis=0, keepdims=True)
            ix = jnp.min(jnp.where(lt == mx, e_iota, E), axis=0, keepdims=True)
            vals.append(mx)
            idxs.append(ix)
            lt = jnp.where(e_iota == ix, -jnp.inf, lt)
        ex = [jnp.exp(vk - vals[0]) for vk in vals]
        den = ex[0] + ex[1] + ex[2] + ex[3]
        gates = [ek / den for ek in ex]
        gate_ref[off + i] = jnp.concatenate(gates + [zeros_rest], axis=0).T

        ohs = [(e_iota == ix) for ix in idxs]
        oh = (ohs[0] | ohs[1] | ohs[2] | ohs[3]).astype(_f32)
        before = _dot(oh.astype(_bf16), tri) + carry
        ranks = [jnp.sum(jnp.where(o, before, 0.0), axis=0, keepdims=True) for o in ohs]
        idx_s[off + i] = jnp.concatenate(idxs + idxs, axis=0)
        rank_s[off + i] = jnp.concatenate(ranks + ranks, axis=0).astype(jnp.int32)
        return carry + jnp.sum(oh, axis=1, keepdims=True)

    n_p, n_s = lgp_ref.shape[0], lgs_ref.shape[0]
    counts = lax.fori_loop(0, n_p, functools.partial(tile_body, lgp_ref, 0), jnp.zeros((E, 1), _f32))
    counts = lax.fori_loop(0, n_s, functools.partial(tile_body, lgs_ref, n_p), counts)
    padded = jnp.ceil(counts / SLOT_PAD) * SLOT_PAD
    low = (lax.broadcasted_iota(jnp.int32, (E, E), 1)
           <= lax.broadcasted_iota(jnp.int32, (E, E), 0)).astype(_f32)
    pad_end = _dot_exact(low, jnp.broadcast_to(padded, (E, LANES)))[:, 0:1]
    pad_start = pad_end - padded

    def dest_body(i, c):
        ix = idx_s[i]
        rk = rank_s[i]
        rows = []
        for k in range(TOP_K):
            st = jnp.sum(jnp.where(e_iota == ix[k:k + 1], pad_start, 0.0), axis=0, keepdims=True)
            rows.append(st.astype(jnp.int32) + rk[k:k + 1])
        dest_ref[i] = jnp.concatenate(rows + rows, axis=0)
        return c

    lax.fori_loop(0, n_p + n_s, dest_body, 0)

    sub = lax.broadcasted_iota(jnp.int32, (E, LANES), 0)
    lane = lax.broadcasted_iota(jnp.int32, (E, LANES), 1)

    def to_lanes(col):
        return jnp.sum(jnp.where(sub == lane, col, 0.0), axis=0, keepdims=True).astype(jnp.int32)

    later = jnp.min(jnp.where((sub > lane) & (counts > 0.0), sub, E), axis=0, keepdims=True)
    meta_ref[...] = jnp.concatenate(
        [to_lanes(counts), to_lanes(pad_start), to_lanes(padded), later,
         jnp.zeros((SUBLANES - 4, LANES), jnp.int32)], axis=0)


def _plan(logits_p, logits_s):
    n_tiles = (logits_p.shape[0] + logits_s.shape[0]) // ROW_TILE
    return pl.pallas_call(
        _plan_kernel,
        out_shape=[
            jax.ShapeDtypeStruct((n_tiles, 2 * TOP_K, ROW_TILE), jnp.int32),
            jax.ShapeDtypeStruct((n_tiles, ROW_TILE, LANES), _f32),
            jax.ShapeDtypeStruct((SUBLANES, LANES), jnp.int32),
        ],
        scratch_shapes=[pltpu.VMEM((n_tiles, 2 * TOP_K, ROW_TILE), jnp.int32),
                        pltpu.VMEM((n_tiles, 2 * TOP_K, ROW_TILE), jnp.int32)],
        compiler_params=pltpu.CompilerParams(vmem_limit_bytes=VMEM_LIMIT),
        name="plan",
    )(logits_p.reshape(-1, ROW_TILE, LANES), logits_s.reshape(-1, ROW_TILE, LANES))


def _dispatch(v_p, v_s, dests, n_rows_out):
    n_p, n_s = v_p.shape[0], v_s.shape[0]
    width = v_p.shape[1]
    w = SC_WINDOW
    n_pw, n_windows = n_p // w, (n_p + n_s) // w
    mesh = plsc.VectorSubcoreMesh(core_axis_name="core", subcore_axis_name="subcore")
    n_workers = mesh.num_cores * mesh.num_subcores

    @functools.partial(
        pl.kernel, mesh=mesh, name="dispatch",
        out_type=jax.ShapeDtypeStruct((n_rows_out, width), jnp.int32),
        scratch_types=[pltpu.VMEM((w, width), jnp.int32)] + [pltpu.VMEM((w,), jnp.int32)] * TOP_K
        + [pltpu.SemaphoreType.DMA])
    def scatter_rows(vp_hbm, vs_hbm, d0_hbm, d1_hbm, d2_hbm, d3_hbm, o_hbm, rows, i0, i1, i2, i3, sem):
        worker = lax.axis_index("subcore") * mesh.num_cores + lax.axis_index("core")
        idx = (i0, i1, i2, i3)

        def scatter_window(c):
            t0 = pl.multiple_of(c * w, w)
            for d_hbm, iv in zip((d0_hbm, d1_hbm, d2_hbm, d3_hbm), idx):
                pltpu.sync_copy(d_hbm.at[pl.ds(t0, w)], iv)
            copies = [pltpu.async_copy(rows, o_hbm.at[iv], sem) for iv in idx]
            for cp in copies:
                cp.wait()

        for j in range(-(-n_windows // n_workers)):
            c = j * n_workers + worker

            @pl.when(c < n_pw)
            def _():
                pltpu.sync_copy(vp_hbm.at[pl.ds(pl.multiple_of(c * w, w), w)], rows)
                scatter_window(c)

            @pl.when((c >= n_pw) & (c < n_windows))
            def _():
                pltpu.sync_copy(vs_hbm.at[pl.ds(pl.multiple_of((c - n_pw) * w, w), w)], rows)
                scatter_window(c)

    return scatter_rows(v_p, v_s, *dests)


def _experts_kernel(cnt_ref, row0_ref, pad_ref, nxt_ref, x_hbm, wu_hbm, bu_ref, wd_hbm, bd_ref, y_hbm,
                    xbuf, ybuf, wu_st, wd_st, wu_bf, wd_bf, ysz, xsem, ysem, wsem):
    def w_fetch(e):
        return (pltpu.make_async_copy(wu_hbm.at[e], wu_st, wsem.at[0]),
                pltpu.make_async_copy(wd_hbm.at[e], wd_st, wsem.at[1]))

    def x_fetch(row, size, slot):
        rows = pl.ds(pl.multiple_of(row, SLOT_PAD), size)
        return pltpu.make_async_copy(x_hbm.at[rows, :], xbuf.at[slot, pl.ds(0, size), :], xsem.at[slot])

    def y_store(row, size, slot):
        rows = pl.ds(pl.multiple_of(row, SLOT_PAD), size)
        return pltpu.make_async_copy(ybuf.at[slot, pl.ds(0, size), :], y_hbm.at[rows, :], ysem.at[slot])

    def y_wait(slot):
        for size in (BIG_BLOCK, SLOT_PAD):
            @pl.when(ysz[slot] == size)
            def _():
                y_store(0, size, slot).wait()

    def fetch_first(e, slot):
        @pl.when(pad_ref[e] >= BIG_BLOCK)
        def _():
            x_fetch(row0_ref[e], BIG_BLOCK, slot).start()

        @pl.when((pad_ref[e] > 0) & (pad_ref[e] < BIG_BLOCK))
        def _():
            x_fetch(row0_ref[e], SLOT_PAD, slot).start()

    ysz[0] = 0
    ysz[1] = 0
    for c in w_fetch(0):
        c.start()
    fetch_first(jnp.where(pad_ref[0] > 0, 0, nxt_ref[0]), 0)

    def expert_body(e, n_done):
        for c in w_fetch(e):
            c.wait()
        padded = pad_ref[e]

        @pl.when(padded > 0)
        def _():
            wu_bf[...] = wu_st[...].astype(_bf16)
            wd_bf[...] = wd_st[...].astype(_bf16)

        @pl.when(e + 1 < E)
        def _():
            for c in w_fetch(e + 1):
                c.start()

        b_up = bu_ref[e]
        b_down = bd_ref[e]
        row0 = row0_ref[e]
        n_big = padded // BIG_BLOCK
        n_small = (padded - n_big * BIG_BLOCK) // SLOT_PAD
        small0 = row0 + n_big * BIG_BLOCK
        nxt = nxt_ref[e]

        def pass_body(size, j, n_done):
            slot = n_done % 2
            big = size == BIG_BLOCK
            row = row0 + j * BIG_BLOCK if big else small0 + j * SLOT_PAD
            x_fetch(row, size, slot).wait()

            more = j + 1 < (n_big if big else n_small)
            tail = (n_small > 0) if big else False

            @pl.when(more)
            def _():
                x_fetch(row + size, size, 1 - slot).start()

            if big:
                @pl.when(jnp.logical_not(more) & tail)
                def _():
                    x_fetch(small0, SLOT_PAD, 1 - slot).start()

            @pl.when(jnp.logical_not(more) & jnp.logical_not(tail) & (nxt < E))
            def _():
                fetch_first(jnp.minimum(nxt, E - 1), 1 - slot)

            rows = lax.broadcasted_iota(jnp.int32, (size, 1), 0)
            x = _unpack_rows(jnp.where(rows < cnt_ref[e] - (row - row0), xbuf[slot, pl.ds(0, size), :], 0))
            hcat = _dot(x, wu_bf[...]) + b_up
            glu = jnp.minimum(hcat[:, 0:F], SWIGLU_LIMIT)
            lin = jnp.clip(hcat[:, F:2 * F], -SWIGLU_LIMIT, SWIGLU_LIMIT)
            act = glu * _sigmoid(SWIGLU_ALPHA * glu) * (lin + 1.0)
            y = _pack_rows(_dot(act.astype(_bf16), wd_bf[...]) + b_down)

            y_wait(slot)
            ybuf[slot, pl.ds(0, size), :] = y
            y_store(row, size, slot).start()
            ysz[slot] = size
            return n_done + 1

        n_done = lax.fori_loop(0, n_big, functools.partial(pass_body, BIG_BLOCK), n_done)
        return lax.fori_loop(0, n_small, functools.partial(pass_body, SLOT_PAD), n_done)

    lax.fori_loop(0, E, expert_body, 0)
    y_wait(0)
    y_wait(1)


def _experts(cnt, row0, padded, nxt, xpad, w_up, b_up, w_down, b_down):
    def full(a):
        nd = a.ndim
        return pl.BlockSpec(a.shape, lambda i, *_: (0,) * nd)

    return pl.pallas_call(
        _experts_kernel,
        out_shape=jax.ShapeDtypeStruct(xpad.shape, jnp.int32),
        grid_spec=pltpu.PrefetchScalarGridSpec(
            num_scalar_prefetch=4,
            grid=(1,),
            in_specs=[pl.BlockSpec(memory_space=pl.ANY),
                      pl.BlockSpec(memory_space=pl.ANY), full(b_up),
                      pl.BlockSpec(memory_space=pl.ANY), full(b_down)],
            out_specs=pl.BlockSpec(memory_space=pl.ANY),
            scratch_shapes=[pltpu.VMEM((2, BIG_BLOCK, D // 2), jnp.int32),
                            pltpu.VMEM((2, BIG_BLOCK, D // 2), jnp.int32),
                            pltpu.VMEM((D, 2 * F), _f32), pltpu.VMEM((F, D), _f32),
                            pltpu.VMEM((D, 2 * F), _bf16), pltpu.VMEM((F, D), _bf16),
                            pltpu.SMEM((2,), jnp.int32),
                            pltpu.SemaphoreType.DMA((2,)), pltpu.SemaphoreType.DMA((2,)),
                            pltpu.SemaphoreType.DMA((2,))],
        ),
        compiler_params=pltpu.CompilerParams(vmem_limit_bytes=VMEM_LIMIT),
        name="experts",
    )(cnt, row0, padded, nxt, xpad, w_up, b_up, w_down, b_down)


def _gather_rows(ypad, dest_all):
    n_out = dest_all.shape[0]
    width = ypad.shape[1]
    w = SC_WINDOW
    mesh = plsc.VectorSubcoreMesh(core_axis_name="core", subcore_axis_name="subcore")
    n_workers = mesh.num_cores * mesh.num_subcores
    n_windows = n_out // w

    @functools.partial(
        pl.kernel, mesh=mesh, name="gather_rows",
        out_type=jax.ShapeDtypeStruct((n_out, width), jnp.int32),
        scratch_types=[pltpu.VMEM((w, width), jnp.int32), pltpu.VMEM((w,), jnp.int32), pltpu.SemaphoreType.DMA])
    def gather_rows(y_hbm, d_hbm, o_hbm, rows, iv, sem):
        worker = lax.axis_index("subcore") * mesh.num_cores + lax.axis_index("core")

        @pl.loop(0, n_windows // n_workers)
        def _(j):
            r0 = pl.multiple_of((j * n_workers + worker) * w, w)
            pltpu.sync_copy(d_hbm.at[pl.ds(r0, w)], iv)
            pltpu.async_copy(y_hbm.at[iv], rows, sem).wait()
            pltpu.sync_copy(rows, o_hbm.at[pl.ds(r0, w)])

    assert n_windows % n_workers == 0
    return gather_rows(ypad, dest_all)


def _combine_kernel(h_ref, mod_ref, gate_ref, g2_ref, b2_ref, y_ref, *aliased_and_out):
    o_ref = aliased_and_out[-1]
    gates = gate_ref[...]
    half = D // 2
    f_hi = jnp.zeros((h_ref.shape[0], half), _f32)
    f_lo = jnp.zeros((h_ref.shape[0], half), _f32)
    for k in range(TOP_K):
        p = y_ref[k]
        gk = gates[:, k:k + 1]
        f_hi = f_hi + gk * lax.bitcast_convert_type(p & jnp.int32(-65536), _f32)
        f_lo = f_lo + gk * lax.bitcast_convert_type(lax.shift_left(p, 16), _f32)
    f = jnp.concatenate([f_hi, f_lo], axis=1)
    gate2 = mod_ref[...][:, 5 * D:6 * D]
    pre = ALPHA * h_ref[...] + _per_seq(f, gate2, jnp.multiply)
    o_ref[...] = _layer_norm(pre, g2_ref[...], b2_ref[...])


def _combine(h, mod3, gates, ln2_g, ln2_b, y4, token0, rows_per_mod, row0, out_so_far):
    t = COMBINE_TILE
    hoff = row0 // t
    goff = (token0 + row0) // t
    g = mod3.shape[1]
    tiles_per_mod = rows_per_mod // t
    in_specs = [pl.BlockSpec((t, D), lambda i: (i + hoff, 0)),
                pl.BlockSpec((None, g, 6 * D), lambda i: ((i + hoff) // tiles_per_mod, 0, 0)),
                pl.BlockSpec((t, LANES), lambda i: (i + goff, 0)),
                pl.BlockSpec((1, D), lambda i: (0, 0)),
                pl.BlockSpec((1, D), lambda i: (0, 0)),
                pl.BlockSpec((TOP_K, t, D // 2), lambda i: (0, i, 0))]
    args = [h, mod3, gates, ln2_g, ln2_b, y4]
    aliases = {}
    if out_so_far is not None:
        in_specs.append(pl.BlockSpec(memory_space=pl.ANY))
        args.append(out_so_far)
        aliases = {len(args) - 1: 0}
    return pl.pallas_call(
        _combine_kernel,
        out_shape=jax.ShapeDtypeStruct(h.shape, _f32),
        grid=(y4.shape[1] // t,),
        in_specs=in_specs,
        out_specs=pl.BlockSpec((t, D), lambda i: (i + hoff, 0)),
        input_output_aliases=aliases,
        compiler_params=pltpu.CompilerParams(vmem_limit_bytes=VMEM_LIMIT),
        name="combine",
    )(*args)


def _time_major(a):
    return a.transpose(1, 0, 2)


def kernel(x_prompt, x_sample, c_prompt, c_sample, state_conv, state_pool, w_ada, b_ada, w_in,
           conv_w, w_out_a, w_pool, ls_pool, w_out_b, w_o, ln1_g, ln1_b, w_router, b_router,
           w_up, b_up, w_down, b_down, ln2_g, ln2_b):
    n_seq_p, seq, _ = x_prompt.shape
    n_seq_s, dec_seq, _ = x_sample.shape
    n_p, n_s = n_seq_p * seq, n_seq_s * dec_seq
    n = n_p + n_s
    n_slots = TOP_K * n + E * SLOT_PAD
    l = 0

    mod = _ada(jnp.concatenate([c_prompt, c_sample], axis=0), w_ada[l], b_ada[l][None])
    mod_p = mod[:n_seq_p][:, None, :]
    mod_s = mod[n_seq_p:][None]

    weights = (
        w_in[l].astype(_bf16), conv_w[l], w_out_a[l].astype(_bf16), w_pool[l].astype(_bf16),
        ls_pool[l][None], w_out_b[l].astype(_bf16), w_o[l].astype(_bf16), ln1_g[l][None], ln1_b[l][None],
        jnp.pad(w_router[l], ((0, 0), (0, LANES - E))), jnp.pad(b_router[l], (0, LANES - E))[None],
    )
    hc_p, hp_p = _hist_steps(CONV_HIST, 1), _hist_steps(POOL_HIST, 1)
    h_p, v_p, lg_p, nc_p, np_p = _mixer(
        x_prompt.reshape(n_p, D), mod_p, jnp.zeros((n_seq_p * hc_p, C), _f32),
        jnp.zeros((n_seq_p * hp_p, C), _f32), weights, PROMPT_ROW_TILE, PROMPT_SUB_TILES, 0)
    hc_s, hp_s = _hist_steps(CONV_HIST, n_seq_s), _hist_steps(POOL_HIST, n_seq_s)
    hist_c = jnp.pad(_time_major(state_conv[l]), ((hc_s - CONV_HIST, 0), (0, 0), (0, 0)))
    hist_p = jnp.pad(_time_major(state_pool[l]), ((hp_s - POOL_HIST, 0), (0, 0), (0, 0)))
    h_s, v_s, lg_s, nc_s, np_s = _mixer(
        _time_major(x_sample).reshape(n_s, D), mod_s, hist_c.reshape(hc_s * n_seq_s, C),
        hist_p.reshape(hp_s * n_seq_s, C), weights, SAMPLE_ROW_TILE, 1, PAST_LEN)

    dest8, gates3, meta = _plan(lg_p, lg_s)
    cnt, row0, padded, nxt = (meta[i, :E] for i in range(4))
    gates = gates3.reshape(n, LANES)

    dests = [dest8[:, k, :].reshape(n) for k in range(TOP_K)]
    xpad = _dispatch(v_p, v_s, dests, n_slots)
    ypad = _experts(cnt, row0, padded, nxt, xpad, w_up[l], b_up[l][:, None, :], w_down[l], b_down[l][:, None, :])

    def gathered(t0, rows):
        idx = jnp.concatenate([dk[t0:t0 + rows] for dk in dests])
        return _gather_rows(ypad, idx).reshape(TOP_K, rows, D // 2)

    chunk = n_p // COMBINE_CHUNKS
    y4_p = [gathered(c * chunk, chunk) for c in range(COMBINE_CHUNKS)]
    y4_s = gathered(n_p, n_s)
    g2, b2 = ln2_g[l][None], ln2_b[l][None]
    y_p = None
    for c in range(COMBINE_CHUNKS):
        y_p = _combine(h_p, mod_p, gates, g2, b2, y4_p[c], 0, seq, c * chunk, y_p)
    y_s = _combine(h_s, mod_s, gates, g2, b2, y4_s, n_p, n_s, 0, None)

    y_prompt = y_p.reshape(n_seq_p, seq, D)
    y_sample = _time_major(y_s.reshape(dec_seq, n_seq_s, D))
    new_conv_p = nc_p.reshape(n_seq_p, hc_p, C)[:, hc_p - CONV_HIST:][None]
    new_pool_p = np_p.reshape(n_seq_p, hp_p, C)[:, hp_p - POOL_HIST:][None]
    new_conv_s = _time_major(nc_s.reshape(hc_s, n_seq_s, C)[hc_s - CONV_HIST:])[None]
    new_pool_s = _time_major(np_s.reshape(hp_s, n_seq_s, C)[hp_s - POOL_HIST:])[None]
    return (y_prompt, y_sample, new_conv_p, new_pool_p, new_conv_s, new_pool_s)
```

```python
import functools

import jax
import jax.numpy as jnp
from jax import lax
from jax.experimental import pallas as pl
from jax.experimental.pallas import tpu as pltpu
from jax.experimental.pallas import tpu_sc as plsc

D = 1024
C = 512
N_GROUPS = 4
GROUP = C // N_GROUPS
CONV_HIST = 2
POOL_HIST = 15
E = 32
TOP_K = 4
F = 1024
SWIGLU_LIMIT = 7.0
SWIGLU_ALPHA = 1.702
LN_EPS = 1e-5
DEPTH = 1
ALPHA = (2 * DEPTH) ** 0.25
PAST_LEN = 16384

LANES = 128
SUBLANES = 8
ROW_TILE = 512
PROMPT_ROW_TILE = 1024
PROMPT_SUB_TILES = 1
SAMPLE_ROW_TILE = 512
SLOT_PAD = 256
BIG_BLOCK = 1024
COMBINE_TILE = 512
COMBINE_CHUNKS = 4
SC_WINDOW = 128
VMEM_LIMIT = 56 * 1024 * 1024

_f32 = jnp.float32
_bf16 = jnp.bfloat16


def _dot(a, b):
    return jnp.dot(a, b, preferred_element_type=_f32)


def _dot_exact(a, b):
    return lax.dot_general(a, b, (((1,), (0,)), ((), ())),
                           precision=lax.Precision.HIGHEST, preferred_element_type=_f32)


def _dot_split(a, b):
    a_hi, b_hi = a.astype(_bf16), b.astype(_bf16)
    a_lo = (a - a_hi.astype(_f32)).astype(_bf16)
    b_lo = (b - b_hi.astype(_f32)).astype(_bf16)
    return _dot(a_hi, b_hi) + _dot(a_lo, b_hi) + _dot(a_hi, b_lo)


def _pack_rows(x):
    w = x.shape[1] // 2
    hi = lax.bitcast_convert_type(x[:, :w].astype(_bf16).astype(_f32), jnp.int32)
    lo = lax.bitcast_convert_type(x[:, w:].astype(_bf16).astype(_f32), jnp.int32)
    return hi | lax.shift_right_logical(lo, 16)


def _unpack_rows(p):
    hi = lax.bitcast_convert_type(p & jnp.int32(-65536), _f32)
    lo = lax.bitcast_convert_type(lax.shift_left(p, 16), _f32)
    return jnp.concatenate([hi, lo], axis=1).astype(_bf16)


def _sigmoid(x):
    return 0.5 * jnp.tanh(0.5 * x) + 0.5


def _per_seq(x, m, op):
    g = m.shape[0]
    if g == 1:
        return op(x, m)
    r, n = x.shape
    return op(x.reshape(r // g, g, n), m[None]).reshape(r, n)


def _layer_norm(x, g, b):
    mu = jnp.mean(x, axis=-1, keepdims=True)
    xc = x - mu
    var = jnp.mean(xc * xc, axis=-1, keepdims=True)
    return xc * lax.rsqrt(var + LN_EPS) * g + b


def _hist_steps(needed, g):
    return -(-needed * g // SUBLANES) * SUBLANES // g


def _ada_kernel(c_ref, w_ref, b_ref, o_ref):
    c = c_ref[...]
    o_ref[...] = _dot_split(c * _sigmoid(c), w_ref[...]) + b_ref[...]


def _ada(c, w_ada, b_ada):
    rows = c.shape[0]
    cols = w_ada.shape[1]
    bn = 1536
    return pl.pallas_call(
        _ada_kernel,
        out_shape=jax.ShapeDtypeStruct((rows, cols), _f32),
        grid=(cols // bn,),
        in_specs=[pl.BlockSpec((rows, D), lambda j: (0, 0)),
                  pl.BlockSpec((D, bn), lambda j: (0, j)),
                  pl.BlockSpec((1, bn), lambda j: (0, j))],
        out_specs=pl.BlockSpec((rows, bn), lambda j: (0, j)),
        compiler_params=pltpu.CompilerParams(vmem_limit_bytes=VMEM_LIMIT),
        name="ada",
    )(c, w_ada, b_ada)


def _mixer_kernel(g, tiles_per_seq, start_pos, n_sub,
                  x_ref, mod_ref, hc_ref, hp_ref,
                  win_ref, cw_ref, woa_ref, wpool_ref, ls_ref, wob_ref, wo_ref, g1_ref, b1_ref,
                  wr_ref, br_ref,
                  h_ref, v_ref, lg_ref, nc_ref, np_ref, zbuf, pbuf):
    r = x_ref.shape[0]
    hrc = hc_ref.shape[0]
    hrp = hp_ref.shape[0]
    j = pl.program_id(0) % tiles_per_seq

    @pl.when(j == 0)
    def _():
        zbuf[pl.ds(0, hrc), :] = hc_ref[...]
        pbuf[pl.ds(0, hrp), :] = hp_ref[...]

    @pl.when(j != 0)
    def _():
        zt = zbuf[pl.ds(r, hrc), :]
        pt = pbuf[pl.ds(r, hrp), :]
        zbuf[pl.ds(0, hrc), :] = zt
        pbuf[pl.ds(0, hrp), :] = pt

    m = mod_ref[...]
    shift1, scale1, gate1 = m[:, 0:D], m[:, D:2 * D], m[:, 2 * D:3 * D]
    shift2, scale2 = m[:, 3 * D:4 * D], m[:, 4 * D:5 * D]

    rs = r // n_sub
    for s in range(n_sub):
        rows = pl.ds(s * rs, rs)
        x = x_ref[rows, :]
        u = _per_seq(_per_seq(x, 1.0 + scale1, jnp.multiply), shift1, jnp.add).astype(_bf16)

        z = _dot(u, win_ref[:, C:2 * C]) * _dot(u, win_ref[:, 2 * C:3 * C])
        zrow = hrc + s * rs
        zbuf[pl.ds(zrow, rs), :] = z
        cw = cw_ref[...]
        conv = (cw[0:1] * zbuf[pl.ds(zrow - 2 * g, rs), :] + cw[1:2] * zbuf[pl.ds(zrow - g, rs), :]
                + cw[2:3] * z)
        y_a = _dot((_dot(u, win_ref[:, 0:C]) * conv).astype(_bf16), woa_ref[...])

        xp = _dot(u, win_ref[:, 3 * C:4 * C])
        prow = hrp + s * rs
        pbuf[pl.ds(prow, rs), :] = xp
        pos = (start_pos + j * (r // g) + s * (rs // g)
               + lax.broadcasted_iota(jnp.int32, (rs, 1), 0) // g)
        acc = xp
        yg = []
        for grp in range(N_GROUPS):
            lo = grp * GROUP
            wdw = 2 ** (grp + 1)
            for back in range(wdw // 2, wdw):
                sh = pbuf[pl.ds(prow - back * g, rs), lo:C]
                acc = jnp.concatenate([acc[:, 0:lo], acc[:, lo:C] + sh], axis=1) if lo else acc + sh
            inv_cnt = 1.0 / jnp.minimum(wdw, pos + 1).astype(_f32)
            diff = acc[:, lo:lo + GROUP] * inv_cnt - xp[:, lo:lo + GROUP]
            yg.append(_dot(diff.astype(_bf16), wpool_ref[grp]))
        y_b = _dot((jnp.concatenate(yg, axis=1) * ls_ref[...]).astype(_bf16), wob_ref[...])

        g_a = _dot(u, win_ref[:, 4 * C:4 * C + D])
        g_b = _dot(u, win_ref[:, 4 * C + D:4 * C + 2 * D])
        merged = _sigmoid(g_a) * y_a + _sigmoid(g_b) * y_b
        o = _dot(merged.astype(_bf16), wo_ref[...])
        h = _layer_norm(ALPHA * x + _per_seq(o, gate1, jnp.multiply), g1_ref[...], b1_ref[...])
        v = _per_seq(_per_seq(h, 1.0 + scale2, jnp.multiply), shift2, jnp.add)
        h_ref[rows, :] = h
        v_ref[rows, :] = _pack_rows(v)
        lg_ref[rows, :] = _dot_split(v, wr_ref[...]) + br_ref[...]
    nc_ref[...] = zbuf[pl.ds(r, hrc), :]
    np_ref[...] = pbuf[pl.ds(r, hrp), :]


def _mixer(x2, mod3, hc, hp, weights, row_tile, n_sub, start_pos):
    n = x2.shape[0]
    n_mod, g, _ = mod3.shape
    hrc, hrp = hc.shape[0] // n_mod, hp.shape[0] // n_mod
    tiles_per_seq = n // n_mod // row_tile
    once = dict(pipeline_mode=pl.Buffered(1)) if n_mod == 1 else {}

    def full(a):
        nd = a.ndim
        return pl.BlockSpec(a.shape, lambda i: (0,) * nd)

    def seq_block(rows, **kw):
        return pl.BlockSpec((rows, C), lambda i: (i // tiles_per_seq, 0), **kw)

    def row_block(cols):
        return pl.BlockSpec((row_tile, cols), lambda i: (i, 0))

    return pl.pallas_call(
        functools.partial(_mixer_kernel, g, tiles_per_seq, start_pos, n_sub),
        out_shape=[
            jax.ShapeDtypeStruct((n, D), _f32),
            jax.ShapeDtypeStruct((n, D // 2), jnp.int32),
            jax.ShapeDtypeStruct((n, LANES), _f32),
            jax.ShapeDtypeStruct(hc.shape, _f32),
            jax.ShapeDtypeStruct(hp.shape, _f32),
        ],
        grid=(n // row_tile,),
        in_specs=[row_block(D),
                  pl.BlockSpec((None, g, 6 * D), lambda i: (i // tiles_per_seq, 0, 0), **once),
                  seq_block(hrc, **once), seq_block(hrp, **once)] + [full(a) for a in weights],
        out_specs=[row_block(D), row_block(D // 2), row_block(LANES), seq_block(hrc), seq_block(hrp)],
        scratch_shapes=[pltpu.VMEM((hrc + row_tile, C), _f32), pltpu.VMEM((hrp + row_tile, C), _f32)],
        compiler_params=pltpu.CompilerParams(vmem_limit_bytes=VMEM_LIMIT),
        name="mixer",
    )(x2, mod3, hc, hp, *weights)


def _plan_kernel(lgp_ref, lgs_ref, dest_ref, gate_ref, meta_ref, idx_s, rank_s):
    t = ROW_TILE
    e_iota = lax.broadcasted_iota(jnp.int32, (E, t), 0)
    tri = (lax.broadcasted_iota(jnp.int32, (t, t), 0)
           < lax.broadcasted_iota(jnp.int32, (t, t), 1)).astype(_f32).astype(_bf16)
    zeros_rest = jnp.zeros((LANES - TOP_K, t), _f32)

    def tile_body(lg_ref, off, i, carry):
        lt = lg_ref[i].T[0:E, :]
        vals, idxs = [], []
        for _ in range(TOP_K):
            mx = jnp.max(lt, axis=0, keepdims=True)
            ix = jnp.min(jnp.where(lt == mx, e_iota, E), axis=0, keepdims=True)
            vals.append(mx)
            idxs.append(ix)
            lt = jnp.where(e_iota == ix, -jnp.inf, lt)
        ex = [jnp.exp(vk - vals[0]) for vk in vals]
        den = ex[0] + ex[1] + ex[2] + ex[3]
        gates = [ek / den for ek in ex]
        gate_ref[off + i] = jnp.concatenate(gates + [zeros_rest], axis=0).T

        ohs = [(e_iota == ix) for ix in idxs]
        oh = (ohs[0] | ohs[1] | ohs[2] | ohs[3]).astype(_f32)
        before = _dot(oh.astype(_bf16), tri) + carry
        ranks = [jnp.sum(jnp.where(o, before, 0.0), axis=0, keepdims=True) for o in ohs]
        idx_s[off + i] = jnp.concatenate(idxs + idxs, axis=0)
        rank_s[off + i] = jnp.concatenate(ranks + ranks, axis=0).astype(jnp.int32)
        return carry + jnp.sum(oh, axis=1, keepdims=True)

    n_p, n_s = lgp_ref.shape[0], lgs_ref.shape[0]
    counts = lax.fori_loop(0, n_p, functools.partial(tile_body, lgp_ref, 0), jnp.zeros((E, 1), _f32))
    counts = lax.fori_loop(0, n_s, functools.partial(tile_body, lgs_ref, n_p), counts)
    padded = jnp.ceil(counts / SLOT_PAD) * SLOT_PAD
    low = (lax.broadcasted_iota(jnp.int32, (E, E), 1)
           <= lax.broadcasted_iota(jnp.int32, (E, E), 0)).astype(_f32)
    pad_end = _dot_exact(low, jnp.broadcast_to(padded, (E, LANES)))[:, 0:1]
    pad_start = pad_end - padded

    def dest_body(i, c):
        ix = idx_s[i]
        rk = rank_s[i]
        rows = []
        for k in range(TOP_K):
            st = jnp.sum(jnp.where(e_iota == ix[k:k + 1], pad_start, 0.0), axis=0, keepdims=True)
            rows.append(st.astype(jnp.int32) + rk[k:k + 1])
        dest_ref[i] = jnp.concatenate(rows + rows, axis=0)
        return c

    lax.fori_loop(0, n_p + n_s, dest_body, 0)

    sub = lax.broadcasted_iota(jnp.int32, (E, LANES), 0)
    lane = lax.broadcasted_iota(jnp.int32, (E, LANES), 1)

    def to_lanes(col):
        return jnp.sum(jnp.where(sub == lane, col, 0.0), axis=0, keepdims=True).astype(jnp.int32)

    later = jnp.min(jnp.where((sub > lane) & (counts > 0.0), sub, E), axis=0, keepdims=True)
    meta_ref[...] = jnp.concatenate(
        [to_lanes(counts), to_lanes(pad_start), to_lanes(padded), later,
         jnp.zeros((SUBLANES - 4, LANES), jnp.int32)], axis=0)


def _plan(logits_p, logits_s):
    n_tiles = (logits_p.shape[0] + logits_s.shape[0]) // ROW_TILE
    return pl.pallas_call(
        _plan_kernel,
        out_shape=[
            jax.ShapeDtypeStruct((n_tiles, 2 * TOP_K, ROW_TILE), jnp.int32),
            jax.ShapeDtypeStruct((n_tiles, ROW_TILE, LANES), _f32),
            jax.ShapeDtypeStruct((SUBLANES, LANES), jnp.int32),
        ],
        scratch_shapes=[pltpu.VMEM((n_tiles, 2 * TOP_K, ROW_TILE), jnp.int32),
                        pltpu.VMEM((n_tiles, 2 * TOP_K, ROW_TILE), jnp.int32)],
        compiler_params=pltpu.CompilerParams(vmem_limit_bytes=VMEM_LIMIT),
        name="plan",
    )(logits_p.reshape(-1, ROW_TILE, LANES), logits_s.reshape(-1, ROW_TILE, LANES))


def _dispatch(v_p, v_s, dests, n_rows_out):
    n_p, n_s = v_p.shape[0], v_s.shape[0]
    width = v_p.shape[1]
    w = SC_WINDOW
    n_pw, n_windows = n_p // w, (n_p + n_s) // w
    mesh = plsc.VectorSubcoreMesh(core_axis_name="core", subcore_axis_name="subcore")
    n_workers = mesh.num_cores * mesh.num_subcores

    @functools.partial(
        pl.kernel, mesh=mesh, name="dispatch",
        out_type=jax.ShapeDtypeStruct((n_rows_out, width), jnp.int32),
        scratch_types=[pltpu.VMEM((w, width), jnp.int32)] + [pltpu.VMEM((w,), jnp.int32)] * TOP_K
        + [pltpu.SemaphoreType.DMA])
    def scatter_rows(vp_hbm, vs_hbm, d0_hbm, d1_hbm, d2_hbm, d3_hbm, o_hbm, rows, i0, i1, i2, i3, sem):
        worker = lax.axis_index("subcore") * mesh.num_cores + lax.axis_index("core")
        idx = (i0, i1, i2, i3)

        def scatter_window(c):
            t0 = pl.multiple_of(c * w, w)
            for d_hbm, iv in zip((d0_hbm, d1_hbm, d2_hbm, d3_hbm), idx):
                pltpu.sync_copy(d_hbm.at[pl.ds(t0, w)], iv)
            copies = [pltpu.async_copy(rows, o_hbm.at[iv], sem) for iv in idx]
            for cp in copies:
                cp.wait()

        for j in range(-(-n_windows // n_workers)):
            c = j * n_workers + worker

            @pl.when(c < n_pw)
            def _():
                pltpu.sync_copy(vp_hbm.at[pl.ds(pl.multiple_of(c * w, w), w)], rows)
                scatter_window(c)

            @pl.when((c >= n_pw) & (c < n_windows))
            def _():
                pltpu.sync_copy(vs_hbm.at[pl.ds(pl.multiple_of((c - n_pw) * w, w), w)], rows)
                scatter_window(c)

    return scatter_rows(v_p, v_s, *dests)


def _experts_kernel(cnt_ref, row0_ref, pad_ref, nxt_ref, x_hbm, wu_hbm, bu_ref, wd_hbm, bd_ref, y_hbm,
                    xbuf, ybuf, wu_st, wd_st, wu_bf, wd_bf, ysz, xsem, ysem, wsem):
    def w_fetch(e):
        return (pltpu.make_async_copy(wu_hbm.at[e], wu_st, wsem.at[0]),
                pltpu.make_async_copy(wd_hbm.at[e], wd_st, wsem.at[1]))

    def x_fetch(row, size, slot):
        rows = pl.ds(pl.multiple_of(row, SLOT_PAD), size)
        return pltpu.make_async_copy(x_hbm.at[rows, :], xbuf.at[slot, pl.ds(0, size), :], xsem.at[slot])

    def y_store(row, size, slot):
        rows = pl.ds(pl.multiple_of(row, SLOT_PAD), size)
        return pltpu.make_async_copy(ybuf.at[slot, pl.ds(0, size), :], y_hbm.at[rows, :], ysem.at[slot])

    def y_wait(slot):
        for size in (BIG_BLOCK, SLOT_PAD):
            @pl.when(ysz[slot] == size)
            def _():
                y_store(0, size, slot).wait()

    def fetch_first(e, slot):
        @pl.when(pad_ref[e] >= BIG_BLOCK)
        def _():
            x_fetch(row0_ref[e], BIG_BLOCK, slot).start()

        @pl.when((pad_ref[e] > 0) & (pad_ref[e] < BIG_BLOCK))
        def _():
            x_fetch(row0_ref[e], SLOT_PAD, slot).start()

    ysz[0] = 0
    ysz[1] = 0
    for c in w_fetch(0):
        c.start()
    fetch_first(jnp.where(pad_ref[0] > 0, 0, nxt_ref[0]), 0)

    def expert_body(e, n_done):
        for c in w_fetch(e):
            c.wait()
        padded = pad_ref[e]

        @pl.when(padded > 0)
        def _():
            wu_bf[...] = wu_st[...].astype(_bf16)
            wd_bf[...] = wd_st[...].astype(_bf16)

        @pl.when(e + 1 < E)
        def _():
            for c in w_fetch(e + 1):
                c.start()

        b_up = bu_ref[e]
        b_down = bd_ref[e]
        row0 = row0_ref[e]
        n_big = padded // BIG_BLOCK
        n_small = (padded - n_big * BIG_BLOCK) // SLOT_PAD
        small0 = row0 + n_big * BIG_BLOCK
        nxt = nxt_ref[e]

        def pass_body(size, j, n_done):
            slot = n_done % 2
            big = size == BIG_BLOCK
            row = row0 + j * BIG_BLOCK if big else small0 + j * SLOT_PAD
            x_fetch(row, size, slot).wait()

            more = j + 1 < (n_big if big else n_small)
            tail = (n_small > 0) if big else False

            @pl.when(more)
            def _():
                x_fetch(row + size, size, 1 - slot).start()

            if big:
                @pl.when(jnp.logical_not(more) & tail)
                def _():
                    x_fetch(small0, SLOT_PAD, 1 - slot).start()

            @pl.when(jnp.logical_not(more) & jnp.logical_not(tail) & (nxt < E))
            def _():
                fetch_first(jnp.minimum(nxt, E - 1), 1 - slot)

            rows = lax.broadcasted_iota(jnp.int32, (size, 1), 0)
            x = _unpack_rows(jnp.where(rows < cnt_ref[e] - (row - row0), xbuf[slot, pl.ds(0, size), :], 0))
            hcat = _dot(x, wu_bf[...]) + b_up
            glu = jnp.minimum(hcat[:, 0:F], SWIGLU_LIMIT)
            lin = jnp.clip(hcat[:, F:2 * F], -SWIGLU_LIMIT, SWIGLU_LIMIT)
            act = glu * _sigmoid(SWIGLU_ALPHA * glu) * (lin + 1.0)
            y = _pack_rows(_dot(act.astype(_bf16), wd_bf[...]) + b_down)

            y_wait(slot)
            ybuf[slot, pl.ds(0, size), :] = y
            y_store(row, size, slot).start()
            ysz[slot] = size
            return n_done + 1

        n_done = lax.fori_loop(0, n_big, functools.partial(pass_body, BIG_BLOCK), n_done)
        return lax.fori_loop(0, n_small, functools.partial(pass_body, SLOT_PAD), n_done)

    lax.fori_loop(0, E, expert_body, 0)
    y_wait(0)
    y_wait(1)


def _experts(cnt, row0, padded, nxt, xpad, w_up, b_up, w_down, b_down):
    def full(a):
        nd = a.ndim
        return pl.BlockSpec(a.shape, lambda i, *_: (0,) * nd)

    return pl.pallas_call(
        _experts_kernel,
        out_shape=jax.ShapeDtypeStruct(xpad.shape, jnp.int32),
        grid_spec=pltpu.PrefetchScalarGridSpec(
            num_scalar_prefetch=4,
            grid=(1,),
            in_specs=[pl.BlockSpec(memory_space=pl.ANY),
                      pl.BlockSpec(memory_space=pl.ANY), full(b_up),
                      pl.BlockSpec(memory_space=pl.ANY), full(b_down)],
            out_specs=pl.BlockSpec(memory_space=pl.ANY),
            scratch_shapes=[pltpu.VMEM((2, BIG_BLOCK, D // 2), jnp.int32),
                            pltpu.VMEM((2, BIG_BLOCK, D // 2), jnp.int32),
                            pltpu.VMEM((D, 2 * F), _f32), pltpu.VMEM((F, D), _f32),
                            pltpu.VMEM((D, 2 * F), _bf16), pltpu.VMEM((F, D), _bf16),
                            pltpu.SMEM((2,), jnp.int32),
                            pltpu.SemaphoreType.DMA((2,)), pltpu.SemaphoreType.DMA((2,)),
                            pltpu.SemaphoreType.DMA((2,))],
        ),
        compiler_params=pltpu.CompilerParams(vmem_limit_bytes=VMEM_LIMIT),
        name="experts",
    )(cnt, row0, padded, nxt, xpad, w_up, b_up, w_down, b_down)


def _gather_rows(ypad, dest_all):
    n_out = dest_all.shape[0]
    width = ypad.shape[1]
    w = SC_WINDOW
    mesh = plsc.VectorSubcoreMesh(core_axis_name="core", subcore_axis_name="subcore")
    n_workers = mesh.num_cores * mesh.num_subcores
    n_windows = n_out // w

    @functools.partial(
        pl.kernel, mesh=mesh, name="gather_rows",
        out_type=jax.ShapeDtypeStruct((n_out, width), jnp.int32),
        scratch_types=[pltpu.VMEM((w, width), jnp.int32), pltpu.VMEM((w,), jnp.int32), pltpu.SemaphoreType.DMA])
    def gather_rows(y_hbm, d_hbm, o_hbm, rows, iv, sem):
        worker = lax.axis_index("subcore") * mesh.num_cores + lax.axis_index("core")

        @pl.loop(0, n_windows // n_workers)
        def _(j):
            r0 = pl.multiple_of((j * n_workers + worker) * w, w)
            pltpu.sync_copy(d_hbm.at[pl.ds(r0, w)], iv)
            pltpu.async_copy(y_hbm.at[iv], rows, sem).wait()
            pltpu.sync_copy(rows, o_hbm.at[pl.ds(r0, w)])

    assert n_windows % n_workers == 0
    return gather_rows(ypad, dest_all)


def _combine_kernel(h_ref, mod_ref, gate_ref, g2_ref, b2_ref, y_ref, *aliased_and_out):
    o_ref = aliased_and_out[-1]
    gates = gate_ref[...]
    half = D // 2
    f_hi = jnp.zeros((h_ref.shape[0], half), _f32)
    f_lo = jnp.zeros((h_ref.shape[0], half), _f32)
    for k in range(TOP_K):
        p = y_ref[k]
        gk = gates[:, k:k + 1]
        f_hi = f_hi + gk * lax.bitcast_convert_type(p & jnp.int32(-65536), _f32)
        f_lo = f_lo + gk * lax.bitcast_convert_type(lax.shift_left(p, 16), _f32)
    f = jnp.concatenate([f_hi, f_lo], axis=1)
    gate2 = mod_ref[...][:, 5 * D:6 * D]
    pre = ALPHA * h_ref[...] + _per_seq(f, gate2, jnp.multiply)
    o_ref[...] = _layer_norm(pre, g2_ref[...], b2_ref[...])


def _combine(h, mod3, gates, ln2_g, ln2_b, y4, token0, rows_per_mod, row0, out_so_far):
    t = COMBINE_TILE
    hoff = row0 // t
    goff = (token0 + row0) // t
    g = mod3.shape[1]
    tiles_per_mod = rows_per_mod // t
    in_specs = [pl.BlockSpec((t, D), lambda i: (i + hoff, 0)),
                pl.BlockSpec((None, g, 6 * D), lambda i: ((i + hoff) // tiles_per_mod, 0, 0)),
                pl.BlockSpec((t, LANES), lambda i: (i + goff, 0)),
                pl.BlockSpec((1, D), lambda i: (0, 0)),
                pl.BlockSpec((1, D), lambda i: (0, 0)),
                pl.BlockSpec((TOP_K, t, D // 2), lambda i: (0, i, 0))]
    args = [h, mod3, gates, ln2_g, ln2_b, y4]
    aliases = {}
    if out_so_far is not None:
        in_specs.append(pl.BlockSpec(memory_space=pl.ANY))
        args.append(out_so_far)
        aliases = {len(args) - 1: 0}
    return pl.pallas_call(
        _combine_kernel,
        out_shape=jax.ShapeDtypeStruct(h.shape, _f32),
        grid=(y4.shape[1] // t,),
        in_specs=in_specs,
        out_specs=pl.BlockSpec((t, D), lambda i: (i + hoff, 0)),
        input_output_aliases=aliases,
        compiler_params=pltpu.CompilerParams(vmem_limit_bytes=VMEM_LIMIT),
        name="combine",
    )(*args)


def _time_major(a):
    return a.transpose(1, 0, 2)


def kernel(x_prompt, x_sample, c_prompt, c_sample, state_conv, state_pool, w_ada, b_ada, w_in,
           conv_w, w_out_a, w_pool, ls_pool, w_out_b, w_o, ln1_g, ln1_b, w_router, b_router,
           w_up, b_up, w_down, b_down, ln2_g, ln2_b):
    n_seq_p, seq, _ = x_prompt.shape
    n_seq_s, dec_seq, _ = x_sample.shape
    n_p, n_s = n_seq_p * seq, n_seq_s * dec_seq
    n = n_p + n_s
    n_slots = TOP_K * n + E * SLOT_PAD
    l = 0

    mod = _ada(jnp.concatenate([c_prompt, c_sample], axis=0), w_ada[l], b_ada[l][None])
    mod_p = mod[:n_seq_p][:, None, :]
    mod_s = mod[n_seq_p:][None]

    weights = (
        w_in[l].astype(_bf16), conv_w[l], w_out_a[l].astype(_bf16), w_pool[l].astype(_bf16),
        ls_pool[l][None], w_out_b[l].astype(_bf16), w_o[l].astype(_bf16), ln1_g[l][None], ln1_b[l][None],
        jnp.pad(w_router[l], ((0, 0), (0, LANES - E))), jnp.pad(b_router[l], (0, LANES - E))[None],
    )
    hc_p, hp_p = _hist_steps(CONV_HIST, 1), _hist_steps(POOL_HIST, 1)
    h_p, v_p, lg_p, nc_p, np_p = _mixer(
        x_prompt.reshape(n_p, D), mod_p, jnp.zeros((n_seq_p * hc_p, C), _f32),
        jnp.zeros((n_seq_p * hp_p, C), _f32), weights, PROMPT_ROW_TILE, PROMPT_SUB_TILES, 0)
    hc_s, hp_s = _hist_steps(CONV_HIST, n_seq_s), _hist_steps(POOL_HIST, n_seq_s)
    hist_c = jnp.pad(_time_major(state_conv[l]), ((hc_s - CONV_HIST, 0), (0, 0), (0, 0)))
    hist_p = jnp.pad(_time_major(state_pool[l]), ((hp_s - POOL_HIST, 0), (0, 0), (0, 0)))
    h_s, v_s, lg_s, nc_s, np_s = _mixer(
        _time_major(x_sample).reshape(n_s, D), mod_s, hist_c.reshape(hc_s * n_seq_s, C),
        hist_p.reshape(hp_s * n_seq_s, C), weights, SAMPLE_ROW_TILE, 1, PAST_LEN)

    dest8, gates3, meta = _plan(lg_p, lg_s)
    cnt, row0, padded, nxt = (meta[i, :E] for i in range(4))
    gates = gates3.reshape(n, LANES)

    dests = [dest8[:, k, :].reshape(n) for k in range(TOP_K)]
    xpad = _dispatch(v_p, v_s, dests, n_slots)
    ypad = _experts(cnt, row0, padded, nxt, xpad, w_up[l], b_up[l][:, None, :], w_down[l], b_down[l][:, None, :])

    def gathered(t0, rows):
        idx = jnp.concatenate([dk[t0:t0 + rows] for dk in dests])
        return _gather_rows(ypad, idx).reshape(TOP_K, rows, D // 2)

    chunk = n_p // COMBINE_CHUNKS
    y4_p = [gathered(c * chunk, chunk) for c in range(COMBINE_CHUNKS)]
    y4_s = gathered(n_p, n_s)
    g2, b2 = ln2_g[l][None], ln2_b[l][None]
    y_p = None
    for c in range(COMBINE_CHUNKS):
        y_p = _combine(h_p, mod_p, gates, g2, b2, y4_p[c], 0, seq, c * chunk, y_p)
    y_s = _combine(h_s, mod_s, gates, g2, b2, y4_s, n_p, n_s, 0, None)

    y_prompt = y_p.reshape(n_seq_p, seq, D)
    y_sample = _time_major(y_s.reshape(dec_seq, n_seq_s, D))
    new_conv_p = nc_p.reshape(n_seq_p, hc_p, C)[:, hc_p - CONV_HIST:][None]
    new_pool_p = np_p.reshape(n_seq_p, hp_p, C)[:, hp_p - POOL_HIST:][None]
    new_conv_s = _time_major(nc_s.reshape(hc_s, n_seq_s, C)[hc_s - CONV_HIST:])[None]
    new_pool_s = _time_major(np_s.reshape(hp_s, n_seq_s, C)[hp_s - POOL_HIST:])[None]
    return (y_prompt, y_sample, new_conv_p, new_pool_p, new_conv_s, new_pool_s)
```

```python
import functools

import jax
import jax.numpy as jnp
from jax import lax
from jax.experimental import pallas as pl
from jax.experimental.pallas import tpu as pltpu
from jax.experimental.pallas import tpu_sc as plsc

D = 1024
C = 512
N_GROUPS = 4
GROUP = C // N_GROUPS
CONV_HIST = 2
POOL_HIST = 15
E = 32
TOP_K = 4
F = 1024
SWIGLU_LIMIT = 7.0
SWIGLU_ALPHA = 1.702
LN_EPS = 1e-5
DEPTH = 1
ALPHA = (2 * DEPTH) ** 0.25
PAST_LEN = 16384

LANES = 128
SUBLANES = 8
ROW_TILE = 512
PROMPT_ROW_TILE = 1024
PROMPT_SUB_TILES = 1
SAMPLE_ROW_TILE = 512
SLOT_PAD = 256
BIG_BLOCK = 1024
COMBINE_TILE = 512
COMBINE_CHUNKS = 2
SC_WINDOW = 128
VMEM_LIMIT = 56 * 1024 * 1024

_f32 = jnp.float32
_bf16 = jnp.bfloat16


def _dot(a, b):
    return jnp.dot(a, b, preferred_element_type=_f32)


def _dot_exact(a, b):
    return lax.dot_general(a, b, (((1,), (0,)), ((), ())),
                           precision=lax.Precision.HIGHEST, preferred_element_type=_f32)


def _dot_split(a, b):
    a_hi, b_hi = a.astype(_bf16), b.astype(_bf16)
    a_lo = (a - a_hi.astype(_f32)).astype(_bf16)
    b_lo = (b - b_hi.astype(_f32)).astype(_bf16)
    return _dot(a_hi, b_hi) + _dot(a_lo, b_hi) + _dot(a_hi, b_lo)


def _pack_rows(x):
    w = x.shape[1] // 2
    hi = lax.bitcast_convert_type(x[:, :w].astype(_bf16).astype(_f32), jnp.int32)
    lo = lax.bitcast_convert_type(x[:, w:].astype(_bf16).astype(_f32), jnp.int32)
    return hi | lax.shift_right_logical(lo, 16)


def _unpack_rows(p):
    hi = lax.bitcast_convert_type(p & jnp.int32(-65536), _f32)
    lo = lax.bitcast_convert_type(lax.shift_left(p, 16), _f32)
    return jnp.concatenate([hi, lo], axis=1).astype(_bf16)


def _sigmoid(x):
    return 0.5 * jnp.tanh(0.5 * x) + 0.5


def _per_seq(x, m, op):
    g = m.shape[0]
    if g == 1:
        return op(x, m)
    r, n = x.shape
    return op(x.reshape(r // g, g, n), m[None]).reshape(r, n)


def _layer_norm(x, g, b):
    mu = jnp.mean(x, axis=-1, keepdims=True)
    xc = x - mu
    var = jnp.mean(xc * xc, axis=-1, keepdims=True)
    return xc * lax.rsqrt(var + LN_EPS) * g + b


def _hist_steps(needed, g):
    return -(-needed * g // SUBLANES) * SUBLANES // g


def _ada_kernel(c_ref, w_ref, b_ref, o_ref):
    c = c_ref[...]
    o_ref[...] = _dot_split(c * _sigmoid(c), w_ref[...]) + b_ref[...]


def _ada(c, w_ada, b_ada):
    rows = c.shape[0]
    cols = w_ada.shape[1]
    bn = 1536
    return pl.pallas_call(
        _ada_kernel,
        out_shape=jax.ShapeDtypeStruct((rows, cols), _f32),
        grid=(cols // bn,),
        in_specs=[pl.BlockSpec((rows, D), lambda j: (0, 0)),
                  pl.BlockSpec((D, bn), lambda j: (0, j)),
                  pl.BlockSpec((1, bn), lambda j: (0, j))],
        out_specs=pl.BlockSpec((rows, bn), lambda j: (0, j)),
        compiler_params=pltpu.CompilerParams(vmem_limit_bytes=VMEM_LIMIT),
        name="ada",
    )(c, w_ada, b_ada)


def _mixer_kernel(g, tiles_per_seq, start_pos, n_sub,
                  x_ref, mod_ref, hc_ref, hp_ref,
                  win_ref, cw_ref, woa_ref, wpool_ref, ls_ref, wob_ref, wo_ref, g1_ref, b1_ref,
                  wr_ref, br_ref,
                  h_ref, v_ref, lg_ref, nc_ref, np_ref, zbuf, pbuf):
    r = x_ref.shape[0]
    hrc = hc_ref.shape[0]
    hrp = hp_ref.shape[0]
    j = pl.program_id(0) % tiles_per_seq

    @pl.when(j == 0)
    def _():
        zbuf[pl.ds(0, hrc), :] = hc_ref[...]
        pbuf[pl.ds(0, hrp), :] = hp_ref[...]

    @pl.when(j != 0)
    def _():
        zt = zbuf[pl.ds(r, hrc), :]
        pt = pbuf[pl.ds(r, hrp), :]
        zbuf[pl.ds(0, hrc), :] = zt
        pbuf[pl.ds(0, hrp), :] = pt

    m = mod_ref[...]
    shift1, scale1, gate1 = m[:, 0:D], m[:, D:2 * D], m[:, 2 * D:3 * D]
    shift2, scale2 = m[:, 3 * D:4 * D], m[:, 4 * D:5 * D]

    rs = r // n_sub
    for s in range(n_sub):
        rows = pl.ds(s * rs, rs)
        x = x_ref[rows, :]
        u = _per_seq(_per_seq(x, 1.0 + scale1, jnp.multiply), shift1, jnp.add).astype(_bf16)

        z = _dot(u, win_ref[:, C:2 * C]) * _dot(u, win_ref[:, 2 * C:3 * C])
        zrow = hrc + s * rs
        zbuf[pl.ds(zrow, rs), :] = z
        cw = cw_ref[...]
        conv = (cw[0:1] * zbuf[pl.ds(zrow - 2 * g, rs), :] + cw[1:2] * zbuf[pl.ds(zrow - g, rs), :]
                + cw[2:3] * z)
        y_a = _dot((_dot(u, win_ref[:, 0:C]) * conv).astype(_bf16), woa_ref[...])

        xp = _dot(u, win_ref[:, 3 * C:4 * C])
        prow = hrp + s * rs
        pbuf[pl.ds(prow, rs), :] = xp
        pos = (start_pos + j * (r // g) + s * (rs // g)
               + lax.broadcasted_iota(jnp.int32, (rs, 1), 0) // g)
        acc = xp
        yg = []
        for grp in range(N_GROUPS):
            lo = grp * GROUP
            wdw = 2 ** (grp + 1)
            for back in range(wdw // 2, wdw):
                sh = pbuf[pl.ds(prow - back * g, rs), lo:C]
                acc = jnp.concatenate([acc[:, 0:lo], acc[:, lo:C] + sh], axis=1) if lo else acc + sh
            inv_cnt = 1.0 / jnp.minimum(wdw, pos + 1).astype(_f32)
            diff = acc[:, lo:lo + GROUP] * inv_cnt - xp[:, lo:lo + GROUP]
            yg.append(_dot(diff.astype(_bf16), wpool_ref[grp]))
        y_b = _dot((jnp.concatenate(yg, axis=1) * ls_ref[...]).astype(_bf16), wob_ref[...])

        g_a = _dot(u, win_ref[:, 4 * C:4 * C + D])
        g_b = _dot(u, win_ref[:, 4 * C + D:4 * C + 2 * D])
        merged = _sigmoid(g_a) * y_a + _sigmoid(g_b) * y_b
        o = _dot(merged.astype(_bf16), wo_ref[...])
        h = _layer_norm(ALPHA * x + _per_seq(o, gate1, jnp.multiply), g1_ref[...], b1_ref[...])
        v = _per_seq(_per_seq(h, 1.0 + scale2, jnp.multiply), shift2, jnp.add)
        h_ref[rows, :] = h
        v_ref[rows, :] = _pack_rows(v)
        lg_ref[rows, :] = _dot_split(v, wr_ref[...]) + br_ref[...]
    nc_ref[...] = zbuf[pl.ds(r, hrc), :]
    np_ref[...] = pbuf[pl.ds(r, hrp), :]


def _mixer(x2, mod3, hc, hp, weights, row_tile, n_sub, start_pos):
    n = x2.shape[0]
    n_mod, g, _ = mod3.shape
    hrc, hrp = hc.shape[0] // n_mod, hp.shape[0] // n_mod
    tiles_per_seq = n // n_mod // row_tile
    once = dict(pipeline_mode=pl.Buffered(1)) if n_mod == 1 else {}

    def full(a):
        nd = a.ndim
        return pl.BlockSpec(a.shape, lambda i: (0,) * nd)

    def seq_block(rows, **kw):
        return pl.BlockSpec((rows, C), lambda i: (i // tiles_per_seq, 0), **kw)

    def row_block(cols):
        return pl.BlockSpec((row_tile, cols), lambda i: (i, 0))

    return pl.pallas_call(
        functools.partial(_mixer_kernel, g, tiles_per_seq, start_pos, n_sub),
        out_shape=[
            jax.ShapeDtypeStruct((n, D), _f32),
            jax.ShapeDtypeStruct((n, D // 2), jnp.int32),
            jax.ShapeDtypeStruct((n, LANES), _f32),
            jax.ShapeDtypeStruct(hc.shape, _f32),
            jax.ShapeDtypeStruct(hp.shape, _f32),
        ],
        grid=(n // row_tile,),
        in_specs=[row_block(D),
                  pl.BlockSpec((None, g, 6 * D), lambda i: (i // tiles_per_seq, 0, 0), **once),
                  seq_block(hrc, **once), seq_block(hrp, **once)] + [full(a) for a in weights],
        out_specs=[row_block(D), row_block(D // 2), row_block(LANES), seq_block(hrc), seq_block(hrp)],
        scratch_shapes=[pltpu.VMEM((hrc + row_tile, C), _f32), pltpu.VMEM((hrp + row_tile, C), _f32)],
        compiler_params=pltpu.CompilerParams(vmem_limit_bytes=VMEM_LIMIT),
        name="mixer",
    )(x2, mod3, hc, hp, *weights)


def _plan_kernel(lgp_ref, lgs_ref, dest_ref, gate_ref, meta_ref, idx_s, rank_s):
    t = ROW_TILE
    e_iota = lax.broadcasted_iota(jnp.int32, (E, t), 0)
    tri = (lax.broadcasted_iota(jnp.int32, (t, t), 0)
           < lax.broadcasted_iota(jnp.int32, (t, t), 1)).astype(_f32).astype(_bf16)
    zeros_rest = jnp.zeros((LANES - TOP_K, t), _f32)

    def tile_body(lg_ref, off, i, carry):
        lt = lg_ref[i].T[0:E, :]
        vals, idxs = [], []
        for _ in range(TOP_K):
            mx = jnp.max(lt, axis=0, keepdims=True)
            ix = jnp.min(jnp.where(lt == mx, e_iota, E), axis=0, keepdims=True)
            vals.append(mx)
            idxs.append(ix)
            lt = jnp.where(e_iota == ix, -jnp.inf, lt)
        ex = [jnp.exp(vk - vals[0]) for vk in vals]
        den = ex[0] + ex[1] + ex[2] + ex[3]
        gates = [ek / den for ek in ex]
        gate_ref[off + i] = jnp.concatenate(gates + [zeros_rest], axis=0).T

        ohs = [(e_iota == ix) for ix in idxs]
        oh = (ohs[0] | ohs[1] | ohs[2] | ohs[3]).astype(_f32)
        before = _dot(oh.astype(_bf16), tri) + carry
        ranks = [jnp.sum(jnp.where(o, before, 0.0), axis=0, keepdims=True) for o in ohs]
        idx_s[off + i] = jnp.concatenate(idxs + idxs, axis=0)
        rank_s[off + i] = jnp.concatenate(ranks + ranks, axis=0).astype(jnp.int32)
        return carry + jnp.sum(oh, axis=1, keepdims=True)

    n_p, n_s = lgp_ref.shape[0], lgs_ref.shape[0]
    counts = lax.fori_loop(0, n_p, functools.partial(tile_body, lgp_ref, 0), jnp.zeros((E, 1), _f32))
    counts = lax.fori_loop(0, n_s, functools.partial(tile_body, lgs_ref, n_p), counts)
    padded = jnp.ceil(counts / SLOT_PAD) * SLOT_PAD
    low = (lax.broadcasted_iota(jnp.int32, (E, E), 1)
           <= lax.broadcasted_iota(jnp.int32, (E, E), 0)).astype(_f32)
    pad_end = _dot_exact(low, jnp.broadcast_to(padded, (E, LANES)))[:, 0:1]
    pad_start = pad_end - padded

    def dest_body(i, c):
        ix = idx_s[i]
        rk = rank_s[i]
        rows = []
        for k in range(TOP_K):
            st = jnp.sum(jnp.where(e_iota == ix[k:k + 1], pad_start, 0.0), axis=0, keepdims=True)
            rows.append(st.astype(jnp.int32) + rk[k:k + 1])
        dest_ref[i] = jnp.concatenate(rows + rows, axis=0)
        return c

    lax.fori_loop(0, n_p + n_s, dest_body, 0)

    sub = lax.broadcasted_iota(jnp.int32, (E, LANES), 0)
    lane = lax.broadcasted_iota(jnp.int32, (E, LANES), 1)

    def to_lanes(col):
        return jnp.sum(jnp.where(sub == lane, col, 0.0), axis=0, keepdims=True).astype(jnp.int32)

    later = jnp.min(jnp.where((sub > lane) & (counts > 0.0), sub, E), axis=0, keepdims=True)
    meta_ref[...] = jnp.concatenate(
        [to_lanes(counts), to_lanes(pad_start), to_lanes(padded), later,
         jnp.zeros((SUBLANES - 4, LANES), jnp.int32)], axis=0)


def _plan(logits_p, logits_s):
    n_tiles = (logits_p.shape[0] + logits_s.shape[0]) // ROW_TILE
    return pl.pallas_call(
        _plan_kernel,
        out_shape=[
            jax.ShapeDtypeStruct((n_tiles, 2 * TOP_K, ROW_TILE), jnp.int32),
            jax.ShapeDtypeStruct((n_tiles, ROW_TILE, LANES), _f32),
            jax.ShapeDtypeStruct((SUBLANES, LANES), jnp.int32),
        ],
        scratch_shapes=[pltpu.VMEM((n_tiles, 2 * TOP_K, ROW_TILE), jnp.int32),
                        pltpu.VMEM((n_tiles, 2 * TOP_K, ROW_TILE), jnp.int32)],
        compiler_params=pltpu.CompilerParams(vmem_limit_bytes=VMEM_LIMIT),
        name="plan",
    )(logits_p.reshape(-1, ROW_TILE, LANES), logits_s.reshape(-1, ROW_TILE, LANES))


def _dispatch(v_p, v_s, dests, n_rows_out):
    n_p, n_s = v_p.shape[0], v_s.shape[0]
    width = v_p.shape[1]
    w = SC_WINDOW
    n_pw, n_windows = n_p // w, (n_p + n_s) // w
    mesh = plsc.VectorSubcoreMesh(core_axis_name="core", subcore_axis_name="subcore")
    n_workers = mesh.num_cores * mesh.num_subcores

    @functools.partial(
        pl.kernel, mesh=mesh, name="dispatch",
        out_type=jax.ShapeDtypeStruct((n_rows_out, width), jnp.int32),
        scratch_types=[pltpu.VMEM((w, width), jnp.int32)] + [pltpu.VMEM((w,), jnp.int32)] * TOP_K
        + [pltpu.SemaphoreType.DMA])
    def scatter_rows(vp_hbm, vs_hbm, d0_hbm, d1_hbm, d2_hbm, d3_hbm, o_hbm, rows, i0, i1, i2, i3, sem):
        worker = lax.axis_index("subcore") * mesh.num_cores + lax.axis_index("core")
        idx = (i0, i1, i2, i3)

        def scatter_window(c):
            t0 = pl.multiple_of(c * w, w)
            for d_hbm, iv in zip((d0_hbm, d1_hbm, d2_hbm, d3_hbm), idx):
                pltpu.sync_copy(d_hbm.at[pl.ds(t0, w)], iv)
            copies = [pltpu.async_copy(rows, o_hbm.at[iv], sem) for iv in idx]
            for cp in copies:
                cp.wait()

        for j in range(-(-n_windows // n_workers)):
            c = j * n_workers + worker

            @pl.when(c < n_pw)
            def _():
                pltpu.sync_copy(vp_hbm.at[pl.ds(pl.multiple_of(c * w, w), w)], rows)
                scatter_window(c)

            @pl.when((c >= n_pw) & (c < n_windows))
            def _():
                pltpu.sync_copy(vs_hbm.at[pl.ds(pl.multiple_of((c - n_pw) * w, w), w)], rows)
                scatter_window(c)

    return scatter_rows(v_p, v_s, *dests)


def _experts_kernel(cnt_ref, row0_ref, pad_ref, nxt_ref, x_hbm, wu_hbm, bu_ref, wd_hbm, bd_ref, y_hbm,
                    xbuf, ybuf, wu_st, wd_st, wu_bf, wd_bf, ysz, xsem, ysem, wsem):
    def w_fetch(e):
        return (pltpu.make_async_copy(wu_hbm.at[e], wu_st, wsem.at[0]),
                pltpu.make_async_copy(wd_hbm.at[e], wd_st, wsem.at[1]))

    def x_fetch(row, size, slot):
        rows = pl.ds(pl.multiple_of(row, SLOT_PAD), size)
        return pltpu.make_async_copy(x_hbm.at[rows, :], xbuf.at[slot, pl.ds(0, size), :], xsem.at[slot])

    def y_store(row, size, slot):
        rows = pl.ds(pl.multiple_of(row, SLOT_PAD), size)
        return pltpu.make_async_copy(ybuf.at[slot, pl.ds(0, size), :], y_hbm.at[rows, :], ysem.at[slot])

    def y_wait(slot):
        for size in (BIG_BLOCK, SLOT_PAD):
            @pl.when(ysz[slot] == size)
            def _():
                y_store(0, size, slot).wait()

    def fetch_first(e, slot):
        @pl.when(pad_ref[e] >= BIG_BLOCK)
        def _():
            x_fetch(row0_ref[e], BIG_BLOCK, slot).start()

        @pl.when((pad_ref[e] > 0) & (pad_ref[e] < BIG_BLOCK))
        def _():
            x_fetch(row0_ref[e], SLOT_PAD, slot).start()

    ysz[0] = 0
    ysz[1] = 0
    for c in w_fetch(0):
        c.start()
    fetch_first(jnp.where(pad_ref[0] > 0, 0, nxt_ref[0]), 0)

    def expert_body(e, n_done):
        for c in w_fetch(e):
            c.wait()
        padded = pad_ref[e]

        @pl.when(padded > 0)
        def _():
            wu_bf[...] = wu_st[...].astype(_bf16)
            wd_bf[...] = wd_st[...].astype(_bf16)

        @pl.when(e + 1 < E)
        def _():
            for c in w_fetch(e + 1):
                c.start()

        b_up = bu_ref[e]
        b_down = bd_ref[e]
        row0 = row0_ref[e]
        n_big = padded // BIG_BLOCK
        n_small = (padded - n_big * BIG_BLOCK) // SLOT_PAD
        small0 = row0 + n_big * BIG_BLOCK
        nxt = nxt_ref[e]

        def pass_body(size, j, n_done):
            slot = n_done % 2
            big = size == BIG_BLOCK
            row = row0 + j * BIG_BLOCK if big else small0 + j * SLOT_PAD
            x_fetch(row, size, slot).wait()

            more = j + 1 < (n_big if big else n_small)
            tail = (n_small > 0) if big else False

            @pl.when(more)
            def _():
                x_fetch(row + size, size, 1 - slot).start()

            if big:
                @pl.when(jnp.logical_not(more) & tail)
                def _():
                    x_fetch(small0, SLOT_PAD, 1 - slot).start()

            @pl.when(jnp.logical_not(more) & jnp.logical_not(tail) & (nxt < E))
            def _():
                fetch_first(jnp.minimum(nxt, E - 1), 1 - slot)

            rows = lax.broadcasted_iota(jnp.int32, (size, 1), 0)
            x = _unpack_rows(jnp.where(rows < cnt_ref[e] - (row - row0), xbuf[slot, pl.ds(0, size), :], 0))
            hcat = _dot(x, wu_bf[...]) + b_up
            glu = jnp.minimum(hcat[:, 0:F], SWIGLU_LIMIT)
            lin = jnp.clip(hcat[:, F:2 * F], -SWIGLU_LIMIT, SWIGLU_LIMIT)
            act = glu * _sigmoid(SWIGLU_ALPHA * glu) * (lin + 1.0)
            y = _pack_rows(_dot(act.astype(_bf16), wd_bf[...]) + b_down)

            y_wait(slot)
            ybuf[slot, pl.ds(0, size), :] = y
            y_store(row, size, slot).start()
            ysz[slot] = size
            return n_done + 1

        n_done = lax.fori_loop(0, n_big, functools.partial(pass_body, BIG_BLOCK), n_done)
        return lax.fori_loop(0, n_small, functools.partial(pass_body, SLOT_PAD), n_done)

    lax.fori_loop(0, E, expert_body, 0)
    y_wait(0)
    y_wait(1)


def _experts(cnt, row0, padded, nxt, xpad, w_up, b_up, w_down, b_down):
    def full(a):
        nd = a.ndim
        return pl.BlockSpec(a.shape, lambda i, *_: (0,) * nd)

    return pl.pallas_call(
        _experts_kernel,
        out_shape=jax.ShapeDtypeStruct(xpad.shape, jnp.int32),
        grid_spec=pltpu.PrefetchScalarGridSpec(
            num_scalar_prefetch=4,
            grid=(1,),
            in_specs=[pl.BlockSpec(memory_space=pl.ANY),
                      pl.BlockSpec(memory_space=pl.ANY), full(b_up),
                      pl.BlockSpec(memory_space=pl.ANY), full(b_down)],
            out_specs=pl.BlockSpec(memory_space=pl.ANY),
            scratch_shapes=[pltpu.VMEM((2, BIG_BLOCK, D // 2), jnp.int32),
                            pltpu.VMEM((2, BIG_BLOCK, D // 2), jnp.int32),
                            pltpu.VMEM((D, 2 * F), _f32), pltpu.VMEM((F, D), _f32),
                            pltpu.VMEM((D, 2 * F), _bf16), pltpu.VMEM((F, D), _bf16),
                            pltpu.SMEM((2,), jnp.int32),
                            pltpu.SemaphoreType.DMA((2,)), pltpu.SemaphoreType.DMA((2,)),
                            pltpu.SemaphoreType.DMA((2,))],
        ),
        compiler_params=pltpu.CompilerParams(vmem_limit_bytes=VMEM_LIMIT),
        name="experts",
    )(cnt, row0, padded, nxt, xpad, w_up, b_up, w_down, b_down)


def _gather_rows(ypad, dest_all):
    n_out = dest_all.shape[0]
    width = ypad.shape[1]
    w = SC_WINDOW
    mesh = plsc.VectorSubcoreMesh(core_axis_name="core", subcore_axis_name="subcore")
    n_workers = mesh.num_cores * mesh.num_subcores
    n_windows = n_out // w

    @functools.partial(
        pl.kernel, mesh=mesh, name="gather_rows",
        out_type=jax.ShapeDtypeStruct((n_out, width), jnp.int32),
        scratch_types=[pltpu.VMEM((w, width), jnp.int32), pltpu.VMEM((w,), jnp.int32), pltpu.SemaphoreType.DMA])
    def gather_rows(y_hbm, d_hbm, o_hbm, rows, iv, sem):
        worker = lax.axis_index("subcore") * mesh.num_cores + lax.axis_index("core")

        @pl.loop(0, n_windows // n_workers)
        def _(j):
            r0 = pl.multiple_of((j * n_workers + worker) * w, w)
            pltpu.sync_copy(d_hbm.at[pl.ds(r0, w)], iv)
            pltpu.async_copy(y_hbm.at[iv], rows, sem).wait()
            pltpu.sync_copy(rows, o_hbm.at[pl.ds(r0, w)])

    assert n_windows % n_workers == 0
    return gather_rows(ypad, dest_all)


def _combine_kernel(h_ref, mod_ref, gate_ref, g2_ref, b2_ref, y_ref, *aliased_and_out):
    o_ref = aliased_and_out[-1]
    gates = gate_ref[...]
    half = D // 2
    f_hi = jnp.zeros((h_ref.shape[0], half), _f32)
    f_lo = jnp.zeros((h_ref.shape[0], half), _f32)
    for k in range(TOP_K):
        p = y_ref[k]
        gk = gates[:, k:k + 1]
        f_hi = f_hi + gk * lax.bitcast_convert_type(p & jnp.int32(-65536), _f32)
        f_lo = f_lo + gk * lax.bitcast_convert_type(lax.shift_left(p, 16), _f32)
    f = jnp.concatenate([f_hi, f_lo], axis=1)
    gate2 = mod_ref[...][:, 5 * D:6 * D]
    pre = ALPHA * h_ref[...] + _per_seq(f, gate2, jnp.multiply)
    o_ref[...] = _layer_norm(pre, g2_ref[...], b2_ref[...])


def _combine(h, mod3, gates, ln2_g, ln2_b, y4, token0, rows_per_mod, row0, out_so_far):
    t = COMBINE_TILE
    hoff = row0 // t
    goff = (token0 + row0) // t
    g = mod3.shape[1]
    tiles_per_mod = rows_per_mod // t
    in_specs = [pl.BlockSpec((t, D), lambda i: (i + hoff, 0)),
                pl.BlockSpec((None, g, 6 * D), lambda i: ((i + hoff) // tiles_per_mod, 0, 0)),
                pl.BlockSpec((t, LANES), lambda i: (i + goff, 0)),
                pl.BlockSpec((1, D), lambda i: (0, 0)),
                pl.BlockSpec((1, D), lambda i: (0, 0)),
                pl.BlockSpec((TOP_K, t, D // 2), lambda i: (0, i, 0))]
    args = [h, mod3, gates, ln2_g, ln2_b, y4]
    aliases = {}
    if out_so_far is not None:
        in_specs.append(pl.BlockSpec(memory_space=pl.ANY))
        args.append(out_so_far)
        aliases = {len(args) - 1: 0}
    return pl.pallas_call(
        _combine_kernel,
        out_shape=jax.ShapeDtypeStruct(h.shape, _f32),
        grid=(y4.shape[1] // t,),
        in_specs=in_specs,
        out_specs=pl.BlockSpec((t, D), lambda i: (i + hoff, 0)),
        input_output_aliases=aliases,
        compiler_params=pltpu.CompilerParams(vmem_limit_bytes=VMEM_LIMIT),
        name="combine",
    )(*args)


def _time_major(a):
    return a.transpose(1, 0, 2)


def kernel(x_prompt, x_sample, c_prompt, c_sample, state_conv, state_pool, w_ada, b_ada, w_in,
           conv_w, w_out_a, w_pool, ls_pool, w_out_b, w_o, ln1_g, ln1_b, w_router, b_router,
           w_up, b_up, w_down, b_down, ln2_g, ln2_b):
    n_seq_p, seq, _ = x_prompt.shape
    n_seq_s, dec_seq, _ = x_sample.shape
    n_p, n_s = n_seq_p * seq, n_seq_s * dec_seq
    n = n_p + n_s
    n_slots = TOP_K * n + E * SLOT_PAD
    l = 0

    mod = _ada(jnp.concatenate([c_prompt, c_sample], axis=0), w_ada[l], b_ada[l][None])
    mod_p = mod[:n_seq_p][:, None, :]
    mod_s = mod[n_seq_p:][None]

    weights = (
        w_in[l].astype(_bf16), conv_w[l], w_out_a[l].astype(_bf16), w_pool[l].astype(_bf16),
        ls_pool[l][None], w_out_b[l].astype(_bf16), w_o[l].astype(_bf16), ln1_g[l][None], ln1_b[l][None],
        jnp.pad(w_router[l], ((0, 0), (0, LANES - E))), jnp.pad(b_router[l], (0, LANES - E))[None],
    )
    hc_p, hp_p = _hist_steps(CONV_HIST, 1), _hist_steps(POOL_HIST, 1)
    h_p, v_p, lg_p, nc_p, np_p = _mixer(
        x_prompt.reshape(n_p, D), mod_p, jnp.zeros((n_seq_p * hc_p, C), _f32),
        jnp.zeros((n_seq_p * hp_p, C), _f32), weights, PROMPT_ROW_TILE, PROMPT_SUB_TILES, 0)
    hc_s, hp_s = _hist_steps(CONV_HIST, n_seq_s), _hist_steps(POOL_HIST, n_seq_s)
    hist_c = jnp.pad(_time_major(state_conv[l]), ((hc_s - CONV_HIST, 0), (0, 0), (0, 0)))
    hist_p = jnp.pad(_time_major(state_pool[l]), ((hp_s - POOL_HIST, 0), (0, 0), (0, 0)))
    h_s, v_s, lg_s, nc_s, np_s = _mixer(
        _time_major(x_sample).reshape(n_s, D), mod_s, hist_c.reshape(hc_s * n_seq_s, C),
        hist_p.reshape(hp_s * n_seq_s, C), weights, SAMPLE_ROW_TILE, 1, PAST_LEN)

    dest8, gates3, meta = _plan(lg_p, lg_s)
    cnt, row0, padded, nxt = (meta[i, :E] for i in range(4))
    gates = gates3.reshape(n, LANES)

    dests = [dest8[:, k, :].reshape(n) for k in range(TOP_K)]
    xpad = _dispatch(v_p, v_s, dests, n_slots)
    ypad = _experts(cnt, row0, padded, nxt, xpad, w_up[l], b_up[l][:, None, :], w_down[l], b_down[l][:, None, :])

    def gathered(t0, rows):
        idx = jnp.concatenate([dk[t0:t0 + rows] for dk in dests])
        return _gather_rows(ypad, idx).reshape(TOP_K, rows, D // 2)

    chunk = n_p // COMBINE_CHUNKS
    y4_p = [gathered(c * chunk, chunk) for c in range(COMBINE_CHUNKS)]
    y4_s = gathered(n_p, n_s)
    g2, b2 = ln2_g[l][None], ln2_b[l][None]
    y_p = None
    for c in range(COMBINE_CHUNKS):
        y_p = _combine(h_p, mod_p, gates, g2, b2, y4_p[c], 0, seq, c * chunk, y_p)
    y_s = _combine(h_s, mod_s, gates, g2, b2, y4_s, n_p, n_s, 0, None)

    y_prompt = y_p.reshape(n_seq_p, seq, D)
    y_sample = _time_major(y_s.reshape(dec_seq, n_seq_s, D))
    new_conv_p = nc_p.reshape(n_seq_p, hc_p, C)[:, hc_p - CONV_HIST:][None]
    new_pool_p = np_p.reshape(n_seq_p, hp_p, C)[:, hp_p - POOL_HIST:][None]
    new_conv_s = _time_major(nc_s.reshape(hc_s, n_seq_s, C)[hc_s - CONV_HIST:])[None]
    new_pool_s = _time_major(np_s.reshape(hp_s, n_seq_s, C)[hp_s - POOL_HIST:])[None]
    return (y_prompt, y_sample, new_conv_p, new_pool_p, new_conv_s, new_pool_s)
```

```python
import functools

import jax
import jax.numpy as jnp
from jax import lax
from jax.experimental import pallas as pl
from jax.experimental.pallas import tpu as pltpu
from jax.experimental.pallas import tpu_sc as plsc

D = 1024
C = 512
N_GROUPS = 4
GROUP = C // N_GROUPS
CONV_HIST = 2
POOL_HIST = 15
E = 32
TOP_K = 4
F = 1024
SWIGLU_LIMIT = 7.0
SWIGLU_ALPHA = 1.702
LN_EPS = 1e-5
DEPTH = 1
ALPHA = (2 * DEPTH) ** 0.25
PAST_LEN = 16384

LANES = 128
SUBLANES = 8
ROW_TILE = 512
PROMPT_ROW_TILE = 1024
PROMPT_SUB_TILES = 1
SAMPLE_ROW_TILE = 512
SLOT_PAD = 256
BIG_BLOCK = 1024
COMBINE_TILE = 512
COMBINE_CHUNKS = 2
SC_WINDOW = 128
VMEM_LIMIT = 56 * 1024 * 1024

_f32 = jnp.float32
_bf16 = jnp.bfloat16


def _dot(a, b):
    return jnp.dot(a, b, preferred_element_type=_f32)


def _dot_exact(a, b):
    return lax.dot_general(a, b, (((1,), (0,)), ((), ())),
                           precision=lax.Precision.HIGHEST, preferred_element_type=_f32)


def _dot_split(a, b):
    a_hi, b_hi = a.astype(_bf16), b.astype(_bf16)
    a_lo = (a - a_hi.astype(_f32)).astype(_bf16)
    b_lo = (b - b_hi.astype(_f32)).astype(_bf16)
    return _dot(a_hi, b_hi) + _dot(a_lo, b_hi) + _dot(a_hi, b_lo)


def _pack_rows(x):
    w = x.shape[1] // 2
    hi = lax.bitcast_convert_type(x[:, :w].astype(_bf16).astype(_f32), jnp.int32)
    lo = lax.bitcast_convert_type(x[:, w:].astype(_bf16).astype(_f32), jnp.int32)
    return hi | lax.shift_right_logical(lo, 16)


def _unpack_rows(p):
    hi = lax.bitcast_convert_type(p & jnp.int32(-65536), _f32)
    lo = lax.bitcast_convert_type(lax.shift_left(p, 16), _f32)
    return jnp.concatenate([hi, lo], axis=1).astype(_bf16)


def _sigmoid(x):
    return 0.5 * jnp.tanh(0.5 * x) + 0.5


def _per_seq(x, m, op):
    g = m.shape[0]
    if g == 1:
        return op(x, m)
    r, n = x.shape
    return op(x.reshape(r // g, g, n), m[None]).reshape(r, n)


def _layer_norm(x, g, b):
    mu = jnp.mean(x, axis=-1, keepdims=True)
    xc = x - mu
    var = jnp.mean(xc * xc, axis=-1, keepdims=True)
    return xc * lax.rsqrt(var + LN_EPS) * g + b


def _hist_steps(needed, g):
    return -(-needed * g // SUBLANES) * SUBLANES // g


def _ada_kernel(c_ref, w_ref, b_ref, o_ref):
    c = c_ref[...]
    o_ref[...] = _dot_split(c * _sigmoid(c), w_ref[...]) + b_ref[...]


def _ada(c, w_ada, b_ada):
    rows = c.shape[0]
    cols = w_ada.shape[1]
    bn = 1536
    return pl.pallas_call(
        _ada_kernel,
        out_shape=jax.ShapeDtypeStruct((rows, cols), _f32),
        grid=(cols // bn,),
        in_specs=[pl.BlockSpec((rows, D), lambda j: (0, 0)),
                  pl.BlockSpec((D, bn), lambda j: (0, j)),
                  pl.BlockSpec((1, bn), lambda j: (0, j))],
        out_specs=pl.BlockSpec((rows, bn), lambda j: (0, j)),
        compiler_params=pltpu.CompilerParams(vmem_limit_bytes=VMEM_LIMIT),
        name="ada",
    )(c, w_ada, b_ada)


def _mixer_kernel(g, tiles_per_seq, start_pos, n_sub,
                  x_ref, mod_ref, hc_ref, hp_ref,
                  win_ref, cw_ref, woa_ref, wpool_ref, ls_ref, wob_ref, wo_ref, g1_ref, b1_ref,
                  wr_ref, br_ref,
                  h_ref, v_ref, lg_ref, nc_ref, np_ref, zbuf, pbuf):
    r = x_ref.shape[0]
    hrc = hc_ref.shape[0]
    hrp = hp_ref.shape[0]
    j = pl.program_id(0) % tiles_per_seq

    @pl.when(j == 0)
    def _():
        zbuf[pl.ds(0, hrc), :] = hc_ref[...]
        pbuf[pl.ds(0, hrp), :] = hp_ref[...]

    @pl.when(j != 0)
    def _():
        zt = zbuf[pl.ds(r, hrc), :]
        pt = pbuf[pl.ds(r, hrp), :]
        zbuf[pl.ds(0, hrc), :] = zt
        pbuf[pl.ds(0, hrp), :] = pt

    m = mod_ref[...]
    shift1, scale1, gate1 = m[:, 0:D], m[:, D:2 * D], m[:, 2 * D:3 * D]
    shift2, scale2 = m[:, 3 * D:4 * D], m[:, 4 * D:5 * D]

    rs = r // n_sub
    for s in range(n_sub):
        rows = pl.ds(s * rs, rs)
        x = x_ref[rows, :]
        u = _per_seq(_per_seq(x, 1.0 + scale1, jnp.multiply), shift1, jnp.add).astype(_bf16)

        z = _dot(u, win_ref[:, C:2 * C]) * _dot(u, win_ref[:, 2 * C:3 * C])
        zrow = hrc + s * rs
        zbuf[pl.ds(zrow, rs), :] = z
        cw = cw_ref[...]
        conv = (cw[0:1] * zbuf[pl.ds(zrow - 2 * g, rs), :] + cw[1:2] * zbuf[pl.ds(zrow - g, rs), :]
                + cw[2:3] * z)
        y_a = _dot((_dot(u, win_ref[:, 0:C]) * conv).astype(_bf16), woa_ref[...])

        xp = _dot(u, win_ref[:, 3 * C:4 * C])
        prow = hrp + s * rs
        pbuf[pl.ds(prow, rs), :] = xp
        pos = (start_pos + j * (r // g) + s * (rs // g)
               + lax.broadcasted_iota(jnp.int32, (rs, 1), 0) // g)
        acc = xp
        yg = []
        for grp in range(N_GROUPS):
            lo = grp * GROUP
            wdw = 2 ** (grp + 1)
            for back in range(wdw // 2, wdw):
                sh = pbuf[pl.ds(prow - back * g, rs), lo:C]
                acc = jnp.concatenate([acc[:, 0:lo], acc[:, lo:C] + sh], axis=1) if lo else acc + sh
            inv_cnt = 1.0 / jnp.minimum(wdw, pos + 1).astype(_f32)
            diff = acc[:, lo:lo + GROUP] * inv_cnt - xp[:, lo:lo + GROUP]
            yg.append(_dot(diff.astype(_bf16), wpool_ref[grp]))
        y_b = _dot((jnp.concatenate(yg, axis=1) * ls_ref[...]).astype(_bf16), wob_ref[...])

        g_a = _dot(u, win_ref[:, 4 * C:4 * C + D])
        g_b = _dot(u, win_ref[:, 4 * C + D:4 * C + 2 * D])
        merged = _sigmoid(g_a) * y_a + _sigmoid(g_b) * y_b
        o = _dot(merged.astype(_bf16), wo_ref[...])
        h = _layer_norm(ALPHA * x + _per_seq(o, gate1, jnp.multiply), g1_ref[...], b1_ref[...])
        v = _per_seq(_per_seq(h, 1.0 + scale2, jnp.multiply), shift2, jnp.add)
        h_ref[rows, :] = h
        v_ref[rows, :] = _pack_rows(v)
        lg_ref[rows, :] = _dot(v.astype(_bf16), wr_ref[...].astype(_bf16)) + br_ref[...]
    nc_ref[...] = zbuf[pl.ds(r, hrc), :]
    np_ref[...] = pbuf[pl.ds(r, hrp), :]


def _mixer(x2, mod3, hc, hp, weights, row_tile, n_sub, start_pos):
    n = x2.shape[0]
    n_mod, g, _ = mod3.shape
    hrc, hrp = hc.shape[0] // n_mod, hp.shape[0] // n_mod
    tiles_per_seq = n // n_mod // row_tile
    once = dict(pipeline_mode=pl.Buffered(1)) if n_mod == 1 else {}

    def full(a):
        nd = a.ndim
        return pl.BlockSpec(a.shape, lambda i: (0,) * nd)

    def seq_block(rows, **kw):
        return pl.BlockSpec((rows, C), lambda i: (i // tiles_per_seq, 0), **kw)

    def row_block(cols):
        return pl.BlockSpec((row_tile, cols), lambda i: (i, 0))

    return pl.pallas_call(
        functools.partial(_mixer_kernel, g, tiles_per_seq, start_pos, n_sub),
        out_shape=[
            jax.ShapeDtypeStruct((n, D), _f32),
            jax.ShapeDtypeStruct((n, D // 2), jnp.int32),
            jax.ShapeDtypeStruct((n, LANES), _f32),
            jax.ShapeDtypeStruct(hc.shape, _f32),
            jax.ShapeDtypeStruct(hp.shape, _f32),
        ],
        grid=(n // row_tile,),
        in_specs=[row_block(D),
                  pl.BlockSpec((None, g, 6 * D), lambda i: (i // tiles_per_seq, 0, 0), **once),
                  seq_block(hrc, **once), seq_block(hrp, **once)] + [full(a) for a in weights],
        out_specs=[row_block(D), row_block(D // 2), row_block(LANES), seq_block(hrc), seq_block(hrp)],
        scratch_shapes=[pltpu.VMEM((hrc + row_tile, C), _f32), pltpu.VMEM((hrp + row_tile, C), _f32)],
        compiler_params=pltpu.CompilerParams(vmem_limit_bytes=VMEM_LIMIT),
        name="mixer",
    )(x2, mod3, hc, hp, *weights)


def _plan_kernel(lgp_ref, lgs_ref, dest_ref, gate_ref, meta_ref, idx_s, rank_s):
    t = ROW_TILE
    e_iota = lax.broadcasted_iota(jnp.int32, (E, t), 0)
    tri = (lax.broadcasted_iota(jnp.int32, (t, t), 0)
           < lax.broadcasted_iota(jnp.int32, (t, t), 1)).astype(_f32).astype(_bf16)
    zeros_rest = jnp.zeros((LANES - TOP_K, t), _f32)

    def tile_body(lg_ref, off, i, carry):
        lt = lg_ref[i].T[0:E, :]
        vals, idxs = [], []
        for _ in range(TOP_K):
            mx = jnp.max(lt, axis=0, keepdims=True)
            ix = jnp.min(jnp.where(lt == mx, e_iota, E), axis=0, keepdims=True)
            vals.append(mx)
            idxs.append(ix)
            lt = jnp.where(e_iota == ix, -jnp.inf, lt)
        ex = [jnp.exp(vk - vals[0]) for vk in vals]
        den = ex[0] + ex[1] + ex[2] + ex[3]
        gates = [ek / den for ek in ex]
        gate_ref[off + i] = jnp.concatenate(gates + [zeros_rest], axis=0).T

        ohs = [(e_iota == ix) for ix in idxs]
        oh = (ohs[0] | ohs[1] | ohs[2] | ohs[3]).astype(_f32)
        before = _dot(oh.astype(_bf16), tri) + carry
        ranks = [jnp.sum(jnp.where(o, before, 0.0), axis=0, keepdims=True) for o in ohs]
        idx_s[off + i] = jnp.concatenate(idxs + idxs, axis=0)
        rank_s[off + i] = jnp.concatenate(ranks + ranks, axis=0).astype(jnp.int32)
        return carry + jnp.sum(oh, axis=1, keepdims=True)

    n_p, n_s = lgp_ref.shape[0], lgs_ref.shape[0]
    counts = lax.fori_loop(0, n_p, functools.partial(tile_body, lgp_ref, 0), jnp.zeros((E, 1), _f32))
    counts = lax.fori_loop(0, n_s, functools.partial(tile_body, lgs_ref, n_p), counts)
    padded = jnp.ceil(counts / SLOT_PAD) * SLOT_PAD
    low = (lax.broadcasted_iota(jnp.int32, (E, E), 1)
           <= lax.broadcasted_iota(jnp.int32, (E, E), 0)).astype(_f32)
    pad_end = _dot_exact(low, jnp.broadcast_to(padded, (E, LANES)))[:, 0:1]
    pad_start = pad_end - padded

    def dest_body(i, c):
        ix = idx_s[i]
        rk = rank_s[i]
        rows = []
        for k in range(TOP_K):
            st = jnp.sum(jnp.where(e_iota == ix[k:k + 1], pad_start, 0.0), axis=0, keepdims=True)
            rows.append(st.astype(jnp.int32) + rk[k:k + 1])
        dest_ref[i] = jnp.concatenate(rows + rows, axis=0)
        return c

    lax.fori_loop(0, n_p + n_s, dest_body, 0)

    sub = lax.broadcasted_iota(jnp.int32, (E, LANES), 0)
    lane = lax.broadcasted_iota(jnp.int32, (E, LANES), 1)

    def to_lanes(col):
        return jnp.sum(jnp.where(sub == lane, col, 0.0), axis=0, keepdims=True).astype(jnp.int32)

    later = jnp.min(jnp.where((sub > lane) & (counts > 0.0), sub, E), axis=0, keepdims=True)
    meta_ref[...] = jnp.concatenate(
        [to_lanes(counts), to_lanes(pad_start), to_lanes(padded), later,
         jnp.zeros((SUBLANES - 4, LANES), jnp.int32)], axis=0)


def _plan(logits_p, logits_s):
    n_tiles = (logits_p.shape[0] + logits_s.shape[0]) // ROW_TILE
    return pl.pallas_call(
        _plan_kernel,
        out_shape=[
            jax.ShapeDtypeStruct((n_tiles, 2 * TOP_K, ROW_TILE), jnp.int32),
            jax.ShapeDtypeStruct((n_tiles, ROW_TILE, LANES), _f32),
            jax.ShapeDtypeStruct((SUBLANES, LANES), jnp.int32),
        ],
        scratch_shapes=[pltpu.VMEM((n_tiles, 2 * TOP_K, ROW_TILE), jnp.int32),
                        pltpu.VMEM((n_tiles, 2 * TOP_K, ROW_TILE), jnp.int32)],
        compiler_params=pltpu.CompilerParams(vmem_limit_bytes=VMEM_LIMIT),
        name="plan",
    )(logits_p.reshape(-1, ROW_TILE, LANES), logits_s.reshape(-1, ROW_TILE, LANES))


def _dispatch(v_p, v_s, dests, n_rows_out):
    n_p, n_s = v_p.shape[0], v_s.shape[0]
    width = v_p.shape[1]
    w = SC_WINDOW
    n_pw, n_windows = n_p // w, (n_p + n_s) // w
    mesh = plsc.VectorSubcoreMesh(core_axis_name="core", subcore_axis_name="subcore")
    n_workers = mesh.num_cores * mesh.num_subcores

    @functools.partial(
        pl.kernel, mesh=mesh, name="dispatch",
        out_type=jax.ShapeDtypeStruct((n_rows_out, width), jnp.int32),
        scratch_types=[pltpu.VMEM((w, width), jnp.int32)] + [pltpu.VMEM((w,), jnp.int32)] * TOP_K
        + [pltpu.SemaphoreType.DMA])
    def scatter_rows(vp_hbm, vs_hbm, d0_hbm, d1_hbm, d2_hbm, d3_hbm, o_hbm, rows, i0, i1, i2, i3, sem):
        worker = lax.axis_index("subcore") * mesh.num_cores + lax.axis_index("core")
        idx = (i0, i1, i2, i3)

        def scatter_window(c):
            t0 = pl.multiple_of(c * w, w)
            for d_hbm, iv in zip((d0_hbm, d1_hbm, d2_hbm, d3_hbm), idx):
                pltpu.sync_copy(d_hbm.at[pl.ds(t0, w)], iv)
            copies = [pltpu.async_copy(rows, o_hbm.at[iv], sem) for iv in idx]
            for cp in copies:
                cp.wait()

        for j in range(-(-n_windows // n_workers)):
            c = j * n_workers + worker

            @pl.when(c < n_pw)
            def _():
                pltpu.sync_copy(vp_hbm.at[pl.ds(pl.multiple_of(c * w, w), w)], rows)
                scatter_window(c)

            @pl.when((c >= n_pw) & (c < n_windows))
            def _():
                pltpu.sync_copy(vs_hbm.at[pl.ds(pl.multiple_of((c - n_pw) * w, w), w)], rows)
                scatter_window(c)

    return scatter_rows(v_p, v_s, *dests)


def _experts_kernel(cnt_ref, row0_ref, pad_ref, nxt_ref, x_hbm, wu_hbm, bu_ref, wd_hbm, bd_ref, y_hbm,
                    xbuf, ybuf, wu_st, wd_st, wu_bf, wd_bf, ysz, xsem, ysem, wsem):
    def w_fetch(e):
        return (pltpu.make_async_copy(wu_hbm.at[e], wu_st, wsem.at[0]),
                pltpu.make_async_copy(wd_hbm.at[e], wd_st, wsem.at[1]))

    def x_fetch(row, size, slot):
        rows = pl.ds(pl.multiple_of(row, SLOT_PAD), size)
        return pltpu.make_async_copy(x_hbm.at[rows, :], xbuf.at[slot, pl.ds(0, size), :], xsem.at[slot])

    def y_store(row, size, slot):
        rows = pl.ds(pl.multiple_of(row, SLOT_PAD), size)
        return pltpu.make_async_copy(ybuf.at[slot, pl.ds(0, size), :], y_hbm.at[rows, :], ysem.at[slot])

    def y_wait(slot):
        for size in (BIG_BLOCK, SLOT_PAD):
            @pl.when(ysz[slot] == size)
            def _():
                y_store(0, size, slot).wait()

    def fetch_first(e, slot):
        @pl.when(pad_ref[e] >= BIG_BLOCK)
        def _():
            x_fetch(row0_ref[e], BIG_BLOCK, slot).start()

        @pl.when((pad_ref[e] > 0) & (pad_ref[e] < BIG_BLOCK))
        def _():
            x_fetch(row0_ref[e], SLOT_PAD, slot).start()

    ysz[0] = 0
    ysz[1] = 0
    for c in w_fetch(0):
        c.start()
    fetch_first(jnp.where(pad_ref[0] > 0, 0, nxt_ref[0]), 0)

    def expert_body(e, n_done):
        for c in w_fetch(e):
            c.wait()
        padded = pad_ref[e]

        @pl.when(padded > 0)
        def _():
            wu_bf[...] = wu_st[...].astype(_bf16)
            wd_bf[...] = wd_st[...].astype(_bf16)

        @pl.when(e + 1 < E)
        def _():
            for c in w_fetch(e + 1):
                c.start()

        b_up = bu_ref[e]
        b_down = bd_ref[e]
        row0 = row0_ref[e]
        n_big = padded // BIG_BLOCK
        n_small = (padded - n_big * BIG_BLOCK) // SLOT_PAD
        small0 = row0 + n_big * BIG_BLOCK
        nxt = nxt_ref[e]

        def pass_body(size, j, n_done):
            slot = n_done % 2
            big = size == BIG_BLOCK
            row = row0 + j * BIG_BLOCK if big else small0 + j * SLOT_PAD
            x_fetch(row, size, slot).wait()

            more = j + 1 < (n_big if big else n_small)
            tail = (n_small > 0) if big else False

            @pl.when(more)
            def _():
                x_fetch(row + size, size, 1 - slot).start()

            if big:
                @pl.when(jnp.logical_not(more) & tail)
                def _():
                    x_fetch(small0, SLOT_PAD, 1 - slot).start()

            @pl.when(jnp.logical_not(more) & jnp.logical_not(tail) & (nxt < E))
            def _():
                fetch_first(jnp.minimum(nxt, E - 1), 1 - slot)

            rows = lax.broadcasted_iota(jnp.int32, (size, 1), 0)
            x = _unpack_rows(jnp.where(rows < cnt_ref[e] - (row - row0), xbuf[slot, pl.ds(0, size), :], 0))
            hcat = _dot(x, wu_bf[...]) + b_up
            glu = jnp.minimum(hcat[:, 0:F], SWIGLU_LIMIT)
            lin = jnp.clip(hcat[:, F:2 * F], -SWIGLU_LIMIT, SWIGLU_LIMIT)
            act = glu * _sigmoid(SWIGLU_ALPHA * glu) * (lin + 1.0)
            y = _pack_rows(_dot(act.astype(_bf16), wd_bf[...]) + b_down)

            y_wait(slot)
            ybuf[slot, pl.ds(0, size), :] = y
            y_store(row, size, slot).start()
            ysz[slot] = size
            return n_done + 1

        n_done = lax.fori_loop(0, n_big, functools.partial(pass_body, BIG_BLOCK), n_done)
        return lax.fori_loop(0, n_small, functools.partial(pass_body, SLOT_PAD), n_done)

    lax.fori_loop(0, E, expert_body, 0)
    y_wait(0)
    y_wait(1)


def _experts(cnt, row0, padded, nxt, xpad, w_up, b_up, w_down, b_down):
    def full(a):
        nd = a.ndim
        return pl.BlockSpec(a.shape, lambda i, *_: (0,) * nd)

    return pl.pallas_call(
        _experts_kernel,
        out_shape=jax.ShapeDtypeStruct(xpad.shape, jnp.int32),
        grid_spec=pltpu.PrefetchScalarGridSpec(
            num_scalar_prefetch=4,
            grid=(1,),
            in_specs=[pl.BlockSpec(memory_space=pl.ANY),
                      pl.BlockSpec(memory_space=pl.ANY), full(b_up),
                      pl.BlockSpec(memory_space=pl.ANY), full(b_down)],
            out_specs=pl.BlockSpec(memory_space=pl.ANY),
            scratch_shapes=[pltpu.VMEM((2, BIG_BLOCK, D // 2), jnp.int32),
                            pltpu.VMEM((2, BIG_BLOCK, D // 2), jnp.int32),
                            pltpu.VMEM((D, 2 * F), _f32), pltpu.VMEM((F, D), _f32),
                            pltpu.VMEM((D, 2 * F), _bf16), pltpu.VMEM((F, D), _bf16),
                            pltpu.SMEM((2,), jnp.int32),
                            pltpu.SemaphoreType.DMA((2,)), pltpu.SemaphoreType.DMA((2,)),
                            pltpu.SemaphoreType.DMA((2,))],
        ),
        compiler_params=pltpu.CompilerParams(vmem_limit_bytes=VMEM_LIMIT),
        name="experts",
    )(cnt, row0, padded, nxt, xpad, w_up, b_up, w_down, b_down)


def _gather_rows(ypad, dest_all):
    n_out = dest_all.shape[0]
    width = ypad.shape[1]
    w = SC_WINDOW
    mesh = plsc.VectorSubcoreMesh(core_axis_name="core", subcore_axis_name="subcore")
    n_workers = mesh.num_cores * mesh.num_subcores
    n_windows = n_out // w

    @functools.partial(
        pl.kernel, mesh=mesh, name="gather_rows",
        out_type=jax.ShapeDtypeStruct((n_out, width), jnp.int32),
        scratch_types=[pltpu.VMEM((w, width), jnp.int32), pltpu.VMEM((w,), jnp.int32), pltpu.SemaphoreType.DMA])
    def gather_rows(y_hbm, d_hbm, o_hbm, rows, iv, sem):
        worker = lax.axis_index("subcore") * mesh.num_cores + lax.axis_index("core")

        @pl.loop(0, n_windows // n_workers)
        def _(j):
            r0 = pl.multiple_of((j * n_workers + worker) * w, w)
            pltpu.sync_copy(d_hbm.at[pl.ds(r0, w)], iv)
            pltpu.async_copy(y_hbm.at[iv], rows, sem).wait()
            pltpu.sync_copy(rows, o_hbm.at[pl.ds(r0, w)])

    assert n_windows % n_workers == 0
    return gather_rows(ypad, dest_all)


def _combine_kernel(h_ref, mod_ref, gate_ref, g2_ref, b2_ref, y_ref, *aliased_and_out):
    o_ref = aliased_and_out[-1]
    gates = gate_ref[...]
    half = D // 2
    f_hi = jnp.zeros((h_ref.shape[0], half), _f32)
    f_lo = jnp.zeros((h_ref.shape[0], half), _f32)
    for k in range(TOP_K):
        p = y_ref[k]
        gk = gates[:, k:k + 1]
        f_hi = f_hi + gk * lax.bitcast_convert_type(p & jnp.int32(-65536), _f32)
        f_lo = f_lo + gk * lax.bitcast_convert_type(lax.shift_left(p, 16), _f32)
    f = jnp.concatenate([f_hi, f_lo], axis=1)
    gate2 = mod_ref[...][:, 5 * D:6 * D]
    pre = ALPHA * h_ref[...] + _per_seq(f, gate2, jnp.multiply)
    o_ref[...] = _layer_norm(pre, g2_ref[...], b2_ref[...])


def _combine(h, mod3, gates, ln2_g, ln2_b, y4, token0, rows_per_mod, row0, out_so_far):
    t = COMBINE_TILE
    hoff = row0 // t
    goff = (token0 + row0) // t
    g = mod3.shape[1]
    tiles_per_mod = rows_per_mod // t
    in_specs = [pl.BlockSpec((t, D), lambda i: (i + hoff, 0)),
                pl.BlockSpec((None, g, 6 * D), lambda i: ((i + hoff) // tiles_per_mod, 0, 0)),
                pl.BlockSpec((t, LANES), lambda i: (i + goff, 0)),
                pl.BlockSpec((1, D), lambda i: (0, 0)),
                pl.BlockSpec((1, D), lambda i: (0, 0)),
                pl.BlockSpec((TOP_K, t, D // 2), lambda i: (0, i, 0))]
    args = [h, mod3, gates, ln2_g, ln2_b, y4]
    aliases = {}
    if out_so_far is not None:
        in_specs.append(pl.BlockSpec(memory_space=pl.ANY))
        args.append(out_so_far)
        aliases = {len(args) - 1: 0}
    return pl.pallas_call(
        _combine_kernel,
        out_shape=jax.ShapeDtypeStruct(h.shape, _f32),
        grid=(y4.shape[1] // t,),
        in_specs=in_specs,
        out_specs=pl.BlockSpec((t, D), lambda i: (i + hoff, 0)),
        input_output_aliases=aliases,
        compiler_params=pltpu.CompilerParams(vmem_limit_bytes=VMEM_LIMIT),
        name="combine",
    )(*args)


def _time_major(a):
    return a.transpose(1, 0, 2)


def kernel(x_prompt, x_sample, c_prompt, c_sample, state_conv, state_pool, w_ada, b_ada, w_in,
           conv_w, w_out_a, w_pool, ls_pool, w_out_b, w_o, ln1_g, ln1_b, w_router, b_router,
           w_up, b_up, w_down, b_down, ln2_g, ln2_b):
    n_seq_p, seq, _ = x_prompt.shape
    n_seq_s, dec_seq, _ = x_sample.shape
    n_p, n_s = n_seq_p * seq, n_seq_s * dec_seq
    n = n_p + n_s
    n_slots = TOP_K * n + E * SLOT_PAD
    l = 0

    mod = _ada(jnp.concatenate([c_prompt, c_sample], axis=0), w_ada[l], b_ada[l][None])
    mod_p = mod[:n_seq_p][:, None, :]
    mod_s = mod[n_seq_p:][None]

    weights = (
        w_in[l].astype(_bf16), conv_w[l], w_out_a[l].astype(_bf16), w_pool[l].astype(_bf16),
        ls_pool[l][None], w_out_b[l].astype(_bf16), w_o[l].astype(_bf16), ln1_g[l][None], ln1_b[l][None],
        jnp.pad(w_router[l], ((0, 0), (0, LANES - E))), jnp.pad(b_router[l], (0, LANES - E))[None],
    )
    hc_p, hp_p = _hist_steps(CONV_HIST, 1), _hist_steps(POOL_HIST, 1)
    h_p, v_p, lg_p, nc_p, np_p = _mixer(
        x_prompt.reshape(n_p, D), mod_p, jnp.zeros((n_seq_p * hc_p, C), _f32),
        jnp.zeros((n_seq_p * hp_p, C), _f32), weights, PROMPT_ROW_TILE, PROMPT_SUB_TILES, 0)
    hc_s, hp_s = _hist_steps(CONV_HIST, n_seq_s), _hist_steps(POOL_HIST, n_seq_s)
    hist_c = jnp.pad(_time_major(state_conv[l]), ((hc_s - CONV_HIST, 0), (0, 0), (0, 0)))
    hist_p = jnp.pad(_time_major(state_pool[l]), ((hp_s - POOL_HIST, 0), (0, 0), (0, 0)))
    h_s, v_s, lg_s, nc_s, np_s = _mixer(
        _time_major(x_sample).reshape(n_s, D), mod_s, hist_c.reshape(hc_s * n_seq_s, C),
        hist_p.reshape(hp_s * n_seq_s, C), weights, SAMPLE_ROW_TILE, 1, PAST_LEN)

    dest8, gates3, meta = _plan(lg_p, lg_s)
    cnt, row0, padded, nxt = (meta[i, :E] for i in range(4))
    gates = gates3.reshape(n, LANES)

    dests = [dest8[:, k, :].reshape(n) for k in range(TOP_K)]
    xpad = _dispatch(v_p, v_s, dests, n_slots)
    ypad = _experts(cnt, row0, padded, nxt, xpad, w_up[l], b_up[l][:, None, :], w_down[l], b_down[l][:, None, :])

    def gathered(t0, rows):
        idx = jnp.concatenate([dk[t0:t0 + rows] for dk in dests])
        return _gather_rows(ypad, idx).reshape(TOP_K, rows, D // 2)

    chunk = n_p // COMBINE_CHUNKS
    y4_p = [gathered(c * chunk, chunk) for c in range(COMBINE_CHUNKS)]
    y4_s = gathered(n_p, n_s)
    g2, b2 = ln2_g[l][None], ln2_b[l][None]
    y_p = None
    for c in range(COMBINE_CHUNKS):
        y_p = _combine(h_p, mod_p, gates, g2, b2, y4_p[c], 0, seq, c * chunk, y_p)
    y_s = _combine(h_s, mod_s, gates, g2, b2, y4_s, n_p, n_s, 0, None)

    y_prompt = y_p.reshape(n_seq_p, seq, D)
    y_sample = _time_major(y_s.reshape(dec_seq, n_seq_s, D))
    new_conv_p = nc_p.reshape(n_seq_p, hc_p, C)[:, hc_p - CONV_HIST:][None]
    new_pool_p = np_p.reshape(n_seq_p, hp_p, C)[:, hp_p - POOL_HIST:][None]
    new_conv_s = _time_major(nc_s.reshape(hc_s, n_seq_s, C)[hc_s - CONV_HIST:])[None]
    new_pool_s = _time_major(np_s.reshape(hp_s, n_seq_s, C)[hp_s - POOL_HIST:])[None]
    return (y_prompt, y_sample, new_conv_p, new_pool_p, new_conv_s, new_pool_s)
```

```python
import functools

import jax
import jax.numpy as jnp
from jax import lax
from jax.experimental import pallas as pl
from jax.experimental.pallas import tpu as pltpu
from jax.experimental.pallas import tpu_sc as plsc

D = 1024
C = 512
N_GROUPS = 4
GROUP = C // N_GROUPS
CONV_HIST = 2
POOL_HIST = 15
E = 32
TOP_K = 4
F = 1024
SWIGLU_LIMIT = 7.0
SWIGLU_ALPHA = 1.702
LN_EPS = 1e-5
DEPTH = 1
ALPHA = (2 * DEPTH) ** 0.25
PAST_LEN = 16384

LANES = 128
SUBLANES = 8
ROW_TILE = 512
PROMPT_ROW_TILE = 1024
PROMPT_SUB_TILES = 1
SAMPLE_ROW_TILE = 512
SLOT_PAD = 256
BIG_BLOCK = 1024
COMBINE_TILE = 512
COMBINE_CHUNKS = 2
SC_WINDOW = 128
VMEM_LIMIT = 56 * 1024 * 1024

_f32 = jnp.float32
_bf16 = jnp.bfloat16


def _dot(a, b):
    return jnp.dot(a, b, preferred_element_type=_f32)


def _dot_exact(a, b):
    return lax.dot_general(a, b, (((1,), (0,)), ((), ())),
                           precision=lax.Precision.HIGHEST, preferred_element_type=_f32)


def _dot_split(a, b):
    a_hi, b_hi = a.astype(_bf16), b.astype(_bf16)
    a_lo = (a - a_hi.astype(_f32)).astype(_bf16)
    b_lo = (b - b_hi.astype(_f32)).astype(_bf16)
    return _dot(a_hi, b_hi) + _dot(a_lo, b_hi) + _dot(a_hi, b_lo)


def _pack_rows(x):
    w = x.shape[1] // 2
    hi = lax.bitcast_convert_type(x[:, :w].astype(_bf16).astype(_f32), jnp.int32)
    lo = lax.bitcast_convert_type(x[:, w:].astype(_bf16).astype(_f32), jnp.int32)
    return hi | lax.shift_right_logical(lo, 16)


def _unpack_rows(p):
    hi = lax.bitcast_convert_type(p & jnp.int32(-65536), _f32)
    lo = lax.bitcast_convert_type(lax.shift_left(p, 16), _f32)
    return jnp.concatenate([hi, lo], axis=1).astype(_bf16)


def _sigmoid(x):
    return 0.5 * jnp.tanh(0.5 * x) + 0.5


def _per_seq(x, m, op):
    g = m.shape[0]
    if g == 1:
        return op(x, m)
    r, n = x.shape
    return op(x.reshape(r // g, g, n), m[None]).reshape(r, n)


def _layer_norm(x, g, b):
    mu = jnp.mean(x, axis=-1, keepdims=True)
    xc = x - mu
    var = jnp.mean(xc * xc, axis=-1, keepdims=True)
    return xc * lax.rsqrt(var + LN_EPS) * g + b


def _hist_steps(needed, g):
    return -(-needed * g // SUBLANES) * SUBLANES // g


def _ada_kernel(c_ref, w_ref, b_ref, o_ref):
    c = c_ref[...]
    o_ref[...] = _dot((c * _sigmoid(c)).astype(_bf16), w_ref[...].astype(_bf16)) + b_ref[...]


def _ada(c, w_ada, b_ada):
    rows = c.shape[0]
    cols = w_ada.shape[1]
    bn = 1536
    return pl.pallas_call(
        _ada_kernel,
        out_shape=jax.ShapeDtypeStruct((rows, cols), _f32),
        grid=(cols // bn,),
        in_specs=[pl.BlockSpec((rows, D), lambda j: (0, 0)),
                  pl.BlockSpec((D, bn), lambda j: (0, j)),
                  pl.BlockSpec((1, bn), lambda j: (0, j))],
        out_specs=pl.BlockSpec((rows, bn), lambda j: (0, j)),
        compiler_params=pltpu.CompilerParams(vmem_limit_bytes=VMEM_LIMIT),
        name="ada",
    )(c, w_ada, b_ada)


def _mixer_kernel(g, tiles_per_seq, start_pos, n_sub,
                  x_ref, mod_ref, hc_ref, hp_ref,
                  win_ref, cw_ref, woa_ref, wpool_ref, ls_ref, wob_ref, wo_ref, g1_ref, b1_ref,
                  wr_ref, br_ref,
                  h_ref, v_ref, lg_ref, nc_ref, np_ref, zbuf, pbuf):
    r = x_ref.shape[0]
    hrc = hc_ref.shape[0]
    hrp = hp_ref.shape[0]
    j = pl.program_id(0) % tiles_per_seq

    @pl.when(j == 0)
    def _():
        zbuf[pl.ds(0, hrc), :] = hc_ref[...]
        pbuf[pl.ds(0, hrp), :] = hp_ref[...]

    @pl.when(j != 0)
    def _():
        zt = zbuf[pl.ds(r, hrc), :]
        pt = pbuf[pl.ds(r, hrp), :]
        zbuf[pl.ds(0, hrc), :] = zt
        pbuf[pl.ds(0, hrp), :] = pt

    m = mod_ref[...]
    shift1, scale1, gate1 = m[:, 0:D], m[:, D:2 * D], m[:, 2 * D:3 * D]
    shift2, scale2 = m[:, 3 * D:4 * D], m[:, 4 * D:5 * D]

    rs = r // n_sub
    for s in range(n_sub):
        rows = pl.ds(s * rs, rs)
        x = x_ref[rows, :]
        u = _per_seq(_per_seq(x, 1.0 + scale1, jnp.multiply), shift1, jnp.add).astype(_bf16)

        z = _dot(u, win_ref[:, C:2 * C]) * _dot(u, win_ref[:, 2 * C:3 * C])
        zrow = hrc + s * rs
        zbuf[pl.ds(zrow, rs), :] = z
        cw = cw_ref[...]
        conv = (cw[0:1] * zbuf[pl.ds(zrow - 2 * g, rs), :] + cw[1:2] * zbuf[pl.ds(zrow - g, rs), :]
                + cw[2:3] * z)
        y_a = _dot((_dot(u, win_ref[:, 0:C]) * conv).astype(_bf16), woa_ref[...])

        xp = _dot(u, win_ref[:, 3 * C:4 * C])
        prow = hrp + s * rs
        pbuf[pl.ds(prow, rs), :] = xp
        pos = (start_pos + j * (r // g) + s * (rs // g)
               + lax.broadcasted_iota(jnp.int32, (rs, 1), 0) // g)
        acc = xp
        yg = []
        for grp in range(N_GROUPS):
            lo = grp * GROUP
            wdw = 2 ** (grp + 1)
            for back in range(wdw // 2, wdw):
                sh = pbuf[pl.ds(prow - back * g, rs), lo:C]
                acc = jnp.concatenate([acc[:, 0:lo], acc[:, lo:C] + sh], axis=1) if lo else acc + sh
            inv_cnt = 1.0 / jnp.minimum(wdw, pos + 1).astype(_f32)
            diff = acc[:, lo:lo + GROUP] * inv_cnt - xp[:, lo:lo + GROUP]
            yg.append(_dot(diff.astype(_bf16), wpool_ref[grp]))
        y_b = _dot((jnp.concatenate(yg, axis=1) * ls_ref[...]).astype(_bf16), wob_ref[...])

        g_a = _dot(u, win_ref[:, 4 * C:4 * C + D])
        g_b = _dot(u, win_ref[:, 4 * C + D:4 * C + 2 * D])
        merged = _sigmoid(g_a) * y_a + _sigmoid(g_b) * y_b
        o = _dot(merged.astype(_bf16), wo_ref[...])
        h = _layer_norm(ALPHA * x + _per_seq(o, gate1, jnp.multiply), g1_ref[...], b1_ref[...])
        v = _per_seq(_per_seq(h, 1.0 + scale2, jnp.multiply), shift2, jnp.add)
        h_ref[rows, :] = h
        v_ref[rows, :] = _pack_rows(v)
        lg_ref[rows, :] = _dot(v.astype(_bf16), wr_ref[...].astype(_bf16)) + br_ref[...]
    nc_ref[...] = zbuf[pl.ds(r, hrc), :]
    np_ref[...] = pbuf[pl.ds(r, hrp), :]


def _mixer(x2, mod3, hc, hp, weights, row_tile, n_sub, start_pos):
    n = x2.shape[0]
    n_mod, g, _ = mod3.shape
    hrc, hrp = hc.shape[0] // n_mod, hp.shape[0] // n_mod
    tiles_per_seq = n // n_mod // row_tile
    once = dict(pipeline_mode=pl.Buffered(1)) if n_mod == 1 else {}

    def full(a):
        nd = a.ndim
        return pl.BlockSpec(a.shape, lambda i: (0,) * nd)

    def seq_block(rows, **kw):
        return pl.BlockSpec((rows, C), lambda i: (i // tiles_per_seq, 0), **kw)

    def row_block(cols):
        return pl.BlockSpec((row_tile, cols), lambda i: (i, 0))

    return pl.pallas_call(
        functools.partial(_mixer_kernel, g, tiles_per_seq, start_pos, n_sub),
        out_shape=[
            jax.ShapeDtypeStruct((n, D), _f32),
            jax.ShapeDtypeStruct((n, D // 2), jnp.int32),
            jax.ShapeDtypeStruct((n, LANES), _f32),
            jax.ShapeDtypeStruct(hc.shape, _f32),
            jax.ShapeDtypeStruct(hp.shape, _f32),
        ],
        grid=(n // row_tile,),
        in_specs=[row_block(D),
                  pl.BlockSpec((None, g, 6 * D), lambda i: (i // tiles_per_seq, 0, 0), **once),
                  seq_block(hrc, **once), seq_block(hrp, **once)] + [full(a) for a in weights],
        out_specs=[row_block(D), row_block(D // 2), row_block(LANES), seq_block(hrc), seq_block(hrp)],
        scratch_shapes=[pltpu.VMEM((hrc + row_tile, C), _f32), pltpu.VMEM((hrp + row_tile, C), _f32)],
        compiler_params=pltpu.CompilerParams(vmem_limit_bytes=VMEM_LIMIT),
        name="mixer",
    )(x2, mod3, hc, hp, *weights)


def _plan_kernel(lgp_ref, lgs_ref, dest_ref, gate_ref, meta_ref, idx_s, rank_s):
    t = ROW_TILE
    e_iota = lax.broadcasted_iota(jnp.int32, (E, t), 0)
    tri = (lax.broadcasted_iota(jnp.int32, (t, t), 0)
           < lax.broadcasted_iota(jnp.int32, (t, t), 1)).astype(_f32).astype(_bf16)
    zeros_rest = jnp.zeros((LANES - TOP_K, t), _f32)

    def tile_body(lg_ref, off, i, carry):
        lt = lg_ref[i].T[0:E, :]
        vals, idxs = [], []
        for _ in range(TOP_K):
            mx = jnp.max(lt, axis=0, keepdims=True)
            ix = jnp.min(jnp.where(lt == mx, e_iota, E), axis=0, keepdims=True)
            vals.append(mx)
            idxs.append(ix)
            lt = jnp.where(e_iota == ix, -jnp.inf, lt)
        ex = [jnp.exp(vk - vals[0]) for vk in vals]
        den = ex[0] + ex[1] + ex[2] + ex[3]
        gates = [ek / den for ek in ex]
        gate_ref[off + i] = jnp.concatenate(gates + [zeros_rest], axis=0).T

        ohs = [(e_iota == ix) for ix in idxs]
        oh = (ohs[0] | ohs[1] | ohs[2] | ohs[3]).astype(_f32)
        before = _dot(oh.astype(_bf16), tri) + carry
        ranks = [jnp.sum(jnp.where(o, before, 0.0), axis=0, keepdims=True) for o in ohs]
        idx_s[off + i] = jnp.concatenate(idxs + idxs, axis=0)
        rank_s[off + i] = jnp.concatenate(ranks + ranks, axis=0).astype(jnp.int32)
        return carry + jnp.sum(oh, axis=1, keepdims=True)

    n_p, n_s = lgp_ref.shape[0], lgs_ref.shape[0]
    counts = lax.fori_loop(0, n_p, functools.partial(tile_body, lgp_ref, 0), jnp.zeros((E, 1), _f32))
    counts = lax.fori_loop(0, n_s, functools.partial(tile_body, lgs_ref, n_p), counts)
    padded = jnp.ceil(counts / SLOT_PAD) * SLOT_PAD
    low = (lax.broadcasted_iota(jnp.int32, (E, E), 1)
           <= lax.broadcasted_iota(jnp.int32, (E, E), 0)).astype(_f32)
    pad_end = _dot_exact(low, jnp.broadcast_to(padded, (E, LANES)))[:, 0:1]
    pad_start = pad_end - padded

    def dest_body(i, c):
        ix = idx_s[i]
        rk = rank_s[i]
        rows = []
        for k in range(TOP_K):
            st = jnp.sum(jnp.where(e_iota == ix[k:k + 1], pad_start, 0.0), axis=0, keepdims=True)
            rows.append(st.astype(jnp.int32) + rk[k:k + 1])
        dest_ref[i] = jnp.concatenate(rows + rows, axis=0)
        return c

    lax.fori_loop(0, n_p + n_s, dest_body, 0)

    sub = lax.broadcasted_iota(jnp.int32, (E, LANES), 0)
    lane = lax.broadcasted_iota(jnp.int32, (E, LANES), 1)

    def to_lanes(col):
        return jnp.sum(jnp.where(sub == lane, col, 0.0), axis=0, keepdims=True).astype(jnp.int32)

    later = jnp.min(jnp.where((sub > lane) & (counts > 0.0), sub, E), axis=0, keepdims=True)
    meta_ref[...] = jnp.concatenate(
        [to_lanes(counts), to_lanes(pad_start), to_lanes(padded), later,
         jnp.zeros((SUBLANES - 4, LANES), jnp.int32)], axis=0)


def _plan(logits_p, logits_s):
    n_tiles = (logits_p.shape[0] + logits_s.shape[0]) // ROW_TILE
    return pl.pallas_call(
        _plan_kernel,
        out_shape=[
            jax.ShapeDtypeStruct((n_tiles, 2 * TOP_K, ROW_TILE), jnp.int32),
            jax.ShapeDtypeStruct((n_tiles, ROW_TILE, LANES), _f32),
            jax.ShapeDtypeStruct((SUBLANES, LANES), jnp.int32),
        ],
        scratch_shapes=[pltpu.VMEM((n_tiles, 2 * TOP_K, ROW_TILE), jnp.int32),
                        pltpu.VMEM((n_tiles, 2 * TOP_K, ROW_TILE), jnp.int32)],
        compiler_params=pltpu.CompilerParams(vmem_limit_bytes=VMEM_LIMIT),
        name="plan",
    )(logits_p.reshape(-1, ROW_TILE, LANES), logits_s.reshape(-1, ROW_TILE, LANES))


def _dispatch(v_p, v_s, dests, n_rows_out):
    n_p, n_s = v_p.shape[0], v_s.shape[0]
    width = v_p.shape[1]
    w = SC_WINDOW
    n_pw, n_windows = n_p // w, (n_p + n_s) // w
    mesh = plsc.VectorSubcoreMesh(core_axis_name="core", subcore_axis_name="subcore")
    n_workers = mesh.num_cores * mesh.num_subcores

    @functools.partial(
        pl.kernel, mesh=mesh, name="dispatch",
        out_type=jax.ShapeDtypeStruct((n_rows_out, width), jnp.int32),
        scratch_types=[pltpu.VMEM((w, width), jnp.int32)] + [pltpu.VMEM((w,), jnp.int32)] * TOP_K
        + [pltpu.SemaphoreType.DMA])
    def scatter_rows(vp_hbm, vs_hbm, d0_hbm, d1_hbm, d2_hbm, d3_hbm, o_hbm, rows, i0, i1, i2, i3, sem):
        worker = lax.axis_index("subcore") * mesh.num_cores + lax.axis_index("core")
        idx = (i0, i1, i2, i3)

        def scatter_window(c):
            t0 = pl.multiple_of(c * w, w)
            for d_hbm, iv in zip((d0_hbm, d1_hbm, d2_hbm, d3_hbm), idx):
                pltpu.sync_copy(d_hbm.at[pl.ds(t0, w)], iv)
            copies = [pltpu.async_copy(rows, o_hbm.at[iv], sem) for iv in idx]
            for cp in copies:
                cp.wait()

        for j in range(-(-n_windows // n_workers)):
            c = j * n_workers + worker

            @pl.when(c < n_pw)
            def _():
                pltpu.sync_copy(vp_hbm.at[pl.ds(pl.multiple_of(c * w, w), w)], rows)
                scatter_window(c)

            @pl.when((c >= n_pw) & (c < n_windows))
            def _():
                pltpu.sync_copy(vs_hbm.at[pl.ds(pl.multiple_of((c - n_pw) * w, w), w)], rows)
                scatter_window(c)

    return scatter_rows(v_p, v_s, *dests)


def _experts_kernel(cnt_ref, row0_ref, pad_ref, nxt_ref, x_hbm, wu_hbm, bu_ref, wd_hbm, bd_ref, y_hbm,
                    xbuf, ybuf, wu_st, wd_st, wu_bf, wd_bf, ysz, xsem, ysem, wsem):
    def w_fetch(e):
        return (pltpu.make_async_copy(wu_hbm.at[e], wu_st, wsem.at[0]),
                pltpu.make_async_copy(wd_hbm.at[e], wd_st, wsem.at[1]))

    def x_fetch(row, size, slot):
        rows = pl.ds(pl.multiple_of(row, SLOT_PAD), size)
        return pltpu.make_async_copy(x_hbm.at[rows, :], xbuf.at[slot, pl.ds(0, size), :], xsem.at[slot])

    def y_store(row, size, slot):
        rows = pl.ds(pl.multiple_of(row, SLOT_PAD), size)
        return pltpu.make_async_copy(ybuf.at[slot, pl.ds(0, size), :], y_hbm.at[rows, :], ysem.at[slot])

    def y_wait(slot):
        for size in (BIG_BLOCK, SLOT_PAD):
            @pl.when(ysz[slot] == size)
            def _():
                y_store(0, size, slot).wait()

    def fetch_first(e, slot):
        @pl.when(pad_ref[e] >= BIG_BLOCK)
        def _():
            x_fetch(row0_ref[e], BIG_BLOCK, slot).start()

        @pl.when((pad_ref[e] > 0) & (pad_ref[e] < BIG_BLOCK))
        def _():
            x_fetch(row0_ref[e], SLOT_PAD, slot).start()

    ysz[0] = 0
    ysz[1] = 0
    for c in w_fetch(0):
        c.start()
    fetch_first(jnp.where(pad_ref[0] > 0, 0, nxt_ref[0]), 0)

    def expert_body(e, n_done):
        for c in w_fetch(e):
            c.wait()
        padded = pad_ref[e]

        @pl.when(padded > 0)
        def _():
            wu_bf[...] = wu_st[...].astype(_bf16)
            wd_bf[...] = wd_st[...].astype(_bf16)

        @pl.when(e + 1 < E)
        def _():
            for c in w_fetch(e + 1):
                c.start()

        b_up = bu_ref[e]
        b_down = bd_ref[e]
        row0 = row0_ref[e]
        n_big = padded // BIG_BLOCK
        n_small = (padded - n_big * BIG_BLOCK) // SLOT_PAD
        small0 = row0 + n_big * BIG_BLOCK
        nxt = nxt_ref[e]

        def pass_body(size, j, n_done):
            slot = n_done % 2
            big = size == BIG_BLOCK
            row = row0 + j * BIG_BLOCK if big else small0 + j * SLOT_PAD
            x_fetch(row, size, slot).wait()

            more = j + 1 < (n_big if big else n_small)
            tail = (n_small > 0) if big else False

            @pl.when(more)
            def _():
                x_fetch(row + size, size, 1 - slot).start()

            if big:
                @pl.when(jnp.logical_not(more) & tail)
                def _():
                    x_fetch(small0, SLOT_PAD, 1 - slot).start()

            @pl.when(jnp.logical_not(more) & jnp.logical_not(tail) & (nxt < E))
            def _():
                fetch_first(jnp.minimum(nxt, E - 1), 1 - slot)

            rows = lax.broadcasted_iota(jnp.int32, (size, 1), 0)
            x = _unpack_rows(jnp.where(rows < cnt_ref[e] - (row - row0), xbuf[slot, pl.ds(0, size), :], 0))
            hcat = _dot(x, wu_bf[...]) + b_up
            glu = jnp.minimum(hcat[:, 0:F], SWIGLU_LIMIT)
            lin = jnp.clip(hcat[:, F:2 * F], -SWIGLU_LIMIT, SWIGLU_LIMIT)
            act = glu * _sigmoid(SWIGLU_ALPHA * glu) * (lin + 1.0)
            y = _pack_rows(_dot(act.astype(_bf16), wd_bf[...]) + b_down)

            y_wait(slot)
            ybuf[slot, pl.ds(0, size), :] = y
            y_store(row, size, slot).start()
            ysz[slot] = size
            return n_done + 1

        n_done = lax.fori_loop(0, n_big, functools.partial(pass_body, BIG_BLOCK), n_done)
        return lax.fori_loop(0, n_small, functools.partial(pass_body, SLOT_PAD), n_done)

    lax.fori_loop(0, E, expert_body, 0)
    y_wait(0)
    y_wait(1)


def _experts(cnt, row0, padded, nxt, xpad, w_up, b_up, w_down, b_down):
    def full(a):
        nd = a.ndim
        return pl.BlockSpec(a.shape, lambda i, *_: (0,) * nd)

    return pl.pallas_call(
        _experts_kernel,
        out_shape=jax.ShapeDtypeStruct(xpad.shape, jnp.int32),
        grid_spec=pltpu.PrefetchScalarGridSpec(
            num_scalar_prefetch=4,
            grid=(1,),
            in_specs=[pl.BlockSpec(memory_space=pl.ANY),
                      pl.BlockSpec(memory_space=pl.ANY), full(b_up),
                      pl.BlockSpec(memory_space=pl.ANY), full(b_down)],
            out_specs=pl.BlockSpec(memory_space=pl.ANY),
            scratch_shapes=[pltpu.VMEM((2, BIG_BLOCK, D // 2), jnp.int32),
                            pltpu.VMEM((2, BIG_BLOCK, D // 2), jnp.int32),
                            pltpu.VMEM((D, 2 * F), _f32), pltpu.VMEM((F, D), _f32),
                            pltpu.VMEM((D, 2 * F), _bf16), pltpu.VMEM((F, D), _bf16),
                            pltpu.SMEM((2,), jnp.int32),
                            pltpu.SemaphoreType.DMA((2,)), pltpu.SemaphoreType.DMA((2,)),
                            pltpu.SemaphoreType.DMA((2,))],
        ),
        compiler_params=pltpu.CompilerParams(vmem_limit_bytes=VMEM_LIMIT),
        name="experts",
    )(cnt, row0, padded, nxt, xpad, w_up, b_up, w_down, b_down)


def _gather_rows(ypad, dest_all):
    n_out = dest_all.shape[0]
    width = ypad.shape[1]
    w = SC_WINDOW
    mesh = plsc.VectorSubcoreMesh(core_axis_name="core", subcore_axis_name="subcore")
    n_workers = mesh.num_cores * mesh.num_subcores
    n_windows = n_out // w

    @functools.partial(
        pl.kernel, mesh=mesh, name="gather_rows",
        out_type=jax.ShapeDtypeStruct((n_out, width), jnp.int32),
        scratch_types=[pltpu.VMEM((w, width), jnp.int32), pltpu.VMEM((w,), jnp.int32), pltpu.SemaphoreType.DMA])
    def gather_rows(y_hbm, d_hbm, o_hbm, rows, iv, sem):
        worker = lax.axis_index("subcore") * mesh.num_cores + lax.axis_index("core")

        @pl.loop(0, n_windows // n_workers)
        def _(j):
            r0 = pl.multiple_of((j * n_workers + worker) * w, w)
            pltpu.sync_copy(d_hbm.at[pl.ds(r0, w)], iv)
            pltpu.async_copy(y_hbm.at[iv], rows, sem).wait()
            pltpu.sync_copy(rows, o_hbm.at[pl.ds(r0, w)])

    assert n_windows % n_workers == 0
    return gather_rows(ypad, dest_all)


def _combine_kernel(h_ref, mod_ref, gate_ref, g2_ref, b2_ref, y_ref, *aliased_and_out):
    o_ref = aliased_and_out[-1]
    gates = gate_ref[...]
    half = D // 2
    f_hi = jnp.zeros((h_ref.shape[0], half), _f32)
    f_lo = jnp.zeros((h_ref.shape[0], half), _f32)
    for k in range(TOP_K):
        p = y_ref[k]
        gk = gates[:, k:k + 1]
        f_hi = f_hi + gk * lax.bitcast_convert_type(p & jnp.int32(-65536), _f32)
        f_lo = f_lo + gk * lax.bitcast_convert_type(lax.shift_left(p, 16), _f32)
    f = jnp.concatenate([f_hi, f_lo], axis=1)
    gate2 = mod_ref[...][:, 5 * D:6 * D]
    pre = ALPHA * h_ref[...] + _per_seq(f, gate2, jnp.multiply)
    o_ref[...] = _layer_norm(pre, g2_ref[...], b2_ref[...])


def _combine(h, mod3, gates, ln2_g, ln2_b, y4, token0, rows_per_mod, row0, out_so_far):
    t = COMBINE_TILE
    hoff = row0 // t
    goff = (token0 + row0) // t
    g = mod3.shape[1]
    tiles_per_mod = rows_per_mod // t
    in_specs = [pl.BlockSpec((t, D), lambda i: (i + hoff, 0)),
                pl.BlockSpec((None, g, 6 * D), lambda i: ((i + hoff) // tiles_per_mod, 0, 0)),
                pl.BlockSpec((t, LANES), lambda i: (i + goff, 0)),
                pl.BlockSpec((1, D), lambda i: (0, 0)),
                pl.BlockSpec((1, D), lambda i: (0, 0)),
                pl.BlockSpec((TOP_K, t, D // 2), lambda i: (0, i, 0))]
    args = [h, mod3, gates, ln2_g, ln2_b, y4]
    aliases = {}
    if out_so_far is not None:
        in_specs.append(pl.BlockSpec(memory_space=pl.ANY))
        args.append(out_so_far)
        aliases = {len(args) - 1: 0}
    return pl.pallas_call(
        _combine_kernel,
        out_shape=jax.ShapeDtypeStruct(h.shape, _f32),
        grid=(y4.shape[1] // t,),
        in_specs=in_specs,
        out_specs=pl.BlockSpec((t, D), lambda i: (i + hoff, 0)),
        input_output_aliases=aliases,
        compiler_params=pltpu.CompilerParams(vmem_limit_bytes=VMEM_LIMIT),
        name="combine",
    )(*args)


def _time_major(a):
    return a.transpose(1, 0, 2)


def kernel(x_prompt, x_sample, c_prompt, c_sample, state_conv, state_pool, w_ada, b_ada, w_in,
           conv_w, w_out_a, w_pool, ls_pool, w_out_b, w_o, ln1_g, ln1_b, w_router, b_router,
           w_up, b_up, w_down, b_down, ln2_g, ln2_b):
    n_seq_p, seq, _ = x_prompt.shape
    n_seq_s, dec_seq, _ = x_sample.shape
    n_p, n_s = n_seq_p * seq, n_seq_s * dec_seq
    n = n_p + n_s
    n_slots = TOP_K * n + E * SLOT_PAD
    l = 0

    mod = _ada(jnp.concatenate([c_prompt, c_sample], axis=0), w_ada[l], b_ada[l][None])
    mod_p = mod[:n_seq_p][:, None, :]
    mod_s = mod[n_seq_p:][None]

    weights = (
        w_in[l].astype(_bf16), conv_w[l], w_out_a[l].astype(_bf16), w_pool[l].astype(_bf16),
        ls_pool[l][None], w_out_b[l].astype(_bf16), w_o[l].astype(_bf16), ln1_g[l][None], ln1_b[l][None],
        jnp.pad(w_router[l], ((0, 0), (0, LANES - E))), jnp.pad(b_router[l], (0, LANES - E))[None],
    )
    hc_p, hp_p = _hist_steps(CONV_HIST, 1), _hist_steps(POOL_HIST, 1)
    h_p, v_p, lg_p, nc_p, np_p = _mixer(
        x_prompt.reshape(n_p, D), mod_p, jnp.zeros((n_seq_p * hc_p, C), _f32),
        jnp.zeros((n_seq_p * hp_p, C), _f32), weights, PROMPT_ROW_TILE, PROMPT_SUB_TILES, 0)
    hc_s, hp_s = _hist_steps(CONV_HIST, n_seq_s), _hist_steps(POOL_HIST, n_seq_s)
    hist_c = jnp.pad(_time_major(state_conv[l]), ((hc_s - CONV_HIST, 0), (0, 0), (0, 0)))
    hist_p = jnp.pad(_time_major(state_pool[l]), ((hp_s - POOL_HIST, 0), (0, 0), (0, 0)))
    h_s, v_s, lg_s, nc_s, np_s = _mixer(
        _time_major(x_sample).reshape(n_s, D), mod_s, hist_c.reshape(hc_s * n_seq_s, C),
        hist_p.reshape(hp_s * n_seq_s, C), weights, SAMPLE_ROW_TILE, 1, PAST_LEN)

    dest8, gates3, meta = _plan(lg_p, lg_s)
    cnt, row0, padded, nxt = (meta[i, :E] for i in range(4))
    gates = gates3.reshape(n, LANES)

    dests = [dest8[:, k, :].reshape(n) for k in range(TOP_K)]
    xpad = _dispatch(v_p, v_s, dests, n_slots)
    ypad = _experts(cnt, row0, padded, nxt, xpad, w_up[l], b_up[l][:, None, :], w_down[l], b_down[l][:, None, :])

    def gathered(t0, rows):
        idx = jnp.concatenate([dk[t0:t0 + rows] for dk in dests])
        return _gather_rows(ypad, idx).reshape(TOP_K, rows, D // 2)

    chunk = n_p // COMBINE_CHUNKS
    y4_p = [gathered(c * chunk, chunk) for c in range(COMBINE_CHUNKS)]
    y4_s = gathered(n_p, n_s)
    g2, b2 = ln2_g[l][None], ln2_b[l][None]
    y_p = None
    for c in range(COMBINE_CHUNKS):
        y_p = _combine(h_p, mod_p, gates, g2, b2, y4_p[c], 0, seq, c * chunk, y_p)
    y_s = _combine(h_s, mod_s, gates, g2, b2, y4_s, n_p, n_s, 0, None)

    y_prompt = y_p.reshape(n_seq_p, seq, D)
    y_sample = _time_major(y_s.reshape(dec_seq, n_seq_s, D))
    new_conv_p = nc_p.reshape(n_seq_p, hc_p, C)[:, hc_p - CONV_HIST:][None]
    new_pool_p = np_p.reshape(n_seq_p, hp_p, C)[:, hp_p - POOL_HIST:][None]
    new_conv_s = _time_major(nc_s.reshape(hc_s, n_seq_s, C)[hc_s - CONV_HIST:])[None]
    new_pool_s = _time_major(np_s.reshape(hp_s, n_seq_s, C)[hp_s - POOL_HIST:])[None]
    return (y_prompt, y_sample, new_conv_p, new_pool_p, new_conv_s, new_pool_s)
```

```python
import functools

import jax
import jax.numpy as jnp
from jax import lax
from jax.experimental import pallas as pl
from jax.experimental.pallas import tpu as pltpu
from jax.experimental.pallas import tpu_sc as plsc

D = 1024
C = 512
N_GROUPS = 4
GROUP = C // N_GROUPS
CONV_HIST = 2
POOL_HIST = 15
E = 32
TOP_K = 4
F = 1024
SWIGLU_LIMIT = 7.0
SWIGLU_ALPHA = 1.702
LN_EPS = 1e-5
DEPTH = 1
ALPHA = (2 * DEPTH) ** 0.25
PAST_LEN = 16384

LANES = 128
SUBLANES = 8
ROW_TILE = 512
PROMPT_ROW_TILE = 1024
PROMPT_SUB_TILES = 1
SAMPLE_ROW_TILE = 512
SLOT_PAD = 256
BIG_BLOCK = 1024
COMBINE_TILE = 512
COMBINE_CHUNKS = 4
SC_WINDOW = 128
DISPATCH_WINDOW = 64
VMEM_LIMIT = 56 * 1024 * 1024

_f32 = jnp.float32
_bf16 = jnp.bfloat16


def _dot(a, b):
    return jnp.dot(a, b, preferred_element_type=_f32)


def _dot_exact(a, b):
    return lax.dot_general(a, b, (((1,), (0,)), ((), ())),
                           precision=lax.Precision.HIGHEST, preferred_element_type=_f32)


def _dot_split(a, b):
    a_hi, b_hi = a.astype(_bf16), b.astype(_bf16)
    a_lo = (a - a_hi.astype(_f32)).astype(_bf16)
    b_lo = (b - b_hi.astype(_f32)).astype(_bf16)
    return _dot(a_hi, b_hi) + _dot(a_lo, b_hi) + _dot(a_hi, b_lo)


def _pack_rows(x):
    w = x.shape[1] // 2
    hi = lax.bitcast_convert_type(x[:, :w].astype(_bf16).astype(_f32), jnp.int32)
    lo = lax.bitcast_convert_type(x[:, w:].astype(_bf16).astype(_f32), jnp.int32)
    return hi | lax.shift_right_logical(lo, 16)


def _unpack_rows(p):
    hi = lax.bitcast_convert_type(p & jnp.int32(-65536), _f32)
    lo = lax.bitcast_convert_type(lax.shift_left(p, 16), _f32)
    return jnp.concatenate([hi, lo], axis=1).astype(_bf16)


def _sigmoid(x):
    return 0.5 * jnp.tanh(0.5 * x) + 0.5


def _per_seq(x, m, op):
    g = m.shape[0]
    if g == 1:
        return op(x, m)
    r, n = x.shape
    return op(x.reshape(r // g, g, n), m[None]).reshape(r, n)


def _layer_norm(x, g, b):
    mu = jnp.mean(x, axis=-1, keepdims=True)
    xc = x - mu
    var = jnp.mean(xc * xc, axis=-1, keepdims=True)
    return xc * lax.rsqrt(var + LN_EPS) * g + b


def _hist_steps(needed, g):
    return -(-needed * g // SUBLANES) * SUBLANES // g


def _ada_kernel(c_ref, w_ref, b_ref, o_ref):
    c = c_ref[...]
    o_ref[...] = _dot((c * _sigmoid(c)).astype(_bf16), w_ref[...].astype(_bf16)) + b_ref[...]


def _ada(c, w_ada, b_ada):
    rows = c.shape[0]
    cols = w_ada.shape[1]
    bn = 1536
    return pl.pallas_call(
        _ada_kernel,
        out_shape=jax.ShapeDtypeStruct((rows, cols), _f32),
        grid=(cols // bn,),
        in_specs=[pl.BlockSpec((rows, D), lambda j: (0, 0)),
                  pl.BlockSpec((D, bn), lambda j: (0, j)),
                  pl.BlockSpec((1, bn), lambda j: (0, j))],
        out_specs=pl.BlockSpec((rows, bn), lambda j: (0, j)),
        compiler_params=pltpu.CompilerParams(vmem_limit_bytes=VMEM_LIMIT),
        name="ada",
    )(c, w_ada, b_ada)


def _mixer_kernel(g, tiles_per_seq, start_pos, n_sub,
                  x_ref, mod_ref, hc_ref, hp_ref,
                  win_ref, cw_ref, woa_ref, wpool_ref, ls_ref, wob_ref, wo_ref, g1_ref, b1_ref,
                  wr_ref, br_ref,
                  h_ref, v_ref, lg_ref, nc_ref, np_ref, zbuf, pbuf):
    r = x_ref.shape[0]
    hrc = hc_ref.shape[0]
    hrp = hp_ref.shape[0]
    j = pl.program_id(0) % tiles_per_seq

    @pl.when(j == 0)
    def _():
        zbuf[pl.ds(0, hrc), :] = hc_ref[...]
        pbuf[pl.ds(0, hrp), :] = hp_ref[...]

    @pl.when(j != 0)
    def _():
        zt = zbuf[pl.ds(r, hrc), :]
        pt = pbuf[pl.ds(r, hrp), :]
        zbuf[pl.ds(0, hrc), :] = zt
        pbuf[pl.ds(0, hrp), :] = pt

    m = mod_ref[...]
    shift1, scale1, gate1 = m[:, 0:D], m[:, D:2 * D], m[:, 2 * D:3 * D]
    shift2, scale2 = m[:, 3 * D:4 * D], m[:, 4 * D:5 * D]

    rs = r // n_sub
    for s in range(n_sub):
        rows = pl.ds(s * rs, rs)
        x = x_ref[rows, :]
        u = _per_seq(_per_seq(x, 1.0 + scale1, jnp.multiply), shift1, jnp.add).astype(_bf16)

        z = _dot(u, win_ref[:, C:2 * C]) * _dot(u, win_ref[:, 2 * C:3 * C])
        zrow = hrc + s * rs
        zbuf[pl.ds(zrow, rs), :] = z
        cw = cw_ref[...]
        conv = (cw[0:1] * zbuf[pl.ds(zrow - 2 * g, rs), :] + cw[1:2] * zbuf[pl.ds(zrow - g, rs), :]
                + cw[2:3] * z)
        y_a = _dot((_dot(u, win_ref[:, 0:C]) * conv).astype(_bf16), woa_ref[...])

        xp = _dot(u, win_ref[:, 3 * C:4 * C])
        prow = hrp + s * rs
        pbuf[pl.ds(prow, rs), :] = xp
        pos = (start_pos + j * (r // g) + s * (rs // g)
               + lax.broadcasted_iota(jnp.int32, (rs, 1), 0) // g)
        acc = xp
        yg = []
        for grp in range(N_GROUPS):
            lo = grp * GROUP
            wdw = 2 ** (grp + 1)
            for back in range(wdw // 2, wdw):
                sh = pbuf[pl.ds(prow - back * g, rs), lo:C]
                acc = jnp.concatenate([acc[:, 0:lo], acc[:, lo:C] + sh], axis=1) if lo else acc + sh
            inv_cnt = 1.0 / jnp.minimum(wdw, pos + 1).astype(_f32)
            diff = acc[:, lo:lo + GROUP] * inv_cnt - xp[:, lo:lo + GROUP]
            yg.append(_dot(diff.astype(_bf16), wpool_ref[grp]))
        y_b = _dot((jnp.concatenate(yg, axis=1) * ls_ref[...]).astype(_bf16), wob_ref[...])

        g_a = _dot(u, win_ref[:, 4 * C:4 * C + D])
        g_b = _dot(u, win_ref[:, 4 * C + D:4 * C + 2 * D])
        merged = _sigmoid(g_a) * y_a + _sigmoid(g_b) * y_b
        o = _dot(merged.astype(_bf16), wo_ref[...])
        h = _layer_norm(ALPHA * x + _per_seq(o, gate1, jnp.multiply), g1_ref[...], b1_ref[...])
        v = _per_seq(_per_seq(h, 1.0 + scale2, jnp.multiply), shift2, jnp.add)
        h_ref[rows, :] = h
        v_ref[rows, :] = _pack_rows(v)
        lg_ref[rows, :] = _dot(v.astype(_bf16), wr_ref[...].astype(_bf16)) + br_ref[...]
    nc_ref[...] = zbuf[pl.ds(r, hrc), :]
    np_ref[...] = pbuf[pl.ds(r, hrp), :]


def _mixer(x2, mod3, hc, hp, weights, row_tile, n_sub, start_pos):
    n = x2.shape[0]
    n_mod, g, _ = mod3.shape
    hrc, hrp = hc.shape[0] // n_mod, hp.shape[0] // n_mod
    tiles_per_seq = n // n_mod // row_tile
    once = dict(pipeline_mode=pl.Buffered(1)) if n_mod == 1 else {}

    def full(a):
        nd = a.ndim
        return pl.BlockSpec(a.shape, lambda i: (0,) * nd)

    def seq_block(rows, **kw):
        return pl.BlockSpec((rows, C), lambda i: (i // tiles_per_seq, 0), **kw)

    def row_block(cols):
        return pl.BlockSpec((row_tile, cols), lambda i: (i, 0))

    return pl.pallas_call(
        functools.partial(_mixer_kernel, g, tiles_per_seq, start_pos, n_sub),
        out_shape=[
            jax.ShapeDtypeStruct((n, D), _f32),
            jax.ShapeDtypeStruct((n, D // 2), jnp.int32),
            jax.ShapeDtypeStruct((n, LANES), _f32),
            jax.ShapeDtypeStruct(hc.shape, _f32),
            jax.ShapeDtypeStruct(hp.shape, _f32),
        ],
        grid=(n // row_tile,),
        in_specs=[row_block(D),
                  pl.BlockSpec((None, g, 6 * D), lambda i: (i // tiles_per_seq, 0, 0), **once),
                  seq_block(hrc, **once), seq_block(hrp, **once)] + [full(a) for a in weights],
        out_specs=[row_block(D), row_block(D // 2), row_block(LANES), seq_block(hrc), seq_block(hrp)],
        scratch_shapes=[pltpu.VMEM((hrc + row_tile, C), _f32), pltpu.VMEM((hrp + row_tile, C), _f32)],
        compiler_params=pltpu.CompilerParams(vmem_limit_bytes=VMEM_LIMIT),
        name="mixer",
    )(x2, mod3, hc, hp, *weights)


def _plan_kernel(lgp_ref, lgs_ref, dest_ref, gate_ref, meta_ref, idx_s, rank_s):
    t = ROW_TILE
    e_iota = lax.broadcasted_iota(jnp.int32, (E, t), 0)
    tri = (lax.broadcasted_iota(jnp.int32, (t, t), 0)
           < lax.broadcasted_iota(jnp.int32, (t, t), 1)).astype(_f32).astype(_bf16)
    zeros_rest = jnp.zeros((LANES - TOP_K, t), _f32)

    def tile_body(lg_ref, off, i, carry):
        lt = lg_ref[i].T[0:E, :]
        vals, idxs = [], []
        for _ in range(TOP_K):
            mx = jnp.max(lt, axis=0, keepdims=True)
            ix = jnp.min(jnp.where(lt == mx, e_iota, E), axis=0, keepdims=True)
            vals.append(mx)
            idxs.append(ix)
            lt = jnp.where(e_iota == ix, -jnp.inf, lt)
        ex = [jnp.exp(vk - vals[0]) for vk in vals]
        den = ex[0] + ex[1] + ex[2] + ex[3]
        gates = [ek / den for ek in ex]
        gate_ref[off + i] = jnp.concatenate(gates + [zeros_rest], axis=0).T

        ohs = [(e_iota == ix) for ix in idxs]
        oh = (ohs[0] | ohs[1] | ohs[2] | ohs[3]).astype(_f32)
        before = _dot(oh.astype(_bf16), tri) + carry
        ranks = [jnp.sum(jnp.where(o, before, 0.0), axis=0, keepdims=True) for o in ohs]
        idx_s[off + i] = jnp.concatenate(idxs + idxs, axis=0)
        rank_s[off + i] = jnp.concatenate(ranks + ranks, axis=0).astype(jnp.int32)
        return carry + jnp.sum(oh, axis=1, keepdims=True)

    n_p, n_s = lgp_ref.shape[0], lgs_ref.shape[0]
    counts = lax.fori_loop(0, n_p, functools.partial(tile_body, lgp_ref, 0), jnp.zeros((E, 1), _f32))
    counts = lax.fori_loop(0, n_s, functools.partial(tile_body, lgs_ref, n_p), counts)
    padded = jnp.ceil(counts / SLOT_PAD) * SLOT_PAD
    low = (lax.broadcasted_iota(jnp.int32, (E, E), 1)
           <= lax.broadcasted_iota(jnp.int32, (E, E), 0)).astype(_f32)
    pad_end = _dot_exact(low, jnp.broadcast_to(padded, (E, LANES)))[:, 0:1]
    pad_start = pad_end - padded

    def dest_body(i, c):
        ix = idx_s[i]
        rk = rank_s[i]
        rows = []
        for k in range(TOP_K):
            st = jnp.sum(jnp.where(e_iota == ix[k:k + 1], pad_start, 0.0), axis=0, keepdims=True)
            rows.append(st.astype(jnp.int32) + rk[k:k + 1])
        dest_ref[i] = jnp.concatenate(rows + rows, axis=0)
        return c

    lax.fori_loop(0, n_p + n_s, dest_body, 0)

    sub = lax.broadcasted_iota(jnp.int32, (E, LANES), 0)
    lane = lax.broadcasted_iota(jnp.int32, (E, LANES), 1)

    def to_lanes(col):
        return jnp.sum(jnp.where(sub == lane, col, 0.0), axis=0, keepdims=True).astype(jnp.int32)

    later = jnp.min(jnp.where((sub > lane) & (counts > 0.0), sub, E), axis=0, keepdims=True)
    meta_ref[...] = jnp.concatenate(
        [to_lanes(counts), to_lanes(pad_start), to_lanes(padded), later,
         jnp.zeros((SUBLANES - 4, LANES), jnp.int32)], axis=0)


def _plan(logits_p, logits_s):
    n_tiles = (logits_p.shape[0] + logits_s.shape[0]) // ROW_TILE
    return pl.pallas_call(
        _plan_kernel,
        out_shape=[
            jax.ShapeDtypeStruct((n_tiles, 2 * TOP_K, ROW_TILE), jnp.int32),
            jax.ShapeDtypeStruct((n_tiles, ROW_TILE, LANES), _f32),
            jax.ShapeDtypeStruct((SUBLANES, LANES), jnp.int32),
        ],
        scratch_shapes=[pltpu.VMEM((n_tiles, 2 * TOP_K, ROW_TILE), jnp.int32),
                        pltpu.VMEM((n_tiles, 2 * TOP_K, ROW_TILE), jnp.int32)],
        compiler_params=pltpu.CompilerParams(vmem_limit_bytes=VMEM_LIMIT),
        name="plan",
    )(logits_p.reshape(-1, ROW_TILE, LANES), logits_s.reshape(-1, ROW_TILE, LANES))


def _dispatch(v_p, v_s, dests, n_rows_out):
    n_p, n_s = v_p.shape[0], v_s.shape[0]
    width = v_p.shape[1]
    w = DISPATCH_WINDOW
    n_pw, n_windows = n_p // w, (n_p + n_s) // w
    mesh = plsc.VectorSubcoreMesh(core_axis_name="core", subcore_axis_name="subcore")
    n_workers = mesh.num_cores * mesh.num_subcores
    n_rounds = -(-n_windows // n_workers)

    @functools.partial(
        pl.kernel, mesh=mesh, name="dispatch",
        out_type=jax.ShapeDtypeStruct((n_rows_out, width), jnp.int32),
        scratch_types=[pltpu.VMEM((w, width), jnp.int32)] * 2 + [pltpu.VMEM((w,), jnp.int32)] * TOP_K
        + [pltpu.SemaphoreType.DMA] * 3)
    def scatter_rows(vp_hbm, vs_hbm, d0_hbm, d1_hbm, d2_hbm, d3_hbm, o_hbm,
                     rows0, rows1, i0, i1, i2, i3, lsem0, lsem1, ssem):
        worker = lax.axis_index("subcore") * mesh.num_cores + lax.axis_index("core")
        rows, lsem = (rows0, rows1), (lsem0, lsem1)
        idx = (i0, i1, i2, i3)

        def load(c, slot, act):
            @pl.when(c < n_pw)
            def _():
                src = vp_hbm.at[pl.ds(pl.multiple_of(c * w, w), w)]
                act(pltpu.make_async_copy(src, rows[slot], lsem[slot]))

            @pl.when((c >= n_pw) & (c < n_windows))
            def _():
                src = vs_hbm.at[pl.ds(pl.multiple_of((c - n_pw) * w, w), w)]
                act(pltpu.make_async_copy(src, rows[slot], lsem[slot]))

        load(worker, 0, lambda cp: cp.start())
        for j in range(n_rounds):
            c = j * n_workers + worker
            slot = j % 2
            load(c, slot, lambda cp: cp.wait())
            if j + 1 < n_rounds:
                load(c + n_workers, 1 - slot, lambda cp: cp.start())

            @pl.when(c < n_windows)
            def _():
                t0 = pl.multiple_of(c * w, w)
                for d_hbm, iv in zip((d0_hbm, d1_hbm, d2_hbm, d3_hbm), idx):
                    pltpu.sync_copy(d_hbm.at[pl.ds(t0, w)], iv)
                copies = [pltpu.async_copy(rows[slot], o_hbm.at[iv], ssem) for iv in idx]
                for cp in copies:
                    cp.wait()

    return scatter_rows(v_p, v_s, *dests)


def _experts_kernel(cnt_ref, row0_ref, pad_ref, nxt_ref, x_hbm, wu_hbm, bu_ref, wd_hbm, bd_ref, y_hbm,
                    xbuf, ybuf, wu_st, wd_st, wu_bf, wd_bf, ysz, xsem, ysem, wsem):
    def w_fetch(e):
        return (pltpu.make_async_copy(wu_hbm.at[e], wu_st, wsem.at[0]),
                pltpu.make_async_copy(wd_hbm.at[e], wd_st, wsem.at[1]))

    def x_fetch(row, size, slot):
        rows = pl.ds(pl.multiple_of(row, SLOT_PAD), size)
        return pltpu.make_async_copy(x_hbm.at[rows, :], xbuf.at[slot, pl.ds(0, size), :], xsem.at[slot])

    def y_store(row, size, slot):
        rows = pl.ds(pl.multiple_of(row, SLOT_PAD), size)
        return pltpu.make_async_copy(ybuf.at[slot, pl.ds(0, size), :], y_hbm.at[rows, :], ysem.at[slot])

    def y_wait(slot):
        for size in (BIG_BLOCK, SLOT_PAD):
            @pl.when(ysz[slot] == size)
            def _():
                y_store(0, size, slot).wait()

    def fetch_first(e, slot):
        @pl.when(pad_ref[e] >= BIG_BLOCK)
        def _():
            x_fetch(row0_ref[e], BIG_BLOCK, slot).start()

        @pl.when((pad_ref[e] > 0) & (pad_ref[e] < BIG_BLOCK))
        def _():
            x_fetch(row0_ref[e], SLOT_PAD, slot).start()

    ysz[0] = 0
    ysz[1] = 0
    for c in w_fetch(0):
        c.start()
    fetch_first(jnp.where(pad_ref[0] > 0, 0, nxt_ref[0]), 0)

    def expert_body(e, n_done):
        for c in w_fetch(e):
            c.wait()
        padded = pad_ref[e]

        @pl.when(padded > 0)
        def _():
            wu_bf[...] = wu_st[...].astype(_bf16)
            wd_bf[...] = wd_st[...].astype(_bf16)

        @pl.when(e + 1 < E)
        def _():
            for c in w_fetch(e + 1):
                c.start()

        b_up = bu_ref[e]
        b_down = bd_ref[e]
        row0 = row0_ref[e]
        n_big = padded // BIG_BLOCK
        n_small = (padded - n_big * BIG_BLOCK) // SLOT_PAD
        small0 = row0 + n_big * BIG_BLOCK
        nxt = nxt_ref[e]

        def pass_body(size, j, n_done):
            slot = n_done % 2
            big = size == BIG_BLOCK
            row = row0 + j * BIG_BLOCK if big else small0 + j * SLOT_PAD
            x_fetch(row, size, slot).wait()

            more = j + 1 < (n_big if big else n_small)
            tail = (n_small > 0) if big else False

            @pl.when(more)
            def _():
                x_fetch(row + size, size, 1 - slot).start()

            if big:
                @pl.when(jnp.logical_not(more) & tail)
                def _():
                    x_fetch(small0, SLOT_PAD, 1 - slot).start()

            @pl.when(jnp.logical_not(more) & jnp.logical_not(tail) & (nxt < E))
            def _():
                fetch_first(jnp.minimum(nxt, E - 1), 1 - slot)

            rows = lax.broadcasted_iota(jnp.int32, (size, 1), 0)
            x = _unpack_rows(jnp.where(rows < cnt_ref[e] - (row - row0), xbuf[slot, pl.ds(0, size), :], 0))
            hcat = _dot(x, wu_bf[...]) + b_up
            glu = jnp.minimum(hcat[:, 0:F], SWIGLU_LIMIT)
            lin = jnp.clip(hcat[:, F:2 * F], -SWIGLU_LIMIT, SWIGLU_LIMIT)
            act = glu * _sigmoid(SWIGLU_ALPHA * glu) * (lin + 1.0)
            y = _pack_rows(_dot(act.astype(_bf16), wd_bf[...]) + b_down)

            y_wait(slot)
            ybuf[slot, pl.ds(0, size), :] = y
            y_store(row, size, slot).start()
            ysz[slot] = size
            return n_done + 1

        n_done = lax.fori_loop(0, n_big, functools.partial(pass_body, BIG_BLOCK), n_done)
        return lax.fori_loop(0, n_small, functools.partial(pass_body, SLOT_PAD), n_done)

    lax.fori_loop(0, E, expert_body, 0)
    y_wait(0)
    y_wait(1)


def _experts(cnt, row0, padded, nxt, xpad, w_up, b_up, w_down, b_down):
    def full(a):
        nd = a.ndim
        return pl.BlockSpec(a.shape, lambda i, *_: (0,) * nd)

    return pl.pallas_call(
        _experts_kernel,
        out_shape=jax.ShapeDtypeStruct(xpad.shape, jnp.int32),
        grid_spec=pltpu.PrefetchScalarGridSpec(
            num_scalar_prefetch=4,
            grid=(1,),
            in_specs=[pl.BlockSpec(memory_space=pl.ANY),
                      pl.BlockSpec(memory_space=pl.ANY), full(b_up),
                      pl.BlockSpec(memory_space=pl.ANY), full(b_down)],
            out_specs=pl.BlockSpec(memory_space=pl.ANY),
            scratch_shapes=[pltpu.VMEM((2, BIG_BLOCK, D // 2), jnp.int32),
                            pltpu.VMEM((2, BIG_BLOCK, D // 2), jnp.int32),
                            pltpu.VMEM((D, 2 * F), _f32), pltpu.VMEM((F, D), _f32),
                            pltpu.VMEM((D, 2 * F), _bf16), pltpu.VMEM((F, D), _bf16),
                            pltpu.SMEM((2,), jnp.int32),
                            pltpu.SemaphoreType.DMA((2,)), pltpu.SemaphoreType.DMA((2,)),
                            pltpu.SemaphoreType.DMA((2,))],
        ),
        compiler_params=pltpu.CompilerParams(vmem_limit_bytes=VMEM_LIMIT),
        name="experts",
    )(cnt, row0, padded, nxt, xpad, w_up, b_up, w_down, b_down)


def _gather_rows(ypad, dest_all):
    n_out = dest_all.shape[0]
    width = ypad.shape[1]
    w = SC_WINDOW
    mesh = plsc.VectorSubcoreMesh(core_axis_name="core", subcore_axis_name="subcore")
    n_workers = mesh.num_cores * mesh.num_subcores
    n_windows = n_out // w

    @functools.partial(
        pl.kernel, mesh=mesh, name="gather_rows",
        out_type=jax.ShapeDtypeStruct((n_out, width), jnp.int32),
        scratch_types=[pltpu.VMEM((w, width), jnp.int32), pltpu.VMEM((w,), jnp.int32), pltpu.SemaphoreType.DMA])
    def gather_rows(y_hbm, d_hbm, o_hbm, rows, iv, sem):
        worker = lax.axis_index("subcore") * mesh.num_cores + lax.axis_index("core")

        @pl.loop(0, n_windows // n_workers)
        def _(j):
            r0 = pl.multiple_of((j * n_workers + worker) * w, w)
            pltpu.sync_copy(d_hbm.at[pl.ds(r0, w)], iv)
            pltpu.async_copy(y_hbm.at[iv], rows, sem).wait()
            pltpu.sync_copy(rows, o_hbm.at[pl.ds(r0, w)])

    assert n_windows % n_workers == 0
    return gather_rows(ypad, dest_all)


def _combine_kernel(h_ref, mod_ref, gate_ref, g2_ref, b2_ref, y_ref, *aliased_and_out):
    o_ref = aliased_and_out[-1]
    gates = gate_ref[...]
    half = D // 2
    f_hi = jnp.zeros((h_ref.shape[0], half), _f32)
    f_lo = jnp.zeros((h_ref.shape[0], half), _f32)
    for k in range(TOP_K):
        p = y_ref[k]
        gk = gates[:, k:k + 1]
        f_hi = f_hi + gk * lax.bitcast_convert_type(p & jnp.int32(-65536), _f32)
        f_lo = f_lo + gk * lax.bitcast_convert_type(lax.shift_left(p, 16), _f32)
    f = jnp.concatenate([f_hi, f_lo], axis=1)
    gate2 = mod_ref[...][:, 5 * D:6 * D]
    pre = ALPHA * h_ref[...] + _per_seq(f, gate2, jnp.multiply)
    o_ref[...] = _layer_norm(pre, g2_ref[...], b2_ref[...])


def _combine(h, mod3, gates, ln2_g, ln2_b, y4, token0, rows_per_mod, row0, out_so_far):
    t = COMBINE_TILE
    hoff = row0 // t
    goff = (token0 + row0) // t
    g = mod3.shape[1]
    tiles_per_mod = rows_per_mod // t
    in_specs = [pl.BlockSpec((t, D), lambda i: (i + hoff, 0)),
                pl.BlockSpec((None, g, 6 * D), lambda i: ((i + hoff) // tiles_per_mod, 0, 0)),
                pl.BlockSpec((t, LANES), lambda i: (i + goff, 0)),
                pl.BlockSpec((1, D), lambda i: (0, 0)),
                pl.BlockSpec((1, D), lambda i: (0, 0)),
                pl.BlockSpec((TOP_K, t, D // 2), lambda i: (0, i, 0))]
    args = [h, mod3, gates, ln2_g, ln2_b, y4]
    aliases = {}
    if out_so_far is not None:
        in_specs.append(pl.BlockSpec(memory_space=pl.ANY))
        args.append(out_so_far)
        aliases = {len(args) - 1: 0}
    return pl.pallas_call(
        _combine_kernel,
        out_shape=jax.ShapeDtypeStruct(h.shape, _f32),
        grid=(y4.shape[1] // t,),
        in_specs=in_specs,
        out_specs=pl.BlockSpec((t, D), lambda i: (i + hoff, 0)),
        input_output_aliases=aliases,
        compiler_params=pltpu.CompilerParams(vmem_limit_bytes=VMEM_LIMIT),
        name="combine",
    )(*args)


def _time_major(a):
    return a.transpose(1, 0, 2)


def kernel(x_prompt, x_sample, c_prompt, c_sample, state_conv, state_pool, w_ada, b_ada, w_in,
           conv_w, w_out_a, w_pool, ls_pool, w_out_b, w_o, ln1_g, ln1_b, w_router, b_router,
           w_up, b_up, w_down, b_down, ln2_g, ln2_b):
    n_seq_p, seq, _ = x_prompt.shape
    n_seq_s, dec_seq, _ = x_sample.shape
    n_p, n_s = n_seq_p * seq, n_seq_s * dec_seq
    n = n_p + n_s
    n_slots = TOP_K * n + E * SLOT_PAD
    l = 0

    mod = _ada(jnp.concatenate([c_prompt, c_sample], axis=0), w_ada[l], b_ada[l][None])
    mod_p = mod[:n_seq_p][:, None, :]
    mod_s = mod[n_seq_p:][None]

    weights = (
        w_in[l].astype(_bf16), conv_w[l], w_out_a[l].astype(_bf16), w_pool[l].astype(_bf16),
        ls_pool[l][None], w_out_b[l].astype(_bf16), w_o[l].astype(_bf16), ln1_g[l][None], ln1_b[l][None],
        jnp.pad(w_router[l], ((0, 0), (0, LANES - E))), jnp.pad(b_router[l], (0, LANES - E))[None],
    )
    hc_p, hp_p = _hist_steps(CONV_HIST, 1), _hist_steps(POOL_HIST, 1)
    h_p, v_p, lg_p, nc_p, np_p = _mixer(
        x_prompt.reshape(n_p, D), mod_p, jnp.zeros((n_seq_p * hc_p, C), _f32),
        jnp.zeros((n_seq_p * hp_p, C), _f32), weights, PROMPT_ROW_TILE, PROMPT_SUB_TILES, 0)
    hc_s, hp_s = _hist_steps(CONV_HIST, n_seq_s), _hist_steps(POOL_HIST, n_seq_s)
    hist_c = jnp.pad(_time_major(state_conv[l]), ((hc_s - CONV_HIST, 0), (0, 0), (0, 0)))
    hist_p = jnp.pad(_time_major(state_pool[l]), ((hp_s - POOL_HIST, 0), (0, 0), (0, 0)))
    h_s, v_s, lg_s, nc_s, np_s = _mixer(
        _time_major(x_sample).reshape(n_s, D), mod_s, hist_c.reshape(hc_s * n_seq_s, C),
        hist_p.reshape(hp_s * n_seq_s, C), weights, SAMPLE_ROW_TILE, 1, PAST_LEN)

    dest8, gates3, meta = _plan(lg_p, lg_s)
    cnt, row0, padded, nxt = (meta[i, :E] for i in range(4))
    gates = gates3.reshape(n, LANES)

    dests = [dest8[:, k, :].reshape(n) for k in range(TOP_K)]
    xpad = _dispatch(v_p, v_s, dests, n_slots)
    ypad = _experts(cnt, row0, padded, nxt, xpad, w_up[l], b_up[l][:, None, :], w_down[l], b_down[l][:, None, :])

    def gathered(t0, rows):
        idx = jnp.concatenate([dk[t0:t0 + rows] for dk in dests])
        return _gather_rows(ypad, idx).reshape(TOP_K, rows, D // 2)

    chunk = n_p // COMBINE_CHUNKS
    y4_p = [gathered(c * chunk, chunk) for c in range(COMBINE_CHUNKS)]
    y4_s = gathered(n_p, n_s)
    g2, b2 = ln2_g[l][None], ln2_b[l][None]
    y_p = None
    for c in range(COMBINE_CHUNKS):
        y_p = _combine(h_p, mod_p, gates, g2, b2, y4_p[c], 0, seq, c * chunk, y_p)
    y_s = _combine(h_s, mod_s, gates, g2, b2, y4_s, n_p, n_s, 0, None)

    y_prompt = y_p.reshape(n_seq_p, seq, D)
    y_sample = _time_major(y_s.reshape(dec_seq, n_seq_s, D))
    new_conv_p = nc_p.reshape(n_seq_p, hc_p, C)[:, hc_p - CONV_HIST:][None]
    new_pool_p = np_p.reshape(n_seq_p, hp_p, C)[:, hp_p - POOL_HIST:][None]
    new_conv_s = _time_major(nc_s.reshape(hc_s, n_seq_s, C)[hc_s - CONV_HIST:])[None]
    new_pool_s = _time_major(np_s.reshape(hp_s, n_seq_s, C)[hp_s - POOL_HIST:])[None]
    return (y_prompt, y_sample, new_conv_p, new_pool_p, new_conv_s, new_pool_s)
```

```python
import functools

import jax
import jax.numpy as jnp
from jax import lax
from jax.experimental import pallas as pl
from jax.experimental.pallas import tpu as pltpu
from jax.experimental.pallas import tpu_sc as plsc

D = 1024
C = 512
N_GROUPS = 4
GROUP = C // N_GROUPS
CONV_HIST = 2
POOL_HIST = 15
E = 32
TOP_K = 4
F = 1024
SWIGLU_LIMIT = 7.0
SWIGLU_ALPHA = 1.702
LN_EPS = 1e-5
DEPTH = 1
ALPHA = (2 * DEPTH) ** 0.25
PAST_LEN = 16384

LANES = 128
SUBLANES = 8
ROW_TILE = 512
PROMPT_ROW_TILE = 1024
PROMPT_SUB_TILES = 1
SAMPLE_ROW_TILE = 512
SLOT_PAD = 256
BIG_BLOCK = 1024
COMBINE_TILE = 512
COMBINE_CHUNKS = 4
SC_WINDOW = 128
DISPATCH_WINDOW = 64
VMEM_LIMIT = 56 * 1024 * 1024

_f32 = jnp.float32
_bf16 = jnp.bfloat16


def _dot(a, b):
    return jnp.dot(a, b, preferred_element_type=_f32)


def _dot_exact(a, b):
    return lax.dot_general(a, b, (((1,), (0,)), ((), ())),
                           precision=lax.Precision.HIGHEST, preferred_element_type=_f32)


def _dot_split(a, b):
    a_hi, b_hi = a.astype(_bf16), b.astype(_bf16)
    a_lo = (a - a_hi.astype(_f32)).astype(_bf16)
    b_lo = (b - b_hi.astype(_f32)).astype(_bf16)
    return _dot(a_hi, b_hi) + _dot(a_lo, b_hi) + _dot(a_hi, b_lo)


def _pack_rows(x):
    w = x.shape[1] // 2
    hi = lax.bitcast_convert_type(x[:, :w].astype(_bf16).astype(_f32), jnp.int32)
    lo = lax.bitcast_convert_type(x[:, w:].astype(_bf16).astype(_f32), jnp.int32)
    return hi | lax.shift_right_logical(lo, 16)


def _unpack_rows(p):
    hi = lax.bitcast_convert_type(p & jnp.int32(-65536), _f32)
    lo = lax.bitcast_convert_type(lax.shift_left(p, 16), _f32)
    return jnp.concatenate([hi, lo], axis=1).astype(_bf16)


def _sigmoid(x):
    return 0.5 * jnp.tanh(0.5 * x) + 0.5


def _per_seq(x, m, op):
    g = m.shape[0]
    if g == 1:
        return op(x, m)
    r, n = x.shape
    return op(x.reshape(r // g, g, n), m[None]).reshape(r, n)


def _layer_norm(x, g, b):
    mu = jnp.mean(x, axis=-1, keepdims=True)
    xc = x - mu
    var = jnp.mean(xc * xc, axis=-1, keepdims=True)
    return xc * lax.rsqrt(var + LN_EPS) * g + b


def _hist_steps(needed, g):
    return -(-needed * g // SUBLANES) * SUBLANES // g


def _ada_kernel(c_ref, w_ref, b_ref, o_ref):
    c = c_ref[...]
    o_ref[...] = _dot((c * _sigmoid(c)).astype(_bf16), w_ref[...].astype(_bf16)) + b_ref[...]


def _ada(c, w_ada, b_ada):
    rows = c.shape[0]
    cols = w_ada.shape[1]
    bn = 1536
    return pl.pallas_call(
        _ada_kernel,
        out_shape=jax.ShapeDtypeStruct((rows, cols), _f32),
        grid=(cols // bn,),
        in_specs=[pl.BlockSpec((rows, D), lambda j: (0, 0)),
                  pl.BlockSpec((D, bn), lambda j: (0, j)),
                  pl.BlockSpec((1, bn), lambda j: (0, j))],
        out_specs=pl.BlockSpec((rows, bn), lambda j: (0, j)),
        compiler_params=pltpu.CompilerParams(vmem_limit_bytes=VMEM_LIMIT),
        name="ada",
    )(c, w_ada, b_ada)


def _mixer_kernel(g, tiles_per_seq, start_pos, n_sub,
                  x_ref, mod_ref, hc_ref, hp_ref,
                  win_ref, cw_ref, woa_ref, wpool_ref, ls_ref, wob_ref, wo_ref, g1_ref, b1_ref,
                  wr_ref, br_ref,
                  h_ref, v_ref, lg_ref, nc_ref, np_ref, zbuf, pbuf):
    r = x_ref.shape[0]
    hrc = hc_ref.shape[0]
    hrp = hp_ref.shape[0]
    j = pl.program_id(0) % tiles_per_seq

    @pl.when(j == 0)
    def _():
        zbuf[pl.ds(0, hrc), :] = hc_ref[...]
        pbuf[pl.ds(0, hrp), :] = hp_ref[...]

    @pl.when(j != 0)
    def _():
        zt = zbuf[pl.ds(r, hrc), :]
        pt = pbuf[pl.ds(r, hrp), :]
        zbuf[pl.ds(0, hrc), :] = zt
        pbuf[pl.ds(0, hrp), :] = pt

    m = mod_ref[...]
    shift1, scale1, gate1 = m[:, 0:D], m[:, D:2 * D], m[:, 2 * D:3 * D]
    shift2, scale2 = m[:, 3 * D:4 * D], m[:, 4 * D:5 * D]

    rs = r // n_sub
    for s in range(n_sub):
        rows = pl.ds(s * rs, rs)
        x = x_ref[rows, :]
        u = _per_seq(_per_seq(x, 1.0 + scale1, jnp.multiply), shift1, jnp.add).astype(_bf16)

        z = _dot(u, win_ref[:, C:2 * C]) * _dot(u, win_ref[:, 2 * C:3 * C])
        zrow = hrc + s * rs
        zbuf[pl.ds(zrow, rs), :] = z
        cw = cw_ref[...]
        conv = (cw[0:1] * zbuf[pl.ds(zrow - 2 * g, rs), :] + cw[1:2] * zbuf[pl.ds(zrow - g, rs), :]
                + cw[2:3] * z)
        y_a = _dot((_dot(u, win_ref[:, 0:C]) * conv).astype(_bf16), woa_ref[...])

        xp = _dot(u, win_ref[:, 3 * C:4 * C])
        prow = hrp + s * rs
        pbuf[pl.ds(prow, rs), :] = xp
        pos = (start_pos + j * (r // g) + s * (rs // g)
               + lax.broadcasted_iota(jnp.int32, (rs, 1), 0) // g)
        acc = xp
        yg = []
        for grp in range(N_GROUPS):
            lo = grp * GROUP
            wdw = 2 ** (grp + 1)
            for back in range(wdw // 2, wdw):
                sh = pbuf[pl.ds(prow - back * g, rs), lo:C]
                acc = jnp.concatenate([acc[:, 0:lo], acc[:, lo:C] + sh], axis=1) if lo else acc + sh
            inv_cnt = 1.0 / jnp.minimum(wdw, pos + 1).astype(_f32)
            diff = acc[:, lo:lo + GROUP] * inv_cnt - xp[:, lo:lo + GROUP]
            yg.append(_dot(diff.astype(_bf16), wpool_ref[grp]))
        y_b = _dot((jnp.concatenate(yg, axis=1) * ls_ref[...]).astype(_bf16), wob_ref[...])

        g_a = _dot(u, win_ref[:, 4 * C:4 * C + D])
        g_b = _dot(u, win_ref[:, 4 * C + D:4 * C + 2 * D])
        merged = _sigmoid(g_a) * y_a + _sigmoid(g_b) * y_b
        o = _dot(merged.astype(_bf16), wo_ref[...])
        h = _layer_norm(ALPHA * x + _per_seq(o, gate1, jnp.multiply), g1_ref[...], b1_ref[...])
        v = _per_seq(_per_seq(h, 1.0 + scale2, jnp.multiply), shift2, jnp.add)
        h_ref[rows, :] = h
        v_ref[rows, :] = _pack_rows(v)
        lg_ref[rows, :] = _dot(v.astype(_bf16), wr_ref[...].astype(_bf16)) + br_ref[...]
    nc_ref[...] = zbuf[pl.ds(r, hrc), :]
    np_ref[...] = pbuf[pl.ds(r, hrp), :]


def _mixer(x2, mod3, hc, hp, weights, row_tile, n_sub, start_pos):
    n = x2.shape[0]
    n_mod, g, _ = mod3.shape
    hrc, hrp = hc.shape[0] // n_mod, hp.shape[0] // n_mod
    tiles_per_seq = n // n_mod // row_tile
    once = dict(pipeline_mode=pl.Buffered(1)) if n_mod == 1 else {}

    def full(a):
        nd = a.ndim
        return pl.BlockSpec(a.shape, lambda i: (0,) * nd)

    def seq_block(rows, **kw):
        return pl.BlockSpec((rows, C), lambda i: (i // tiles_per_seq, 0), **kw)

    def row_block(cols):
        return pl.BlockSpec((row_tile, cols), lambda i: (i, 0))

    return pl.pallas_call(
        functools.partial(_mixer_kernel, g, tiles_per_seq, start_pos, n_sub),
        out_shape=[
            jax.ShapeDtypeStruct((n, D), _f32),
            jax.ShapeDtypeStruct((n, D // 2), jnp.int32),
            jax.ShapeDtypeStruct((n, LANES), _f32),
            jax.ShapeDtypeStruct(hc.shape, _f32),
            jax.ShapeDtypeStruct(hp.shape, _f32),
        ],
        grid=(n // row_tile,),
        in_specs=[row_block(D),
                  pl.BlockSpec((None, g, 6 * D), lambda i: (i // tiles_per_seq, 0, 0), **once),
                  seq_block(hrc, **once), seq_block(hrp, **once)] + [full(a) for a in weights],
        out_specs=[row_block(D), row_block(D // 2), row_block(LANES), seq_block(hrc), seq_block(hrp)],
        scratch_shapes=[pltpu.VMEM((hrc + row_tile, C), _f32), pltpu.VMEM((hrp + row_tile, C), _f32)],
        compiler_params=pltpu.CompilerParams(vmem_limit_bytes=VMEM_LIMIT),
        name="mixer",
    )(x2, mod3, hc, hp, *weights)


def _plan_kernel(lgp_ref, lgs_ref, dest_ref, gate_ref, meta_ref, idx_s, rank_s):
    t = ROW_TILE
    e_iota = lax.broadcasted_iota(jnp.int32, (E, t), 0)
    tri = (lax.broadcasted_iota(jnp.int32, (t, t), 0)
           < lax.broadcasted_iota(jnp.int32, (t, t), 1)).astype(_f32).astype(_bf16)
    zeros_rest = jnp.zeros((LANES - TOP_K, t), _f32)

    def tile_body(lg_ref, off, i, carry):
        lt = lg_ref[i].T[0:E, :]
        vals, idxs = [], []
        for _ in range(TOP_K):
            mx = jnp.max(lt, axis=0, keepdims=True)
            ix = jnp.min(jnp.where(lt == mx, e_iota, E), axis=0, keepdims=True)
            vals.append(mx)
            idxs.append(ix)
            lt = jnp.where(e_iota == ix, -jnp.inf, lt)
        ex = [jnp.exp(vk - vals[0]) for vk in vals]
        den = ex[0] + ex[1] + ex[2] + ex[3]
        gates = [ek / den for ek in ex]
        gate_ref[off + i] = jnp.concatenate(gates + [zeros_rest], axis=0).T

        ohs = [(e_iota == ix) for ix in idxs]
        oh = (ohs[0] | ohs[1] | ohs[2] | ohs[3]).astype(_f32)
        before = _dot(oh.astype(_bf16), tri) + carry
        ranks = [jnp.sum(jnp.where(o, before, 0.0), axis=0, keepdims=True) for o in ohs]
        idx_s[off + i] = jnp.concatenate(idxs + idxs, axis=0)
        rank_s[off + i] = jnp.concatenate(ranks + ranks, axis=0).astype(jnp.int32)
        return carry + jnp.sum(oh, axis=1, keepdims=True)

    n_p, n_s = lgp_ref.shape[0], lgs_ref.shape[0]
    counts = lax.fori_loop(0, n_p, functools.partial(tile_body, lgp_ref, 0), jnp.zeros((E, 1), _f32))
    counts = lax.fori_loop(0, n_s, functools.partial(tile_body, lgs_ref, n_p), counts)
    padded = jnp.ceil(counts / SLOT_PAD) * SLOT_PAD
    low = (lax.broadcasted_iota(jnp.int32, (E, E), 1)
           <= lax.broadcasted_iota(jnp.int32, (E, E), 0)).astype(_f32)
    pad_end = _dot_exact(low, jnp.broadcast_to(padded, (E, LANES)))[:, 0:1]
    pad_start = pad_end - padded

    def dest_body(i, c):
        ix = idx_s[i]
        rk = rank_s[i]
        rows = []
        for k in range(TOP_K):
            st = jnp.sum(jnp.where(e_iota == ix[k:k + 1], pad_start, 0.0), axis=0, keepdims=True)
            rows.append(st.astype(jnp.int32) + rk[k:k + 1])
        dest_ref[i] = jnp.concatenate(rows + rows, axis=0)
        return c

    lax.fori_loop(0, n_p + n_s, dest_body, 0)

    sub = lax.broadcasted_iota(jnp.int32, (E, LANES), 0)
    lane = lax.broadcasted_iota(jnp.int32, (E, LANES), 1)

    def to_lanes(col):
        return jnp.sum(jnp.where(sub == lane, col, 0.0), axis=0, keepdims=True).astype(jnp.int32)

    later = jnp.min(jnp.where((sub > lane) & (counts > 0.0), sub, E), axis=0, keepdims=True)
    meta_ref[...] = jnp.concatenate(
        [to_lanes(counts), to_lanes(pad_start), to_lanes(padded), later,
         jnp.zeros((SUBLANES - 4, LANES), jnp.int32)], axis=0)


def _plan(logits_p, logits_s):
    n_tiles = (logits_p.shape[0] + logits_s.shape[0]) // ROW_TILE
    return pl.pallas_call(
        _plan_kernel,
        out_shape=[
            jax.ShapeDtypeStruct((n_tiles, 2 * TOP_K, ROW_TILE), jnp.int32),
            jax.ShapeDtypeStruct((n_tiles, ROW_TILE, LANES), _f32),
            jax.ShapeDtypeStruct((SUBLANES, LANES), jnp.int32),
        ],
        scratch_shapes=[pltpu.VMEM((n_tiles, 2 * TOP_K, ROW_TILE), jnp.int32),
                        pltpu.VMEM((n_tiles, 2 * TOP_K, ROW_TILE), jnp.int32)],
        compiler_params=pltpu.CompilerParams(vmem_limit_bytes=VMEM_LIMIT),
        name="plan",
    )(logits_p.reshape(-1, ROW_TILE, LANES), logits_s.reshape(-1, ROW_TILE, LANES))


def _dispatch(v_p, v_s, dests, n_rows_out):
    n_p, n_s = v_p.shape[0], v_s.shape[0]
    width = v_p.shape[1]
    w = DISPATCH_WINDOW
    n_pw, n_windows = n_p // w, (n_p + n_s) // w
    mesh = plsc.VectorSubcoreMesh(core_axis_name="core", subcore_axis_name="subcore")
    n_workers = mesh.num_cores * mesh.num_subcores
    n_rounds = -(-n_windows // n_workers)

    @functools.partial(
        pl.kernel, mesh=mesh, name="dispatch",
        out_type=jax.ShapeDtypeStruct((n_rows_out, width), jnp.int32),
        scratch_types=[pltpu.VMEM((w, width), jnp.int32)] * 2 + [pltpu.VMEM((w,), jnp.int32)] * (2 * TOP_K)
        + [pltpu.SemaphoreType.DMA] * 3)
    def scatter_rows(vp_hbm, vs_hbm, d0_hbm, d1_hbm, d2_hbm, d3_hbm, o_hbm, rows0, rows1, *rest):
        worker = lax.axis_index("subcore") * mesh.num_cores + lax.axis_index("core")
        rows = (rows0, rows1)
        idx = (rest[:TOP_K], rest[TOP_K:2 * TOP_K])
        lsem, ssem = rest[2 * TOP_K:2 * TOP_K + 2], rest[2 * TOP_K + 2]
        d_hbm = (d0_hbm, d1_hbm, d2_hbm, d3_hbm)

        def load(c, slot, act):
            def index_reads():
                t0 = pl.multiple_of(c * w, w)
                for k in range(TOP_K):
                    act(pltpu.make_async_copy(d_hbm[k].at[pl.ds(t0, w)], idx[slot][k], lsem[slot]))

            @pl.when(c < n_pw)
            def _():
                src = vp_hbm.at[pl.ds(pl.multiple_of(c * w, w), w)]
                act(pltpu.make_async_copy(src, rows[slot], lsem[slot]))
                index_reads()

            @pl.when((c >= n_pw) & (c < n_windows))
            def _():
                src = vs_hbm.at[pl.ds(pl.multiple_of((c - n_pw) * w, w), w)]
                act(pltpu.make_async_copy(src, rows[slot], lsem[slot]))
                index_reads()

        load(worker, 0, lambda cp: cp.start())
        for j in range(n_rounds):
            c = j * n_workers + worker
            slot = j % 2
            load(c, slot, lambda cp: cp.wait())
            if j + 1 < n_rounds:
                load(c + n_workers, 1 - slot, lambda cp: cp.start())

            @pl.when(c < n_windows)
            def _():
                copies = [pltpu.async_copy(rows[slot], o_hbm.at[iv], ssem) for iv in idx[slot]]
                for cp in copies:
                    cp.wait()

    return scatter_rows(v_p, v_s, *dests)


def _experts_kernel(cnt_ref, row0_ref, pad_ref, nxt_ref, x_hbm, wu_hbm, bu_ref, wd_hbm, bd_ref, y_hbm,
                    xbuf, ybuf, wu_st, wd_st, wu_bf, wd_bf, ysz, xsem, ysem, wsem):
    def w_fetch(e):
        return (pltpu.make_async_copy(wu_hbm.at[e], wu_st, wsem.at[0]),
                pltpu.make_async_copy(wd_hbm.at[e], wd_st, wsem.at[1]))

    def x_fetch(row, size, slot):
        rows = pl.ds(pl.multiple_of(row, SLOT_PAD), size)
        return pltpu.make_async_copy(x_hbm.at[rows, :], xbuf.at[slot, pl.ds(0, size), :], xsem.at[slot])

    def y_store(row, size, slot):
        rows = pl.ds(pl.multiple_of(row, SLOT_PAD), size)
        return pltpu.make_async_copy(ybuf.at[slot, pl.ds(0, size), :], y_hbm.at[rows, :], ysem.at[slot])

    def y_wait(slot):
        for size in (BIG_BLOCK, SLOT_PAD):
            @pl.when(ysz[slot] == size)
            def _():
                y_store(0, size, slot).wait()

    def fetch_first(e, slot):
        @pl.when(pad_ref[e] >= BIG_BLOCK)
        def _():
            x_fetch(row0_ref[e], BIG_BLOCK, slot).start()

        @pl.when((pad_ref[e] > 0) & (pad_ref[e] < BIG_BLOCK))
        def _():
            x_fetch(row0_ref[e], SLOT_PAD, slot).start()

    ysz[0] = 0
    ysz[1] = 0
    for c in w_fetch(0):
        c.start()
    fetch_first(jnp.where(pad_ref[0] > 0, 0, nxt_ref[0]), 0)

    def expert_body(e, n_done):
        for c in w_fetch(e):
            c.wait()
        padded = pad_ref[e]

        @pl.when(padded > 0)
        def _():
            wu_bf[...] = wu_st[...].astype(_bf16)
            wd_bf[...] = wd_st[...].astype(_bf16)

        @pl.when(e + 1 < E)
        def _():
            for c in w_fetch(e + 1):
                c.start()

        b_up = bu_ref[e]
        b_down = bd_ref[e]
        row0 = row0_ref[e]
        n_big = padded // BIG_BLOCK
        n_small = (padded - n_big * BIG_BLOCK) // SLOT_PAD
        small0 = row0 + n_big * BIG_BLOCK
        nxt = nxt_ref[e]

        def pass_body(size, j, n_done):
            slot = n_done % 2
            big = size == BIG_BLOCK
            row = row0 + j * BIG_BLOCK if big else small0 + j * SLOT_PAD
            x_fetch(row, size, slot).wait()

            more = j + 1 < (n_big if big else n_small)
            tail = (n_small > 0) if big else False

            @pl.when(more)
            def _():
                x_fetch(row + size, size, 1 - slot).start()

            if big:
                @pl.when(jnp.logical_not(more) & tail)
                def _():
                    x_fetch(small0, SLOT_PAD, 1 - slot).start()

            @pl.when(jnp.logical_not(more) & jnp.logical_not(tail) & (nxt < E))
            def _():
                fetch_first(jnp.minimum(nxt, E - 1), 1 - slot)

            rows = lax.broadcasted_iota(jnp.int32, (size, 1), 0)
            x = _unpack_rows(jnp.where(rows < cnt_ref[e] - (row - row0), xbuf[slot, pl.ds(0, size), :], 0))
            hcat = _dot(x, wu_bf[...]) + b_up
            glu = jnp.minimum(hcat[:, 0:F], SWIGLU_LIMIT)
            lin = jnp.clip(hcat[:, F:2 * F], -SWIGLU_LIMIT, SWIGLU_LIMIT)
            act = glu * _sigmoid(SWIGLU_ALPHA * glu) * (lin + 1.0)
            y = _pack_rows(_dot(act.astype(_bf16), wd_bf[...]) + b_down)

            y_wait(slot)
            ybuf[slot, pl.ds(0, size), :] = y
            y_store(row, size, slot).start()
            ysz[slot] = size
            return n_done + 1

        n_done = lax.fori_loop(0, n_big, functools.partial(pass_body, BIG_BLOCK), n_done)
        return lax.fori_loop(0, n_small, functools.partial(pass_body, SLOT_PAD), n_done)

    lax.fori_loop(0, E, expert_body, 0)
    y_wait(0)
    y_wait(1)


def _experts(cnt, row0, padded, nxt, xpad, w_up, b_up, w_down, b_down):
    def full(a):
        nd = a.ndim
        return pl.BlockSpec(a.shape, lambda i, *_: (0,) * nd)

    return pl.pallas_call(
        _experts_kernel,
        out_shape=jax.ShapeDtypeStruct(xpad.shape, jnp.int32),
        grid_spec=pltpu.PrefetchScalarGridSpec(
            num_scalar_prefetch=4,
            grid=(1,),
            in_specs=[pl.BlockSpec(memory_space=pl.ANY),
                      pl.BlockSpec(memory_space=pl.ANY), full(b_up),
                      pl.BlockSpec(memory_space=pl.ANY), full(b_down)],
            out_specs=pl.BlockSpec(memory_space=pl.ANY),
            scratch_shapes=[pltpu.VMEM((2, BIG_BLOCK, D // 2), jnp.int32),
                            pltpu.VMEM((2, BIG_BLOCK, D // 2), jnp.int32),
                            pltpu.VMEM((D, 2 * F), _f32), pltpu.VMEM((F, D), _f32),
                            pltpu.VMEM((D, 2 * F), _bf16), pltpu.VMEM((F, D), _bf16),
                            pltpu.SMEM((2,), jnp.int32),
                            pltpu.SemaphoreType.DMA((2,)), pltpu.SemaphoreType.DMA((2,)),
                            pltpu.SemaphoreType.DMA((2,))],
        ),
        compiler_params=pltpu.CompilerParams(vmem_limit_bytes=VMEM_LIMIT),
        name="experts",
    )(cnt, row0, padded, nxt, xpad, w_up, b_up, w_down, b_down)


def _gather_rows(ypad, dest_all):
    n_out = dest_all.shape[0]
    width = ypad.shape[1]
    w = SC_WINDOW
    mesh = plsc.VectorSubcoreMesh(core_axis_name="core", subcore_axis_name="subcore")
    n_workers = mesh.num_cores * mesh.num_subcores
    n_windows = n_out // w

    @functools.partial(
        pl.kernel, mesh=mesh, name="gather_rows",
        out_type=jax.ShapeDtypeStruct((n_out, width), jnp.int32),
        scratch_types=[pltpu.VMEM((w, width), jnp.int32), pltpu.VMEM((w,), jnp.int32), pltpu.SemaphoreType.DMA])
    def gather_rows(y_hbm, d_hbm, o_hbm, rows, iv, sem):
        worker = lax.axis_index("subcore") * mesh.num_cores + lax.axis_index("core")

        @pl.loop(0, n_windows // n_workers)
        def _(j):
            r0 = pl.multiple_of((j * n_workers + worker) * w, w)
            pltpu.sync_copy(d_hbm.at[pl.ds(r0, w)], iv)
            pltpu.async_copy(y_hbm.at[iv], rows, sem).wait()
            pltpu.sync_copy(rows, o_hbm.at[pl.ds(r0, w)])

    assert n_windows % n_workers == 0
    return gather_rows(ypad, dest_all)


def _combine_kernel(h_ref, mod_ref, gate_ref, g2_ref, b2_ref, y_ref, *aliased_and_out):
    o_ref = aliased_and_out[-1]
    gates = gate_ref[...]
    half = D // 2
    f_hi = jnp.zeros((h_ref.shape[0], half), _f32)
    f_lo = jnp.zeros((h_ref.shape[0], half), _f32)
    for k in range(TOP_K):
        p = y_ref[k]
        gk = gates[:, k:k + 1]
        f_hi = f_hi + gk * lax.bitcast_convert_type(p & jnp.int32(-65536), _f32)
        f_lo = f_lo + gk * lax.bitcast_convert_type(lax.shift_left(p, 16), _f32)
    f = jnp.concatenate([f_hi, f_lo], axis=1)
    gate2 = mod_ref[...][:, 5 * D:6 * D]
    pre = ALPHA * h_ref[...] + _per_seq(f, gate2, jnp.multiply)
    o_ref[...] = _layer_norm(pre, g2_ref[...], b2_ref[...])


def _combine(h, mod3, gates, ln2_g, ln2_b, y4, token0, rows_per_mod, row0, out_so_far):
    t = COMBINE_TILE
    hoff = row0 // t
    goff = (token0 + row0) // t
    g = mod3.shape[1]
    tiles_per_mod = rows_per_mod // t
    in_specs = [pl.BlockSpec((t, D), lambda i: (i + hoff, 0)),
                pl.BlockSpec((None, g, 6 * D), lambda i: ((i + hoff) // tiles_per_mod, 0, 0)),
                pl.BlockSpec((t, LANES), lambda i: (i + goff, 0)),
                pl.BlockSpec((1, D), lambda i: (0, 0)),
                pl.BlockSpec((1, D), lambda i: (0, 0)),
                pl.BlockSpec((TOP_K, t, D // 2), lambda i: (0, i, 0))]
    args = [h, mod3, gates, ln2_g, ln2_b, y4]
    aliases = {}
    if out_so_far is not None:
        in_specs.append(pl.BlockSpec(memory_space=pl.ANY))
        args.append(out_so_far)
        aliases = {len(args) - 1: 0}
    return pl.pallas_call(
        _combine_kernel,
        out_shape=jax.ShapeDtypeStruct(h.shape, _f32),
        grid=(y4.shape[1] // t,),
        in_specs=in_specs,
        out_specs=pl.BlockSpec((t, D), lambda i: (i + hoff, 0)),
        input_output_aliases=aliases,
        compiler_params=pltpu.CompilerParams(vmem_limit_bytes=VMEM_LIMIT),
        name="combine",
    )(*args)


def _time_major(a):
    return a.transpose(1, 0, 2)


def kernel(x_prompt, x_sample, c_prompt, c_sample, state_conv, state_pool, w_ada, b_ada, w_in,
           conv_w, w_out_a, w_pool, ls_pool, w_out_b, w_o, ln1_g, ln1_b, w_router, b_router,
           w_up, b_up, w_down, b_down, ln2_g, ln2_b):
    n_seq_p, seq, _ = x_prompt.shape
    n_seq_s, dec_seq, _ = x_sample.shape
    n_p, n_s = n_seq_p * seq, n_seq_s * dec_seq
    n = n_p + n_s
    n_slots = TOP_K * n + E * SLOT_PAD
    l = 0

    mod = _ada(jnp.concatenate([c_prompt, c_sample], axis=0), w_ada[l], b_ada[l][None])
    mod_p = mod[:n_seq_p][:, None, :]
    mod_s = mod[n_seq_p:][None]

    weights = (
        w_in[l].astype(_bf16), conv_w[l], w_out_a[l].astype(_bf16), w_pool[l].astype(_bf16),
        ls_pool[l][None], w_out_b[l].astype(_bf16), w_o[l].astype(_bf16), ln1_g[l][None], ln1_b[l][None],
        jnp.pad(w_router[l], ((0, 0), (0, LANES - E))), jnp.pad(b_router[l], (0, LANES - E))[None],
    )
    hc_p, hp_p = _hist_steps(CONV_HIST, 1), _hist_steps(POOL_HIST, 1)
    h_p, v_p, lg_p, nc_p, np_p = _mixer(
        x_prompt.reshape(n_p, D), mod_p, jnp.zeros((n_seq_p * hc_p, C), _f32),
        jnp.zeros((n_seq_p * hp_p, C), _f32), weights, PROMPT_ROW_TILE, PROMPT_SUB_TILES, 0)
    hc_s, hp_s = _hist_steps(CONV_HIST, n_seq_s), _hist_steps(POOL_HIST, n_seq_s)
    hist_c = jnp.pad(_time_major(state_conv[l]), ((hc_s - CONV_HIST, 0), (0, 0), (0, 0)))
    hist_p = jnp.pad(_time_major(state_pool[l]), ((hp_s - POOL_HIST, 0), (0, 0), (0, 0)))
    h_s, v_s, lg_s, nc_s, np_s = _mixer(
        _time_major(x_sample).reshape(n_s, D), mod_s, hist_c.reshape(hc_s * n_seq_s, C),
        hist_p.reshape(hp_s * n_seq_s, C), weights, SAMPLE_ROW_TILE, 1, PAST_LEN)

    dest8, gates3, meta = _plan(lg_p, lg_s)
    cnt, row0, padded, nxt = (meta[i, :E] for i in range(4))
    gates = gates3.reshape(n, LANES)

    dests = [dest8[:, k, :].reshape(n) for k in range(TOP_K)]
    xpad = _dispatch(v_p, v_s, dests, n_slots)
    ypad = _experts(cnt, row0, padded, nxt, xpad, w_up[l], b_up[l][:, None, :], w_down[l], b_down[l][:, None, :])

    def gathered(t0, rows):
        idx = jnp.concatenate([dk[t0:t0 + rows] for dk in dests])
        return _gather_rows(ypad, idx).reshape(TOP_K, rows, D // 2)

    chunk = n_p // COMBINE_CHUNKS
    y4_p = [gathered(c * chunk, chunk) for c in range(COMBINE_CHUNKS)]
    y4_s = gathered(n_p, n_s)
    g2, b2 = ln2_g[l][None], ln2_b[l][None]
    y_p = None
    for c in range(COMBINE_CHUNKS):
        y_p = _combine(h_p, mod_p, gates, g2, b2, y4_p[c], 0, seq, c * chunk, y_p)
    y_s = _combine(h_s, mod_s, gates, g2, b2, y4_s, n_p, n_s, 0, None)

    y_prompt = y_p.reshape(n_seq_p, seq, D)
    y_sample = _time_major(y_s.reshape(dec_seq, n_seq_s, D))
    new_conv_p = nc_p.reshape(n_seq_p, hc_p, C)[:, hc_p - CONV_HIST:][None]
    new_pool_p = np_p.reshape(n_seq_p, hp_p, C)[:, hp_p - POOL_HIST:][None]
    new_conv_s = _time_major(nc_s.reshape(hc_s, n_seq_s, C)[hc_s - CONV_HIST:])[None]
    new_pool_s = _time_major(np_s.reshape(hp_s, n_seq_s, C)[hp_s - POOL_HIST:])[None]
    return (y_prompt, y_sample, new_conv_p, new_pool_p, new_conv_s, new_pool_s)
```

```python
import functools

import jax
import jax.numpy as jnp
from jax import lax
from jax.experimental import pallas as pl
from jax.experimental.pallas import tpu as pltpu
from jax.experimental.pallas import tpu_sc as plsc

D = 1024
C = 512
N_GROUPS = 4
GROUP = C // N_GROUPS
CONV_HIST = 2
POOL_HIST = 15
E = 32
TOP_K = 4
F = 1024
SWIGLU_LIMIT = 7.0
SWIGLU_ALPHA = 1.702
LN_EPS = 1e-5
DEPTH = 1
ALPHA = (2 * DEPTH) ** 0.25
PAST_LEN = 16384

LANES = 128
SUBLANES = 8
ROW_TILE = 512
PROMPT_ROW_TILE = 1024
PROMPT_SUB_TILES = 1
SAMPLE_ROW_TILE = 512
SLOT_PAD = 256
PASS_UNITS = (4, 5, 3, 2, 1)
PASS_SIZES = tuple(u * SLOT_PAD for u in PASS_UNITS)
MAX_PASS = max(PASS_SIZES)
COMBINE_TILE = 512
COMBINE_CHUNKS = 4
SC_WINDOW = 128
DISPATCH_WINDOW = 64
VMEM_LIMIT = 56 * 1024 * 1024

_f32 = jnp.float32
_bf16 = jnp.bfloat16


def _dot(a, b):
    return jnp.dot(a, b, preferred_element_type=_f32)


def _dot_exact(a, b):
    return lax.dot_general(a, b, (((1,), (0,)), ((), ())),
                           precision=lax.Precision.HIGHEST, preferred_element_type=_f32)


def _dot_split(a, b):
    a_hi, b_hi = a.astype(_bf16), b.astype(_bf16)
    a_lo = (a - a_hi.astype(_f32)).astype(_bf16)
    b_lo = (b - b_hi.astype(_f32)).astype(_bf16)
    return _dot(a_hi, b_hi) + _dot(a_lo, b_hi) + _dot(a_hi, b_lo)


def _pack_rows(x):
    w = x.shape[1] // 2
    hi = lax.bitcast_convert_type(x[:, :w].astype(_bf16).astype(_f32), jnp.int32)
    lo = lax.bitcast_convert_type(x[:, w:].astype(_bf16).astype(_f32), jnp.int32)
    return hi | lax.shift_right_logical(lo, 16)


def _unpack_rows(p):
    hi = lax.bitcast_convert_type(p & jnp.int32(-65536), _f32)
    lo = lax.bitcast_convert_type(lax.shift_left(p, 16), _f32)
    return jnp.concatenate([hi, lo], axis=1).astype(_bf16)


def _sigmoid(x):
    return 0.5 * jnp.tanh(0.5 * x) + 0.5


def _per_seq(x, m, op):
    g = m.shape[0]
    if g == 1:
        return op(x, m)
    r, n = x.shape
    return op(x.reshape(r // g, g, n), m[None]).reshape(r, n)


def _layer_norm(x, g, b):
    mu = jnp.mean(x, axis=-1, keepdims=True)
    xc = x - mu
    var = jnp.mean(xc * xc, axis=-1, keepdims=True)
    return xc * lax.rsqrt(var + LN_EPS) * g + b


def _hist_steps(needed, g):
    return -(-needed * g // SUBLANES) * SUBLANES // g


def _ada_kernel(c_ref, w_ref, b_ref, o_ref):
    c = c_ref[...]
    o_ref[...] = _dot((c * _sigmoid(c)).astype(_bf16), w_ref[...].astype(_bf16)) + b_ref[...]


def _ada(c, w_ada, b_ada):
    rows = c.shape[0]
    cols = w_ada.shape[1]
    bn = 1536
    return pl.pallas_call(
        _ada_kernel,
        out_shape=jax.ShapeDtypeStruct((rows, cols), _f32),
        grid=(cols // bn,),
        in_specs=[pl.BlockSpec((rows, D), lambda j: (0, 0)),
                  pl.BlockSpec((D, bn), lambda j: (0, j)),
                  pl.BlockSpec((1, bn), lambda j: (0, j))],
        out_specs=pl.BlockSpec((rows, bn), lambda j: (0, j)),
        compiler_params=pltpu.CompilerParams(vmem_limit_bytes=VMEM_LIMIT),
        name="ada",
    )(c, w_ada, b_ada)


def _mixer_kernel(g, tiles_per_seq, start_pos, n_sub,
                  x_ref, mod_ref, hc_ref, hp_ref,
                  win_ref, cw_ref, woa_ref, wpool_ref, ls_ref, wob_ref, wo_ref, g1_ref, b1_ref,
                  wr_ref, br_ref,
                  h_ref, v_ref, lg_ref, nc_ref, np_ref, zbuf, pbuf):
    r = x_ref.shape[0]
    hrc = hc_ref.shape[0]
    hrp = hp_ref.shape[0]
    j = pl.program_id(0) % tiles_per_seq

    @pl.when(j == 0)
    def _():
        zbuf[pl.ds(0, hrc), :] = hc_ref[...]
        pbuf[pl.ds(0, hrp), :] = hp_ref[...]

    @pl.when(j != 0)
    def _():
        zt = zbuf[pl.ds(r, hrc), :]
        pt = pbuf[pl.ds(r, hrp), :]
        zbuf[pl.ds(0, hrc), :] = zt
        pbuf[pl.ds(0, hrp), :] = pt

    m = mod_ref[...]
    shift1, scale1, gate1 = m[:, 0:D], m[:, D:2 * D], m[:, 2 * D:3 * D]
    shift2, scale2 = m[:, 3 * D:4 * D], m[:, 4 * D:5 * D]

    rs = r // n_sub
    for s in range(n_sub):
        rows = pl.ds(s * rs, rs)
        x = x_ref[rows, :]
        u = _per_seq(_per_seq(x, 1.0 + scale1, jnp.multiply), shift1, jnp.add).astype(_bf16)

        z = _dot(u, win_ref[:, C:2 * C]) * _dot(u, win_ref[:, 2 * C:3 * C])
        zrow = hrc + s * rs
        zbuf[pl.ds(zrow, rs), :] = z
        cw = cw_ref[...]
        conv = (cw[0:1] * zbuf[pl.ds(zrow - 2 * g, rs), :] + cw[1:2] * zbuf[pl.ds(zrow - g, rs), :]
                + cw[2:3] * z)
        y_a = _dot((_dot(u, win_ref[:, 0:C]) * conv).astype(_bf16), woa_ref[...])

        xp = _dot(u, win_ref[:, 3 * C:4 * C])
        prow = hrp + s * rs
        pbuf[pl.ds(prow, rs), :] = xp
        pos = (start_pos + j * (r // g) + s * (rs // g)
               + lax.broadcasted_iota(jnp.int32, (rs, 1), 0) // g)
        acc = xp
        yg = []
        for grp in range(N_GROUPS):
            lo = grp * GROUP
            wdw = 2 ** (grp + 1)
            for back in range(wdw // 2, wdw):
                sh = pbuf[pl.ds(prow - back * g, rs), lo:C]
                acc = jnp.concatenate([acc[:, 0:lo], acc[:, lo:C] + sh], axis=1) if lo else acc + sh
            inv_cnt = 1.0 / jnp.minimum(wdw, pos + 1).astype(_f32)
            diff = acc[:, lo:lo + GROUP] * inv_cnt - xp[:, lo:lo + GROUP]
            yg.append(_dot(diff.astype(_bf16), wpool_ref[grp]))
        y_b = _dot((jnp.concatenate(yg, axis=1) * ls_ref[...]).astype(_bf16), wob_ref[...])

        g_a = _dot(u, win_ref[:, 4 * C:4 * C + D])
        g_b = _dot(u, win_ref[:, 4 * C + D:4 * C + 2 * D])
        merged = _sigmoid(g_a) * y_a + _sigmoid(g_b) * y_b
        o = _dot(merged.astype(_bf16), wo_ref[...])
        h = _layer_norm(ALPHA * x + _per_seq(o, gate1, jnp.multiply), g1_ref[...], b1_ref[...])
        v = _per_seq(_per_seq(h, 1.0 + scale2, jnp.multiply), shift2, jnp.add)
        h_ref[rows, :] = h
        v_ref[rows, :] = _pack_rows(v)
        lg_ref[rows, :] = _dot(v.astype(_bf16), wr_ref[...].astype(_bf16)) + br_ref[...]
    nc_ref[...] = zbuf[pl.ds(r, hrc), :]
    np_ref[...] = pbuf[pl.ds(r, hrp), :]


def _mixer(x2, mod3, hc, hp, weights, row_tile, n_sub, start_pos):
    n = x2.shape[0]
    n_mod, g, _ = mod3.shape
    hrc, hrp = hc.shape[0] // n_mod, hp.shape[0] // n_mod
    tiles_per_seq = n // n_mod // row_tile
    once = dict(pipeline_mode=pl.Buffered(1)) if n_mod == 1 else {}

    def full(a):
        nd = a.ndim
        return pl.BlockSpec(a.shape, lambda i: (0,) * nd)

    def seq_block(rows, **kw):
        return pl.BlockSpec((rows, C), lambda i: (i // tiles_per_seq, 0), **kw)

    def row_block(cols):
        return pl.BlockSpec((row_tile, cols), lambda i: (i, 0))

    return pl.pallas_call(
        functools.partial(_mixer_kernel, g, tiles_per_seq, start_pos, n_sub),
        out_shape=[
            jax.ShapeDtypeStruct((n, D), _f32),
            jax.ShapeDtypeStruct((n, D // 2), jnp.int32),
            jax.ShapeDtypeStruct((n, LANES), _f32),
            jax.ShapeDtypeStruct(hc.shape, _f32),
            jax.ShapeDtypeStruct(hp.shape, _f32),
        ],
        grid=(n // row_tile,),
        in_specs=[row_block(D),
                  pl.BlockSpec((None, g, 6 * D), lambda i: (i // tiles_per_seq, 0, 0), **once),
                  seq_block(hrc, **once), seq_block(hrp, **once)] + [full(a) for a in weights],
        out_specs=[row_block(D), row_block(D // 2), row_block(LANES), seq_block(hrc), seq_block(hrp)],
        scratch_shapes=[pltpu.VMEM((hrc + row_tile, C), _f32), pltpu.VMEM((hrp + row_tile, C), _f32)],
        compiler_params=pltpu.CompilerParams(vmem_limit_bytes=VMEM_LIMIT),
        name="mixer",
    )(x2, mod3, hc, hp, *weights)


def _plan_kernel(lgp_ref, lgs_ref, dest_ref, gate_ref, meta_ref, idx_s, rank_s):
    t = ROW_TILE
    e_iota = lax.broadcasted_iota(jnp.int32, (E, t), 0)
    tri = (lax.broadcasted_iota(jnp.int32, (t, t), 0)
           < lax.broadcasted_iota(jnp.int32, (t, t), 1)).astype(_f32).astype(_bf16)
    zeros_rest = jnp.zeros((LANES - TOP_K, t), _f32)

    def tile_body(lg_ref, off, i, carry):
        lt = lg_ref[i].T[0:E, :]
        vals, idxs = [], []
        for _ in range(TOP_K):
            mx = jnp.max(lt, axis=0, keepdims=True)
            ix = jnp.min(jnp.where(lt == mx, e_iota, E), axis=0, keepdims=True)
            vals.append(mx)
            idxs.append(ix)
            lt = jnp.where(e_iota == ix, -jnp.inf, lt)
        ex = [jnp.exp(vk - vals[0]) for vk in vals]
        den = ex[0] + ex[1] + ex[2] + ex[3]
        gates = [ek / den for ek in ex]
        gate_ref[off + i] = jnp.concatenate(gates + [zeros_rest], axis=0).T

        ohs = [(e_iota == ix) for ix in idxs]
        oh = (ohs[0] | ohs[1] | ohs[2] | ohs[3]).astype(_f32)
        before = _dot(oh.astype(_bf16), tri) + carry
        ranks = [jnp.sum(jnp.where(o, before, 0.0), axis=0, keepdims=True) for o in ohs]
        idx_s[off + i] = jnp.concatenate(idxs + idxs, axis=0)
        rank_s[off + i] = jnp.concatenate(ranks + ranks, axis=0).astype(jnp.int32)
        return carry + jnp.sum(oh, axis=1, keepdims=True)

    n_p, n_s = lgp_ref.shape[0], lgs_ref.shape[0]
    counts = lax.fori_loop(0, n_p, functools.partial(tile_body, lgp_ref, 0), jnp.zeros((E, 1), _f32))
    counts = lax.fori_loop(0, n_s, functools.partial(tile_body, lgs_ref, n_p), counts)
    padded = jnp.ceil(counts / SLOT_PAD) * SLOT_PAD
    low = (lax.broadcasted_iota(jnp.int32, (E, E), 1)
           <= lax.broadcasted_iota(jnp.int32, (E, E), 0)).astype(_f32)
    pad_end = _dot_exact(low, jnp.broadcast_to(padded, (E, LANES)))[:, 0:1]
    pad_start = pad_end - padded

    def dest_body(i, c):
        ix = idx_s[i]
        rk = rank_s[i]
        rows = []
        for k in range(TOP_K):
            st = jnp.sum(jnp.where(e_iota == ix[k:k + 1], pad_start, 0.0), axis=0, keepdims=True)
            rows.append(st.astype(jnp.int32) + rk[k:k + 1])
        dest_ref[i] = jnp.concatenate(rows + rows, axis=0)
        return c

    lax.fori_loop(0, n_p + n_s, dest_body, 0)

    sub = lax.broadcasted_iota(jnp.int32, (E, LANES), 0)
    lane = lax.broadcasted_iota(jnp.int32, (E, LANES), 1)

    def to_lanes(col):
        return jnp.sum(jnp.where(sub == lane, col, 0.0), axis=0, keepdims=True).astype(jnp.int32)

    later = jnp.min(jnp.where((sub > lane) & (counts > 0.0), sub, E), axis=0, keepdims=True)
    meta_ref[...] = jnp.concatenate(
        [to_lanes(counts), to_lanes(pad_start), to_lanes(padded), later,
         jnp.zeros((SUBLANES - 4, LANES), jnp.int32)], axis=0)


def _plan(logits_p, logits_s):
    n_tiles = (logits_p.shape[0] + logits_s.shape[0]) // ROW_TILE
    return pl.pallas_call(
        _plan_kernel,
        out_shape=[
            jax.ShapeDtypeStruct((n_tiles, 2 * TOP_K, ROW_TILE), jnp.int32),
            jax.ShapeDtypeStruct((n_tiles, ROW_TILE, LANES), _f32),
            jax.ShapeDtypeStruct((SUBLANES, LANES), jnp.int32),
        ],
        scratch_shapes=[pltpu.VMEM((n_tiles, 2 * TOP_K, ROW_TILE), jnp.int32),
                        pltpu.VMEM((n_tiles, 2 * TOP_K, ROW_TILE), jnp.int32)],
        compiler_params=pltpu.CompilerParams(vmem_limit_bytes=VMEM_LIMIT),
        name="plan",
    )(logits_p.reshape(-1, ROW_TILE, LANES), logits_s.reshape(-1, ROW_TILE, LANES))


def _dispatch(v_p, v_s, dests, n_rows_out):
    n_p, n_s = v_p.shape[0], v_s.shape[0]
    width = v_p.shape[1]
    w = DISPATCH_WINDOW
    n_pw, n_windows = n_p // w, (n_p + n_s) // w
    mesh = plsc.VectorSubcoreMesh(core_axis_name="core", subcore_axis_name="subcore")
    n_workers = mesh.num_cores * mesh.num_subcores
    n_rounds = -(-n_windows // n_workers)

    @functools.partial(
        pl.kernel, mesh=mesh, name="dispatch",
        out_type=jax.ShapeDtypeStruct((n_rows_out, width), jnp.int32),
        scratch_types=[pltpu.VMEM((w, width), jnp.int32)] * 2 + [pltpu.VMEM((w,), jnp.int32)] * (2 * TOP_K)
        + [pltpu.SemaphoreType.DMA] * 3)
    def scatter_rows(vp_hbm, vs_hbm, d0_hbm, d1_hbm, d2_hbm, d3_hbm, o_hbm, rows0, rows1, *rest):
        worker = lax.axis_index("subcore") * mesh.num_cores + lax.axis_index("core")
        rows = (rows0, rows1)
        idx = (rest[:TOP_K], rest[TOP_K:2 * TOP_K])
        lsem, ssem = rest[2 * TOP_K:2 * TOP_K + 2], rest[2 * TOP_K + 2]
        d_hbm = (d0_hbm, d1_hbm, d2_hbm, d3_hbm)

        def load(c, slot, act):
            def index_reads():
                t0 = pl.multiple_of(c * w, w)
                for k in range(TOP_K):
                    act(pltpu.make_async_copy(d_hbm[k].at[pl.ds(t0, w)], idx[slot][k], lsem[slot]))

            @pl.when(c < n_pw)
            def _():
                src = vp_hbm.at[pl.ds(pl.multiple_of(c * w, w), w)]
                act(pltpu.make_async_copy(src, rows[slot], lsem[slot]))
                index_reads()

            @pl.when((c >= n_pw) & (c < n_windows))
            def _():
                src = vs_hbm.at[pl.ds(pl.multiple_of((c - n_pw) * w, w), w)]
                act(pltpu.make_async_copy(src, rows[slot], lsem[slot]))
                index_reads()

        load(worker, 0, lambda cp: cp.start())
        for j in range(n_rounds):
            c = j * n_workers + worker
            slot = j % 2
            load(c, slot, lambda cp: cp.wait())
            if j + 1 < n_rounds:
                load(c + n_workers, 1 - slot, lambda cp: cp.start())

            @pl.when(c < n_windows)
            def _():
                copies = [pltpu.async_copy(rows[slot], o_hbm.at[iv], ssem) for iv in idx[slot]]
                for cp in copies:
                    cp.wait()

    return scatter_rows(v_p, v_s, *dests)


def _pass_counts(padded):
    u = padded // SLOT_PAD
    q, r = u // 4, u % 4
    whole = u >= 4
    n5 = jnp.where(whole & (r == 1), 1, jnp.where(whole & (r == 2) & (q >= 2), 2, 0))
    n3 = jnp.where(whole & (r == 3) | (u == 3), 1, jnp.where(u == 6, 2, 0))
    n4 = jnp.where(whole, q - jnp.where(r == 1, 1, 0) - jnp.where(r == 2, jnp.where(q >= 2, 2, 1), 0), 0)
    return (n4, n5, n3, (u == 2).astype(jnp.int32), (u == 1).astype(jnp.int32))


def _experts_kernel(cnt_ref, row0_ref, pad_ref, nxt_ref, x_hbm, wu_hbm, bu_ref, wd_hbm, bd_ref, y_hbm,
                    xbuf, ybuf, wu_st, wd_st, wu_bf, wd_bf, ysz, xsem, ysem, wsem):
    def w_fetch(e):
        return (pltpu.make_async_copy(wu_hbm.at[e], wu_st, wsem.at[0]),
                pltpu.make_async_copy(wd_hbm.at[e], wd_st, wsem.at[1]))

    def x_fetch(row, size, slot):
        rows = pl.ds(pl.multiple_of(row, SLOT_PAD), size)
        return pltpu.make_async_copy(x_hbm.at[rows, :], xbuf.at[slot, pl.ds(0, size), :], xsem.at[slot])

    def y_store(row, size, slot):
        rows = pl.ds(pl.multiple_of(row, SLOT_PAD), size)
        return pltpu.make_async_copy(ybuf.at[slot, pl.ds(0, size), :], y_hbm.at[rows, :], ysem.at[slot])

    def by_size(size, fn):
        for s in PASS_SIZES:
            pl.when(size == s)(functools.partial(fn, s))

    def y_wait(slot):
        by_size(ysz[slot], lambda s: y_store(0, s, slot).wait())

    def first_size(counts, groups=range(len(PASS_SIZES))):
        size = 0
        for g in reversed(groups):
            size = jnp.where(counts[g] > 0, PASS_SIZES[g], size)
        return size

    def fetch_first(e, slot):
        size = first_size(_pass_counts(pad_ref[e]))
        by_size(size, lambda s: x_fetch(row0_ref[e], s, slot).start())

    ysz[0] = 0
    ysz[1] = 0
    for c in w_fetch(0):
        c.start()
    fetch_first(jnp.where(pad_ref[0] > 0, 0, nxt_ref[0]), 0)

    def expert_body(e, n_done):
        for c in w_fetch(e):
            c.wait()
        padded = pad_ref[e]

        @pl.when(padded > 0)
        def _():
            wu_bf[...] = wu_st[...].astype(_bf16)
            wd_bf[...] = wd_st[...].astype(_bf16)

        @pl.when(e + 1 < E)
        def _():
            for c in w_fetch(e + 1):
                c.start()

        b_up = bu_ref[e]
        b_down = bd_ref[e]
        row0 = row0_ref[e]
        nxt = nxt_ref[e]
        counts = _pass_counts(padded)
        starts = [row0]
        for g in range(len(PASS_SIZES) - 1):
            starts.append(starts[g] + counts[g] * PASS_SIZES[g])

        def pass_body(g, j, n_done):
            size = PASS_SIZES[g]
            slot = n_done % 2
            row = starts[g] + j * size
            x_fetch(row, size, slot).wait()

            more = j + 1 < counts[g]
            later = first_size(counts, range(g + 1, len(PASS_SIZES)))
            by_size(jnp.where(more, size, later), lambda s: x_fetch(row + size, s, 1 - slot).start())

            @pl.when(jnp.logical_not(more) & (later == 0) & (nxt < E))
            def _():
                fetch_first(jnp.minimum(nxt, E - 1), 1 - slot)

            rows = lax.broadcasted_iota(jnp.int32, (size, 1), 0)
            x = _unpack_rows(jnp.where(rows < cnt_ref[e] - (row - row0), xbuf[slot, pl.ds(0, size), :], 0))
            hcat = _dot(x, wu_bf[...]) + b_up
            glu = jnp.minimum(hcat[:, 0:F], SWIGLU_LIMIT)
            lin = jnp.clip(hcat[:, F:2 * F], -SWIGLU_LIMIT, SWIGLU_LIMIT)
            act = glu * _sigmoid(SWIGLU_ALPHA * glu) * (lin + 1.0)
            y = _pack_rows(_dot(act.astype(_bf16), wd_bf[...]) + b_down)

            y_wait(slot)
            ybuf[slot, pl.ds(0, size), :] = y
            y_store(row, size, slot).start()
            ysz[slot] = size
            return n_done + 1

        for g in range(len(PASS_SIZES)):
            n_done = lax.fori_loop(0, counts[g], functools.partial(pass_body, g), n_done)
        return n_done

    lax.fori_loop(0, E, expert_body, 0)
    y_wait(0)
    y_wait(1)


def _experts(cnt, row0, padded, nxt, xpad, w_up, b_up, w_down, b_down):
    def full(a):
        nd = a.ndim
        return pl.BlockSpec(a.shape, lambda i, *_: (0,) * nd)

    return pl.pallas_call(
        _experts_kernel,
        out_shape=jax.ShapeDtypeStruct(xpad.shape, jnp.int32),
        grid_spec=pltpu.PrefetchScalarGridSpec(
            num_scalar_prefetch=4,
            grid=(1,),
            in_specs=[pl.BlockSpec(memory_space=pl.ANY),
                      pl.BlockSpec(memory_space=pl.ANY), full(b_up),
                      pl.BlockSpec(memory_space=pl.ANY), full(b_down)],
            out_specs=pl.BlockSpec(memory_space=pl.ANY),
            scratch_shapes=[pltpu.VMEM((2, MAX_PASS, D // 2), jnp.int32),
                            pltpu.VMEM((2, MAX_PASS, D // 2), jnp.int32),
                            pltpu.VMEM((D, 2 * F), _f32), pltpu.VMEM((F, D), _f32),
                            pltpu.VMEM((D, 2 * F), _bf16), pltpu.VMEM((F, D), _bf16),
                            pltpu.SMEM((2,), jnp.int32),
                            pltpu.SemaphoreType.DMA((2,)), pltpu.SemaphoreType.DMA((2,)),
                            pltpu.SemaphoreType.DMA((2,))],
        ),
        compiler_params=pltpu.CompilerParams(vmem_limit_bytes=VMEM_LIMIT),
        name="experts",
    )(cnt, row0, padded, nxt, xpad, w_up, b_up, w_down, b_down)


def _gather_rows(ypad, dest_all):
    n_out = dest_all.shape[0]
    width = ypad.shape[1]
    w = SC_WINDOW
    mesh = plsc.VectorSubcoreMesh(core_axis_name="core", subcore_axis_name="subcore")
    n_workers = mesh.num_cores * mesh.num_subcores
    n_windows = n_out // w

    @functools.partial(
        pl.kernel, mesh=mesh, name="gather_rows",
        out_type=jax.ShapeDtypeStruct((n_out, width), jnp.int32),
        scratch_types=[pltpu.VMEM((w, width), jnp.int32), pltpu.VMEM((w,), jnp.int32), pltpu.SemaphoreType.DMA])
    def gather_rows(y_hbm, d_hbm, o_hbm, rows, iv, sem):
        worker = lax.axis_index("subcore") * mesh.num_cores + lax.axis_index("core")

        @pl.loop(0, n_windows // n_workers)
        def _(j):
            r0 = pl.multiple_of((j * n_workers + worker) * w, w)
            pltpu.sync_copy(d_hbm.at[pl.ds(r0, w)], iv)
            pltpu.async_copy(y_hbm.at[iv], rows, sem).wait()
            pltpu.sync_copy(rows, o_hbm.at[pl.ds(r0, w)])

    assert n_windows % n_workers == 0
    return gather_rows(ypad, dest_all)


def _combine_kernel(h_ref, mod_ref, gate_ref, g2_ref, b2_ref, y_ref, *aliased_and_out):
    o_ref = aliased_and_out[-1]
    gates = gate_ref[...]
    half = D // 2
    f_hi = jnp.zeros((h_ref.shape[0], half), _f32)
    f_lo = jnp.zeros((h_ref.shape[0], half), _f32)
    for k in range(TOP_K):
        p = y_ref[k]
        gk = gates[:, k:k + 1]
        f_hi = f_hi + gk * lax.bitcast_convert_type(p & jnp.int32(-65536), _f32)
        f_lo = f_lo + gk * lax.bitcast_convert_type(lax.shift_left(p, 16), _f32)
    f = jnp.concatenate([f_hi, f_lo], axis=1)
    gate2 = mod_ref[...][:, 5 * D:6 * D]
    pre = ALPHA * h_ref[...] + _per_seq(f, gate2, jnp.multiply)
    o_ref[...] = _layer_norm(pre, g2_ref[...], b2_ref[...])


def _combine(h, mod3, gates, ln2_g, ln2_b, y4, token0, rows_per_mod, row0, out_so_far):
    t = COMBINE_TILE
    hoff = row0 // t
    goff = (token0 + row0) // t
    g = mod3.shape[1]
    tiles_per_mod = rows_per_mod // t
    in_specs = [pl.BlockSpec((t, D), lambda i: (i + hoff, 0)),
                pl.BlockSpec((None, g, 6 * D), lambda i: ((i + hoff) // tiles_per_mod, 0, 0)),
                pl.BlockSpec((t, LANES), lambda i: (i + goff, 0)),
                pl.BlockSpec((1, D), lambda i: (0, 0)),
                pl.BlockSpec((1, D), lambda i: (0, 0)),
                pl.BlockSpec((TOP_K, t, D // 2), lambda i: (0, i, 0))]
    args = [h, mod3, gates, ln2_g, ln2_b, y4]
    aliases = {}
    if out_so_far is not None:
        in_specs.append(pl.BlockSpec(memory_space=pl.ANY))
        args.append(out_so_far)
        aliases = {len(args) - 1: 0}
    return pl.pallas_call(
        _combine_kernel,
        out_shape=jax.ShapeDtypeStruct(h.shape, _f32),
        grid=(y4.shape[1] // t,),
        in_specs=in_specs,
        out_specs=pl.BlockSpec((t, D), lambda i: (i + hoff, 0)),
        input_output_aliases=aliases,
        compiler_params=pltpu.CompilerParams(vmem_limit_bytes=VMEM_LIMIT),
        name="combine",
    )(*args)


def _time_major(a):
    return a.transpose(1, 0, 2)


def kernel(x_prompt, x_sample, c_prompt, c_sample, state_conv, state_pool, w_ada, b_ada, w_in,
           conv_w, w_out_a, w_pool, ls_pool, w_out_b, w_o, ln1_g, ln1_b, w_router, b_router,
           w_up, b_up, w_down, b_down, ln2_g, ln2_b):
    n_seq_p, seq, _ = x_prompt.shape
    n_seq_s, dec_seq, _ = x_sample.shape
    n_p, n_s = n_seq_p * seq, n_seq_s * dec_seq
    n = n_p + n_s
    n_slots = TOP_K * n + E * SLOT_PAD
    l = 0

    mod = _ada(jnp.concatenate([c_prompt, c_sample], axis=0), w_ada[l], b_ada[l][None])
    mod_p = mod[:n_seq_p][:, None, :]
    mod_s = mod[n_seq_p:][None]

    weights = (
        w_in[l].astype(_bf16), conv_w[l], w_out_a[l].astype(_bf16), w_pool[l].astype(_bf16),
        ls_pool[l][None], w_out_b[l].astype(_bf16), w_o[l].astype(_bf16), ln1_g[l][None], ln1_b[l][None],
        jnp.pad(w_router[l], ((0, 0), (0, LANES - E))), jnp.pad(b_router[l], (0, LANES - E))[None],
    )
    hc_p, hp_p = _hist_steps(CONV_HIST, 1), _hist_steps(POOL_HIST, 1)
    h_p, v_p, lg_p, nc_p, np_p = _mixer(
        x_prompt.reshape(n_p, D), mod_p, jnp.zeros((n_seq_p * hc_p, C), _f32),
        jnp.zeros((n_seq_p * hp_p, C), _f32), weights, PROMPT_ROW_TILE, PROMPT_SUB_TILES, 0)
    hc_s, hp_s = _hist_steps(CONV_HIST, n_seq_s), _hist_steps(POOL_HIST, n_seq_s)
    hist_c = jnp.pad(_time_major(state_conv[l]), ((hc_s - CONV_HIST, 0), (0, 0), (0, 0)))
    hist_p = jnp.pad(_time_major(state_pool[l]), ((hp_s - POOL_HIST, 0), (0, 0), (0, 0)))
    h_s, v_s, lg_s, nc_s, np_s = _mixer(
        _time_major(x_sample).reshape(n_s, D), mod_s, hist_c.reshape(hc_s * n_seq_s, C),
        hist_p.reshape(hp_s * n_seq_s, C), weights, SAMPLE_ROW_TILE, 1, PAST_LEN)

    dest8, gates3, meta = _plan(lg_p, lg_s)
    cnt, row0, padded, nxt = (meta[i, :E] for i in range(4))
    gates = gates3.reshape(n, LANES)

    dests = [dest8[:, k, :].reshape(n) for k in range(TOP_K)]
    xpad = _dispatch(v_p, v_s, dests, n_slots)
    ypad = _experts(cnt, row0, padded, nxt, xpad, w_up[l], b_up[l][:, None, :], w_down[l], b_down[l][:, None, :])

    def gathered(t0, rows):
        idx = jnp.concatenate([dk[t0:t0 + rows] for dk in dests])
        return _gather_rows(ypad, idx).reshape(TOP_K, rows, D // 2)

    chunk = n_p // COMBINE_CHUNKS
    y4_p = [gathered(c * chunk, chunk) for c in range(COMBINE_CHUNKS)]
    y4_s = gathered(n_p, n_s)
    g2, b2 = ln2_g[l][None], ln2_b[l][None]
    y_p = None
    for c in range(COMBINE_CHUNKS):
        y_p = _combine(h_p, mod_p, gates, g2, b2, y4_p[c], 0, seq, c * chunk, y_p)
    y_s = _combine(h_s, mod_s, gates, g2, b2, y4_s, n_p, n_s, 0, None)

    y_prompt = y_p.reshape(n_seq_p, seq, D)
    y_sample = _time_major(y_s.reshape(dec_seq, n_seq_s, D))
    new_conv_p = nc_p.reshape(n_seq_p, hc_p, C)[:, hc_p - CONV_HIST:][None]
    new_pool_p = np_p.reshape(n_seq_p, hp_p, C)[:, hp_p - POOL_HIST:][None]
    new_conv_s = _time_major(nc_s.reshape(hc_s, n_seq_s, C)[hc_s - CONV_HIST:])[None]
    new_pool_s = _time_major(np_s.reshape(hp_s, n_seq_s, C)[hp_s - POOL_HIST:])[None]
    return (y_prompt, y_sample, new_conv_p, new_pool_p, new_conv_s, new_pool_s)
```

```python
import functools

import jax
import jax.numpy as jnp
from jax import lax
from jax.experimental import pallas as pl
from jax.experimental.pallas import tpu as pltpu
from jax.experimental.pallas import tpu_sc as plsc

D = 1024
C = 512
N_GROUPS = 4
GROUP = C // N_GROUPS
CONV_HIST = 2
POOL_HIST = 15
E = 32
TOP_K = 4
F = 1024
SWIGLU_LIMIT = 7.0
SWIGLU_ALPHA = 1.702
LN_EPS = 1e-5
DEPTH = 1
ALPHA = (2 * DEPTH) ** 0.25
PAST_LEN = 16384

LANES = 128
SUBLANES = 8
ROW_TILE = 512
PROMPT_ROW_TILE = 1024
SAMPLE_ROW_TILE = 512
SLOT_PAD = 256
BIG_BLOCK = 1024
COMBINE_TILE = 512
COMBINE_CHUNKS = 4
SC_WINDOW = 128
DISPATCH_WINDOW = 64
ADA_COLS = 1536
VMEM_LIMIT = 56 * 1024 * 1024

_f32 = jnp.float32
_bf16 = jnp.bfloat16


def _dot(a, b):
    return jnp.dot(a, b, preferred_element_type=_f32)


def _dot_exact(a, b):
    return lax.dot_general(a, b, (((1,), (0,)), ((), ())),
                           precision=lax.Precision.HIGHEST, preferred_element_type=_f32)


def _pack_rows(x):
    w = x.shape[1] // 2
    hi = lax.bitcast_convert_type(x[:, :w].astype(_bf16).astype(_f32), jnp.int32)
    lo = lax.bitcast_convert_type(x[:, w:].astype(_bf16).astype(_f32), jnp.int32)
    return hi | lax.shift_right_logical(lo, 16)


def _unpack_rows(p):
    hi = lax.bitcast_convert_type(p & jnp.int32(-65536), _f32)
    lo = lax.bitcast_convert_type(lax.shift_left(p, 16), _f32)
    return jnp.concatenate([hi, lo], axis=1).astype(_bf16)


def _sigmoid(x):
    return 0.5 * jnp.tanh(0.5 * x) + 0.5


def _per_seq(x, m, op):
    g = m.shape[0]
    if g == 1:
        return op(x, m)
    r, n = x.shape
    return op(x.reshape(r // g, g, n), m[None]).reshape(r, n)


def _layer_norm(x, g, b):
    mu = jnp.mean(x, axis=-1, keepdims=True)
    xc = x - mu
    var = jnp.mean(xc * xc, axis=-1, keepdims=True)
    return xc * lax.rsqrt(var + LN_EPS) * g + b


def _hist_steps(needed, g):
    return -(-needed * g // SUBLANES) * SUBLANES // g


def _ada_kernel(c_ref, w_ref, b_ref, o_ref):
    c = c_ref[...]
    o_ref[...] = _dot((c * _sigmoid(c)).astype(_bf16), w_ref[...].astype(_bf16)) + b_ref[...]


def _ada(c, w_ada, b_ada):
    rows = c.shape[0]
    cols = w_ada.shape[1]
    bn = ADA_COLS
    return pl.pallas_call(
        _ada_kernel,
        out_shape=jax.ShapeDtypeStruct((rows, cols), _f32),
        grid=(cols // bn,),
        in_specs=[pl.BlockSpec((rows, D), lambda j: (0, 0)),
                  pl.BlockSpec((D, bn), lambda j: (0, j)),
                  pl.BlockSpec((1, bn), lambda j: (0, j))],
        out_specs=pl.BlockSpec((rows, bn), lambda j: (0, j)),
        compiler_params=pltpu.CompilerParams(vmem_limit_bytes=VMEM_LIMIT),
        name="ada",
    )(c, w_ada, b_ada)


def _mixer_kernel(g, tiles_per_seq, start_pos,
                  x_ref, mod_ref, hc_ref, hp_ref,
                  win_ref, cw_ref, woa_ref, wpool_ref, ls_ref, wob_ref, wo_ref, g1_ref, b1_ref,
                  wr_ref, br_ref,
                  h_ref, v_ref, lg_ref, nc_ref, np_ref, zbuf, pbuf):
    r = x_ref.shape[0]
    hrc = hc_ref.shape[0]
    hrp = hp_ref.shape[0]
    j = pl.program_id(0) % tiles_per_seq

    @pl.when(j == 0)
    def _():
        zbuf[pl.ds(0, hrc), :] = hc_ref[...]
        pbuf[pl.ds(0, hrp), :] = hp_ref[...]

    @pl.when(j != 0)
    def _():
        zt = zbuf[pl.ds(r, hrc), :]
        pt = pbuf[pl.ds(r, hrp), :]
        zbuf[pl.ds(0, hrc), :] = zt
        pbuf[pl.ds(0, hrp), :] = pt

    m = mod_ref[...]
    shift1, scale1, gate1 = m[:, 0:D], m[:, D:2 * D], m[:, 2 * D:3 * D]
    shift2, scale2 = m[:, 3 * D:4 * D], m[:, 4 * D:5 * D]

    x = x_ref[...]
    u = _per_seq(_per_seq(x, 1.0 + scale1, jnp.multiply), shift1, jnp.add).astype(_bf16)

    z = _dot(u, win_ref[:, C:2 * C]) * _dot(u, win_ref[:, 2 * C:3 * C])
    zbuf[pl.ds(hrc, r), :] = z
    cw = cw_ref[...]
    conv = (cw[0:1] * zbuf[pl.ds(hrc - 2 * g, r), :] + cw[1:2] * zbuf[pl.ds(hrc - g, r), :]
            + cw[2:3] * z)
    y_a = _dot((_dot(u, win_ref[:, 0:C]) * conv).astype(_bf16), woa_ref[...])

    xp = _dot(u, win_ref[:, 3 * C:4 * C])
    pbuf[pl.ds(hrp, r), :] = xp
    pos = start_pos + j * (r // g) + lax.broadcasted_iota(jnp.int32, (r, 1), 0) // g
    acc = xp
    yg = []
    for grp in range(N_GROUPS):
        lo = grp * GROUP
        wdw = 2 ** (grp + 1)
        for back in range(wdw // 2, wdw):
            sh = pbuf[pl.ds(hrp - back * g, r), lo:C]
            acc = jnp.concatenate([acc[:, 0:lo], acc[:, lo:C] + sh], axis=1) if lo else acc + sh
        inv_cnt = 1.0 / jnp.minimum(wdw, pos + 1).astype(_f32)
        diff = acc[:, lo:lo + GROUP] * inv_cnt - xp[:, lo:lo + GROUP]
        yg.append(_dot(diff.astype(_bf16), wpool_ref[grp]))
    y_b = _dot((jnp.concatenate(yg, axis=1) * ls_ref[...]).astype(_bf16), wob_ref[...])

    g_a = _dot(u, win_ref[:, 4 * C:4 * C + D])
    g_b = _dot(u, win_ref[:, 4 * C + D:4 * C + 2 * D])
    merged = _sigmoid(g_a) * y_a + _sigmoid(g_b) * y_b
    o = _dot(merged.astype(_bf16), wo_ref[...])
    h = _layer_norm(ALPHA * x + _per_seq(o, gate1, jnp.multiply), g1_ref[...], b1_ref[...])
    v = _per_seq(_per_seq(h, 1.0 + scale2, jnp.multiply), shift2, jnp.add)
    h_ref[...] = h
    v_ref[...] = _pack_rows(v)
    lg_ref[...] = _dot(v.astype(_bf16), wr_ref[...].astype(_bf16)) + br_ref[...]
    nc_ref[...] = zbuf[pl.ds(r, hrc), :]
    np_ref[...] = pbuf[pl.ds(r, hrp), :]


def _mixer(x2, mod3, hc, hp, weights, row_tile, start_pos):
    n = x2.shape[0]
    n_mod, g, _ = mod3.shape
    hrc, hrp = hc.shape[0] // n_mod, hp.shape[0] // n_mod
    tiles_per_seq = n // n_mod // row_tile
    once = dict(pipeline_mode=pl.Buffered(1)) if n_mod == 1 else {}

    def full(a):
        nd = a.ndim
        return pl.BlockSpec(a.shape, lambda i: (0,) * nd)

    def seq_block(rows, **kw):
        return pl.BlockSpec((rows, C), lambda i: (i // tiles_per_seq, 0), **kw)

    def row_block(cols):
        return pl.BlockSpec((row_tile, cols), lambda i: (i, 0))

    return pl.pallas_call(
        functools.partial(_mixer_kernel, g, tiles_per_seq, start_pos),
        out_shape=[
            jax.ShapeDtypeStruct((n, D), _f32),
            jax.ShapeDtypeStruct((n, D // 2), jnp.int32),
            jax.ShapeDtypeStruct((n, LANES), _f32),
            jax.ShapeDtypeStruct(hc.shape, _f32),
            jax.ShapeDtypeStruct(hp.shape, _f32),
        ],
        grid=(n // row_tile,),
        in_specs=[row_block(D),
                  pl.BlockSpec((None, g, 6 * D), lambda i: (i // tiles_per_seq, 0, 0), **once),
                  seq_block(hrc, **once), seq_block(hrp, **once)] + [full(a) for a in weights],
        out_specs=[row_block(D), row_block(D // 2), row_block(LANES), seq_block(hrc), seq_block(hrp)],
        scratch_shapes=[pltpu.VMEM((hrc + row_tile, C), _f32), pltpu.VMEM((hrp + row_tile, C), _f32)],
        compiler_params=pltpu.CompilerParams(vmem_limit_bytes=VMEM_LIMIT),
        name="mixer",
    )(x2, mod3, hc, hp, *weights)


def _plan_kernel(lgp_ref, lgs_ref, dest_ref, gate_ref, meta_ref, idx_s, rank_s):
    t = ROW_TILE
    e_iota = lax.broadcasted_iota(jnp.int32, (E, t), 0)
    tri = (lax.broadcasted_iota(jnp.int32, (t, t), 0)
           < lax.broadcasted_iota(jnp.int32, (t, t), 1)).astype(_f32).astype(_bf16)
    zeros_rest = jnp.zeros((LANES - TOP_K, t), _f32)

    def tile_body(lg_ref, off, i, carry):
        lt = lg_ref[i].T[0:E, :]
        vals, idxs = [], []
        for _ in range(TOP_K):
            mx = jnp.max(lt, axis=0, keepdims=True)
            ix = jnp.min(jnp.where(lt == mx, e_iota, E), axis=0, keepdims=True)
            vals.append(mx)
            idxs.append(ix)
            lt = jnp.where(e_iota == ix, -jnp.inf, lt)
        ex = [jnp.exp(vk - vals[0]) for vk in vals]
        den = ex[0] + ex[1] + ex[2] + ex[3]
        gates = [ek / den for ek in ex]
        gate_ref[off + i] = jnp.concatenate(gates + [zeros_rest], axis=0).T

        ohs = [(e_iota == ix) for ix in idxs]
        oh = (ohs[0] | ohs[1] | ohs[2] | ohs[3]).astype(_f32)
        before = _dot(oh.astype(_bf16), tri) + carry
        ranks = [jnp.sum(jnp.where(o, before, 0.0), axis=0, keepdims=True) for o in ohs]
        idx_s[off + i] = jnp.concatenate(idxs + idxs, axis=0)
        rank_s[off + i] = jnp.concatenate(ranks + ranks, axis=0).astype(jnp.int32)
        return carry + jnp.sum(oh, axis=1, keepdims=True)

    n_p, n_s = lgp_ref.shape[0], lgs_ref.shape[0]
    counts = lax.fori_loop(0, n_p, functools.partial(tile_body, lgp_ref, 0), jnp.zeros((E, 1), _f32))
    counts = lax.fori_loop(0, n_s, functools.partial(tile_body, lgs_ref, n_p), counts)
    padded = jnp.ceil(counts / SLOT_PAD) * SLOT_PAD
    low = (lax.broadcasted_iota(jnp.int32, (E, E), 1)
           <= lax.broadcasted_iota(jnp.int32, (E, E), 0)).astype(_f32)
    pad_end = _dot_exact(low, jnp.broadcast_to(padded, (E, LANES)))[:, 0:1]
    pad_start = pad_end - padded

    def dest_body(i, c):
        ix = idx_s[i]
        rk = rank_s[i]
        rows = []
        for k in range(TOP_K):
            st = jnp.sum(jnp.where(e_iota == ix[k:k + 1], pad_start, 0.0), axis=0, keepdims=True)
            rows.append(st.astype(jnp.int32) + rk[k:k + 1])
        dest_ref[i] = jnp.concatenate(rows + rows, axis=0)
        return c

    lax.fori_loop(0, n_p + n_s, dest_body, 0)

    sub = lax.broadcasted_iota(jnp.int32, (E, LANES), 0)
    lane = lax.broadcasted_iota(jnp.int32, (E, LANES), 1)

    def to_lanes(col):
        return jnp.sum(jnp.where(sub == lane, col, 0.0), axis=0, keepdims=True).astype(jnp.int32)

    later = jnp.min(jnp.where((sub > lane) & (counts > 0.0), sub, E), axis=0, keepdims=True)
    meta_ref[...] = jnp.concatenate(
        [to_lanes(counts), to_lanes(pad_start), to_lanes(padded), later,
         jnp.zeros((SUBLANES - 4, LANES), jnp.int32)], axis=0)


def _plan(logits_p, logits_s):
    n_tiles = (logits_p.shape[0] + logits_s.shape[0]) // ROW_TILE
    return pl.pallas_call(
        _plan_kernel,
        out_shape=[
            jax.ShapeDtypeStruct((n_tiles, 2 * TOP_K, ROW_TILE), jnp.int32),
            jax.ShapeDtypeStruct((n_tiles, ROW_TILE, LANES), _f32),
            jax.ShapeDtypeStruct((SUBLANES, LANES), jnp.int32),
        ],
        scratch_shapes=[pltpu.VMEM((n_tiles, 2 * TOP_K, ROW_TILE), jnp.int32),
                        pltpu.VMEM((n_tiles, 2 * TOP_K, ROW_TILE), jnp.int32)],
        compiler_params=pltpu.CompilerParams(vmem_limit_bytes=VMEM_LIMIT),
        name="plan",
    )(logits_p.reshape(-1, ROW_TILE, LANES), logits_s.reshape(-1, ROW_TILE, LANES))


def _dispatch(v_p, v_s, dests, n_rows_out):
    n_p, n_s = v_p.shape[0], v_s.shape[0]
    width = v_p.shape[1]
    w = DISPATCH_WINDOW
    n_pw, n_windows = n_p // w, (n_p + n_s) // w
    mesh = plsc.VectorSubcoreMesh(core_axis_name="core", subcore_axis_name="subcore")
    n_workers = mesh.num_cores * mesh.num_subcores
    n_rounds = -(-n_windows // n_workers)

    @functools.partial(
        pl.kernel, mesh=mesh, name="dispatch",
        out_type=jax.ShapeDtypeStruct((n_rows_out, width), jnp.int32),
        scratch_types=[pltpu.VMEM((w, width), jnp.int32)] * 2 + [pltpu.VMEM((w,), jnp.int32)] * (2 * TOP_K)
        + [pltpu.SemaphoreType.DMA] * 3)
    def scatter_rows(vp_hbm, vs_hbm, d0_hbm, d1_hbm, d2_hbm, d3_hbm, o_hbm, rows0, rows1, *rest):
        worker = lax.axis_index("subcore") * mesh.num_cores + lax.axis_index("core")
        rows = (rows0, rows1)
        idx = (rest[:TOP_K], rest[TOP_K:2 * TOP_K])
        lsem, ssem = rest[2 * TOP_K:2 * TOP_K + 2], rest[2 * TOP_K + 2]
        d_hbm = (d0_hbm, d1_hbm, d2_hbm, d3_hbm)

        def load(c, slot, act):
            def index_reads():
                t0 = pl.multiple_of(c * w, w)
                for k in range(TOP_K):
                    act(pltpu.make_async_copy(d_hbm[k].at[pl.ds(t0, w)], idx[slot][k], lsem[slot]))

            @pl.when(c < n_pw)
            def _():
                src = vp_hbm.at[pl.ds(pl.multiple_of(c * w, w), w)]
                act(pltpu.make_async_copy(src, rows[slot], lsem[slot]))
                index_reads()

            @pl.when((c >= n_pw) & (c < n_windows))
            def _():
                src = vs_hbm.at[pl.ds(pl.multiple_of((c - n_pw) * w, w), w)]
                act(pltpu.make_async_copy(src, rows[slot], lsem[slot]))
                index_reads()

        load(worker, 0, lambda cp: cp.start())
        for j in range(n_rounds):
            c = j * n_workers + worker
            slot = j % 2
            load(c, slot, lambda cp: cp.wait())
            if j + 1 < n_rounds:
                load(c + n_workers, 1 - slot, lambda cp: cp.start())

            @pl.when(c < n_windows)
            def _():
                copies = [pltpu.async_copy(rows[slot], o_hbm.at[iv], ssem) for iv in idx[slot]]
                for cp in copies:
                    cp.wait()

    return scatter_rows(v_p, v_s, *dests)


def _experts_kernel(cnt_ref, row0_ref, pad_ref, nxt_ref, x_hbm, wu_hbm, bu_ref, wd_hbm, bd_ref, y_hbm,
                    xbuf, ybuf, wu_st, wd_st, wu_bf, wd_bf, ysz, xsem, ysem, wsem):
    def w_fetch(e):
        return (pltpu.make_async_copy(wu_hbm.at[e], wu_st, wsem.at[0]),
                pltpu.make_async_copy(wd_hbm.at[e], wd_st, wsem.at[1]))

    def x_fetch(row, size, slot):
        rows = pl.ds(pl.multiple_of(row, SLOT_PAD), size)
        return pltpu.make_async_copy(x_hbm.at[rows, :], xbuf.at[slot, pl.ds(0, size), :], xsem.at[slot])

    def y_store(row, size, slot):
        rows = pl.ds(pl.multiple_of(row, SLOT_PAD), size)
        return pltpu.make_async_copy(ybuf.at[slot, pl.ds(0, size), :], y_hbm.at[rows, :], ysem.at[slot])

    def y_wait(slot):
        for size in (BIG_BLOCK, SLOT_PAD):
            @pl.when(ysz[slot] == size)
            def _():
                y_store(0, size, slot).wait()

    def fetch_first(e, slot):
        @pl.when(pad_ref[e] >= BIG_BLOCK)
        def _():
            x_fetch(row0_ref[e], BIG_BLOCK, slot).start()

        @pl.when((pad_ref[e] > 0) & (pad_ref[e] < BIG_BLOCK))
        def _():
            x_fetch(row0_ref[e], SLOT_PAD, slot).start()

    ysz[0] = 0
    ysz[1] = 0
    for c in w_fetch(0):
        c.start()
    fetch_first(jnp.where(pad_ref[0] > 0, 0, nxt_ref[0]), 0)

    def expert_body(e, n_done):
        for c in w_fetch(e):
            c.wait()
        padded = pad_ref[e]

        @pl.when(padded > 0)
        def _():
            wu_bf[...] = wu_st[...].astype(_bf16)
            wd_bf[...] = wd_st[...].astype(_bf16)

        @pl.when(e + 1 < E)
        def _():
            for c in w_fetch(e + 1):
                c.start()

        b_up = bu_ref[e]
        b_down = bd_ref[e]
        row0 = row0_ref[e]
        n_big = padded // BIG_BLOCK
        n_small = (padded - n_big * BIG_BLOCK) // SLOT_PAD
        small0 = row0 + n_big * BIG_BLOCK
        nxt = nxt_ref[e]

        def pass_body(size, j, n_done):
            slot = n_done % 2
            big = size == BIG_BLOCK
            row = row0 + j * BIG_BLOCK if big else small0 + j * SLOT_PAD
            x_fetch(row, size, slot).wait()

            more = j + 1 < (n_big if big else n_small)
            tail = (n_small > 0) if big else False

            @pl.when(more)
            def _():
                x_fetch(row + size, size, 1 - slot).start()

            if big:
                @pl.when(jnp.logical_not(more) & tail)
                def _():
                    x_fetch(small0, SLOT_PAD, 1 - slot).start()

            @pl.when(jnp.logical_not(more) & jnp.logical_not(tail) & (nxt < E))
            def _():
                fetch_first(jnp.minimum(nxt, E - 1), 1 - slot)

            rows = lax.broadcasted_iota(jnp.int32, (size, 1), 0)
            x = _unpack_rows(jnp.where(rows < cnt_ref[e] - (row - row0), xbuf[slot, pl.ds(0, size), :], 0))
            hcat = _dot(x, wu_bf[...]) + b_up
            glu = jnp.minimum(hcat[:, 0:F], SWIGLU_LIMIT)
            lin = jnp.clip(hcat[:, F:2 * F], -SWIGLU_LIMIT, SWIGLU_LIMIT)
            act = glu * _sigmoid(SWIGLU_ALPHA * glu) * (lin + 1.0)
            y = _pack_rows(_dot(act.astype(_bf16), wd_bf[...]) + b_down)

            y_wait(slot)
            ybuf[slot, pl.ds(0, size), :] = y
            y_store(row, size, slot).start()
            ysz[slot] = size
            return n_done + 1

        n_done = lax.fori_loop(0, n_big, functools.partial(pass_body, BIG_BLOCK), n_done)
        return lax.fori_loop(0, n_small, functools.partial(pass_body, SLOT_PAD), n_done)

    lax.fori_loop(0, E, expert_body, 0)
    y_wait(0)
    y_wait(1)


def _experts(cnt, row0, padded, nxt, xpad, w_up, b_up, w_down, b_down):
    def full(a):
        nd = a.ndim
        return pl.BlockSpec(a.shape, lambda i, *_: (0,) * nd)

    return pl.pallas_call(
        _experts_kernel,
        out_shape=jax.ShapeDtypeStruct(xpad.shape, jnp.int32),
        grid_spec=pltpu.PrefetchScalarGridSpec(
            num_scalar_prefetch=4,
            grid=(1,),
            in_specs=[pl.BlockSpec(memory_space=pl.ANY),
                      pl.BlockSpec(memory_space=pl.ANY), full(b_up),
                      pl.BlockSpec(memory_space=pl.ANY), full(b_down)],
            out_specs=pl.BlockSpec(memory_space=pl.ANY),
            scratch_shapes=[pltpu.VMEM((2, BIG_BLOCK, D // 2), jnp.int32),
                            pltpu.VMEM((2, BIG_BLOCK, D // 2), jnp.int32),
                            pltpu.VMEM((D, 2 * F), _f32), pltpu.VMEM((F, D), _f32),
                            pltpu.VMEM((D, 2 * F), _bf16), pltpu.VMEM((F, D), _bf16),
                            pltpu.SMEM((2,), jnp.int32),
                            pltpu.SemaphoreType.DMA((2,)), pltpu.SemaphoreType.DMA((2,)),
                            pltpu.SemaphoreType.DMA((2,))],
        ),
        compiler_params=pltpu.CompilerParams(vmem_limit_bytes=VMEM_LIMIT),
        name="experts",
    )(cnt, row0, padded, nxt, xpad, w_up, b_up, w_down, b_down)


def _gather_rows(ypad, dest_all):
    n_out = dest_all.shape[0]
    width = ypad.shape[1]
    w = SC_WINDOW
    mesh = plsc.VectorSubcoreMesh(core_axis_name="core", subcore_axis_name="subcore")
    n_workers = mesh.num_cores * mesh.num_subcores
    n_windows = n_out // w

    @functools.partial(
        pl.kernel, mesh=mesh, name="gather_rows",
        out_type=jax.ShapeDtypeStruct((n_out, width), jnp.int32),
        scratch_types=[pltpu.VMEM((w, width), jnp.int32), pltpu.VMEM((w,), jnp.int32), pltpu.SemaphoreType.DMA])
    def gather_rows(y_hbm, d_hbm, o_hbm, rows, iv, sem):
        worker = lax.axis_index("subcore") * mesh.num_cores + lax.axis_index("core")

        @pl.loop(0, n_windows // n_workers)
        def _(j):
            r0 = pl.multiple_of((j * n_workers + worker) * w, w)
            pltpu.sync_copy(d_hbm.at[pl.ds(r0, w)], iv)
            pltpu.async_copy(y_hbm.at[iv], rows, sem).wait()
            pltpu.sync_copy(rows, o_hbm.at[pl.ds(r0, w)])

    assert n_windows % n_workers == 0
    return gather_rows(ypad, dest_all)


def _combine_kernel(h_ref, mod_ref, gate_ref, g2_ref, b2_ref, y_ref, *aliased_and_out):
    o_ref = aliased_and_out[-1]
    gates = gate_ref[...]
    half = D // 2
    f_hi = jnp.zeros((h_ref.shape[0], half), _f32)
    f_lo = jnp.zeros((h_ref.shape[0], half), _f32)
    for k in range(TOP_K):
        p = y_ref[k]
        gk = gates[:, k:k + 1]
        f_hi = f_hi + gk * lax.bitcast_convert_type(p & jnp.int32(-65536), _f32)
        f_lo = f_lo + gk * lax.bitcast_convert_type(lax.shift_left(p, 16), _f32)
    f = jnp.concatenate([f_hi, f_lo], axis=1)
    gate2 = mod_ref[...][:, 5 * D:6 * D]
    pre = ALPHA * h_ref[...] + _per_seq(f, gate2, jnp.multiply)
    o_ref[...] = _layer_norm(pre, g2_ref[...], b2_ref[...])


def _combine(h, mod3, gates, ln2_g, ln2_b, y4, token0, rows_per_mod, row0, out_so_far):
    t = COMBINE_TILE
    hoff = row0 // t
    goff = (token0 + row0) // t
    g = mod3.shape[1]
    tiles_per_mod = rows_per_mod // t
    in_specs = [pl.BlockSpec((t, D), lambda i: (i + hoff, 0)),
                pl.BlockSpec((None, g, 6 * D), lambda i: ((i + hoff) // tiles_per_mod, 0, 0)),
                pl.BlockSpec((t, LANES), lambda i: (i + goff, 0)),
                pl.BlockSpec((1, D), lambda i: (0, 0)),
                pl.BlockSpec((1, D), lambda i: (0, 0)),
                pl.BlockSpec((TOP_K, t, D // 2), lambda i: (0, i, 0))]
    args = [h, mod3, gates, ln2_g, ln2_b, y4]
    aliases = {}
    if out_so_far is not None:
        in_specs.append(pl.BlockSpec(memory_space=pl.ANY))
        args.append(out_so_far)
        aliases = {len(args) - 1: 0}
    return pl.pallas_call(
        _combine_kernel,
        out_shape=jax.ShapeDtypeStruct(h.shape, _f32),
        grid=(y4.shape[1] // t,),
        in_specs=in_specs,
        out_specs=pl.BlockSpec((t, D), lambda i: (i + hoff, 0)),
        input_output_aliases=aliases,
        compiler_params=pltpu.CompilerParams(vmem_limit_bytes=VMEM_LIMIT),
        name="combine",
    )(*args)


def _time_major(a):
    return a.transpose(1, 0, 2)


def kernel(x_prompt, x_sample, c_prompt, c_sample, state_conv, state_pool, w_ada, b_ada, w_in,
           conv_w, w_out_a, w_pool, ls_pool, w_out_b, w_o, ln1_g, ln1_b, w_router, b_router,
           w_up, b_up, w_down, b_down, ln2_g, ln2_b):
    n_seq_p, seq, _ = x_prompt.shape
    n_seq_s, dec_seq, _ = x_sample.shape
    n_p, n_s = n_seq_p * seq, n_seq_s * dec_seq
    n = n_p + n_s
    n_slots = TOP_K * n + E * SLOT_PAD
    l = 0

    mod = _ada(jnp.concatenate([c_prompt, c_sample], axis=0), w_ada[l], b_ada[l][None])
    mod_p = mod[:n_seq_p][:, None, :]
    mod_s = mod[n_seq_p:][None]

    weights = (
        w_in[l].astype(_bf16), conv_w[l], w_out_a[l].astype(_bf16), w_pool[l].astype(_bf16),
        ls_pool[l][None], w_out_b[l].astype(_bf16), w_o[l].astype(_bf16), ln1_g[l][None], ln1_b[l][None],
        jnp.pad(w_router[l], ((0, 0), (0, LANES - E))), jnp.pad(b_router[l], (0, LANES - E))[None],
    )
    hc_p, hp_p = _hist_steps(CONV_HIST, 1), _hist_steps(POOL_HIST, 1)
    h_p, v_p, lg_p, nc_p, np_p = _mixer(
        x_prompt.reshape(n_p, D), mod_p, jnp.zeros((n_seq_p * hc_p, C), _f32),
        jnp.zeros((n_seq_p * hp_p, C), _f32), weights, PROMPT_ROW_TILE, 0)
    hc_s, hp_s = _hist_steps(CONV_HIST, n_seq_s), _hist_steps(POOL_HIST, n_seq_s)
    hist_c = jnp.pad(_time_major(state_conv[l]), ((hc_s - CONV_HIST, 0), (0, 0), (0, 0)))
    hist_p = jnp.pad(_time_major(state_pool[l]), ((hp_s - POOL_HIST, 0), (0, 0), (0, 0)))
    h_s, v_s, lg_s, nc_s, np_s = _mixer(
        _time_major(x_sample).reshape(n_s, D), mod_s, hist_c.reshape(hc_s * n_seq_s, C),
        hist_p.reshape(hp_s * n_seq_s, C), weights, SAMPLE_ROW_TILE, PAST_LEN)

    dest8, gates3, meta = _plan(lg_p, lg_s)
    cnt, row0, padded, nxt = (meta[i, :E] for i in range(4))
    gates = gates3.reshape(n, LANES)

    dests = [dest8[:, k, :].reshape(n) for k in range(TOP_K)]
    xpad = _dispatch(v_p, v_s, dests, n_slots)
    ypad = _experts(cnt, row0, padded, nxt, xpad, w_up[l], b_up[l][:, None, :], w_down[l], b_down[l][:, None, :])

    def gathered(t0, rows):
        idx = jnp.concatenate([dk[t0:t0 + rows] for dk in dests])
        return _gather_rows(ypad, idx).reshape(TOP_K, rows, D // 2)

    chunk = n_p // COMBINE_CHUNKS
    y4_p = [gathered(c * chunk, chunk) for c in range(COMBINE_CHUNKS)]
    y4_s = gathered(n_p, n_s)
    g2, b2 = ln2_g[l][None], ln2_b[l][None]
    y_p = None
    for c in range(COMBINE_CHUNKS):
        y_p = _combine(h_p, mod_p, gates, g2, b2, y4_p[c], 0, seq, c * chunk, y_p)
    y_s = _combine(h_s, mod_s, gates, g2, b2, y4_s, n_p, n_s, 0, None)

    y_prompt = y_p.reshape(n_seq_p, seq, D)
    y_sample = _time_major(y_s.reshape(dec_seq, n_seq_s, D))
    new_conv_p = nc_p.reshape(n_seq_p, hc_p, C)[:, hc_p - CONV_HIST:][None]
    new_pool_p = np_p.reshape(n_seq_p, hp_p, C)[:, hp_p - POOL_HIST:][None]
    new_conv_s = _time_major(nc_s.reshape(hc_s, n_seq_s, C)[hc_s - CONV_HIST:])[None]
    new_pool_s = _time_major(np_s.reshape(hp_s, n_seq_s, C)[hp_s - POOL_HIST:])[None]
    return (y_prompt, y_sample, new_conv_p, new_pool_p, new_conv_s, new_pool_s)
```

```python
import functools

import jax
import jax.numpy as jnp
from jax import lax
from jax.experimental import pallas as pl
from jax.experimental.pallas import tpu as pltpu
from jax.experimental.pallas import tpu_sc as plsc

D = 1024
C = 512
N_GROUPS = 4
GROUP = C // N_GROUPS
CONV_HIST = 2
POOL_HIST = 15
E = 32
TOP_K = 4
F = 1024
SWIGLU_LIMIT = 7.0
SWIGLU_ALPHA = 1.702
LN_EPS = 1e-5
DEPTH = 1
ALPHA = (2 * DEPTH) ** 0.25
PAST_LEN = 16384

LANES = 128
SUBLANES = 8
ROW_TILE = 512
PROMPT_ROW_TILE = 1024
SAMPLE_ROW_TILE = 512
SLOT_PAD = 256
BIG_BLOCK = 1024
COMBINE_TILE = 512
COMBINE_CHUNKS = 8
SC_WINDOW = 128
DISPATCH_WINDOW = 64
ADA_COLS = 1536
VMEM_LIMIT = 56 * 1024 * 1024

_f32 = jnp.float32
_bf16 = jnp.bfloat16


def _dot(a, b):
    return jnp.dot(a, b, preferred_element_type=_f32)


def _dot_exact(a, b):
    return lax.dot_general(a, b, (((1,), (0,)), ((), ())),
                           precision=lax.Precision.HIGHEST, preferred_element_type=_f32)


def _pack_rows(x):
    w = x.shape[1] // 2
    hi = lax.bitcast_convert_type(x[:, :w].astype(_bf16).astype(_f32), jnp.int32)
    lo = lax.bitcast_convert_type(x[:, w:].astype(_bf16).astype(_f32), jnp.int32)
    return hi | lax.shift_right_logical(lo, 16)


def _unpack_rows(p):
    hi = lax.bitcast_convert_type(p & jnp.int32(-65536), _f32)
    lo = lax.bitcast_convert_type(lax.shift_left(p, 16), _f32)
    return jnp.concatenate([hi, lo], axis=1).astype(_bf16)


def _sigmoid(x):
    return 0.5 * jnp.tanh(0.5 * x) + 0.5


def _per_seq(x, m, op):
    g = m.shape[0]
    if g == 1:
        return op(x, m)
    r, n = x.shape
    return op(x.reshape(r // g, g, n), m[None]).reshape(r, n)


def _layer_norm(x, g, b):
    mu = jnp.mean(x, axis=-1, keepdims=True)
    xc = x - mu
    var = jnp.mean(xc * xc, axis=-1, keepdims=True)
    return xc * lax.rsqrt(var + LN_EPS) * g + b


def _hist_steps(needed, g):
    return -(-needed * g // SUBLANES) * SUBLANES // g


def _ada_kernel(c_ref, w_ref, b_ref, o_ref):
    c = c_ref[...]
    o_ref[...] = _dot((c * _sigmoid(c)).astype(_bf16), w_ref[...].astype(_bf16)) + b_ref[...]


def _ada(c, w_ada, b_ada):
    rows = c.shape[0]
    cols = w_ada.shape[1]
    bn = ADA_COLS
    return pl.pallas_call(
        _ada_kernel,
        out_shape=jax.ShapeDtypeStruct((rows, cols), _f32),
        grid=(cols // bn,),
        in_specs=[pl.BlockSpec((rows, D), lambda j: (0, 0)),
                  pl.BlockSpec((D, bn), lambda j: (0, j)),
                  pl.BlockSpec((1, bn), lambda j: (0, j))],
        out_specs=pl.BlockSpec((rows, bn), lambda j: (0, j)),
        compiler_params=pltpu.CompilerParams(vmem_limit_bytes=VMEM_LIMIT),
        name="ada",
    )(c, w_ada, b_ada)


def _mixer_kernel(g, tiles_per_seq, start_pos,
                  x_ref, mod_ref, hc_ref, hp_ref,
                  win_ref, cw_ref, woa_ref, wpool_ref, ls_ref, wob_ref, wo_ref, g1_ref, b1_ref,
                  wr_ref, br_ref,
                  h_ref, v_ref, lg_ref, nc_ref, np_ref, zbuf, pbuf):
    r = x_ref.shape[0]
    hrc = hc_ref.shape[0]
    hrp = hp_ref.shape[0]
    j = pl.program_id(0) % tiles_per_seq

    @pl.when(j == 0)
    def _():
        zbuf[pl.ds(0, hrc), :] = hc_ref[...]
        pbuf[pl.ds(0, hrp), :] = hp_ref[...]

    @pl.when(j != 0)
    def _():
        zt = zbuf[pl.ds(r, hrc), :]
        pt = pbuf[pl.ds(r, hrp), :]
        zbuf[pl.ds(0, hrc), :] = zt
        pbuf[pl.ds(0, hrp), :] = pt

    m = mod_ref[...]
    shift1, scale1, gate1 = m[:, 0:D], m[:, D:2 * D], m[:, 2 * D:3 * D]
    shift2, scale2 = m[:, 3 * D:4 * D], m[:, 4 * D:5 * D]

    x = x_ref[...]
    u = _per_seq(_per_seq(x, 1.0 + scale1, jnp.multiply), shift1, jnp.add).astype(_bf16)

    z = _dot(u, win_ref[:, C:2 * C]) * _dot(u, win_ref[:, 2 * C:3 * C])
    zbuf[pl.ds(hrc, r), :] = z
    cw = cw_ref[...]
    conv = (cw[0:1] * zbuf[pl.ds(hrc - 2 * g, r), :] + cw[1:2] * zbuf[pl.ds(hrc - g, r), :]
            + cw[2:3] * z)
    y_a = _dot((_dot(u, win_ref[:, 0:C]) * conv).astype(_bf16), woa_ref[...])

    xp = _dot(u, win_ref[:, 3 * C:4 * C])
    pbuf[pl.ds(hrp, r), :] = xp
    pos = start_pos + j * (r // g) + lax.broadcasted_iota(jnp.int32, (r, 1), 0) // g
    acc = xp
    yg = []
    for grp in range(N_GROUPS):
        lo = grp * GROUP
        wdw = 2 ** (grp + 1)
        for back in range(wdw // 2, wdw):
            sh = pbuf[pl.ds(hrp - back * g, r), lo:C]
            acc = jnp.concatenate([acc[:, 0:lo], acc[:, lo:C] + sh], axis=1) if lo else acc + sh
        inv_cnt = 1.0 / jnp.minimum(wdw, pos + 1).astype(_f32)
        diff = acc[:, lo:lo + GROUP] * inv_cnt - xp[:, lo:lo + GROUP]
        yg.append(_dot(diff.astype(_bf16), wpool_ref[grp]))
    y_b = _dot((jnp.concatenate(yg, axis=1) * ls_ref[...]).astype(_bf16), wob_ref[...])

    g_a = _dot(u, win_ref[:, 4 * C:4 * C + D])
    g_b = _dot(u, win_ref[:, 4 * C + D:4 * C + 2 * D])
    merged = _sigmoid(g_a) * y_a + _sigmoid(g_b) * y_b
    o = _dot(merged.astype(_bf16), wo_ref[...])
    h = _layer_norm(ALPHA * x + _per_seq(o, gate1, jnp.multiply), g1_ref[...], b1_ref[...])
    v = _per_seq(_per_seq(h, 1.0 + scale2, jnp.multiply), shift2, jnp.add)
    h_ref[...] = h
    v_ref[...] = _pack_rows(v)
    lg_ref[...] = _dot(v.astype(_bf16), wr_ref[...].astype(_bf16)) + br_ref[...]
    nc_ref[...] = zbuf[pl.ds(r, hrc), :]
    np_ref[...] = pbuf[pl.ds(r, hrp), :]


def _mixer(x2, mod3, hc, hp, weights, row_tile, start_pos):
    n = x2.shape[0]
    n_mod, g, _ = mod3.shape
    hrc, hrp = hc.shape[0] // n_mod, hp.shape[0] // n_mod
    tiles_per_seq = n // n_mod // row_tile
    once = dict(pipeline_mode=pl.Buffered(1)) if n_mod == 1 else {}

    def full(a):
        nd = a.ndim
        return pl.BlockSpec(a.shape, lambda i: (0,) * nd)

    def seq_block(rows, **kw):
        return pl.BlockSpec((rows, C), lambda i: (i // tiles_per_seq, 0), **kw)

    def row_block(cols):
        return pl.BlockSpec((row_tile, cols), lambda i: (i, 0))

    return pl.pallas_call(
        functools.partial(_mixer_kernel, g, tiles_per_seq, start_pos),
        out_shape=[
            jax.ShapeDtypeStruct((n, D), _f32),
            jax.ShapeDtypeStruct((n, D // 2), jnp.int32),
            jax.ShapeDtypeStruct((n, LANES), _f32),
            jax.ShapeDtypeStruct(hc.shape, _f32),
            jax.ShapeDtypeStruct(hp.shape, _f32),
        ],
        grid=(n // row_tile,),
        in_specs=[row_block(D),
                  pl.BlockSpec((None, g, 6 * D), lambda i: (i // tiles_per_seq, 0, 0), **once),
                  seq_block(hrc, **once), seq_block(hrp, **once)] + [full(a) for a in weights],
        out_specs=[row_block(D), row_block(D // 2), row_block(LANES), seq_block(hrc), seq_block(hrp)],
        scratch_shapes=[pltpu.VMEM((hrc + row_tile, C), _f32), pltpu.VMEM((hrp + row_tile, C), _f32)],
        compiler_params=pltpu.CompilerParams(vmem_limit_bytes=VMEM_LIMIT),
        name="mixer",
    )(x2, mod3, hc, hp, *weights)


def _plan_kernel(lgp_ref, lgs_ref, dest_ref, gate_ref, meta_ref, idx_s, rank_s):
    t = ROW_TILE
    e_iota = lax.broadcasted_iota(jnp.int32, (E, t), 0)
    tri = (lax.broadcasted_iota(jnp.int32, (t, t), 0)
           < lax.broadcasted_iota(jnp.int32, (t, t), 1)).astype(_f32).astype(_bf16)
    zeros_rest = jnp.zeros((LANES - TOP_K, t), _f32)

    def tile_body(lg_ref, off, i, carry):
        lt = lg_ref[i].T[0:E, :]
        vals, idxs = [], []
        for _ in range(TOP_K):
            mx = jnp.max(lt, axis=0, keepdims=True)
            ix = jnp.min(jnp.where(lt == mx, e_iota, E), axis=0, keepdims=True)
            vals.append(mx)
            idxs.append(ix)
            lt = jnp.where(e_iota == ix, -jnp.inf, lt)
        ex = [jnp.exp(vk - vals[0]) for vk in vals]
        den = ex[0] + ex[1] + ex[2] + ex[3]
        gates = [ek / den for ek in ex]
        gate_ref[off + i] = jnp.concatenate(gates + [zeros_rest], axis=0).T

        ohs = [(e_iota == ix) for ix in idxs]
        oh = (ohs[0] | ohs[1] | ohs[2] | ohs[3]).astype(_f32)
        before = _dot(oh.astype(_bf16), tri) + carry
        ranks = [jnp.sum(jnp.where(o, before, 0.0), axis=0, keepdims=True) for o in ohs]
        idx_s[off + i] = jnp.concatenate(idxs + idxs, axis=0)
        rank_s[off + i] = jnp.concatenate(ranks + ranks, axis=0).astype(jnp.int32)
        return carry + jnp.sum(oh, axis=1, keepdims=True)

    n_p, n_s = lgp_ref.shape[0], lgs_ref.shape[0]
    counts = lax.fori_loop(0, n_p, functools.partial(tile_body, lgp_ref, 0), jnp.zeros((E, 1), _f32))
    counts = lax.fori_loop(0, n_s, functools.partial(tile_body, lgs_ref, n_p), counts)
    padded = jnp.ceil(counts / SLOT_PAD) * SLOT_PAD
    low = (lax.broadcasted_iota(jnp.int32, (E, E), 1)
           <= lax.broadcasted_iota(jnp.int32, (E, E), 0)).astype(_f32)
    pad_end = _dot_exact(low, jnp.broadcast_to(padded, (E, LANES)))[:, 0:1]
    pad_start = pad_end - padded

    def dest_body(i, c):
        ix = idx_s[i]
        rk = rank_s[i]
        rows = []
        for k in range(TOP_K):
            st = jnp.sum(jnp.where(e_iota == ix[k:k + 1], pad_start, 0.0), axis=0, keepdims=True)
            rows.append(st.astype(jnp.int32) + rk[k:k + 1])
        dest_ref[i] = jnp.concatenate(rows + rows, axis=0)
        return c

    lax.fori_loop(0, n_p + n_s, dest_body, 0)

    sub = lax.broadcasted_iota(jnp.int32, (E, LANES), 0)
    lane = lax.broadcasted_iota(jnp.int32, (E, LANES), 1)

    def to_lanes(col):
        return jnp.sum(jnp.where(sub == lane, col, 0.0), axis=0, keepdims=True).astype(jnp.int32)

    later = jnp.min(jnp.where((sub > lane) & (counts > 0.0), sub, E), axis=0, keepdims=True)
    meta_ref[...] = jnp.concatenate(
        [to_lanes(counts), to_lanes(pad_start), to_lanes(padded), later,
         jnp.zeros((SUBLANES - 4, LANES), jnp.int32)], axis=0)


def _plan(logits_p, logits_s):
    n_tiles = (logits_p.shape[0] + logits_s.shape[0]) // ROW_TILE
    return pl.pallas_call(
        _plan_kernel,
        out_shape=[
            jax.ShapeDtypeStruct((n_tiles, 2 * TOP_K, ROW_TILE), jnp.int32),
            jax.ShapeDtypeStruct((n_tiles, ROW_TILE, LANES), _f32),
            jax.ShapeDtypeStruct((SUBLANES, LANES), jnp.int32),
        ],
        scratch_shapes=[pltpu.VMEM((n_tiles, 2 * TOP_K, ROW_TILE), jnp.int32),
                        pltpu.VMEM((n_tiles, 2 * TOP_K, ROW_TILE), jnp.int32)],
        compiler_params=pltpu.CompilerParams(vmem_limit_bytes=VMEM_LIMIT),
        name="plan",
    )(logits_p.reshape(-1, ROW_TILE, LANES), logits_s.reshape(-1, ROW_TILE, LANES))


def _dispatch(v_p, v_s, dests, n_rows_out):
    n_p, n_s = v_p.shape[0], v_s.shape[0]
    width = v_p.shape[1]
    w = DISPATCH_WINDOW
    n_pw, n_windows = n_p // w, (n_p + n_s) // w
    mesh = plsc.VectorSubcoreMesh(core_axis_name="core", subcore_axis_name="subcore")
    n_workers = mesh.num_cores * mesh.num_subcores
    n_rounds = -(-n_windows // n_workers)

    @functools.partial(
        pl.kernel, mesh=mesh, name="dispatch",
        out_type=jax.ShapeDtypeStruct((n_rows_out, width), jnp.int32),
        scratch_types=[pltpu.VMEM((w, width), jnp.int32)] * 2 + [pltpu.VMEM((w,), jnp.int32)] * (2 * TOP_K)
        + [pltpu.SemaphoreType.DMA] * 3)
    def scatter_rows(vp_hbm, vs_hbm, d0_hbm, d1_hbm, d2_hbm, d3_hbm, o_hbm, rows0, rows1, *rest):
        worker = lax.axis_index("subcore") * mesh.num_cores + lax.axis_index("core")
        rows = (rows0, rows1)
        idx = (rest[:TOP_K], rest[TOP_K:2 * TOP_K])
        lsem, ssem = rest[2 * TOP_K:2 * TOP_K + 2], rest[2 * TOP_K + 2]
        d_hbm = (d0_hbm, d1_hbm, d2_hbm, d3_hbm)

        def load(c, slot, act):
            def index_reads():
                t0 = pl.multiple_of(c * w, w)
                for k in range(TOP_K):
                    act(pltpu.make_async_copy(d_hbm[k].at[pl.ds(t0, w)], idx[slot][k], lsem[slot]))

            @pl.when(c < n_pw)
            def _():
                src = vp_hbm.at[pl.ds(pl.multiple_of(c * w, w), w)]
                act(pltpu.make_async_copy(src, rows[slot], lsem[slot]))
                index_reads()

            @pl.when((c >= n_pw) & (c < n_windows))
            def _():
                src = vs_hbm.at[pl.ds(pl.multiple_of((c - n_pw) * w, w), w)]
                act(pltpu.make_async_copy(src, rows[slot], lsem[slot]))
                index_reads()

        load(worker, 0, lambda cp: cp.start())
        for j in range(n_rounds):
            c = j * n_workers + worker
            slot = j % 2
            load(c, slot, lambda cp: cp.wait())
            if j + 1 < n_rounds:
                load(c + n_workers, 1 - slot, lambda cp: cp.start())

            @pl.when(c < n_windows)
            def _():
                copies = [pltpu.async_copy(rows[slot], o_hbm.at[iv], ssem) for iv in idx[slot]]
                for cp in copies:
                    cp.wait()

    return scatter_rows(v_p, v_s, *dests)


def _experts_kernel(cnt_ref, row0_ref, pad_ref, nxt_ref, x_hbm, wu_hbm, bu_ref, wd_hbm, bd_ref, y_hbm,
                    xbuf, ybuf, wu_st, wd_st, wu_bf, wd_bf, ysz, xsem, ysem, wsem):
    def w_fetch(e):
        return (pltpu.make_async_copy(wu_hbm.at[e], wu_st, wsem.at[0]),
                pltpu.make_async_copy(wd_hbm.at[e], wd_st, wsem.at[1]))

    def x_fetch(row, size, slot):
        rows = pl.ds(pl.multiple_of(row, SLOT_PAD), size)
        return pltpu.make_async_copy(x_hbm.at[rows, :], xbuf.at[slot, pl.ds(0, size), :], xsem.at[slot])

    def y_store(row, size, slot):
        rows = pl.ds(pl.multiple_of(row, SLOT_PAD), size)
        return pltpu.make_async_copy(ybuf.at[slot, pl.ds(0, size), :], y_hbm.at[rows, :], ysem.at[slot])

    def y_wait(slot):
        for size in (BIG_BLOCK, SLOT_PAD):
            @pl.when(ysz[slot] == size)
            def _():
                y_store(0, size, slot).wait()

    def fetch_first(e, slot):
        @pl.when(pad_ref[e] >= BIG_BLOCK)
        def _():
            x_fetch(row0_ref[e], BIG_BLOCK, slot).start()

        @pl.when((pad_ref[e] > 0) & (pad_ref[e] < BIG_BLOCK))
        def _():
            x_fetch(row0_ref[e], SLOT_PAD, slot).start()

    ysz[0] = 0
    ysz[1] = 0
    for c in w_fetch(0):
        c.start()
    fetch_first(jnp.where(pad_ref[0] > 0, 0, nxt_ref[0]), 0)

    def expert_body(e, n_done):
        for c in w_fetch(e):
            c.wait()
        padded = pad_ref[e]

        @pl.when(padded > 0)
        def _():
            wu_bf[...] = wu_st[...].astype(_bf16)
            wd_bf[...] = wd_st[...].astype(_bf16)

        @pl.when(e + 1 < E)
        def _():
            for c in w_fetch(e + 1):
                c.start()

        b_up = bu_ref[e]
        b_down = bd_ref[e]
        row0 = row0_ref[e]
        n_big = padded // BIG_BLOCK
        n_small = (padded - n_big * BIG_BLOCK) // SLOT_PAD
        small0 = row0 + n_big * BIG_BLOCK
        nxt = nxt_ref[e]

        def pass_body(size, j, n_done):
            slot = n_done % 2
            big = size == BIG_BLOCK
            row = row0 + j * BIG_BLOCK if big else small0 + j * SLOT_PAD
            x_fetch(row, size, slot).wait()

            more = j + 1 < (n_big if big else n_small)
            tail = (n_small > 0) if big else False

            @pl.when(more)
            def _():
                x_fetch(row + size, size, 1 - slot).start()

            if big:
                @pl.when(jnp.logical_not(more) & tail)
                def _():
                    x_fetch(small0, SLOT_PAD, 1 - slot).start()

            @pl.when(jnp.logical_not(more) & jnp.logical_not(tail) & (nxt < E))
            def _():
                fetch_first(jnp.minimum(nxt, E - 1), 1 - slot)

            rows = lax.broadcasted_iota(jnp.int32, (size, 1), 0)
            x = _unpack_rows(jnp.where(rows < cnt_ref[e] - (row - row0), xbuf[slot, pl.ds(0, size), :], 0))
            hcat = _dot(x, wu_bf[...]) + b_up
            glu = jnp.minimum(hcat[:, 0:F], SWIGLU_LIMIT)
            lin = jnp.clip(hcat[:, F:2 * F], -SWIGLU_LIMIT, SWIGLU_LIMIT)
            act = glu * _sigmoid(SWIGLU_ALPHA * glu) * (lin + 1.0)
            y = _pack_rows(_dot(act.astype(_bf16), wd_bf[...]) + b_down)

            y_wait(slot)
            ybuf[slot, pl.ds(0, size), :] = y
            y_store(row, size, slot).start()
            ysz[slot] = size
            return n_done + 1

        n_done = lax.fori_loop(0, n_big, functools.partial(pass_body, BIG_BLOCK), n_done)
        return lax.fori_loop(0, n_small, functools.partial(pass_body, SLOT_PAD), n_done)

    lax.fori_loop(0, E, expert_body, 0)
    y_wait(0)
    y_wait(1)


def _experts(cnt, row0, padded, nxt, xpad, w_up, b_up, w_down, b_down):
    def full(a):
        nd = a.ndim
        return pl.BlockSpec(a.shape, lambda i, *_: (0,) * nd)

    return pl.pallas_call(
        _experts_kernel,
        out_shape=jax.ShapeDtypeStruct(xpad.shape, jnp.int32),
        grid_spec=pltpu.PrefetchScalarGridSpec(
            num_scalar_prefetch=4,
            grid=(1,),
            in_specs=[pl.BlockSpec(memory_space=pl.ANY),
                      pl.BlockSpec(memory_space=pl.ANY), full(b_up),
                      pl.BlockSpec(memory_space=pl.ANY), full(b_down)],
            out_specs=pl.BlockSpec(memory_space=pl.ANY),
            scratch_shapes=[pltpu.VMEM((2, BIG_BLOCK, D // 2), jnp.int32),
                            pltpu.VMEM((2, BIG_BLOCK, D // 2), jnp.int32),
                            pltpu.VMEM((D, 2 * F), _f32), pltpu.VMEM((F, D), _f32),
                            pltpu.VMEM((D, 2 * F), _bf16), pltpu.VMEM((F, D), _bf16),
                            pltpu.SMEM((2,), jnp.int32),
                            pltpu.SemaphoreType.DMA((2,)), pltpu.SemaphoreType.DMA((2,)),
                            pltpu.SemaphoreType.DMA((2,))],
        ),
        compiler_params=pltpu.CompilerParams(vmem_limit_bytes=VMEM_LIMIT),
        name="experts",
    )(cnt, row0, padded, nxt, xpad, w_up, b_up, w_down, b_down)


def _gather_rows(ypad, dest_all):
    n_out = dest_all.shape[0]
    width = ypad.shape[1]
    w = SC_WINDOW
    mesh = plsc.VectorSubcoreMesh(core_axis_name="core", subcore_axis_name="subcore")
    n_workers = mesh.num_cores * mesh.num_subcores
    n_windows = n_out // w

    @functools.partial(
        pl.kernel, mesh=mesh, name="gather_rows",
        out_type=jax.ShapeDtypeStruct((n_out, width), jnp.int32),
        scratch_types=[pltpu.VMEM((w, width), jnp.int32), pltpu.VMEM((w,), jnp.int32), pltpu.SemaphoreType.DMA])
    def gather_rows(y_hbm, d_hbm, o_hbm, rows, iv, sem):
        worker = lax.axis_index("subcore") * mesh.num_cores + lax.axis_index("core")

        @pl.loop(0, n_windows // n_workers)
        def _(j):
            r0 = pl.multiple_of((j * n_workers + worker) * w, w)
            pltpu.sync_copy(d_hbm.at[pl.ds(r0, w)], iv)
            pltpu.async_copy(y_hbm.at[iv], rows, sem).wait()
            pltpu.sync_copy(rows, o_hbm.at[pl.ds(r0, w)])

    assert n_windows % n_workers == 0
    return gather_rows(ypad, dest_all)


def _combine_kernel(h_ref, mod_ref, gate_ref, g2_ref, b2_ref, y_ref, *aliased_and_out):
    o_ref = aliased_and_out[-1]
    gates = gate_ref[...]
    half = D // 2
    f_hi = jnp.zeros((h_ref.shape[0], half), _f32)
    f_lo = jnp.zeros((h_ref.shape[0], half), _f32)
    for k in range(TOP_K):
        p = y_ref[k]
        gk = gates[:, k:k + 1]
        f_hi = f_hi + gk * lax.bitcast_convert_type(p & jnp.int32(-65536), _f32)
        f_lo = f_lo + gk * lax.bitcast_convert_type(lax.shift_left(p, 16), _f32)
    f = jnp.concatenate([f_hi, f_lo], axis=1)
    gate2 = mod_ref[...][:, 5 * D:6 * D]
    pre = ALPHA * h_ref[...] + _per_seq(f, gate2, jnp.multiply)
    o_ref[...] = _layer_norm(pre, g2_ref[...], b2_ref[...])


def _combine(h, mod3, gates, ln2_g, ln2_b, y4, token0, rows_per_mod, row0, out_so_far):
    t = COMBINE_TILE
    hoff = row0 // t
    goff = (token0 + row0) // t
    g = mod3.shape[1]
    tiles_per_mod = rows_per_mod // t
    in_specs = [pl.BlockSpec((t, D), lambda i: (i + hoff, 0)),
                pl.BlockSpec((None, g, 6 * D), lambda i: ((i + hoff) // tiles_per_mod, 0, 0)),
                pl.BlockSpec((t, LANES), lambda i: (i + goff, 0)),
                pl.BlockSpec((1, D), lambda i: (0, 0)),
                pl.BlockSpec((1, D), lambda i: (0, 0)),
                pl.BlockSpec((TOP_K, t, D // 2), lambda i: (0, i, 0))]
    args = [h, mod3, gates, ln2_g, ln2_b, y4]
    aliases = {}
    if out_so_far is not None:
        in_specs.append(pl.BlockSpec(memory_space=pl.ANY))
        args.append(out_so_far)
        aliases = {len(args) - 1: 0}
    return pl.pallas_call(
        _combine_kernel,
        out_shape=jax.ShapeDtypeStruct(h.shape, _f32),
        grid=(y4.shape[1] // t,),
        in_specs=in_specs,
        out_specs=pl.BlockSpec((t, D), lambda i: (i + hoff, 0)),
        input_output_aliases=aliases,
        compiler_params=pltpu.CompilerParams(vmem_limit_bytes=VMEM_LIMIT),
        name="combine",
    )(*args)


def _time_major(a):
    return a.transpose(1, 0, 2)


def kernel(x_prompt, x_sample, c_prompt, c_sample, state_conv, state_pool, w_ada, b_ada, w_in,
           conv_w, w_out_a, w_pool, ls_pool, w_out_b, w_o, ln1_g, ln1_b, w_router, b_router,
           w_up, b_up, w_down, b_down, ln2_g, ln2_b):
    n_seq_p, seq, _ = x_prompt.shape
    n_seq_s, dec_seq, _ = x_sample.shape
    n_p, n_s = n_seq_p * seq, n_seq_s * dec_seq
    n = n_p + n_s
    n_slots = TOP_K * n + E * SLOT_PAD
    l = 0

    mod = _ada(jnp.concatenate([c_prompt, c_sample], axis=0), w_ada[l], b_ada[l][None])
    mod_p = mod[:n_seq_p][:, None, :]
    mod_s = mod[n_seq_p:][None]

    weights = (
        w_in[l].astype(_bf16), conv_w[l], w_out_a[l].astype(_bf16), w_pool[l].astype(_bf16),
        ls_pool[l][None], w_out_b[l].astype(_bf16), w_o[l].astype(_bf16), ln1_g[l][None], ln1_b[l][None],
        jnp.pad(w_router[l], ((0, 0), (0, LANES - E))), jnp.pad(b_router[l], (0, LANES - E))[None],
    )
    hc_p, hp_p = _hist_steps(CONV_HIST, 1), _hist_steps(POOL_HIST, 1)
    h_p, v_p, lg_p, nc_p, np_p = _mixer(
        x_prompt.reshape(n_p, D), mod_p, jnp.zeros((n_seq_p * hc_p, C), _f32),
        jnp.zeros((n_seq_p * hp_p, C), _f32), weights, PROMPT_ROW_TILE, 0)
    hc_s, hp_s = _hist_steps(CONV_HIST, n_seq_s), _hist_steps(POOL_HIST, n_seq_s)
    hist_c = jnp.pad(_time_major(state_conv[l]), ((hc_s - CONV_HIST, 0), (0, 0), (0, 0)))
    hist_p = jnp.pad(_time_major(state_pool[l]), ((hp_s - POOL_HIST, 0), (0, 0), (0, 0)))
    h_s, v_s, lg_s, nc_s, np_s = _mixer(
        _time_major(x_sample).reshape(n_s, D), mod_s, hist_c.reshape(hc_s * n_seq_s, C),
        hist_p.reshape(hp_s * n_seq_s, C), weights, SAMPLE_ROW_TILE, PAST_LEN)

    dest8, gates3, meta = _plan(lg_p, lg_s)
    cnt, row0, padded, nxt = (meta[i, :E] for i in range(4))
    gates = gates3.reshape(n, LANES)

    dests = [dest8[:, k, :].reshape(n) for k in range(TOP_K)]
    xpad = _dispatch(v_p, v_s, dests, n_slots)
    ypad = _experts(cnt, row0, padded, nxt, xpad, w_up[l], b_up[l][:, None, :], w_down[l], b_down[l][:, None, :])

    def gathered(t0, rows):
        idx = jnp.concatenate([dk[t0:t0 + rows] for dk in dests])
        return _gather_rows(ypad, idx).reshape(TOP_K, rows, D // 2)

    chunk = n_p // COMBINE_CHUNKS
    y4_p = [gathered(c * chunk, chunk) for c in range(COMBINE_CHUNKS)]
    y4_s = gathered(n_p, n_s)
    g2, b2 = ln2_g[l][None], ln2_b[l][None]
    y_p = None
    for c in range(COMBINE_CHUNKS):
        y_p = _combine(h_p, mod_p, gates, g2, b2, y4_p[c], 0, seq, c * chunk, y_p)
    y_s = _combine(h_s, mod_s, gates, g2, b2, y4_s, n_p, n_s, 0, None)

    y_prompt = y_p.reshape(n_seq_p, seq, D)
    y_sample = _time_major(y_s.reshape(dec_seq, n_seq_s, D))
    new_conv_p = nc_p.reshape(n_seq_p, hc_p, C)[:, hc_p - CONV_HIST:][None]
    new_pool_p = np_p.reshape(n_seq_p, hp_p, C)[:, hp_p - POOL_HIST:][None]
    new_conv_s = _time_major(nc_s.reshape(hc_s, n_seq_s, C)[hc_s - CONV_HIST:])[None]
    new_pool_s = _time_major(np_s.reshape(hp_s, n_seq_s, C)[hp_s - POOL_HIST:])[None]
    return (y_prompt, y_sample, new_conv_p, new_pool_p, new_conv_s, new_pool_s)
```

```python
import functools

import jax
import jax.numpy as jnp
from jax import lax
from jax.experimental import pallas as pl
from jax.experimental.pallas import tpu as pltpu
from jax.experimental.pallas import tpu_sc as plsc

D = 1024
C = 512
N_GROUPS = 4
GROUP = C // N_GROUPS
CONV_HIST = 2
POOL_HIST = 15
E = 32
TOP_K = 4
F = 1024
SWIGLU_LIMIT = 7.0
SWIGLU_ALPHA = 1.702
LN_EPS = 1e-5
DEPTH = 1
ALPHA = (2 * DEPTH) ** 0.25
PAST_LEN = 16384

LANES = 128
SUBLANES = 8
ROW_TILE = 512
PROMPT_ROW_TILE = 1024
SAMPLE_ROW_TILE = 512
SLOT_PAD = 256
BIG_BLOCK = 1024
COMBINE_TILE = 512
COMBINE_CHUNKS = 4
SC_WINDOW = 128
DISPATCH_WINDOW = 64
ADA_COLS = 1536
VMEM_LIMIT = 56 * 1024 * 1024

_f32 = jnp.float32
_bf16 = jnp.bfloat16


def _dot(a, b):
    return jnp.dot(a, b, preferred_element_type=_f32)


def _dot_exact(a, b):
    return lax.dot_general(a, b, (((1,), (0,)), ((), ())),
                           precision=lax.Precision.HIGHEST, preferred_element_type=_f32)


def _pack_rows(x):
    w = x.shape[1] // 2
    hi = lax.bitcast_convert_type(x[:, :w].astype(_bf16).astype(_f32), jnp.int32)
    lo = lax.bitcast_convert_type(x[:, w:].astype(_bf16).astype(_f32), jnp.int32)
    return hi | lax.shift_right_logical(lo, 16)


def _unpack_rows(p):
    hi = lax.bitcast_convert_type(p & jnp.int32(-65536), _f32)
    lo = lax.bitcast_convert_type(lax.shift_left(p, 16), _f32)
    return jnp.concatenate([hi, lo], axis=1).astype(_bf16)


def _sigmoid(x):
    return 0.5 * jnp.tanh(0.5 * x) + 0.5


def _per_seq(x, m, op):
    g = m.shape[0]
    if g == 1:
        return op(x, m)
    r, n = x.shape
    return op(x.reshape(r // g, g, n), m[None]).reshape(r, n)


def _layer_norm(x, g, b):
    mu = jnp.mean(x, axis=-1, keepdims=True)
    xc = x - mu
    var = jnp.mean(xc * xc, axis=-1, keepdims=True)
    return xc * lax.rsqrt(var + LN_EPS) * g + b


def _hist_steps(needed, g):
    return -(-needed * g // SUBLANES) * SUBLANES // g


def _ada_kernel(c_ref, w_ref, b_ref, o_ref):
    c = c_ref[...]
    o_ref[...] = _dot((c * _sigmoid(c)).astype(_bf16), w_ref[...].astype(_bf16)) + b_ref[...]


def _ada(c, w_ada, b_ada):
    rows = c.shape[0]
    cols = w_ada.shape[1]
    bn = ADA_COLS
    return pl.pallas_call(
        _ada_kernel,
        out_shape=jax.ShapeDtypeStruct((rows, cols), _f32),
        grid=(cols // bn,),
        in_specs=[pl.BlockSpec((rows, D), lambda j: (0, 0)),
                  pl.BlockSpec((D, bn), lambda j: (0, j)),
                  pl.BlockSpec((1, bn), lambda j: (0, j))],
        out_specs=pl.BlockSpec((rows, bn), lambda j: (0, j)),
        compiler_params=pltpu.CompilerParams(vmem_limit_bytes=VMEM_LIMIT),
        name="ada",
    )(c, w_ada, b_ada)


def _mixer_kernel(g, tiles_per_seq, start_pos,
                  x_ref, mod_ref, hc_ref, hp_ref,
                  win_ref, cw_ref, woa_ref, wpool_ref, ls_ref, wob_ref, wo_ref, g1_ref, b1_ref,
                  wr_ref, br_ref,
                  h_ref, v_ref, lg_ref, nc_ref, np_ref, zbuf, pbuf):
    r = x_ref.shape[0]
    hrc = hc_ref.shape[0]
    hrp = hp_ref.shape[0]
    j = pl.program_id(0) % tiles_per_seq

    @pl.when(j == 0)
    def _():
        zbuf[pl.ds(0, hrc), :] = hc_ref[...]
        pbuf[pl.ds(0, hrp), :] = hp_ref[...]

    @pl.when(j != 0)
    def _():
        zt = zbuf[pl.ds(r, hrc), :]
        pt = pbuf[pl.ds(r, hrp), :]
        zbuf[pl.ds(0, hrc), :] = zt
        pbuf[pl.ds(0, hrp), :] = pt

    m = mod_ref[...]
    shift1, scale1, gate1 = m[:, 0:D], m[:, D:2 * D], m[:, 2 * D:3 * D]
    shift2, scale2 = m[:, 3 * D:4 * D], m[:, 4 * D:5 * D]

    x = x_ref[...]
    u = _per_seq(_per_seq(x, 1.0 + scale1, jnp.multiply), shift1, jnp.add).astype(_bf16)

    z = _dot(u, win_ref[:, C:2 * C]) * _dot(u, win_ref[:, 2 * C:3 * C])
    zbuf[pl.ds(hrc, r), :] = z
    cw = cw_ref[...]
    conv = (cw[0:1] * zbuf[pl.ds(hrc - 2 * g, r), :] + cw[1:2] * zbuf[pl.ds(hrc - g, r), :]
            + cw[2:3] * z)
    y_a = _dot((_dot(u, win_ref[:, 0:C]) * conv).astype(_bf16), woa_ref[...])

    xp = _dot(u, win_ref[:, 3 * C:4 * C])
    pbuf[pl.ds(hrp, r), :] = xp
    pos = start_pos + j * (r // g) + lax.broadcasted_iota(jnp.int32, (r, 1), 0) // g
    acc = xp
    yg = []
    for grp in range(N_GROUPS):
        lo = grp * GROUP
        wdw = 2 ** (grp + 1)
        for back in range(wdw // 2, wdw):
            sh = pbuf[pl.ds(hrp - back * g, r), lo:C]
            acc = jnp.concatenate([acc[:, 0:lo], acc[:, lo:C] + sh], axis=1) if lo else acc + sh
        inv_cnt = 1.0 / jnp.minimum(wdw, pos + 1).astype(_f32)
        diff = acc[:, lo:lo + GROUP] * inv_cnt - xp[:, lo:lo + GROUP]
        yg.append(_dot(diff.astype(_bf16), wpool_ref[grp]))
    y_b = _dot((jnp.concatenate(yg, axis=1) * ls_ref[...]).astype(_bf16), wob_ref[...])

    g_a = _dot(u, win_ref[:, 4 * C:4 * C + D])
    g_b = _dot(u, win_ref[:, 4 * C + D:4 * C + 2 * D])
    merged = _sigmoid(g_a) * y_a + _sigmoid(g_b) * y_b
    o = _dot(merged.astype(_bf16), wo_ref[...])
    h = _layer_norm(ALPHA * x + _per_seq(o, gate1, jnp.multiply), g1_ref[...], b1_ref[...])
    v = _per_seq(_per_seq(h, 1.0 + scale2, jnp.multiply), shift2, jnp.add)
    h_ref[...] = h
    v_ref[...] = _pack_rows(v)
    lg_ref[...] = _dot(v.astype(_bf16), wr_ref[...].astype(_bf16)) + br_ref[...]
    nc_ref[...] = zbuf[pl.ds(r, hrc), :]
    np_ref[...] = pbuf[pl.ds(r, hrp), :]


def _mixer(x2, mod3, hc, hp, weights, row_tile, start_pos):
    n = x2.shape[0]
    n_mod, g, _ = mod3.shape
    hrc, hrp = hc.shape[0] // n_mod, hp.shape[0] // n_mod
    tiles_per_seq = n // n_mod // row_tile
    once = dict(pipeline_mode=pl.Buffered(1)) if n_mod == 1 else {}

    def full(a):
        nd = a.ndim
        return pl.BlockSpec(a.shape, lambda i: (0,) * nd)

    def seq_block(rows, **kw):
        return pl.BlockSpec((rows, C), lambda i: (i // tiles_per_seq, 0), **kw)

    def row_block(cols):
        return pl.BlockSpec((row_tile, cols), lambda i: (i, 0))

    return pl.pallas_call(
        functools.partial(_mixer_kernel, g, tiles_per_seq, start_pos),
        out_shape=[
            jax.ShapeDtypeStruct((n, D), _f32),
            jax.ShapeDtypeStruct((n, D // 2), jnp.int32),
            jax.ShapeDtypeStruct((n, LANES), _f32),
            jax.ShapeDtypeStruct(hc.shape, _f32),
            jax.ShapeDtypeStruct(hp.shape, _f32),
        ],
        grid=(n // row_tile,),
        in_specs=[row_block(D),
                  pl.BlockSpec((None, g, 6 * D), lambda i: (i // tiles_per_seq, 0, 0), **once),
                  seq_block(hrc, **once), seq_block(hrp, **once)] + [full(a) for a in weights],
        out_specs=[row_block(D), row_block(D // 2), row_block(LANES), seq_block(hrc), seq_block(hrp)],
        scratch_shapes=[pltpu.VMEM((hrc + row_tile, C), _f32), pltpu.VMEM((hrp + row_tile, C), _f32)],
        compiler_params=pltpu.CompilerParams(vmem_limit_bytes=VMEM_LIMIT),
        name="mixer",
    )(x2, mod3, hc, hp, *weights)


def _plan_kernel(lgp_ref, lgs_ref, dest_ref, gate_ref, meta_ref, idx_s, rank_s):
    t = ROW_TILE
    e_iota = lax.broadcasted_iota(jnp.int32, (E, t), 0)
    tri = (lax.broadcasted_iota(jnp.int32, (t, t), 0)
           < lax.broadcasted_iota(jnp.int32, (t, t), 1)).astype(_f32).astype(_bf16)
    zeros_rest = jnp.zeros((LANES - TOP_K, t), _f32)

    def tile_body(lg_ref, off, i, carry):
        lt = lg_ref[i].T[0:E, :]
        vals, idxs = [], []
        for _ in range(TOP_K):
            mx = jnp.max(lt, axis=0, keepdims=True)
            ix = jnp.min(jnp.where(lt == mx, e_iota, E), axis=0, keepdims=True)
            vals.append(mx)
            idxs.append(ix)
            lt = jnp.where(e_iota == ix, -jnp.inf, lt)
        ex = [jnp.exp(vk - vals[0]) for vk in vals]
        den = ex[0] + ex[1] + ex[2] + ex[3]
        gates = [ek / den for ek in ex]
        gate_ref[off + i] = jnp.concatenate(gates + [zeros_rest], axis=0).T

        ohs = [(e_iota == ix) for ix in idxs]
        oh = (ohs[0] | ohs[1] | ohs[2] | ohs[3]).astype(_f32)
        before = _dot(oh.astype(_bf16), tri) + carry
        ranks = [jnp.sum(jnp.where(o, before, 0.0), axis=0, keepdims=True) for o in ohs]
        idx_s[off + i] = jnp.concatenate(idxs + idxs, axis=0)
        rank_s[off + i] = jnp.concatenate(ranks + ranks, axis=0).astype(jnp.int32)
        return carry + jnp.sum(oh, axis=1, keepdims=True)

    n_p, n_s = lgp_ref.shape[0], lgs_ref.shape[0]
    counts = lax.fori_loop(0, n_p, functools.partial(tile_body, lgp_ref, 0), jnp.zeros((E, 1), _f32))
    counts = lax.fori_loop(0, n_s, functools.partial(tile_body, lgs_ref, n_p), counts)
    padded = jnp.ceil(counts / SLOT_PAD) * SLOT_PAD
    low = (lax.broadcasted_iota(jnp.int32, (E, E), 1)
           <= lax.broadcasted_iota(jnp.int32, (E, E), 0)).astype(_f32)
    pad_end = _dot_exact(low, jnp.broadcast_to(padded, (E, LANES)))[:, 0:1]
    pad_start = pad_end - padded

    def dest_body(i, c):
        ix = idx_s[i]
        rk = rank_s[i]
        rows = []
        for k in range(TOP_K):
            st = jnp.sum(jnp.where(e_iota == ix[k:k + 1], pad_start, 0.0), axis=0, keepdims=True)
            rows.append(st.astype(jnp.int32) + rk[k:k + 1])
        dest_ref[i] = jnp.concatenate(rows + rows, axis=0)
        return c

    lax.fori_loop(0, n_p + n_s, dest_body, 0)

    sub = lax.broadcasted_iota(jnp.int32, (E, LANES), 0)
    lane = lax.broadcasted_iota(jnp.int32, (E, LANES), 1)

    def to_lanes(col):
        return jnp.sum(jnp.where(sub == lane, col, 0.0), axis=0, keepdims=True).astype(jnp.int32)

    later = jnp.min(jnp.where((sub > lane) & (counts > 0.0), sub, E), axis=0, keepdims=True)
    meta_ref[...] = jnp.concatenate(
        [to_lanes(counts), to_lanes(pad_start), to_lanes(padded), later,
         jnp.zeros((SUBLANES - 4, LANES), jnp.int32)], axis=0)


def _plan(logits_p, logits_s):
    n_tiles = (logits_p.shape[0] + logits_s.shape[0]) // ROW_TILE
    return pl.pallas_call(
        _plan_kernel,
        out_shape=[
            jax.ShapeDtypeStruct((n_tiles, 2 * TOP_K, ROW_TILE), jnp.int32),
            jax.ShapeDtypeStruct((n_tiles, ROW_TILE, LANES), _f32),
            jax.ShapeDtypeStruct((SUBLANES, LANES), jnp.int32),
        ],
        scratch_shapes=[pltpu.VMEM((n_tiles, 2 * TOP_K, ROW_TILE), jnp.int32),
                        pltpu.VMEM((n_tiles, 2 * TOP_K, ROW_TILE), jnp.int32)],
        compiler_params=pltpu.CompilerParams(vmem_limit_bytes=VMEM_LIMIT),
        name="plan",
    )(logits_p.reshape(-1, ROW_TILE, LANES), logits_s.reshape(-1, ROW_TILE, LANES))


def _dispatch(v_p, v_s, dests, n_rows_out):
    n_p, n_s = v_p.shape[0], v_s.shape[0]
    width = v_p.shape[1]
    w = DISPATCH_WINDOW
    n_pw, n_windows = n_p // w, (n_p + n_s) // w
    mesh = plsc.VectorSubcoreMesh(core_axis_name="core", subcore_axis_name="subcore")
    n_workers = mesh.num_cores * mesh.num_subcores
    n_rounds = -(-n_windows // n_workers)

    @functools.partial(
        pl.kernel, mesh=mesh, name="dispatch",
        out_type=jax.ShapeDtypeStruct((n_rows_out, width), jnp.int32),
        scratch_types=[pltpu.VMEM((w, width), jnp.int32)] * 2 + [pltpu.VMEM((w,), jnp.int32)] * (2 * TOP_K)
        + [pltpu.SemaphoreType.DMA] * 3)
    def scatter_rows(vp_hbm, vs_hbm, d0_hbm, d1_hbm, d2_hbm, d3_hbm, o_hbm, rows0, rows1, *rest):
        worker = lax.axis_index("subcore") * mesh.num_cores + lax.axis_index("core")
        rows = (rows0, rows1)
        idx = (rest[:TOP_K], rest[TOP_K:2 * TOP_K])
        lsem, ssem = rest[2 * TOP_K:2 * TOP_K + 2], rest[2 * TOP_K + 2]
        d_hbm = (d0_hbm, d1_hbm, d2_hbm, d3_hbm)

        def load(c, slot, act):
            def index_reads():
                t0 = pl.multiple_of(c * w, w)
                for k in range(TOP_K):
                    act(pltpu.make_async_copy(d_hbm[k].at[pl.ds(t0, w)], idx[slot][k], lsem[slot]))

            @pl.when(c < n_pw)
            def _():
                src = vp_hbm.at[pl.ds(pl.multiple_of(c * w, w), w)]
                act(pltpu.make_async_copy(src, rows[slot], lsem[slot]))
                index_reads()

            @pl.when((c >= n_pw) & (c < n_windows))
            def _():
                src = vs_hbm.at[pl.ds(pl.multiple_of((c - n_pw) * w, w), w)]
                act(pltpu.make_async_copy(src, rows[slot], lsem[slot]))
                index_reads()

        load(worker, 0, lambda cp: cp.start())
        for j in range(n_rounds):
            c = j * n_workers + worker
            slot = j % 2
            load(c, slot, lambda cp: cp.wait())
            if j + 1 < n_rounds:
                load(c + n_workers, 1 - slot, lambda cp: cp.start())

            @pl.when(c < n_windows)
            def _():
                copies = [pltpu.async_copy(rows[slot], o_hbm.at[iv], ssem) for iv in idx[slot]]
                for cp in copies:
                    cp.wait()

    return scatter_rows(v_p, v_s, *dests)


def _experts_kernel(cnt_ref, row0_ref, pad_ref, nxt_ref, x_hbm, wu_hbm, bu_ref, wd_hbm, bd_ref, y_hbm,
                    xbuf, ybuf, wu_st, wd_st, wu_bf, wd_bf, ysz, xsem, ysem, wsem):
    def w_fetch(e):
        return (pltpu.make_async_copy(wu_hbm.at[e], wu_st, wsem.at[0]),
                pltpu.make_async_copy(wd_hbm.at[e], wd_st, wsem.at[1]))

    def x_fetch(row, size, slot):
        rows = pl.ds(pl.multiple_of(row, SLOT_PAD), size)
        return pltpu.make_async_copy(x_hbm.at[rows, :], xbuf.at[slot, pl.ds(0, size), :], xsem.at[slot])

    def y_store(row, size, slot):
        rows = pl.ds(pl.multiple_of(row, SLOT_PAD), size)
        return pltpu.make_async_copy(ybuf.at[slot, pl.ds(0, size), :], y_hbm.at[rows, :], ysem.at[slot])

    def y_wait(slot):
        for size in (BIG_BLOCK, SLOT_PAD):
            @pl.when(ysz[slot] == size)
            def _():
                y_store(0, size, slot).wait()

    def fetch_first(e, slot):
        @pl.when(pad_ref[e] >= BIG_BLOCK)
        def _():
            x_fetch(row0_ref[e], BIG_BLOCK, slot).start()

        @pl.when((pad_ref[e] > 0) & (pad_ref[e] < BIG_BLOCK))
        def _():
            x_fetch(row0_ref[e], SLOT_PAD, slot).start()

    ysz[0] = 0
    ysz[1] = 0
    for c in w_fetch(0):
        c.start()
    fetch_first(jnp.where(pad_ref[0] > 0, 0, nxt_ref[0]), 0)

    def expert_body(e, n_done):
        for c in w_fetch(e):
            c.wait()
        padded = pad_ref[e]

        @pl.when(padded > 0)
        def _():
            wu_bf[...] = wu_st[...].astype(_bf16)
            wd_bf[...] = wd_st[...].astype(_bf16)

        @pl.when(e + 1 < E)
        def _():
            for c in w_fetch(e + 1):
                c.start()

        b_up = bu_ref[e]
        b_down = bd_ref[e]
        row0 = row0_ref[e]
        n_big = padded // BIG_BLOCK
        n_small = (padded - n_big * BIG_BLOCK) // SLOT_PAD
        small0 = row0 + n_big * BIG_BLOCK
        nxt = nxt_ref[e]

        def pass_body(size, j, n_done):
            slot = n_done % 2
            big = size == BIG_BLOCK
            row = row0 + j * BIG_BLOCK if big else small0 + j * SLOT_PAD
            x_fetch(row, size, slot).wait()

            more = j + 1 < (n_big if big else n_small)
            tail = (n_small > 0) if big else False

            @pl.when(more)
            def _():
                x_fetch(row + size, size, 1 - slot).start()

            if big:
                @pl.when(jnp.logical_not(more) & tail)
                def _():
                    x_fetch(small0, SLOT_PAD, 1 - slot).start()

            @pl.when(jnp.logical_not(more) & jnp.logical_not(tail) & (nxt < E))
            def _():
                fetch_first(jnp.minimum(nxt, E - 1), 1 - slot)

            rows = lax.broadcasted_iota(jnp.int32, (size, 1), 0)
            x = _unpack_rows(jnp.where(rows < cnt_ref[e] - (row - row0), xbuf[slot, pl.ds(0, size), :], 0))
            hcat = _dot(x, wu_bf[...]) + b_up
            glu = jnp.minimum(hcat[:, 0:F], SWIGLU_LIMIT)
            lin = jnp.clip(hcat[:, F:2 * F], -SWIGLU_LIMIT, SWIGLU_LIMIT)
            act = glu * _sigmoid(SWIGLU_ALPHA * glu) * (lin + 1.0)
            y = _pack_rows(_dot(act.astype(_bf16), wd_bf[...]) + b_down)

            y_wait(slot)
            ybuf[slot, pl.ds(0, size), :] = y
            y_store(row, size, slot).start()
            ysz[slot] = size
            return n_done + 1

        n_done = lax.fori_loop(0, n_big, functools.partial(pass_body, BIG_BLOCK), n_done)
        return lax.fori_loop(0, n_small, functools.partial(pass_body, SLOT_PAD), n_done)

    lax.fori_loop(0, E, expert_body, 0)
    y_wait(0)
    y_wait(1)


def _experts(cnt, row0, padded, nxt, xpad, w_up, b_up, w_down, b_down):
    def full(a):
        nd = a.ndim
        return pl.BlockSpec(a.shape, lambda i, *_: (0,) * nd)

    return pl.pallas_call(
        _experts_kernel,
        out_shape=jax.ShapeDtypeStruct(xpad.shape, jnp.int32),
        grid_spec=pltpu.PrefetchScalarGridSpec(
            num_scalar_prefetch=4,
            grid=(1,),
            in_specs=[pl.BlockSpec(memory_space=pl.ANY),
                      pl.BlockSpec(memory_space=pl.ANY), full(b_up),
                      pl.BlockSpec(memory_space=pl.ANY), full(b_down)],
            out_specs=pl.BlockSpec(memory_space=pl.ANY),
            scratch_shapes=[pltpu.VMEM((2, BIG_BLOCK, D // 2), jnp.int32),
                            pltpu.VMEM((2, BIG_BLOCK, D // 2), jnp.int32),
                            pltpu.VMEM((D, 2 * F), _f32), pltpu.VMEM((F, D), _f32),
                            pltpu.VMEM((D, 2 * F), _bf16), pltpu.VMEM((F, D), _bf16),
                            pltpu.SMEM((2,), jnp.int32),
                            pltpu.SemaphoreType.DMA((2,)), pltpu.SemaphoreType.DMA((2,)),
                            pltpu.SemaphoreType.DMA((2,))],
        ),
        compiler_params=pltpu.CompilerParams(vmem_limit_bytes=VMEM_LIMIT),
        name="experts",
    )(cnt, row0, padded, nxt, xpad, w_up, b_up, w_down, b_down)


def _gather_rows(ypad, dest_all):
    n_out = dest_all.shape[0]
    width = ypad.shape[1]
    w = SC_WINDOW
    mesh = plsc.VectorSubcoreMesh(core_axis_name="core", subcore_axis_name="subcore")
    n_workers = mesh.num_cores * mesh.num_subcores
    n_windows = n_out // w

    @functools.partial(
        pl.kernel, mesh=mesh, name="gather_rows",
        out_type=jax.ShapeDtypeStruct((n_out, width), jnp.int32),
        scratch_types=[pltpu.VMEM((w, width), jnp.int32), pltpu.VMEM((w,), jnp.int32), pltpu.SemaphoreType.DMA])
    def gather_rows(y_hbm, d_hbm, o_hbm, rows, iv, sem):
        worker = lax.axis_index("subcore") * mesh.num_cores + lax.axis_index("core")

        @pl.loop(0, n_windows // n_workers)
        def _(j):
            r0 = pl.multiple_of((j * n_workers + worker) * w, w)
            pltpu.sync_copy(d_hbm.at[pl.ds(r0, w)], iv)
            pltpu.async_copy(y_hbm.at[iv], rows, sem).wait()
            pltpu.sync_copy(rows, o_hbm.at[pl.ds(r0, w)])

    assert n_windows % n_workers == 0
    return gather_rows(ypad, dest_all)


def _combine_kernel(h_ref, mod_ref, gate_ref, g2_ref, b2_ref, y_ref, *aliased_and_out):
    o_ref = aliased_and_out[-1]
    gates = gate_ref[...]
    half = D // 2
    f_hi = jnp.zeros((h_ref.shape[0], half), _f32)
    f_lo = jnp.zeros((h_ref.shape[0], half), _f32)
    for k in range(TOP_K):
        p = y_ref[k]
        gk = gates[:, k:k + 1]
        f_hi = f_hi + gk * lax.bitcast_convert_type(p & jnp.int32(-65536), _f32)
        f_lo = f_lo + gk * lax.bitcast_convert_type(lax.shift_left(p, 16), _f32)
    f = jnp.concatenate([f_hi, f_lo], axis=1)
    gate2 = mod_ref[...][:, 5 * D:6 * D]
    pre = ALPHA * h_ref[...] + _per_seq(f, gate2, jnp.multiply)
    o_ref[...] = _layer_norm(pre, g2_ref[...], b2_ref[...])


def _combine(h, mod3, gates, ln2_g, ln2_b, y4, token0, rows_per_mod, row0, out_so_far):
    t = COMBINE_TILE
    hoff = row0 // t
    goff = (token0 + row0) // t
    g = mod3.shape[1]
    tiles_per_mod = rows_per_mod // t
    in_specs = [pl.BlockSpec((t, D), lambda i: (i + hoff, 0)),
                pl.BlockSpec((None, g, 6 * D), lambda i: ((i + hoff) // tiles_per_mod, 0, 0)),
                pl.BlockSpec((t, LANES), lambda i: (i + goff, 0)),
                pl.BlockSpec((1, D), lambda i: (0, 0)),
                pl.BlockSpec((1, D), lambda i: (0, 0)),
                pl.BlockSpec((TOP_K, t, D // 2), lambda i: (0, i, 0))]
    args = [h, mod3, gates, ln2_g, ln2_b, y4]
    aliases = {}
    if out_so_far is not None:
        in_specs.append(pl.BlockSpec(memory_space=pl.ANY))
        args.append(out_so_far)
        aliases = {len(args) - 1: 0}
    return pl.pallas_call(
        _combine_kernel,
        out_shape=jax.ShapeDtypeStruct(h.shape, _f32),
        grid=(y4.shape[1] // t,),
        in_specs=in_specs,
        out_specs=pl.BlockSpec((t, D), lambda i: (i + hoff, 0)),
        input_output_aliases=aliases,
        compiler_params=pltpu.CompilerParams(vmem_limit_bytes=VMEM_LIMIT),
        name="combine",
    )(*args)


def _time_major(a):
    return a.transpose(1, 0, 2)


def kernel(x_prompt, x_sample, c_prompt, c_sample, state_conv, state_pool, w_ada, b_ada, w_in,
           conv_w, w_out_a, w_pool, ls_pool, w_out_b, w_o, ln1_g, ln1_b, w_router, b_router,
           w_up, b_up, w_down, b_down, ln2_g, ln2_b):
    n_seq_p, seq, _ = x_prompt.shape
    n_seq_s, dec_seq, _ = x_sample.shape
    n_p, n_s = n_seq_p * seq, n_seq_s * dec_seq
    n = n_p + n_s
    n_slots = TOP_K * n + E * SLOT_PAD
    l = 0

    mod = _ada(jnp.concatenate([c_prompt, c_sample], axis=0), w_ada[l], b_ada[l][None])
    mod_p = mod[:n_seq_p][:, None, :]
    mod_s = mod[n_seq_p:][None]

    weights = (
        w_in[l].astype(_bf16), conv_w[l], w_out_a[l].astype(_bf16), w_pool[l].astype(_bf16),
        ls_pool[l][None], w_out_b[l].astype(_bf16), w_o[l].astype(_bf16), ln1_g[l][None], ln1_b[l][None],
        jnp.pad(w_router[l], ((0, 0), (0, LANES - E))), jnp.pad(b_router[l], (0, LANES - E))[None],
    )
    hc_p, hp_p = _hist_steps(CONV_HIST, 1), _hist_steps(POOL_HIST, 1)
    h_p, v_p, lg_p, nc_p, np_p = _mixer(
        x_prompt.reshape(n_p, D), mod_p, jnp.zeros((n_seq_p * hc_p, C), _f32),
        jnp.zeros((n_seq_p * hp_p, C), _f32), weights, PROMPT_ROW_TILE, 0)
    hc_s, hp_s = _hist_steps(CONV_HIST, n_seq_s), _hist_steps(POOL_HIST, n_seq_s)
    hist_c = jnp.pad(_time_major(state_conv[l]), ((hc_s - CONV_HIST, 0), (0, 0), (0, 0)))
    hist_p = jnp.pad(_time_major(state_pool[l]), ((hp_s - POOL_HIST, 0), (0, 0), (0, 0)))
    h_s, v_s, lg_s, nc_s, np_s = _mixer(
        _time_major(x_sample).reshape(n_s, D), mod_s, hist_c.reshape(hc_s * n_seq_s, C),
        hist_p.reshape(hp_s * n_seq_s, C), weights, SAMPLE_ROW_TILE, PAST_LEN)

    dest8, gates3, meta = _plan(lg_p, lg_s)
    cnt, row0, padded, nxt = (meta[i, :E] for i in range(4))
    gates = gates3.reshape(n, LANES)

    dest_kn = dest8[:, :TOP_K, :].transpose(1, 0, 2).reshape(TOP_K, n)
    xpad = _dispatch(v_p, v_s, [dest_kn[k] for k in range(TOP_K)], n_slots)
    ypad = _experts(cnt, row0, padded, nxt, xpad, w_up[l], b_up[l][:, None, :], w_down[l], b_down[l][:, None, :])

    def gathered(t0, rows):
        idx = dest_kn[:, t0:t0 + rows].reshape(TOP_K * rows)
        return _gather_rows(ypad, idx).reshape(TOP_K, rows, D // 2)

    chunk = n_p // COMBINE_CHUNKS
    y4_p = [gathered(c * chunk, chunk) for c in range(COMBINE_CHUNKS)]
    y4_s = gathered(n_p, n_s)
    g2, b2 = ln2_g[l][None], ln2_b[l][None]
    y_p = None
    for c in range(COMBINE_CHUNKS):
        y_p = _combine(h_p, mod_p, gates, g2, b2, y4_p[c], 0, seq, c * chunk, y_p)
    y_s = _combine(h_s, mod_s, gates, g2, b2, y4_s, n_p, n_s, 0, None)

    y_prompt = y_p.reshape(n_seq_p, seq, D)
    y_sample = _time_major(y_s.reshape(dec_seq, n_seq_s, D))
    new_conv_p = nc_p.reshape(n_seq_p, hc_p, C)[:, hc_p - CONV_HIST:][None]
    new_pool_p = np_p.reshape(n_seq_p, hp_p, C)[:, hp_p - POOL_HIST:][None]
    new_conv_s = _time_major(nc_s.reshape(hc_s, n_seq_s, C)[hc_s - CONV_HIST:])[None]
    new_pool_s = _time_major(np_s.reshape(hp_s, n_seq_s, C)[hp_s - POOL_HIST:])[None]
    return (y_prompt, y_sample, new_conv_p, new_pool_p, new_conv_s, new_pool_s)
```

```python
import functools

import jax
import jax.numpy as jnp
from jax import lax
from jax.experimental import pallas as pl
from jax.experimental.pallas import tpu as pltpu
from jax.experimental.pallas import tpu_sc as plsc

D = 1024
C = 512
N_GROUPS = 4
GROUP = C // N_GROUPS
CONV_HIST = 2
POOL_HIST = 15
DOUBLING_HIST = 32
E = 32
TOP_K = 4
F = 1024
SWIGLU_LIMIT = 7.0
SWIGLU_ALPHA = 1.702
LN_EPS = 1e-5
DEPTH = 1
ALPHA = (2 * DEPTH) ** 0.25
PAST_LEN = 16384

LANES = 128
SUBLANES = 8
ROW_TILE = 512
PROMPT_ROW_TILE = 1024
SAMPLE_ROW_TILE = 512
SLOT_PAD = 256
BIG_BLOCK = 1024
COMBINE_TILE = 512
COMBINE_CHUNKS = 4
SC_WINDOW = 128
DISPATCH_WINDOW = 64
ADA_COLS = 1536
VMEM_LIMIT = 56 * 1024 * 1024

_f32 = jnp.float32
_bf16 = jnp.bfloat16


def _dot(a, b):
    return jnp.dot(a, b, preferred_element_type=_f32)


def _dot_exact(a, b):
    return lax.dot_general(a, b, (((1,), (0,)), ((), ())),
                           precision=lax.Precision.HIGHEST, preferred_element_type=_f32)


def _pack_rows(x):
    w = x.shape[1] // 2
    hi = lax.bitcast_convert_type(x[:, :w].astype(_bf16).astype(_f32), jnp.int32)
    lo = lax.bitcast_convert_type(x[:, w:].astype(_bf16).astype(_f32), jnp.int32)
    return hi | lax.shift_right_logical(lo, 16)


def _unpack_rows(p):
    hi = lax.bitcast_convert_type(p & jnp.int32(-65536), _f32)
    lo = lax.bitcast_convert_type(lax.shift_left(p, 16), _f32)
    return jnp.concatenate([hi, lo], axis=1).astype(_bf16)


def _sigmoid(x):
    return 0.5 * jnp.tanh(0.5 * x) + 0.5


def _per_seq(x, m, op):
    g = m.shape[0]
    if g == 1:
        return op(x, m)
    r, n = x.shape
    return op(x.reshape(r // g, g, n), m[None]).reshape(r, n)


def _layer_norm(x, g, b):
    mu = jnp.mean(x, axis=-1, keepdims=True)
    xc = x - mu
    var = jnp.mean(xc * xc, axis=-1, keepdims=True)
    return xc * lax.rsqrt(var + LN_EPS) * g + b


def _hist_steps(needed, g):
    return -(-needed * g // SUBLANES) * SUBLANES // g


def _ada_kernel(c_ref, w_ref, b_ref, o_ref):
    c = c_ref[...]
    o_ref[...] = _dot((c * _sigmoid(c)).astype(_bf16), w_ref[...].astype(_bf16)) + b_ref[...]


def _ada(c, w_ada, b_ada):
    rows = c.shape[0]
    cols = w_ada.shape[1]
    bn = ADA_COLS
    return pl.pallas_call(
        _ada_kernel,
        out_shape=jax.ShapeDtypeStruct((rows, cols), _f32),
        grid=(cols // bn,),
        in_specs=[pl.BlockSpec((rows, D), lambda j: (0, 0)),
                  pl.BlockSpec((D, bn), lambda j: (0, j)),
                  pl.BlockSpec((1, bn), lambda j: (0, j))],
        out_specs=pl.BlockSpec((rows, bn), lambda j: (0, j)),
        compiler_params=pltpu.CompilerParams(vmem_limit_bytes=VMEM_LIMIT),
        name="ada",
    )(c, w_ada, b_ada)


def _mixer_kernel(g, tiles_per_seq, start_pos,
                  x_ref, mod_ref, hc_ref, hp_ref,
                  win_ref, cw_ref, woa_ref, wpool_ref, ls_ref, wob_ref, wo_ref, g1_ref, b1_ref,
                  wr_ref, br_ref,
                  h_ref, v_ref, lg_ref, nc_ref, np_ref, zbuf, pbuf, *lvl):
    r = x_ref.shape[0]
    hrc = hc_ref.shape[0]
    hrp = hp_ref.shape[0]
    j = pl.program_id(0) % tiles_per_seq

    @pl.when(j == 0)
    def _():
        zbuf[pl.ds(0, hrc), :] = hc_ref[...]
        pbuf[pl.ds(0, hrp), :] = hp_ref[...]

    @pl.when(j != 0)
    def _():
        zt = zbuf[pl.ds(r, hrc), :]
        pt = pbuf[pl.ds(r, hrp), :]
        zbuf[pl.ds(0, hrc), :] = zt
        pbuf[pl.ds(0, hrp), :] = pt

    m = mod_ref[...]
    shift1, scale1, gate1 = m[:, 0:D], m[:, D:2 * D], m[:, 2 * D:3 * D]
    shift2, scale2 = m[:, 3 * D:4 * D], m[:, 4 * D:5 * D]

    x = x_ref[...]
    u = _per_seq(_per_seq(x, 1.0 + scale1, jnp.multiply), shift1, jnp.add).astype(_bf16)

    z = _dot(u, win_ref[:, C:2 * C]) * _dot(u, win_ref[:, 2 * C:3 * C])
    zbuf[pl.ds(hrc, r), :] = z
    cw = cw_ref[...]
    conv = (cw[0:1] * zbuf[pl.ds(hrc - 2 * g, r), :] + cw[1:2] * zbuf[pl.ds(hrc - g, r), :]
            + cw[2:3] * z)
    y_a = _dot((_dot(u, win_ref[:, 0:C]) * conv).astype(_bf16), woa_ref[...])

    xp = _dot(u, win_ref[:, 3 * C:4 * C])
    pbuf[pl.ds(hrp, r), :] = xp
    pos = start_pos + j * (r // g) + lax.broadcasted_iota(jnp.int32, (r, 1), 0) // g
    wins = []
    if lvl:
        s2, s4, s8 = lvl
        s2[...] = pbuf[pl.ds(hrp - 24, r + 24), :] + pbuf[pl.ds(hrp - 25, r + 24), :]
        s4[...] = s2[pl.ds(8, r + 16), GROUP:C] + s2[pl.ds(6, r + 16), GROUP:C]
        s8[...] = s4[pl.ds(8, r + 8), GROUP:3 * GROUP] + s4[pl.ds(4, r + 8), GROUP:3 * GROUP]
        wins = [s2[pl.ds(24, r), 0:GROUP], s4[pl.ds(16, r), 0:GROUP], s8[pl.ds(8, r), 0:GROUP],
                s8[pl.ds(8, r), GROUP:2 * GROUP] + s8[pl.ds(0, r), GROUP:2 * GROUP]]
    else:
        acc = xp
        for grp in range(N_GROUPS):
            lo = grp * GROUP
            wdw = 2 ** (grp + 1)
            for back in range(wdw // 2, wdw):
                sh = pbuf[pl.ds(hrp - back * g, r), lo:C]
                acc = jnp.concatenate([acc[:, 0:lo], acc[:, lo:C] + sh], axis=1) if lo else acc + sh
            wins.append(acc[:, lo:lo + GROUP])
    yg = []
    for grp in range(N_GROUPS):
        lo = grp * GROUP
        inv_cnt = 1.0 / jnp.minimum(2 ** (grp + 1), pos + 1).astype(_f32)
        diff = wins[grp] * inv_cnt - xp[:, lo:lo + GROUP]
        yg.append(_dot(diff.astype(_bf16), wpool_ref[grp]))
    y_b = _dot((jnp.concatenate(yg, axis=1) * ls_ref[...]).astype(_bf16), wob_ref[...])

    g_a = _dot(u, win_ref[:, 4 * C:4 * C + D])
    g_b = _dot(u, win_ref[:, 4 * C + D:4 * C + 2 * D])
    merged = _sigmoid(g_a) * y_a + _sigmoid(g_b) * y_b
    o = _dot(merged.astype(_bf16), wo_ref[...])
    h = _layer_norm(ALPHA * x + _per_seq(o, gate1, jnp.multiply), g1_ref[...], b1_ref[...])
    v = _per_seq(_per_seq(h, 1.0 + scale2, jnp.multiply), shift2, jnp.add)
    h_ref[...] = h
    v_ref[...] = _pack_rows(v)
    lg_ref[...] = _dot(v.astype(_bf16), wr_ref[...].astype(_bf16)) + br_ref[...]
    nc_ref[...] = zbuf[pl.ds(r, hrc), :]
    np_ref[...] = pbuf[pl.ds(r, hrp), :]


def _mixer(x2, mod3, hc, hp, weights, row_tile, start_pos):
    n = x2.shape[0]
    n_mod, g, _ = mod3.shape
    hrc, hrp = hc.shape[0] // n_mod, hp.shape[0] // n_mod
    tiles_per_seq = n // n_mod // row_tile
    once = dict(pipeline_mode=pl.Buffered(1)) if n_mod == 1 else {}

    def full(a):
        nd = a.ndim
        return pl.BlockSpec(a.shape, lambda i: (0,) * nd)

    def seq_block(rows, **kw):
        return pl.BlockSpec((rows, C), lambda i: (i // tiles_per_seq, 0), **kw)

    def row_block(cols):
        return pl.BlockSpec((row_tile, cols), lambda i: (i, 0))

    levels = []
    if g == 1:
        assert hrp >= DOUBLING_HIST
        levels = [pltpu.VMEM((row_tile + 24, C), _f32), pltpu.VMEM((row_tile + 16, 3 * GROUP), _f32),
                  pltpu.VMEM((row_tile + 8, 2 * GROUP), _f32)]

    return pl.pallas_call(
        functools.partial(_mixer_kernel, g, tiles_per_seq, start_pos),
        out_shape=[
            jax.ShapeDtypeStruct((n, D), _f32),
            jax.ShapeDtypeStruct((n, D // 2), jnp.int32),
            jax.ShapeDtypeStruct((n, LANES), _f32),
            jax.ShapeDtypeStruct(hc.shape, _f32),
            jax.ShapeDtypeStruct(hp.shape, _f32),
        ],
        grid=(n // row_tile,),
        in_specs=[row_block(D),
                  pl.BlockSpec((None, g, 6 * D), lambda i: (i // tiles_per_seq, 0, 0), **once),
                  seq_block(hrc, **once), seq_block(hrp, **once)] + [full(a) for a in weights],
        out_specs=[row_block(D), row_block(D // 2), row_block(LANES), seq_block(hrc), seq_block(hrp)],
        scratch_shapes=[pltpu.VMEM((hrc + row_tile, C), _f32), pltpu.VMEM((hrp + row_tile, C), _f32)] + levels,
        compiler_params=pltpu.CompilerParams(vmem_limit_bytes=VMEM_LIMIT),
        name="mixer",
    )(x2, mod3, hc, hp, *weights)


def _plan_kernel(lgp_ref, lgs_ref, dest_ref, gate_ref, meta_ref, idx_s, rank_s):
    t = ROW_TILE
    e_iota = lax.broadcasted_iota(jnp.int32, (E, t), 0)
    tri = (lax.broadcasted_iota(jnp.int32, (t, t), 0)
           < lax.broadcasted_iota(jnp.int32, (t, t), 1)).astype(_f32).astype(_bf16)
    zeros_rest = jnp.zeros((LANES - TOP_K, t), _f32)

    def tile_body(lg_ref, off, i, carry):
        lt = lg_ref[i].T[0:E, :]
        vals, idxs = [], []
        for _ in range(TOP_K):
            mx = jnp.max(lt, axis=0, keepdims=True)
            ix = jnp.min(jnp.where(lt == mx, e_iota, E), axis=0, keepdims=True)
            vals.append(mx)
            idxs.append(ix)
            lt = jnp.where(e_iota == ix, -jnp.inf, lt)
        ex = [jnp.exp(vk - vals[0]) for vk in vals]
        den = ex[0] + ex[1] + ex[2] + ex[3]
        gates = [ek / den for ek in ex]
        gate_ref[off + i] = jnp.concatenate(gates + [zeros_rest], axis=0).T

        ohs = [(e_iota == ix) for ix in idxs]
        oh = (ohs[0] | ohs[1] | ohs[2] | ohs[3]).astype(_f32)
        before = _dot(oh.astype(_bf16), tri) + carry
        ranks = [jnp.sum(jnp.where(o, before, 0.0), axis=0, keepdims=True) for o in ohs]
        idx_s[off + i] = jnp.concatenate(idxs + idxs, axis=0)
        rank_s[off + i] = jnp.concatenate(ranks + ranks, axis=0).astype(jnp.int32)
        return carry + jnp.sum(oh, axis=1, keepdims=True)

    n_p, n_s = lgp_ref.shape[0], lgs_ref.shape[0]
    counts = lax.fori_loop(0, n_p, functools.partial(tile_body, lgp_ref, 0), jnp.zeros((E, 1), _f32))
    counts = lax.fori_loop(0, n_s, functools.partial(tile_body, lgs_ref, n_p), counts)
    padded = jnp.ceil(counts / SLOT_PAD) * SLOT_PAD
    low = (lax.broadcasted_iota(jnp.int32, (E, E), 1)
           <= lax.broadcasted_iota(jnp.int32, (E, E), 0)).astype(_f32)
    pad_end = _dot_exact(low, jnp.broadcast_to(padded, (E, LANES)))[:, 0:1]
    pad_start = pad_end - padded

    def dest_body(i, c):
        ix = idx_s[i]
        rk = rank_s[i]
        rows = []
        for k in range(TOP_K):
            st = jnp.sum(jnp.where(e_iota == ix[k:k + 1], pad_start, 0.0), axis=0, keepdims=True)
            rows.append(st.astype(jnp.int32) + rk[k:k + 1])
        dest_ref[i] = jnp.concatenate(rows + rows, axis=0)
        return c

    lax.fori_loop(0, n_p + n_s, dest_body, 0)

    sub = lax.broadcasted_iota(jnp.int32, (E, LANES), 0)
    lane = lax.broadcasted_iota(jnp.int32, (E, LANES), 1)

    def to_lanes(col):
        return jnp.sum(jnp.where(sub == lane, col, 0.0), axis=0, keepdims=True).astype(jnp.int32)

    later = jnp.min(jnp.where((sub > lane) & (counts > 0.0), sub, E), axis=0, keepdims=True)
    meta_ref[...] = jnp.concatenate(
        [to_lanes(counts), to_lanes(pad_start), to_lanes(padded), later,
         jnp.zeros((SUBLANES - 4, LANES), jnp.int32)], axis=0)


def _plan(logits_p, logits_s):
    n_tiles = (logits_p.shape[0] + logits_s.shape[0]) // ROW_TILE
    return pl.pallas_call(
        _plan_kernel,
        out_shape=[
            jax.ShapeDtypeStruct((n_tiles, 2 * TOP_K, ROW_TILE), jnp.int32),
            jax.ShapeDtypeStruct((n_tiles, ROW_TILE, LANES), _f32),
            jax.ShapeDtypeStruct((SUBLANES, LANES), jnp.int32),
        ],
        scratch_shapes=[pltpu.VMEM((n_tiles, 2 * TOP_K, ROW_TILE), jnp.int32),
                        pltpu.VMEM((n_tiles, 2 * TOP_K, ROW_TILE), jnp.int32)],
        compiler_params=pltpu.CompilerParams(vmem_limit_bytes=VMEM_LIMIT),
        name="plan",
    )(logits_p.reshape(-1, ROW_TILE, LANES), logits_s.reshape(-1, ROW_TILE, LANES))


def _dispatch(v_p, v_s, dests, n_rows_out):
    n_p, n_s = v_p.shape[0], v_s.shape[0]
    width = v_p.shape[1]
    w = DISPATCH_WINDOW
    n_pw, n_windows = n_p // w, (n_p + n_s) // w
    mesh = plsc.VectorSubcoreMesh(core_axis_name="core", subcore_axis_name="subcore")
    n_workers = mesh.num_cores * mesh.num_subcores
    n_rounds = -(-n_windows // n_workers)

    @functools.partial(
        pl.kernel, mesh=mesh, name="dispatch",
        out_type=jax.ShapeDtypeStruct((n_rows_out, width), jnp.int32),
        scratch_types=[pltpu.VMEM((w, width), jnp.int32)] * 2 + [pltpu.VMEM((w,), jnp.int32)] * (2 * TOP_K)
        + [pltpu.SemaphoreType.DMA] * 3)
    def scatter_rows(vp_hbm, vs_hbm, d0_hbm, d1_hbm, d2_hbm, d3_hbm, o_hbm, rows0, rows1, *rest):
        worker = lax.axis_index("subcore") * mesh.num_cores + lax.axis_index("core")
        rows = (rows0, rows1)
        idx = (rest[:TOP_K], rest[TOP_K:2 * TOP_K])
        lsem, ssem = rest[2 * TOP_K:2 * TOP_K + 2], rest[2 * TOP_K + 2]
        d_hbm = (d0_hbm, d1_hbm, d2_hbm, d3_hbm)

        def load(c, slot, act):
            def index_reads():
                t0 = pl.multiple_of(c * w, w)
                for k in range(TOP_K):
                    act(pltpu.make_async_copy(d_hbm[k].at[pl.ds(t0, w)], idx[slot][k], lsem[slot]))

            @pl.when(c < n_pw)
            def _():
                src = vp_hbm.at[pl.ds(pl.multiple_of(c * w, w), w)]
                act(pltpu.make_async_copy(src, rows[slot], lsem[slot]))
                index_reads()

            @pl.when((c >= n_pw) & (c < n_windows))
            def _():
                src = vs_hbm.at[pl.ds(pl.multiple_of((c - n_pw) * w, w), w)]
                act(pltpu.make_async_copy(src, rows[slot], lsem[slot]))
                index_reads()

        load(worker, 0, lambda cp: cp.start())
        for j in range(n_rounds):
            c = j * n_workers + worker
            slot = j % 2
            load(c, slot, lambda cp: cp.wait())
            if j + 1 < n_rounds:
                load(c + n_workers, 1 - slot, lambda cp: cp.start())

            @pl.when(c < n_windows)
            def _():
                copies = [pltpu.async_copy(rows[slot], o_hbm.at[iv], ssem) for iv in idx[slot]]
                for cp in copies:
                    cp.wait()

    return scatter_rows(v_p, v_s, *dests)


def _experts_kernel(cnt_ref, row0_ref, pad_ref, nxt_ref, x_hbm, wu_hbm, bu_ref, wd_hbm, bd_ref, y_hbm,
                    xbuf, ybuf, wu_st, wd_st, wu_bf, wd_bf, ysz, xsem, ysem, wsem):
    def w_fetch(e):
        return (pltpu.make_async_copy(wu_hbm.at[e], wu_st, wsem.at[0]),
                pltpu.make_async_copy(wd_hbm.at[e], wd_st, wsem.at[1]))

    def x_fetch(row, size, slot):
        rows = pl.ds(pl.multiple_of(row, SLOT_PAD), size)
        return pltpu.make_async_copy(x_hbm.at[rows, :], xbuf.at[slot, pl.ds(0, size), :], xsem.at[slot])

    def y_store(row, size, slot):
        rows = pl.ds(pl.multiple_of(row, SLOT_PAD), size)
        return pltpu.make_async_copy(ybuf.at[slot, pl.ds(0, size), :], y_hbm.at[rows, :], ysem.at[slot])

    def y_wait(slot):
        for size in (BIG_BLOCK, SLOT_PAD):
            @pl.when(ysz[slot] == size)
            def _():
                y_store(0, size, slot).wait()

    def fetch_first(e, slot):
        @pl.when(pad_ref[e] >= BIG_BLOCK)
        def _():
            x_fetch(row0_ref[e], BIG_BLOCK, slot).start()

        @pl.when((pad_ref[e] > 0) & (pad_ref[e] < BIG_BLOCK))
        def _():
            x_fetch(row0_ref[e], SLOT_PAD, slot).start()

    ysz[0] = 0
    ysz[1] = 0
    for c in w_fetch(0):
        c.start()
    fetch_first(jnp.where(pad_ref[0] > 0, 0, nxt_ref[0]), 0)

    def expert_body(e, n_done):
        for c in w_fetch(e):
            c.wait()
        padded = pad_ref[e]

        @pl.when(padded > 0)
        def _():
            wu_bf[...] = wu_st[...].astype(_bf16)
            wd_bf[...] = wd_st[...].astype(_bf16)

        @pl.when(e + 1 < E)
        def _():
            for c in w_fetch(e + 1):
                c.start()

        b_up = bu_ref[e]
        b_down = bd_ref[e]
        row0 = row0_ref[e]
        n_big = padded // BIG_BLOCK
        n_small = (padded - n_big * BIG_BLOCK) // SLOT_PAD
        small0 = row0 + n_big * BIG_BLOCK
        nxt = nxt_ref[e]

        def pass_body(size, j, n_done):
            slot = n_done % 2
            big = size == BIG_BLOCK
            row = row0 + j * BIG_BLOCK if big else small0 + j * SLOT_PAD
            x_fetch(row, size, slot).wait()

            more = j + 1 < (n_big if big else n_small)
            tail = (n_small > 0) if big else False

            @pl.when(more)
            def _():
                x_fetch(row + size, size, 1 - slot).start()

            if big:
                @pl.when(jnp.logical_not(more) & tail)
                def _():
                    x_fetch(small0, SLOT_PAD, 1 - slot).start()

            @pl.when(jnp.logical_not(more) & jnp.logical_not(tail) & (nxt < E))
            def _():
                fetch_first(jnp.minimum(nxt, E - 1), 1 - slot)

            rows = lax.broadcasted_iota(jnp.int32, (size, 1), 0)
            x = _unpack_rows(jnp.where(rows < cnt_ref[e] - (row - row0), xbuf[slot, pl.ds(0, size), :], 0))
            hcat = _dot(x, wu_bf[...]) + b_up
            glu = jnp.minimum(hcat[:, 0:F], SWIGLU_LIMIT)
            lin = jnp.clip(hcat[:, F:2 * F], -SWIGLU_LIMIT, SWIGLU_LIMIT)
            act = glu * _sigmoid(SWIGLU_ALPHA * glu) * (lin + 1.0)
            y = _pack_rows(_dot(act.astype(_bf16), wd_bf[...]) + b_down)

            y_wait(slot)
            ybuf[slot, pl.ds(0, size), :] = y
            y_store(row, size, slot).start()
            ysz[slot] = size
            return n_done + 1

        n_done = lax.fori_loop(0, n_big, functools.partial(pass_body, BIG_BLOCK), n_done)
        return lax.fori_loop(0, n_small, functools.partial(pass_body, SLOT_PAD), n_done)

    lax.fori_loop(0, E, expert_body, 0)
    y_wait(0)
    y_wait(1)


def _experts(cnt, row0, padded, nxt, xpad, w_up, b_up, w_down, b_down):
    def full(a):
        nd = a.ndim
        return pl.BlockSpec(a.shape, lambda i, *_: (0,) * nd)

    return pl.pallas_call(
        _experts_kernel,
        out_shape=jax.ShapeDtypeStruct(xpad.shape, jnp.int32),
        grid_spec=pltpu.PrefetchScalarGridSpec(
            num_scalar_prefetch=4,
            grid=(1,),
            in_specs=[pl.BlockSpec(memory_space=pl.ANY),
                      pl.BlockSpec(memory_space=pl.ANY), full(b_up),
                      pl.BlockSpec(memory_space=pl.ANY), full(b_down)],
            out_specs=pl.BlockSpec(memory_space=pl.ANY),
            scratch_shapes=[pltpu.VMEM((2, BIG_BLOCK, D // 2), jnp.int32),
                            pltpu.VMEM((2, BIG_BLOCK, D // 2), jnp.int32),
                            pltpu.VMEM((D, 2 * F), _f32), pltpu.VMEM((F, D), _f32),
                            pltpu.VMEM((D, 2 * F), _bf16), pltpu.VMEM((F, D), _bf16),
                            pltpu.SMEM((2,), jnp.int32),
                            pltpu.SemaphoreType.DMA((2,)), pltpu.SemaphoreType.DMA((2,)),
                            pltpu.SemaphoreType.DMA((2,))],
        ),
        compiler_params=pltpu.CompilerParams(vmem_limit_bytes=VMEM_LIMIT),
        name="experts",
    )(cnt, row0, padded, nxt, xpad, w_up, b_up, w_down, b_down)


def _gather_rows(ypad, dest_all):
    n_out = dest_all.shape[0]
    width = ypad.shape[1]
    w = SC_WINDOW
    mesh = plsc.VectorSubcoreMesh(core_axis_name="core", subcore_axis_name="subcore")
    n_workers = mesh.num_cores * mesh.num_subcores
    n_windows = n_out // w

    @functools.partial(
        pl.kernel, mesh=mesh, name="gather_rows",
        out_type=jax.ShapeDtypeStruct((n_out, width), jnp.int32),
        scratch_types=[pltpu.VMEM((w, width), jnp.int32), pltpu.VMEM((w,), jnp.int32), pltpu.SemaphoreType.DMA])
    def gather_rows(y_hbm, d_hbm, o_hbm, rows, iv, sem):
        worker = lax.axis_index("subcore") * mesh.num_cores + lax.axis_index("core")

        @pl.loop(0, n_windows // n_workers)
        def _(j):
            r0 = pl.multiple_of((j * n_workers + worker) * w, w)
            pltpu.sync_copy(d_hbm.at[pl.ds(r0, w)], iv)
            pltpu.async_copy(y_hbm.at[iv], rows, sem).wait()
            pltpu.sync_copy(rows, o_hbm.at[pl.ds(r0, w)])

    assert n_windows % n_workers == 0
    return gather_rows(ypad, dest_all)


def _combine_kernel(h_ref, mod_ref, gate_ref, g2_ref, b2_ref, y_ref, *aliased_and_out):
    o_ref = aliased_and_out[-1]
    gates = gate_ref[...]
    half = D // 2
    f_hi = jnp.zeros((h_ref.shape[0], half), _f32)
    f_lo = jnp.zeros((h_ref.shape[0], half), _f32)
    for k in range(TOP_K):
        p = y_ref[k]
        gk = gates[:, k:k + 1]
        f_hi = f_hi + gk * lax.bitcast_convert_type(p & jnp.int32(-65536), _f32)
        f_lo = f_lo + gk * lax.bitcast_convert_type(lax.shift_left(p, 16), _f32)
    f = jnp.concatenate([f_hi, f_lo], axis=1)
    gate2 = mod_ref[...][:, 5 * D:6 * D]
    pre = ALPHA * h_ref[...] + _per_seq(f, gate2, jnp.multiply)
    o_ref[...] = _layer_norm(pre, g2_ref[...], b2_ref[...])


def _combine(h, mod3, gates, ln2_g, ln2_b, y4, token0, rows_per_mod, row0, out_so_far):
    t = COMBINE_TILE
    hoff = row0 // t
    goff = (token0 + row0) // t
    g = mod3.shape[1]
    tiles_per_mod = rows_per_mod // t
    in_specs = [pl.BlockSpec((t, D), lambda i: (i + hoff, 0)),
                pl.BlockSpec((None, g, 6 * D), lambda i: ((i + hoff) // tiles_per_mod, 0, 0)),
                pl.BlockSpec((t, LANES), lambda i: (i + goff, 0)),
                pl.BlockSpec((1, D), lambda i: (0, 0)),
                pl.BlockSpec((1, D), lambda i: (0, 0)),
                pl.BlockSpec((TOP_K, t, D // 2), lambda i: (0, i, 0))]
    args = [h, mod3, gates, ln2_g, ln2_b, y4]
    aliases = {}
    if out_so_far is not None:
        in_specs.append(pl.BlockSpec(memory_space=pl.ANY))
        args.append(out_so_far)
        aliases = {len(args) - 1: 0}
    return pl.pallas_call(
        _combine_kernel,
        out_shape=jax.ShapeDtypeStruct(h.shape, _f32),
        grid=(y4.shape[1] // t,),
        in_specs=in_specs,
        out_specs=pl.BlockSpec((t, D), lambda i: (i + hoff, 0)),
        input_output_aliases=aliases,
        compiler_params=pltpu.CompilerParams(vmem_limit_bytes=VMEM_LIMIT),
        name="combine",
    )(*args)


def _time_major(a):
    return a.transpose(1, 0, 2)


def kernel(x_prompt, x_sample, c_prompt, c_sample, state_conv, state_pool, w_ada, b_ada, w_in,
           conv_w, w_out_a, w_pool, ls_pool, w_out_b, w_o, ln1_g, ln1_b, w_router, b_router,
           w_up, b_up, w_down, b_down, ln2_g, ln2_b):
    n_seq_p, seq, _ = x_prompt.shape
    n_seq_s, dec_seq, _ = x_sample.shape
    n_p, n_s = n_seq_p * seq, n_seq_s * dec_seq
    n = n_p + n_s
    n_slots = TOP_K * n + E * SLOT_PAD
    l = 0

    mod = _ada(jnp.concatenate([c_prompt, c_sample], axis=0), w_ada[l], b_ada[l][None])
    mod_p = mod[:n_seq_p][:, None, :]
    mod_s = mod[n_seq_p:][None]

    weights = (
        w_in[l].astype(_bf16), conv_w[l], w_out_a[l].astype(_bf16), w_pool[l].astype(_bf16),
        ls_pool[l][None], w_out_b[l].astype(_bf16), w_o[l].astype(_bf16), ln1_g[l][None], ln1_b[l][None],
        jnp.pad(w_router[l], ((0, 0), (0, LANES - E))), jnp.pad(b_router[l], (0, LANES - E))[None],
    )
    hc_p, hp_p = _hist_steps(CONV_HIST, 1), DOUBLING_HIST
    h_p, v_p, lg_p, nc_p, np_p = _mixer(
        x_prompt.reshape(n_p, D), mod_p, jnp.zeros((n_seq_p * hc_p, C), _f32),
        jnp.zeros((n_seq_p * hp_p, C), _f32), weights, PROMPT_ROW_TILE, 0)
    hc_s, hp_s = _hist_steps(CONV_HIST, n_seq_s), _hist_steps(POOL_HIST, n_seq_s)
    hist_c = jnp.pad(_time_major(state_conv[l]), ((hc_s - CONV_HIST, 0), (0, 0), (0, 0)))
    hist_p = jnp.pad(_time_major(state_pool[l]), ((hp_s - POOL_HIST, 0), (0, 0), (0, 0)))
    h_s, v_s, lg_s, nc_s, np_s = _mixer(
        _time_major(x_sample).reshape(n_s, D), mod_s, hist_c.reshape(hc_s * n_seq_s, C),
        hist_p.reshape(hp_s * n_seq_s, C), weights, SAMPLE_ROW_TILE, PAST_LEN)

    dest8, gates3, meta = _plan(lg_p, lg_s)
    cnt, row0, padded, nxt = (meta[i, :E] for i in range(4))
    gates = gates3.reshape(n, LANES)

    dests = [dest8[:, k, :].reshape(n) for k in range(TOP_K)]
    xpad = _dispatch(v_p, v_s, dests, n_slots)
    ypad = _experts(cnt, row0, padded, nxt, xpad, w_up[l], b_up[l][:, None, :], w_down[l], b_down[l][:, None, :])

    def gathered(t0, rows):
        idx = jnp.concatenate([dk[t0:t0 + rows] for dk in dests])
        return _gather_rows(ypad, idx).reshape(TOP_K, rows, D // 2)

    chunk = n_p // COMBINE_CHUNKS
    y4_p = [gathered(c * chunk, chunk) for c in range(COMBINE_CHUNKS)]
    y4_s = gathered(n_p, n_s)
    g2, b2 = ln2_g[l][None], ln2_b[l][None]
    y_p = None
    for c in range(COMBINE_CHUNKS):
        y_p = _combine(h_p, mod_p, gates, g2, b2, y4_p[c], 0, seq, c * chunk, y_p)
    y_s = _combine(h_s, mod_s, gates, g2, b2, y4_s, n_p, n_s, 0, None)

    y_prompt = y_p.reshape(n_seq_p, seq, D)
    y_sample = _time_major(y_s.reshape(dec_seq, n_seq_s, D))
    new_conv_p = nc_p.reshape(n_seq_p, hc_p, C)[:, hc_p - CONV_HIST:][None]
    new_pool_p = np_p.reshape(n_seq_p, hp_p, C)[:, hp_p - POOL_HIST:][None]
    new_conv_s = _time_major(nc_s.reshape(hc_s, n_seq_s, C)[hc_s - CONV_HIST:])[None]
    new_pool_s = _time_major(np_s.reshape(hp_s, n_seq_s, C)[hp_s - POOL_HIST:])[None]
    return (y_prompt, y_sample, new_conv_p, new_pool_p, new_conv_s, new_pool_s)
```

```python
import functools

import jax
import jax.numpy as jnp
from jax import lax
from jax.experimental import pallas as pl
from jax.experimental.pallas import tpu as pltpu
from jax.experimental.pallas import tpu_sc as plsc

D = 1024
C = 512
N_GROUPS = 4
GROUP = C // N_GROUPS
CONV_HIST = 2
POOL_HIST = 15
DOUBLING_HIST = 32
E = 32
TOP_K = 4
F = 1024
SWIGLU_LIMIT = 7.0
SWIGLU_ALPHA = 1.702
LN_EPS = 1e-5
DEPTH = 1
ALPHA = (2 * DEPTH) ** 0.25
PAST_LEN = 16384

LANES = 128
SUBLANES = 8
ROW_TILE = 512
PROMPT_ROW_TILE = 1024
SAMPLE_ROW_TILE = 1024
SLOT_PAD = 256
BIG_BLOCK = 1024
COMBINE_TILE = 512
COMBINE_CHUNKS = 4
SC_WINDOW = 128
DISPATCH_WINDOW = 64
ADA_COLS = 1536
VMEM_LIMIT = 56 * 1024 * 1024

_f32 = jnp.float32
_bf16 = jnp.bfloat16


def _dot(a, b):
    return jnp.dot(a, b, preferred_element_type=_f32)


def _dot_exact(a, b):
    return lax.dot_general(a, b, (((1,), (0,)), ((), ())),
                           precision=lax.Precision.HIGHEST, preferred_element_type=_f32)


def _pack_rows(x):
    w = x.shape[1] // 2
    hi = lax.bitcast_convert_type(x[:, :w].astype(_bf16).astype(_f32), jnp.int32)
    lo = lax.bitcast_convert_type(x[:, w:].astype(_bf16).astype(_f32), jnp.int32)
    return hi | lax.shift_right_logical(lo, 16)


def _unpack_rows(p):
    hi = lax.bitcast_convert_type(p & jnp.int32(-65536), _f32)
    lo = lax.bitcast_convert_type(lax.shift_left(p, 16), _f32)
    return jnp.concatenate([hi, lo], axis=1).astype(_bf16)


def _sigmoid(x):
    return 0.5 * jnp.tanh(0.5 * x) + 0.5


def _per_seq(x, m, op):
    g = m.shape[0]
    if g == 1:
        return op(x, m)
    r, n = x.shape
    return op(x.reshape(r // g, g, n), m[None]).reshape(r, n)


def _layer_norm(x, g, b):
    mu = jnp.mean(x, axis=-1, keepdims=True)
    xc = x - mu
    var = jnp.mean(xc * xc, axis=-1, keepdims=True)
    return xc * lax.rsqrt(var + LN_EPS) * g + b


def _hist_steps(needed, g):
    return -(-needed * g // SUBLANES) * SUBLANES // g


def _ada_kernel(c_ref, w_ref, b_ref, o_ref):
    c = c_ref[...]
    o_ref[...] = _dot((c * _sigmoid(c)).astype(_bf16), w_ref[...].astype(_bf16)) + b_ref[...]


def _ada(c, w_ada, b_ada):
    rows = c.shape[0]
    cols = w_ada.shape[1]
    bn = ADA_COLS
    return pl.pallas_call(
        _ada_kernel,
        out_shape=jax.ShapeDtypeStruct((rows, cols), _f32),
        grid=(cols // bn,),
        in_specs=[pl.BlockSpec((rows, D), lambda j: (0, 0)),
                  pl.BlockSpec((D, bn), lambda j: (0, j)),
                  pl.BlockSpec((1, bn), lambda j: (0, j))],
        out_specs=pl.BlockSpec((rows, bn), lambda j: (0, j)),
        compiler_params=pltpu.CompilerParams(vmem_limit_bytes=VMEM_LIMIT),
        name="ada",
    )(c, w_ada, b_ada)


def _mixer_kernel(g, tiles_per_seq, start_pos,
                  x_ref, mod_ref, hc_ref, hp_ref,
                  win_ref, cw_ref, woa_ref, wpool_ref, ls_ref, wob_ref, wo_ref, g1_ref, b1_ref,
                  wr_ref, br_ref,
                  h_ref, v_ref, lg_ref, nc_ref, np_ref, zbuf, pbuf, *lvl):
    r = x_ref.shape[0]
    hrc = hc_ref.shape[0]
    hrp = hp_ref.shape[0]
    j = pl.program_id(0) % tiles_per_seq

    @pl.when(j == 0)
    def _():
        zbuf[pl.ds(0, hrc), :] = hc_ref[...]
        pbuf[pl.ds(0, hrp), :] = hp_ref[...]

    @pl.when(j != 0)
    def _():
        zt = zbuf[pl.ds(r, hrc), :]
        pt = pbuf[pl.ds(r, hrp), :]
        zbuf[pl.ds(0, hrc), :] = zt
        pbuf[pl.ds(0, hrp), :] = pt

    m = mod_ref[...]
    shift1, scale1, gate1 = m[:, 0:D], m[:, D:2 * D], m[:, 2 * D:3 * D]
    shift2, scale2 = m[:, 3 * D:4 * D], m[:, 4 * D:5 * D]

    x = x_ref[...]
    u = _per_seq(_per_seq(x, 1.0 + scale1, jnp.multiply), shift1, jnp.add).astype(_bf16)

    z = _dot(u, win_ref[:, C:2 * C]) * _dot(u, win_ref[:, 2 * C:3 * C])
    zbuf[pl.ds(hrc, r), :] = z
    cw = cw_ref[...]
    conv = (cw[0:1] * zbuf[pl.ds(hrc - 2 * g, r), :] + cw[1:2] * zbuf[pl.ds(hrc - g, r), :]
            + cw[2:3] * z)
    y_a = _dot((_dot(u, win_ref[:, 0:C]) * conv).astype(_bf16), woa_ref[...])

    xp = _dot(u, win_ref[:, 3 * C:4 * C])
    pbuf[pl.ds(hrp, r), :] = xp
    pos = start_pos + j * (r // g) + lax.broadcasted_iota(jnp.int32, (r, 1), 0) // g
    wins = []
    if lvl:
        s2, s4, s8 = lvl
        s2[...] = pbuf[pl.ds(hrp - 24, r + 24), :] + pbuf[pl.ds(hrp - 25, r + 24), :]
        s4[...] = s2[pl.ds(8, r + 16), GROUP:C] + s2[pl.ds(6, r + 16), GROUP:C]
        s8[...] = s4[pl.ds(8, r + 8), GROUP:3 * GROUP] + s4[pl.ds(4, r + 8), GROUP:3 * GROUP]
        wins = [s2[pl.ds(24, r), 0:GROUP], s4[pl.ds(16, r), 0:GROUP], s8[pl.ds(8, r), 0:GROUP],
                s8[pl.ds(8, r), GROUP:2 * GROUP] + s8[pl.ds(0, r), GROUP:2 * GROUP]]
    else:
        acc = xp
        for grp in range(N_GROUPS):
            lo = grp * GROUP
            wdw = 2 ** (grp + 1)
            for back in range(wdw // 2, wdw):
                sh = pbuf[pl.ds(hrp - back * g, r), lo:C]
                acc = jnp.concatenate([acc[:, 0:lo], acc[:, lo:C] + sh], axis=1) if lo else acc + sh
            wins.append(acc[:, lo:lo + GROUP])
    yg = []
    for grp in range(N_GROUPS):
        lo = grp * GROUP
        inv_cnt = 1.0 / jnp.minimum(2 ** (grp + 1), pos + 1).astype(_f32)
        diff = wins[grp] * inv_cnt - xp[:, lo:lo + GROUP]
        yg.append(_dot(diff.astype(_bf16), wpool_ref[grp]))
    y_b = _dot((jnp.concatenate(yg, axis=1) * ls_ref[...]).astype(_bf16), wob_ref[...])

    g_a = _dot(u, win_ref[:, 4 * C:4 * C + D])
    g_b = _dot(u, win_ref[:, 4 * C + D:4 * C + 2 * D])
    merged = _sigmoid(g_a) * y_a + _sigmoid(g_b) * y_b
    o = _dot(merged.astype(_bf16), wo_ref[...])
    h = _layer_norm(ALPHA * x + _per_seq(o, gate1, jnp.multiply), g1_ref[...], b1_ref[...])
    v = _per_seq(_per_seq(h, 1.0 + scale2, jnp.multiply), shift2, jnp.add)
    h_ref[...] = h
    v_ref[...] = _pack_rows(v)
    lg_ref[...] = _dot(v.astype(_bf16), wr_ref[...].astype(_bf16)) + br_ref[...]
    nc_ref[...] = zbuf[pl.ds(r, hrc), :]
    np_ref[...] = pbuf[pl.ds(r, hrp), :]


def _mixer(x2, mod3, hc, hp, weights, row_tile, start_pos):
    n = x2.shape[0]
    n_mod, g, _ = mod3.shape
    hrc, hrp = hc.shape[0] // n_mod, hp.shape[0] // n_mod
    tiles_per_seq = n // n_mod // row_tile
    once = dict(pipeline_mode=pl.Buffered(1)) if n_mod == 1 else {}

    def full(a):
        nd = a.ndim
        return pl.BlockSpec(a.shape, lambda i: (0,) * nd)

    def seq_block(rows, **kw):
        return pl.BlockSpec((rows, C), lambda i: (i // tiles_per_seq, 0), **kw)

    def row_block(cols):
        return pl.BlockSpec((row_tile, cols), lambda i: (i, 0))

    levels = []
    if g == 1:
        assert hrp >= DOUBLING_HIST
        levels = [pltpu.VMEM((row_tile + 24, C), _f32), pltpu.VMEM((row_tile + 16, 3 * GROUP), _f32),
                  pltpu.VMEM((row_tile + 8, 2 * GROUP), _f32)]

    return pl.pallas_call(
        functools.partial(_mixer_kernel, g, tiles_per_seq, start_pos),
        out_shape=[
            jax.ShapeDtypeStruct((n, D), _f32),
            jax.ShapeDtypeStruct((n, D // 2), jnp.int32),
            jax.ShapeDtypeStruct((n, LANES), _f32),
            jax.ShapeDtypeStruct(hc.shape, _f32),
            jax.ShapeDtypeStruct(hp.shape, _f32),
        ],
        grid=(n // row_tile,),
        in_specs=[row_block(D),
                  pl.BlockSpec((None, g, 6 * D), lambda i: (i // tiles_per_seq, 0, 0), **once),
                  seq_block(hrc, **once), seq_block(hrp, **once)] + [full(a) for a in weights],
        out_specs=[row_block(D), row_block(D // 2), row_block(LANES), seq_block(hrc), seq_block(hrp)],
        scratch_shapes=[pltpu.VMEM((hrc + row_tile, C), _f32), pltpu.VMEM((hrp + row_tile, C), _f32)] + levels,
        compiler_params=pltpu.CompilerParams(vmem_limit_bytes=VMEM_LIMIT),
        name="mixer",
    )(x2, mod3, hc, hp, *weights)


def _plan_kernel(lgp_ref, lgs_ref, dest_ref, gate_ref, meta_ref, idx_s, rank_s):
    t = ROW_TILE
    e_iota = lax.broadcasted_iota(jnp.int32, (E, t), 0)
    tri = (lax.broadcasted_iota(jnp.int32, (t, t), 0)
           < lax.broadcasted_iota(jnp.int32, (t, t), 1)).astype(_f32).astype(_bf16)
    zeros_rest = jnp.zeros((LANES - TOP_K, t), _f32)

    def tile_body(lg_ref, off, i, carry):
        lt = lg_ref[i].T[0:E, :]
        vals, idxs = [], []
        for _ in range(TOP_K):
            mx = jnp.max(lt, axis=0, keepdims=True)
            ix = jnp.min(jnp.where(lt == mx, e_iota, E), axis=0, keepdims=True)
            vals.append(mx)
            idxs.append(ix)
            lt = jnp.where(e_iota == ix, -jnp.inf, lt)
        ex = [jnp.exp(vk - vals[0]) for vk in vals]
        den = ex[0] + ex[1] + ex[2] + ex[3]
        gates = [ek / den for ek in ex]
        gate_ref[off + i] = jnp.concatenate(gates + [zeros_rest], axis=0).T

        ohs = [(e_iota == ix) for ix in idxs]
        oh = (ohs[0] | ohs[1] | ohs[2] | ohs[3]).astype(_f32)
        before = _dot(oh.astype(_bf16), tri) + carry
        ranks = [jnp.sum(jnp.where(o, before, 0.0), axis=0, keepdims=True) for o in ohs]
        idx_s[off + i] = jnp.concatenate(idxs + idxs, axis=0)
        rank_s[off + i] = jnp.concatenate(ranks + ranks, axis=0).astype(jnp.int32)
        return carry + jnp.sum(oh, axis=1, keepdims=True)

    n_p, n_s = lgp_ref.shape[0], lgs_ref.shape[0]
    counts = lax.fori_loop(0, n_p, functools.partial(tile_body, lgp_ref, 0), jnp.zeros((E, 1), _f32))
    counts = lax.fori_loop(0, n_s, functools.partial(tile_body, lgs_ref, n_p), counts)
    padded = jnp.ceil(counts / SLOT_PAD) * SLOT_PAD
    low = (lax.broadcasted_iota(jnp.int32, (E, E), 1)
           <= lax.broadcasted_iota(jnp.int32, (E, E), 0)).astype(_f32)
    pad_end = _dot_exact(low, jnp.broadcast_to(padded, (E, LANES)))[:, 0:1]
    pad_start = pad_end - padded

    def dest_body(i, c):
        ix = idx_s[i]
        rk = rank_s[i]
        rows = []
        for k in range(TOP_K):
            st = jnp.sum(jnp.where(e_iota == ix[k:k + 1], pad_start, 0.0), axis=0, keepdims=True)
            rows.append(st.astype(jnp.int32) + rk[k:k + 1])
        dest_ref[i] = jnp.concatenate(rows + rows, axis=0)
        return c

    lax.fori_loop(0, n_p + n_s, dest_body, 0)

    sub = lax.broadcasted_iota(jnp.int32, (E, LANES), 0)
    lane = lax.broadcasted_iota(jnp.int32, (E, LANES), 1)

    def to_lanes(col):
        return jnp.sum(jnp.where(sub == lane, col, 0.0), axis=0, keepdims=True).astype(jnp.int32)

    later = jnp.min(jnp.where((sub > lane) & (counts > 0.0), sub, E), axis=0, keepdims=True)
    meta_ref[...] = jnp.concatenate(
        [to_lanes(counts), to_lanes(pad_start), to_lanes(padded), later,
         jnp.zeros((SUBLANES - 4, LANES), jnp.int32)], axis=0)


def _plan(logits_p, logits_s):
    n_tiles = (logits_p.shape[0] + logits_s.shape[0]) // ROW_TILE
    return pl.pallas_call(
        _plan_kernel,
        out_shape=[
            jax.ShapeDtypeStruct((n_tiles, 2 * TOP_K, ROW_TILE), jnp.int32),
            jax.ShapeDtypeStruct((n_tiles, ROW_TILE, LANES), _f32),
            jax.ShapeDtypeStruct((SUBLANES, LANES), jnp.int32),
        ],
        scratch_shapes=[pltpu.VMEM((n_tiles, 2 * TOP_K, ROW_TILE), jnp.int32),
                        pltpu.VMEM((n_tiles, 2 * TOP_K, ROW_TILE), jnp.int32)],
        compiler_params=pltpu.CompilerParams(vmem_limit_bytes=VMEM_LIMIT),
        name="plan",
    )(logits_p.reshape(-1, ROW_TILE, LANES), logits_s.reshape(-1, ROW_TILE, LANES))


def _dispatch(v_p, v_s, dests, n_rows_out):
    n_p, n_s = v_p.shape[0], v_s.shape[0]
    width = v_p.shape[1]
    w = DISPATCH_WINDOW
    n_pw, n_windows = n_p // w, (n_p + n_s) // w
    mesh = plsc.VectorSubcoreMesh(core_axis_name="core", subcore_axis_name="subcore")
    n_workers = mesh.num_cores * mesh.num_subcores
    n_rounds = -(-n_windows // n_workers)

    @functools.partial(
        pl.kernel, mesh=mesh, name="dispatch",
        out_type=jax.ShapeDtypeStruct((n_rows_out, width), jnp.int32),
        scratch_types=[pltpu.VMEM((w, width), jnp.int32)] * 2 + [pltpu.VMEM((w,), jnp.int32)] * (2 * TOP_K)
        + [pltpu.SemaphoreType.DMA] * 3)
    def scatter_rows(vp_hbm, vs_hbm, d0_hbm, d1_hbm, d2_hbm, d3_hbm, o_hbm, rows0, rows1, *rest):
        worker = lax.axis_index("subcore") * mesh.num_cores + lax.axis_index("core")
        rows = (rows0, rows1)
        idx = (rest[:TOP_K], rest[TOP_K:2 * TOP_K])
        lsem, ssem = rest[2 * TOP_K:2 * TOP_K + 2], rest[2 * TOP_K + 2]
        d_hbm = (d0_hbm, d1_hbm, d2_hbm, d3_hbm)

        def load(c, slot, act):
            def index_reads():
                t0 = pl.multiple_of(c * w, w)
                for k in range(TOP_K):
                    act(pltpu.make_async_copy(d_hbm[k].at[pl.ds(t0, w)], idx[slot][k], lsem[slot]))

            @pl.when(c < n_pw)
            def _():
                src = vp_hbm.at[pl.ds(pl.multiple_of(c * w, w), w)]
                act(pltpu.make_async_copy(src, rows[slot], lsem[slot]))
                index_reads()

            @pl.when((c >= n_pw) & (c < n_windows))
            def _():
                src = vs_hbm.at[pl.ds(pl.multiple_of((c - n_pw) * w, w), w)]
                act(pltpu.make_async_copy(src, rows[slot], lsem[slot]))
                index_reads()

        load(worker, 0, lambda cp: cp.start())
        for j in range(n_rounds):
            c = j * n_workers + worker
            slot = j % 2
            load(c, slot, lambda cp: cp.wait())
            if j + 1 < n_rounds:
                load(c + n_workers, 1 - slot, lambda cp: cp.start())

            @pl.when(c < n_windows)
            def _():
                copies = [pltpu.async_copy(rows[slot], o_hbm.at[iv], ssem) for iv in idx[slot]]
                for cp in copies:
                    cp.wait()

    return scatter_rows(v_p, v_s, *dests)


def _experts_kernel(cnt_ref, row0_ref, pad_ref, nxt_ref, x_hbm, wu_hbm, bu_ref, wd_hbm, bd_ref, y_hbm,
                    xbuf, ybuf, wu_st, wd_st, wu_bf, wd_bf, ysz, xsem, ysem, wsem):
    def w_fetch(e):
        return (pltpu.make_async_copy(wu_hbm.at[e], wu_st, wsem.at[0]),
                pltpu.make_async_copy(wd_hbm.at[e], wd_st, wsem.at[1]))

    def x_fetch(row, size, slot):
        rows = pl.ds(pl.multiple_of(row, SLOT_PAD), size)
        return pltpu.make_async_copy(x_hbm.at[rows, :], xbuf.at[slot, pl.ds(0, size), :], xsem.at[slot])

    def y_store(row, size, slot):
        rows = pl.ds(pl.multiple_of(row, SLOT_PAD), size)
        return pltpu.make_async_copy(ybuf.at[slot, pl.ds(0, size), :], y_hbm.at[rows, :], ysem.at[slot])

    def y_wait(slot):
        for size in (BIG_BLOCK, SLOT_PAD):
            @pl.when(ysz[slot] == size)
            def _():
                y_store(0, size, slot).wait()

    def fetch_first(e, slot):
        @pl.when(pad_ref[e] >= BIG_BLOCK)
        def _():
            x_fetch(row0_ref[e], BIG_BLOCK, slot).start()

        @pl.when((pad_ref[e] > 0) & (pad_ref[e] < BIG_BLOCK))
        def _():
            x_fetch(row0_ref[e], SLOT_PAD, slot).start()

    ysz[0] = 0
    ysz[1] = 0
    for c in w_fetch(0):
        c.start()
    fetch_first(jnp.where(pad_ref[0] > 0, 0, nxt_ref[0]), 0)

    def expert_body(e, n_done):
        for c in w_fetch(e):
            c.wait()
        padded = pad_ref[e]

        @pl.when(padded > 0)
        def _():
            wu_bf[...] = wu_st[...].astype(_bf16)
            wd_bf[...] = wd_st[...].astype(_bf16)

        @pl.when(e + 1 < E)
        def _():
            for c in w_fetch(e + 1):
                c.start()

        b_up = bu_ref[e]
        b_down = bd_ref[e]
        row0 = row0_ref[e]
        n_big = padded // BIG_BLOCK
        n_small = (padded - n_big * BIG_BLOCK) // SLOT_PAD
        small0 = row0 + n_big * BIG_BLOCK
        nxt = nxt_ref[e]

        def pass_body(size, j, n_done):
            slot = n_done % 2
            big = size == BIG_BLOCK
            row = row0 + j * BIG_BLOCK if big else small0 + j * SLOT_PAD
            x_fetch(row, size, slot).wait()

            more = j + 1 < (n_big if big else n_small)
            tail = (n_small > 0) if big else False

            @pl.when(more)
            def _():
                x_fetch(row + size, size, 1 - slot).start()

            if big:
                @pl.when(jnp.logical_not(more) & tail)
                def _():
                    x_fetch(small0, SLOT_PAD, 1 - slot).start()

            @pl.when(jnp.logical_not(more) & jnp.logical_not(tail) & (nxt < E))
            def _():
                fetch_first(jnp.minimum(nxt, E - 1), 1 - slot)

            rows = lax.broadcasted_iota(jnp.int32, (size, 1), 0)
            x = _unpack_rows(jnp.where(rows < cnt_ref[e] - (row - row0), xbuf[slot, pl.ds(0, size), :], 0))
            hcat = _dot(x, wu_bf[...]) + b_up
            glu = jnp.minimum(hcat[:, 0:F], SWIGLU_LIMIT)
            lin = jnp.clip(hcat[:, F:2 * F], -SWIGLU_LIMIT, SWIGLU_LIMIT)
            act = glu * _sigmoid(SWIGLU_ALPHA * glu) * (lin + 1.0)
            y = _pack_rows(_dot(act.astype(_bf16), wd_bf[...]) + b_down)

            y_wait(slot)
            ybuf[slot, pl.ds(0, size), :] = y
            y_store(row, size, slot).start()
            ysz[slot] = size
            return n_done + 1

        n_done = lax.fori_loop(0, n_big, functools.partial(pass_body, BIG_BLOCK), n_done)
        return lax.fori_loop(0, n_small, functools.partial(pass_body, SLOT_PAD), n_done)

    lax.fori_loop(0, E, expert_body, 0)
    y_wait(0)
    y_wait(1)


def _experts(cnt, row0, padded, nxt, xpad, w_up, b_up, w_down, b_down):
    def full(a):
        nd = a.ndim
        return pl.BlockSpec(a.shape, lambda i, *_: (0,) * nd)

    return pl.pallas_call(
        _experts_kernel,
        out_shape=jax.ShapeDtypeStruct(xpad.shape, jnp.int32),
        grid_spec=pltpu.PrefetchScalarGridSpec(
            num_scalar_prefetch=4,
            grid=(1,),
            in_specs=[pl.BlockSpec(memory_space=pl.ANY),
                      pl.BlockSpec(memory_space=pl.ANY), full(b_up),
                      pl.BlockSpec(memory_space=pl.ANY), full(b_down)],
            out_specs=pl.BlockSpec(memory_space=pl.ANY),
            scratch_shapes=[pltpu.VMEM((2, BIG_BLOCK, D // 2), jnp.int32),
                            pltpu.VMEM((2, BIG_BLOCK, D // 2), jnp.int32),
                            pltpu.VMEM((D, 2 * F), _f32), pltpu.VMEM((F, D), _f32),
                            pltpu.VMEM((D, 2 * F), _bf16), pltpu.VMEM((F, D), _bf16),
                            pltpu.SMEM((2,), jnp.int32),
                            pltpu.SemaphoreType.DMA((2,)), pltpu.SemaphoreType.DMA((2,)),
                            pltpu.SemaphoreType.DMA((2,))],
        ),
        compiler_params=pltpu.CompilerParams(vmem_limit_bytes=VMEM_LIMIT),
        name="experts",
    )(cnt, row0, padded, nxt, xpad, w_up, b_up, w_down, b_down)


def _gather_rows(ypad, dest_all):
    n_out = dest_all.shape[0]
    width = ypad.shape[1]
    w = SC_WINDOW
    mesh = plsc.VectorSubcoreMesh(core_axis_name="core", subcore_axis_name="subcore")
    n_workers = mesh.num_cores * mesh.num_subcores
    n_windows = n_out // w

    @functools.partial(
        pl.kernel, mesh=mesh, name="gather_rows",
        out_type=jax.ShapeDtypeStruct((n_out, width), jnp.int32),
        scratch_types=[pltpu.VMEM((w, width), jnp.int32), pltpu.VMEM((w,), jnp.int32), pltpu.SemaphoreType.DMA])
    def gather_rows(y_hbm, d_hbm, o_hbm, rows, iv, sem):
        worker = lax.axis_index("subcore") * mesh.num_cores + lax.axis_index("core")

        @pl.loop(0, n_windows // n_workers)
        def _(j):
            r0 = pl.multiple_of((j * n_workers + worker) * w, w)
            pltpu.sync_copy(d_hbm.at[pl.ds(r0, w)], iv)
            pltpu.async_copy(y_hbm.at[iv], rows, sem).wait()
            pltpu.sync_copy(rows, o_hbm.at[pl.ds(r0, w)])

    assert n_windows % n_workers == 0
    return gather_rows(ypad, dest_all)


def _combine_kernel(h_ref, mod_ref, gate_ref, g2_ref, b2_ref, y_ref, *aliased_and_out):
    o_ref = aliased_and_out[-1]
    gates = gate_ref[...]
    half = D // 2
    f_hi = jnp.zeros((h_ref.shape[0], half), _f32)
    f_lo = jnp.zeros((h_ref.shape[0], half), _f32)
    for k in range(TOP_K):
        p = y_ref[k]
        gk = gates[:, k:k + 1]
        f_hi = f_hi + gk * lax.bitcast_convert_type(p & jnp.int32(-65536), _f32)
        f_lo = f_lo + gk * lax.bitcast_convert_type(lax.shift_left(p, 16), _f32)
    f = jnp.concatenate([f_hi, f_lo], axis=1)
    gate2 = mod_ref[...][:, 5 * D:6 * D]
    pre = ALPHA * h_ref[...] + _per_seq(f, gate2, jnp.multiply)
    o_ref[...] = _layer_norm(pre, g2_ref[...], b2_ref[...])


def _combine(h, mod3, gates, ln2_g, ln2_b, y4, token0, rows_per_mod, row0, out_so_far):
    t = COMBINE_TILE
    hoff = row0 // t
    goff = (token0 + row0) // t
    g = mod3.shape[1]
    tiles_per_mod = rows_per_mod // t
    in_specs = [pl.BlockSpec((t, D), lambda i: (i + hoff, 0)),
                pl.BlockSpec((None, g, 6 * D), lambda i: ((i + hoff) // tiles_per_mod, 0, 0)),
                pl.BlockSpec((t, LANES), lambda i: (i + goff, 0)),
                pl.BlockSpec((1, D), lambda i: (0, 0)),
                pl.BlockSpec((1, D), lambda i: (0, 0)),
                pl.BlockSpec((TOP_K, t, D // 2), lambda i: (0, i, 0))]
    args = [h, mod3, gates, ln2_g, ln2_b, y4]
    aliases = {}
    if out_so_far is not None:
        in_specs.append(pl.BlockSpec(memory_space=pl.ANY))
        args.append(out_so_far)
        aliases = {len(args) - 1: 0}
    return pl.pallas_call(
        _combine_kernel,
        out_shape=jax.ShapeDtypeStruct(h.shape, _f32),
        grid=(y4.shape[1] // t,),
        in_specs=in_specs,
        out_specs=pl.BlockSpec((t, D), lambda i: (i + hoff, 0)),
        input_output_aliases=aliases,
        compiler_params=pltpu.CompilerParams(vmem_limit_bytes=VMEM_LIMIT),
        name="combine",
    )(*args)


def _time_major(a):
    return a.transpose(1, 0, 2)


def kernel(x_prompt, x_sample, c_prompt, c_sample, state_conv, state_pool, w_ada, b_ada, w_in,
           conv_w, w_out_a, w_pool, ls_pool, w_out_b, w_o, ln1_g, ln1_b, w_router, b_router,
           w_up, b_up, w_down, b_down, ln2_g, ln2_b):
    n_seq_p, seq, _ = x_prompt.shape
    n_seq_s, dec_seq, _ = x_sample.shape
    n_p, n_s = n_seq_p * seq, n_seq_s * dec_seq
    n = n_p + n_s
    n_slots = TOP_K * n + E * SLOT_PAD
    l = 0

    mod = _ada(jnp.concatenate([c_prompt, c_sample], axis=0), w_ada[l], b_ada[l][None])
    mod_p = mod[:n_seq_p][:, None, :]
    mod_s = mod[n_seq_p:][None]

    weights = (
        w_in[l].astype(_bf16), conv_w[l], w_out_a[l].astype(_bf16), w_pool[l].astype(_bf16),
        ls_pool[l][None], w_out_b[l].astype(_bf16), w_o[l].astype(_bf16), ln1_g[l][None], ln1_b[l][None],
        jnp.pad(w_router[l], ((0, 0), (0, LANES - E))), jnp.pad(b_router[l], (0, LANES - E))[None],
    )
    hc_p, hp_p = _hist_steps(CONV_HIST, 1), DOUBLING_HIST
    h_p, v_p, lg_p, nc_p, np_p = _mixer(
        x_prompt.reshape(n_p, D), mod_p, jnp.zeros((n_seq_p * hc_p, C), _f32),
        jnp.zeros((n_seq_p * hp_p, C), _f32), weights, PROMPT_ROW_TILE, 0)
    hc_s, hp_s = _hist_steps(CONV_HIST, n_seq_s), _hist_steps(POOL_HIST, n_seq_s)
    hist_c = jnp.pad(_time_major(state_conv[l]), ((hc_s - CONV_HIST, 0), (0, 0), (0, 0)))
    hist_p = jnp.pad(_time_major(state_pool[l]), ((hp_s - POOL_HIST, 0), (0, 0), (0, 0)))
    h_s, v_s, lg_s, nc_s, np_s = _mixer(
        _time_major(x_sample).reshape(n_s, D), mod_s, hist_c.reshape(hc_s * n_seq_s, C),
        hist_p.reshape(hp_s * n_seq_s, C), weights, SAMPLE_ROW_TILE, PAST_LEN)

    dest8, gates3, meta = _plan(lg_p, lg_s)
    cnt, row0, padded, nxt = (meta[i, :E] for i in range(4))
    gates = gates3.reshape(n, LANES)

    dests = [dest8[:, k, :].reshape(n) for k in range(TOP_K)]
    xpad = _dispatch(v_p, v_s, dests, n_slots)
    ypad = _experts(cnt, row0, padded, nxt, xpad, w_up[l], b_up[l][:, None, :], w_down[l], b_down[l][:, None, :])

    def gathered(t0, rows):
        idx = jnp.concatenate([dk[t0:t0 + rows] for dk in dests])
        return _gather_rows(ypad, idx).reshape(TOP_K, rows, D // 2)

    chunk = n_p // COMBINE_CHUNKS
    y4_p = [gathered(c * chunk, chunk) for c in range(COMBINE_CHUNKS)]
    y4_s = gathered(n_p, n_s)
    g2, b2 = ln2_g[l][None], ln2_b[l][None]
    y_p = None
    for c in range(COMBINE_CHUNKS):
        y_p = _combine(h_p, mod_p, gates, g2, b2, y4_p[c], 0, seq, c * chunk, y_p)
    y_s = _combine(h_s, mod_s, gates, g2, b2, y4_s, n_p, n_s, 0, None)

    y_prompt = y_p.reshape(n_seq_p, seq, D)
    y_sample = _time_major(y_s.reshape(dec_seq, n_seq_s, D))
    new_conv_p = nc_p.reshape(n_seq_p, hc_p, C)[:, hc_p - CONV_HIST:][None]
    new_pool_p = np_p.reshape(n_seq_p, hp_p, C)[:, hp_p - POOL_HIST:][None]
    new_conv_s = _time_major(nc_s.reshape(hc_s, n_seq_s, C)[hc_s - CONV_HIST:])[None]
    new_pool_s = _time_major(np_s.reshape(hp_s, n_seq_s, C)[hp_s - POOL_HIST:])[None]
    return (y_prompt, y_sample, new_conv_p, new_pool_p, new_conv_s, new_pool_s)
```

```python
import functools

import jax
import jax.numpy as jnp
from jax import lax
from jax.experimental import pallas as pl
from jax.experimental.pallas import tpu as pltpu
from jax.experimental.pallas import tpu_sc as plsc

D = 1024
C = 512
N_GROUPS = 4
GROUP = C // N_GROUPS
CONV_HIST = 2
POOL_HIST = 15
DOUBLING_HIST = 32
E = 32
TOP_K = 4
F = 1024
SWIGLU_LIMIT = 7.0
SWIGLU_ALPHA = 1.702
LN_EPS = 1e-5
DEPTH = 1
ALPHA = (2 * DEPTH) ** 0.25
PAST_LEN = 16384

LANES = 128
SUBLANES = 8
ROW_TILE = 512
PROMPT_ROW_TILE = 1024
SAMPLE_ROW_TILE = 512
SLOT_PAD = 256
BIG_BLOCK = 1024
COMBINE_TILE = 1024
COMBINE_CHUNKS = 4
SC_WINDOW = 128
DISPATCH_WINDOW = 64
ADA_COLS = 1536
VMEM_LIMIT = 56 * 1024 * 1024

_f32 = jnp.float32
_bf16 = jnp.bfloat16


def _dot(a, b):
    return jnp.dot(a, b, preferred_element_type=_f32)


def _dot_exact(a, b):
    return lax.dot_general(a, b, (((1,), (0,)), ((), ())),
                           precision=lax.Precision.HIGHEST, preferred_element_type=_f32)


def _pack_rows(x):
    w = x.shape[1] // 2
    hi = lax.bitcast_convert_type(x[:, :w].astype(_bf16).astype(_f32), jnp.int32)
    lo = lax.bitcast_convert_type(x[:, w:].astype(_bf16).astype(_f32), jnp.int32)
    return hi | lax.shift_right_logical(lo, 16)


def _unpack_rows(p):
    hi = lax.bitcast_convert_type(p & jnp.int32(-65536), _f32)
    lo = lax.bitcast_convert_type(lax.shift_left(p, 16), _f32)
    return jnp.concatenate([hi, lo], axis=1).astype(_bf16)


def _sigmoid(x):
    return 0.5 * jnp.tanh(0.5 * x) + 0.5


def _per_seq(x, m, op):
    g = m.shape[0]
    if g == 1:
        return op(x, m)
    r, n = x.shape
    return op(x.reshape(r // g, g, n), m[None]).reshape(r, n)


def _layer_norm(x, g, b):
    mu = jnp.mean(x, axis=-1, keepdims=True)
    xc = x - mu
    var = jnp.mean(xc * xc, axis=-1, keepdims=True)
    return xc * lax.rsqrt(var + LN_EPS) * g + b


def _hist_steps(needed, g):
    return -(-needed * g // SUBLANES) * SUBLANES // g


def _ada_kernel(c_ref, w_ref, b_ref, o_ref):
    c = c_ref[...]
    o_ref[...] = _dot((c * _sigmoid(c)).astype(_bf16), w_ref[...].astype(_bf16)) + b_ref[...]


def _ada(c, w_ada, b_ada):
    rows = c.shape[0]
    cols = w_ada.shape[1]
    bn = ADA_COLS
    return pl.pallas_call(
        _ada_kernel,
        out_shape=jax.ShapeDtypeStruct((rows, cols), _f32),
        grid=(cols // bn,),
        in_specs=[pl.BlockSpec((rows, D), lambda j: (0, 0)),
                  pl.BlockSpec((D, bn), lambda j: (0, j)),
                  pl.BlockSpec((1, bn), lambda j: (0, j))],
        out_specs=pl.BlockSpec((rows, bn), lambda j: (0, j)),
        compiler_params=pltpu.CompilerParams(vmem_limit_bytes=VMEM_LIMIT),
        name="ada",
    )(c, w_ada, b_ada)


def _mixer_kernel(g, tiles_per_seq, start_pos,
                  x_ref, mod_ref, hc_ref, hp_ref,
                  win_ref, cw_ref, woa_ref, wpool_ref, ls_ref, wob_ref, wo_ref, g1_ref, b1_ref,
                  wr_ref, br_ref,
                  h_ref, v_ref, lg_ref, nc_ref, np_ref, zbuf, pbuf, *lvl):
    r = x_ref.shape[0]
    hrc = hc_ref.shape[0]
    hrp = hp_ref.shape[0]
    j = pl.program_id(0) % tiles_per_seq

    @pl.when(j == 0)
    def _():
        zbuf[pl.ds(0, hrc), :] = hc_ref[...]
        pbuf[pl.ds(0, hrp), :] = hp_ref[...]

    @pl.when(j != 0)
    def _():
        zt = zbuf[pl.ds(r, hrc), :]
        pt = pbuf[pl.ds(r, hrp), :]
        zbuf[pl.ds(0, hrc), :] = zt
        pbuf[pl.ds(0, hrp), :] = pt

    m = mod_ref[...]
    shift1, scale1, gate1 = m[:, 0:D], m[:, D:2 * D], m[:, 2 * D:3 * D]
    shift2, scale2 = m[:, 3 * D:4 * D], m[:, 4 * D:5 * D]

    x = x_ref[...]
    u = _per_seq(_per_seq(x, 1.0 + scale1, jnp.multiply), shift1, jnp.add).astype(_bf16)

    z = _dot(u, win_ref[:, C:2 * C]) * _dot(u, win_ref[:, 2 * C:3 * C])
    zbuf[pl.ds(hrc, r), :] = z
    cw = cw_ref[...]
    conv = (cw[0:1] * zbuf[pl.ds(hrc - 2 * g, r), :] + cw[1:2] * zbuf[pl.ds(hrc - g, r), :]
            + cw[2:3] * z)
    y_a = _dot((_dot(u, win_ref[:, 0:C]) * conv).astype(_bf16), woa_ref[...])

    xp = _dot(u, win_ref[:, 3 * C:4 * C])
    pbuf[pl.ds(hrp, r), :] = xp
    pos = start_pos + j * (r // g) + lax.broadcasted_iota(jnp.int32, (r, 1), 0) // g
    wins = []
    if lvl:
        s2, s4, s8 = lvl
        s2[...] = pbuf[pl.ds(hrp - 24, r + 24), :] + pbuf[pl.ds(hrp - 25, r + 24), :]
        s4[...] = s2[pl.ds(8, r + 16), GROUP:C] + s2[pl.ds(6, r + 16), GROUP:C]
        s8[...] = s4[pl.ds(8, r + 8), GROUP:3 * GROUP] + s4[pl.ds(4, r + 8), GROUP:3 * GROUP]
        wins = [s2[pl.ds(24, r), 0:GROUP], s4[pl.ds(16, r), 0:GROUP], s8[pl.ds(8, r), 0:GROUP],
                s8[pl.ds(8, r), GROUP:2 * GROUP] + s8[pl.ds(0, r), GROUP:2 * GROUP]]
    else:
        acc = xp
        for grp in range(N_GROUPS):
            lo = grp * GROUP
            wdw = 2 ** (grp + 1)
            for back in range(wdw // 2, wdw):
                sh = pbuf[pl.ds(hrp - back * g, r), lo:C]
                acc = jnp.concatenate([acc[:, 0:lo], acc[:, lo:C] + sh], axis=1) if lo else acc + sh
            wins.append(acc[:, lo:lo + GROUP])
    yg = []
    for grp in range(N_GROUPS):
        lo = grp * GROUP
        inv_cnt = 1.0 / jnp.minimum(2 ** (grp + 1), pos + 1).astype(_f32)
        diff = wins[grp] * inv_cnt - xp[:, lo:lo + GROUP]
        yg.append(_dot(diff.astype(_bf16), wpool_ref[grp]))
    y_b = _dot((jnp.concatenate(yg, axis=1) * ls_ref[...]).astype(_bf16), wob_ref[...])

    g_a = _dot(u, win_ref[:, 4 * C:4 * C + D])
    g_b = _dot(u, win_ref[:, 4 * C + D:4 * C + 2 * D])
    merged = _sigmoid(g_a) * y_a + _sigmoid(g_b) * y_b
    o = _dot(merged.astype(_bf16), wo_ref[...])
    h = _layer_norm(ALPHA * x + _per_seq(o, gate1, jnp.multiply), g1_ref[...], b1_ref[...])
    v = _per_seq(_per_seq(h, 1.0 + scale2, jnp.multiply), shift2, jnp.add)
    h_ref[...] = h
    v_ref[...] = _pack_rows(v)
    lg_ref[...] = _dot(v.astype(_bf16), wr_ref[...].astype(_bf16)) + br_ref[...]
    nc_ref[...] = zbuf[pl.ds(r, hrc), :]
    np_ref[...] = pbuf[pl.ds(r, hrp), :]


def _mixer(x2, mod3, hc, hp, weights, row_tile, start_pos):
    n = x2.shape[0]
    n_mod, g, _ = mod3.shape
    hrc, hrp = hc.shape[0] // n_mod, hp.shape[0] // n_mod
    tiles_per_seq = n // n_mod // row_tile
    once = dict(pipeline_mode=pl.Buffered(1)) if n_mod == 1 else {}

    def full(a):
        nd = a.ndim
        return pl.BlockSpec(a.shape, lambda i: (0,) * nd)

    def seq_block(rows, **kw):
        return pl.BlockSpec((rows, C), lambda i: (i // tiles_per_seq, 0), **kw)

    def row_block(cols):
        return pl.BlockSpec((row_tile, cols), lambda i: (i, 0))

    levels = []
    if g == 1:
        assert hrp >= DOUBLING_HIST
        levels = [pltpu.VMEM((row_tile + 24, C), _f32), pltpu.VMEM((row_tile + 16, 3 * GROUP), _f32),
                  pltpu.VMEM((row_tile + 8, 2 * GROUP), _f32)]

    return pl.pallas_call(
        functools.partial(_mixer_kernel, g, tiles_per_seq, start_pos),
        out_shape=[
            jax.ShapeDtypeStruct((n, D), _f32),
            jax.ShapeDtypeStruct((n, D // 2), jnp.int32),
            jax.ShapeDtypeStruct((n, LANES), _f32),
            jax.ShapeDtypeStruct(hc.shape, _f32),
            jax.ShapeDtypeStruct(hp.shape, _f32),
        ],
        grid=(n // row_tile,),
        in_specs=[row_block(D),
                  pl.BlockSpec((None, g, 6 * D), lambda i: (i // tiles_per_seq, 0, 0), **once),
                  seq_block(hrc, **once), seq_block(hrp, **once)] + [full(a) for a in weights],
        out_specs=[row_block(D), row_block(D // 2), row_block(LANES), seq_block(hrc), seq_block(hrp)],
        scratch_shapes=[pltpu.VMEM((hrc + row_tile, C), _f32), pltpu.VMEM((hrp + row_tile, C), _f32)] + levels,
        compiler_params=pltpu.CompilerParams(vmem_limit_bytes=VMEM_LIMIT),
        name="mixer",
    )(x2, mod3, hc, hp, *weights)


def _plan_kernel(lgp_ref, lgs_ref, dest_ref, gate_ref, meta_ref, idx_s, rank_s):
    t = ROW_TILE
    e_iota = lax.broadcasted_iota(jnp.int32, (E, t), 0)
    tri = (lax.broadcasted_iota(jnp.int32, (t, t), 0)
           < lax.broadcasted_iota(jnp.int32, (t, t), 1)).astype(_f32).astype(_bf16)
    zeros_rest = jnp.zeros((LANES - TOP_K, t), _f32)

    def tile_body(lg_ref, off, i, carry):
        lt = lg_ref[i].T[0:E, :]
        vals, idxs = [], []
        for _ in range(TOP_K):
            mx = jnp.max(lt, axis=0, keepdims=True)
            ix = jnp.min(jnp.where(lt == mx, e_iota, E), axis=0, keepdims=True)
            vals.append(mx)
            idxs.append(ix)
            lt = jnp.where(e_iota == ix, -jnp.inf, lt)
        ex = [jnp.exp(vk - vals[0]) for vk in vals]
        den = ex[0] + ex[1] + ex[2] + ex[3]
        gates = [ek / den for ek in ex]
        gate_ref[off + i] = jnp.concatenate(gates + [zeros_rest], axis=0).T

        ohs = [(e_iota == ix) for ix in idxs]
        oh = (ohs[0] | ohs[1] | ohs[2] | ohs[3]).astype(_f32)
        before = _dot(oh.astype(_bf16), tri) + carry
        ranks = [jnp.sum(jnp.where(o, before, 0.0), axis=0, keepdims=True) for o in ohs]
        idx_s[off + i] = jnp.concatenate(idxs + idxs, axis=0)
        rank_s[off + i] = jnp.concatenate(ranks + ranks, axis=0).astype(jnp.int32)
        return carry + jnp.sum(oh, axis=1, keepdims=True)

    n_p, n_s = lgp_ref.shape[0], lgs_ref.shape[0]
    counts = lax.fori_loop(0, n_p, functools.partial(tile_body, lgp_ref, 0), jnp.zeros((E, 1), _f32))
    counts = lax.fori_loop(0, n_s, functools.partial(tile_body, lgs_ref, n_p), counts)
    padded = jnp.ceil(counts / SLOT_PAD) * SLOT_PAD
    low = (lax.broadcasted_iota(jnp.int32, (E, E), 1)
           <= lax.broadcasted_iota(jnp.int32, (E, E), 0)).astype(_f32)
    pad_end = _dot_exact(low, jnp.broadcast_to(padded, (E, LANES)))[:, 0:1]
    pad_start = pad_end - padded

    def dest_body(i, c):
        ix = idx_s[i]
        rk = rank_s[i]
        rows = []
        for k in range(TOP_K):
            st = jnp.sum(jnp.where(e_iota == ix[k:k + 1], pad_start, 0.0), axis=0, keepdims=True)
            rows.append(st.astype(jnp.int32) + rk[k:k + 1])
        dest_ref[i] = jnp.concatenate(rows + rows, axis=0)
        return c

    lax.fori_loop(0, n_p + n_s, dest_body, 0)

    sub = lax.broadcasted_iota(jnp.int32, (E, LANES), 0)
    lane = lax.broadcasted_iota(jnp.int32, (E, LANES), 1)

    def to_lanes(col):
        return jnp.sum(jnp.where(sub == lane, col, 0.0), axis=0, keepdims=True).astype(jnp.int32)

    later = jnp.min(jnp.where((sub > lane) & (counts > 0.0), sub, E), axis=0, keepdims=True)
    meta_ref[...] = jnp.concatenate(
        [to_lanes(counts), to_lanes(pad_start), to_lanes(padded), later,
         jnp.zeros((SUBLANES - 4, LANES), jnp.int32)], axis=0)


def _plan(logits_p, logits_s):
    n_tiles = (logits_p.shape[0] + logits_s.shape[0]) // ROW_TILE
    return pl.pallas_call(
        _plan_kernel,
        out_shape=[
            jax.ShapeDtypeStruct((n_tiles, 2 * TOP_K, ROW_TILE), jnp.int32),
            jax.ShapeDtypeStruct((n_tiles, ROW_TILE, LANES), _f32),
            jax.ShapeDtypeStruct((SUBLANES, LANES), jnp.int32),
        ],
        scratch_shapes=[pltpu.VMEM((n_tiles, 2 * TOP_K, ROW_TILE), jnp.int32),
                        pltpu.VMEM((n_tiles, 2 * TOP_K, ROW_TILE), jnp.int32)],
        compiler_params=pltpu.CompilerParams(vmem_limit_bytes=VMEM_LIMIT),
        name="plan",
    )(logits_p.reshape(-1, ROW_TILE, LANES), logits_s.reshape(-1, ROW_TILE, LANES))


def _dispatch(v_p, v_s, dests, n_rows_out):
    n_p, n_s = v_p.shape[0], v_s.shape[0]
    width = v_p.shape[1]
    w = DISPATCH_WINDOW
    n_pw, n_windows = n_p // w, (n_p + n_s) // w
    mesh = plsc.VectorSubcoreMesh(core_axis_name="core", subcore_axis_name="subcore")
    n_workers = mesh.num_cores * mesh.num_subcores
    n_rounds = -(-n_windows // n_workers)

    @functools.partial(
        pl.kernel, mesh=mesh, name="dispatch",
        out_type=jax.ShapeDtypeStruct((n_rows_out, width), jnp.int32),
        scratch_types=[pltpu.VMEM((w, width), jnp.int32)] * 2 + [pltpu.VMEM((w,), jnp.int32)] * (2 * TOP_K)
        + [pltpu.SemaphoreType.DMA] * 3)
    def scatter_rows(vp_hbm, vs_hbm, d0_hbm, d1_hbm, d2_hbm, d3_hbm, o_hbm, rows0, rows1, *rest):
        worker = lax.axis_index("subcore") * mesh.num_cores + lax.axis_index("core")
        rows = (rows0, rows1)
        idx = (rest[:TOP_K], rest[TOP_K:2 * TOP_K])
        lsem, ssem = rest[2 * TOP_K:2 * TOP_K + 2], rest[2 * TOP_K + 2]
        d_hbm = (d0_hbm, d1_hbm, d2_hbm, d3_hbm)

        def load(c, slot, act):
            def index_reads():
                t0 = pl.multiple_of(c * w, w)
                for k in range(TOP_K):
                    act(pltpu.make_async_copy(d_hbm[k].at[pl.ds(t0, w)], idx[slot][k], lsem[slot]))

            @pl.when(c < n_pw)
            def _():
                src = vp_hbm.at[pl.ds(pl.multiple_of(c * w, w), w)]
                act(pltpu.make_async_copy(src, rows[slot], lsem[slot]))
                index_reads()

            @pl.when((c >= n_pw) & (c < n_windows))
            def _():
                src = vs_hbm.at[pl.ds(pl.multiple_of((c - n_pw) * w, w), w)]
                act(pltpu.make_async_copy(src, rows[slot], lsem[slot]))
                index_reads()

        load(worker, 0, lambda cp: cp.start())
        for j in range(n_rounds):
            c = j * n_workers + worker
            slot = j % 2
            load(c, slot, lambda cp: cp.wait())
            if j + 1 < n_rounds:
                load(c + n_workers, 1 - slot, lambda cp: cp.start())

            @pl.when(c < n_windows)
            def _():
                copies = [pltpu.async_copy(rows[slot], o_hbm.at[iv], ssem) for iv in idx[slot]]
                for cp in copies:
                    cp.wait()

    return scatter_rows(v_p, v_s, *dests)


def _experts_kernel(cnt_ref, row0_ref, pad_ref, nxt_ref, x_hbm, wu_hbm, bu_ref, wd_hbm, bd_ref, y_hbm,
                    xbuf, ybuf, wu_st, wd_st, wu_bf, wd_bf, ysz, xsem, ysem, wsem):
    def w_fetch(e):
        return (pltpu.make_async_copy(wu_hbm.at[e], wu_st, wsem.at[0]),
                pltpu.make_async_copy(wd_hbm.at[e], wd_st, wsem.at[1]))

    def x_fetch(row, size, slot):
        rows = pl.ds(pl.multiple_of(row, SLOT_PAD), size)
        return pltpu.make_async_copy(x_hbm.at[rows, :], xbuf.at[slot, pl.ds(0, size), :], xsem.at[slot])

    def y_store(row, size, slot):
        rows = pl.ds(pl.multiple_of(row, SLOT_PAD), size)
        return pltpu.make_async_copy(ybuf.at[slot, pl.ds(0, size), :], y_hbm.at[rows, :], ysem.at[slot])

    def y_wait(slot):
        for size in (BIG_BLOCK, SLOT_PAD):
            @pl.when(ysz[slot] == size)
            def _():
                y_store(0, size, slot).wait()

    def fetch_first(e, slot):
        @pl.when(pad_ref[e] >= BIG_BLOCK)
        def _():
            x_fetch(row0_ref[e], BIG_BLOCK, slot).start()

        @pl.when((pad_ref[e] > 0) & (pad_ref[e] < BIG_BLOCK))
        def _():
            x_fetch(row0_ref[e], SLOT_PAD, slot).start()

    ysz[0] = 0
    ysz[1] = 0
    for c in w_fetch(0):
        c.start()
    fetch_first(jnp.where(pad_ref[0] > 0, 0, nxt_ref[0]), 0)

    def expert_body(e, n_done):
        for c in w_fetch(e):
            c.wait()
        padded = pad_ref[e]

        @pl.when(padded > 0)
        def _():
            wu_bf[...] = wu_st[...].astype(_bf16)
            wd_bf[...] = wd_st[...].astype(_bf16)

        @pl.when(e + 1 < E)
        def _():
            for c in w_fetch(e + 1):
                c.start()

        b_up = bu_ref[e]
        b_down = bd_ref[e]
        row0 = row0_ref[e]
        n_big = padded // BIG_BLOCK
        n_small = (padded - n_big * BIG_BLOCK) // SLOT_PAD
        small0 = row0 + n_big * BIG_BLOCK
        nxt = nxt_ref[e]

        def pass_body(size, j, n_done):
            slot = n_done % 2
            big = size == BIG_BLOCK
            row = row0 + j * BIG_BLOCK if big else small0 + j * SLOT_PAD
            x_fetch(row, size, slot).wait()

            more = j + 1 < (n_big if big else n_small)
            tail = (n_small > 0) if big else False

            @pl.when(more)
            def _():
                x_fetch(row + size, size, 1 - slot).start()

            if big:
                @pl.when(jnp.logical_not(more) & tail)
                def _():
                    x_fetch(small0, SLOT_PAD, 1 - slot).start()

            @pl.when(jnp.logical_not(more) & jnp.logical_not(tail) & (nxt < E))
            def _():
                fetch_first(jnp.minimum(nxt, E - 1), 1 - slot)

            rows = lax.broadcasted_iota(jnp.int32, (size, 1), 0)
            x = _unpack_rows(jnp.where(rows < cnt_ref[e] - (row - row0), xbuf[slot, pl.ds(0, size), :], 0))
            hcat = _dot(x, wu_bf[...]) + b_up
            glu = jnp.minimum(hcat[:, 0:F], SWIGLU_LIMIT)
            lin = jnp.clip(hcat[:, F:2 * F], -SWIGLU_LIMIT, SWIGLU_LIMIT)
            act = glu * _sigmoid(SWIGLU_ALPHA * glu) * (lin + 1.0)
            y = _pack_rows(_dot(act.astype(_bf16), wd_bf[...]) + b_down)

            y_wait(slot)
            ybuf[slot, pl.ds(0, size), :] = y
            y_store(row, size, slot).start()
            ysz[slot] = size
            return n_done + 1

        n_done = lax.fori_loop(0, n_big, functools.partial(pass_body, BIG_BLOCK), n_done)
        return lax.fori_loop(0, n_small, functools.partial(pass_body, SLOT_PAD), n_done)

    lax.fori_loop(0, E, expert_body, 0)
    y_wait(0)
    y_wait(1)


def _experts(cnt, row0, padded, nxt, xpad, w_up, b_up, w_down, b_down):
    def full(a):
        nd = a.ndim
        return pl.BlockSpec(a.shape, lambda i, *_: (0,) * nd)

    return pl.pallas_call(
        _experts_kernel,
        out_shape=jax.ShapeDtypeStruct(xpad.shape, jnp.int32),
        grid_spec=pltpu.PrefetchScalarGridSpec(
            num_scalar_prefetch=4,
            grid=(1,),
            in_specs=[pl.BlockSpec(memory_space=pl.ANY),
                      pl.BlockSpec(memory_space=pl.ANY), full(b_up),
                      pl.BlockSpec(memory_space=pl.ANY), full(b_down)],
            out_specs=pl.BlockSpec(memory_space=pl.ANY),
            scratch_shapes=[pltpu.VMEM((2, BIG_BLOCK, D // 2), jnp.int32),
                            pltpu.VMEM((2, BIG_BLOCK, D // 2), jnp.int32),
                            pltpu.VMEM((D, 2 * F), _f32), pltpu.VMEM((F, D), _f32),
                            pltpu.VMEM((D, 2 * F), _bf16), pltpu.VMEM((F, D), _bf16),
                            pltpu.SMEM((2,), jnp.int32),
                            pltpu.SemaphoreType.DMA((2,)), pltpu.SemaphoreType.DMA((2,)),
                            pltpu.SemaphoreType.DMA((2,))],
        ),
        compiler_params=pltpu.CompilerParams(vmem_limit_bytes=VMEM_LIMIT),
        name="experts",
    )(cnt, row0, padded, nxt, xpad, w_up, b_up, w_down, b_down)


def _gather_rows(ypad, dest_all):
    n_out = dest_all.shape[0]
    width = ypad.shape[1]
    w = SC_WINDOW
    mesh = plsc.VectorSubcoreMesh(core_axis_name="core", subcore_axis_name="subcore")
    n_workers = mesh.num_cores * mesh.num_subcores
    n_windows = n_out // w

    @functools.partial(
        pl.kernel, mesh=mesh, name="gather_rows",
        out_type=jax.ShapeDtypeStruct((n_out, width), jnp.int32),
        scratch_types=[pltpu.VMEM((w, width), jnp.int32), pltpu.VMEM((w,), jnp.int32), pltpu.SemaphoreType.DMA])
    def gather_rows(y_hbm, d_hbm, o_hbm, rows, iv, sem):
        worker = lax.axis_index("subcore") * mesh.num_cores + lax.axis_index("core")

        @pl.loop(0, n_windows // n_workers)
        def _(j):
            r0 = pl.multiple_of((j * n_workers + worker) * w, w)
            pltpu.sync_copy(d_hbm.at[pl.ds(r0, w)], iv)
            pltpu.async_copy(y_hbm.at[iv], rows, sem).wait()
            pltpu.sync_copy(rows, o_hbm.at[pl.ds(r0, w)])

    assert n_windows % n_workers == 0
    return gather_rows(ypad, dest_all)


def _combine_kernel(h_ref, mod_ref, gate_ref, g2_ref, b2_ref, y_ref, *aliased_and_out):
    o_ref = aliased_and_out[-1]
    gates = gate_ref[...]
    half = D // 2
    f_hi = jnp.zeros((h_ref.shape[0], half), _f32)
    f_lo = jnp.zeros((h_ref.shape[0], half), _f32)
    for k in range(TOP_K):
        p = y_ref[k]
        gk = gates[:, k:k + 1]
        f_hi = f_hi + gk * lax.bitcast_convert_type(p & jnp.int32(-65536), _f32)
        f_lo = f_lo + gk * lax.bitcast_convert_type(lax.shift_left(p, 16), _f32)
    f = jnp.concatenate([f_hi, f_lo], axis=1)
    gate2 = mod_ref[...][:, 5 * D:6 * D]
    pre = ALPHA * h_ref[...] + _per_seq(f, gate2, jnp.multiply)
    o_ref[...] = _layer_norm(pre, g2_ref[...], b2_ref[...])


def _combine(h, mod3, gates, ln2_g, ln2_b, y4, token0, rows_per_mod, row0, out_so_far):
    t = COMBINE_TILE
    hoff = row0 // t
    goff = (token0 + row0) // t
    g = mod3.shape[1]
    tiles_per_mod = rows_per_mod // t
    in_specs = [pl.BlockSpec((t, D), lambda i: (i + hoff, 0)),
                pl.BlockSpec((None, g, 6 * D), lambda i: ((i + hoff) // tiles_per_mod, 0, 0)),
                pl.BlockSpec((t, LANES), lambda i: (i + goff, 0)),
                pl.BlockSpec((1, D), lambda i: (0, 0)),
                pl.BlockSpec((1, D), lambda i: (0, 0)),
                pl.BlockSpec((TOP_K, t, D // 2), lambda i: (0, i, 0))]
    args = [h, mod3, gates, ln2_g, ln2_b, y4]
    aliases = {}
    if out_so_far is not None:
        in_specs.append(pl.BlockSpec(memory_space=pl.ANY))
        args.append(out_so_far)
        aliases = {len(args) - 1: 0}
    return pl.pallas_call(
        _combine_kernel,
        out_shape=jax.ShapeDtypeStruct(h.shape, _f32),
        grid=(y4.shape[1] // t,),
        in_specs=in_specs,
        out_specs=pl.BlockSpec((t, D), lambda i: (i + hoff, 0)),
        input_output_aliases=aliases,
        compiler_params=pltpu.CompilerParams(vmem_limit_bytes=VMEM_LIMIT),
        name="combine",
    )(*args)


def _time_major(a):
    return a.transpose(1, 0, 2)


def kernel(x_prompt, x_sample, c_prompt, c_sample, state_conv, state_pool, w_ada, b_ada, w_in,
           conv_w, w_out_a, w_pool, ls_pool, w_out_b, w_o, ln1_g, ln1_b, w_router, b_router,
           w_up, b_up, w_down, b_down, ln2_g, ln2_b):
    n_seq_p, seq, _ = x_prompt.shape
    n_seq_s, dec_seq, _ = x_sample.shape
    n_p, n_s = n_seq_p * seq, n_seq_s * dec_seq
    n = n_p + n_s
    n_slots = TOP_K * n + E * SLOT_PAD
    l = 0

    mod = _ada(jnp.concatenate([c_prompt, c_sample], axis=0), w_ada[l], b_ada[l][None])
    mod_p = mod[:n_seq_p][:, None, :]
    mod_s = mod[n_seq_p:][None]

    weights = (
        w_in[l].astype(_bf16), conv_w[l], w_out_a[l].astype(_bf16), w_pool[l].astype(_bf16),
        ls_pool[l][None], w_out_b[l].astype(_bf16), w_o[l].astype(_bf16), ln1_g[l][None], ln1_b[l][None],
        jnp.pad(w_router[l], ((0, 0), (0, LANES - E))), jnp.pad(b_router[l], (0, LANES - E))[None],
    )
    hc_p, hp_p = _hist_steps(CONV_HIST, 1), DOUBLING_HIST
    h_p, v_p, lg_p, nc_p, np_p = _mixer(
        x_prompt.reshape(n_p, D), mod_p, jnp.zeros((n_seq_p * hc_p, C), _f32),
        jnp.zeros((n_seq_p * hp_p, C), _f32), weights, PROMPT_ROW_TILE, 0)
    hc_s, hp_s = _hist_steps(CONV_HIST, n_seq_s), _hist_steps(POOL_HIST, n_seq_s)
    hist_c = jnp.pad(_time_major(state_conv[l]), ((hc_s - CONV_HIST, 0), (0, 0), (0, 0)))
    hist_p = jnp.pad(_time_major(state_pool[l]), ((hp_s - POOL_HIST, 0), (0, 0), (0, 0)))
    h_s, v_s, lg_s, nc_s, np_s = _mixer(
        _time_major(x_sample).reshape(n_s, D), mod_s, hist_c.reshape(hc_s * n_seq_s, C),
        hist_p.reshape(hp_s * n_seq_s, C), weights, SAMPLE_ROW_TILE, PAST_LEN)

    dest8, gates3, meta = _plan(lg_p, lg_s)
    cnt, row0, padded, nxt = (meta[i, :E] for i in range(4))
    gates = gates3.reshape(n, LANES)

    dests = [dest8[:, k, :].reshape(n) for k in range(TOP_K)]
    xpad = _dispatch(v_p, v_s, dests, n_slots)
    ypad = _experts(cnt, row0, padded, nxt, xpad, w_up[l], b_up[l][:, None, :], w_down[l], b_down[l][:, None, :])

    def gathered(t0, rows):
        idx = jnp.concatenate([dk[t0:t0 + rows] for dk in dests])
        return _gather_rows(ypad, idx).reshape(TOP_K, rows, D // 2)

    chunk = n_p // COMBINE_CHUNKS
    y4_p = [gathered(c * chunk, chunk) for c in range(COMBINE_CHUNKS)]
    y4_s = gathered(n_p, n_s)
    g2, b2 = ln2_g[l][None], ln2_b[l][None]
    y_p = None
    for c in range(COMBINE_CHUNKS):
        y_p = _combine(h_p, mod_p, gates, g2, b2, y4_p[c], 0, seq, c * chunk, y_p)
    y_s = _combine(h_s, mod_s, gates, g2, b2, y4_s, n_p, n_s, 0, None)

    y_prompt = y_p.reshape(n_seq_p, seq, D)
    y_sample = _time_major(y_s.reshape(dec_seq, n_seq_s, D))
    new_conv_p = nc_p.reshape(n_seq_p, hc_p, C)[:, hc_p - CONV_HIST:][None]
    new_pool_p = np_p.reshape(n_seq_p, hp_p, C)[:, hp_p - POOL_HIST:][None]
    new_conv_s = _time_major(nc_s.reshape(hc_s, n_seq_s, C)[hc_s - CONV_HIST:])[None]
    new_pool_s = _time_major(np_s.reshape(hp_s, n_seq_s, C)[hp_s - POOL_HIST:])[None]
    return (y_prompt, y_sample, new_conv_p, new_pool_p, new_conv_s, new_pool_s)
```

```python
import functools

import jax
import jax.numpy as jnp
from jax import lax
from jax.experimental import pallas as pl
from jax.experimental.pallas import tpu as pltpu
from jax.experimental.pallas import tpu_sc as plsc

D = 1024
C = 512
N_GROUPS = 4
GROUP = C // N_GROUPS
CONV_HIST = 2
POOL_HIST = 15
DOUBLING_HIST = 32
E = 32
TOP_K = 4
F = 1024
SWIGLU_LIMIT = 7.0
SWIGLU_ALPHA = 1.702
LN_EPS = 1e-5
DEPTH = 1
ALPHA = (2 * DEPTH) ** 0.25
PAST_LEN = 16384

LANES = 128
SUBLANES = 8
ROW_TILE = 1024
PROMPT_ROW_TILE = 1024
SAMPLE_ROW_TILE = 512
SLOT_PAD = 256
BIG_BLOCK = 1024
COMBINE_TILE = 1024
COMBINE_CHUNKS = 4
SC_WINDOW = 128
DISPATCH_WINDOW = 64
ADA_COLS = 1536
VMEM_LIMIT = 56 * 1024 * 1024

_f32 = jnp.float32
_bf16 = jnp.bfloat16


def _dot(a, b):
    return jnp.dot(a, b, preferred_element_type=_f32)


def _dot_exact(a, b):
    return lax.dot_general(a, b, (((1,), (0,)), ((), ())),
                           precision=lax.Precision.HIGHEST, preferred_element_type=_f32)


def _pack_rows(x):
    w = x.shape[1] // 2
    hi = lax.bitcast_convert_type(x[:, :w].astype(_bf16).astype(_f32), jnp.int32)
    lo = lax.bitcast_convert_type(x[:, w:].astype(_bf16).astype(_f32), jnp.int32)
    return hi | lax.shift_right_logical(lo, 16)


def _unpack_rows(p):
    hi = lax.bitcast_convert_type(p & jnp.int32(-65536), _f32)
    lo = lax.bitcast_convert_type(lax.shift_left(p, 16), _f32)
    return jnp.concatenate([hi, lo], axis=1).astype(_bf16)


def _sigmoid(x):
    return 0.5 * jnp.tanh(0.5 * x) + 0.5


def _per_seq(x, m, op):
    g = m.shape[0]
    if g == 1:
        return op(x, m)
    r, n = x.shape
    return op(x.reshape(r // g, g, n), m[None]).reshape(r, n)


def _layer_norm(x, g, b):
    mu = jnp.mean(x, axis=-1, keepdims=True)
    xc = x - mu
    var = jnp.mean(xc * xc, axis=-1, keepdims=True)
    return xc * lax.rsqrt(var + LN_EPS) * g + b


def _hist_steps(needed, g):
    return -(-needed * g // SUBLANES) * SUBLANES // g


def _ada_kernel(c_ref, w_ref, b_ref, o_ref):
    c = c_ref[...]
    o_ref[...] = _dot((c * _sigmoid(c)).astype(_bf16), w_ref[...].astype(_bf16)) + b_ref[...]


def _ada(c, w_ada, b_ada):
    rows = c.shape[0]
    cols = w_ada.shape[1]
    bn = ADA_COLS
    return pl.pallas_call(
        _ada_kernel,
        out_shape=jax.ShapeDtypeStruct((rows, cols), _f32),
        grid=(cols // bn,),
        in_specs=[pl.BlockSpec((rows, D), lambda j: (0, 0)),
                  pl.BlockSpec((D, bn), lambda j: (0, j)),
                  pl.BlockSpec((1, bn), lambda j: (0, j))],
        out_specs=pl.BlockSpec((rows, bn), lambda j: (0, j)),
        compiler_params=pltpu.CompilerParams(vmem_limit_bytes=VMEM_LIMIT),
        name="ada",
    )(c, w_ada, b_ada)


def _mixer_kernel(g, tiles_per_seq, start_pos,
                  x_ref, mod_ref, hc_ref, hp_ref,
                  win_ref, cw_ref, woa_ref, wpool_ref, ls_ref, wob_ref, wo_ref, g1_ref, b1_ref,
                  wr_ref, br_ref,
                  h_ref, v_ref, lg_ref, nc_ref, np_ref, zbuf, pbuf, *lvl):
    r = x_ref.shape[0]
    hrc = hc_ref.shape[0]
    hrp = hp_ref.shape[0]
    j = pl.program_id(0) % tiles_per_seq

    @pl.when(j == 0)
    def _():
        zbuf[pl.ds(0, hrc), :] = hc_ref[...]
        pbuf[pl.ds(0, hrp), :] = hp_ref[...]

    @pl.when(j != 0)
    def _():
        zt = zbuf[pl.ds(r, hrc), :]
        pt = pbuf[pl.ds(r, hrp), :]
        zbuf[pl.ds(0, hrc), :] = zt
        pbuf[pl.ds(0, hrp), :] = pt

    m = mod_ref[...]
    shift1, scale1, gate1 = m[:, 0:D], m[:, D:2 * D], m[:, 2 * D:3 * D]
    shift2, scale2 = m[:, 3 * D:4 * D], m[:, 4 * D:5 * D]

    x = x_ref[...]
    u = _per_seq(_per_seq(x, 1.0 + scale1, jnp.multiply), shift1, jnp.add).astype(_bf16)

    z = _dot(u, win_ref[:, C:2 * C]) * _dot(u, win_ref[:, 2 * C:3 * C])
    zbuf[pl.ds(hrc, r), :] = z
    cw = cw_ref[...]
    conv = (cw[0:1] * zbuf[pl.ds(hrc - 2 * g, r), :] + cw[1:2] * zbuf[pl.ds(hrc - g, r), :]
            + cw[2:3] * z)
    y_a = _dot((_dot(u, win_ref[:, 0:C]) * conv).astype(_bf16), woa_ref[...])

    xp = _dot(u, win_ref[:, 3 * C:4 * C])
    pbuf[pl.ds(hrp, r), :] = xp
    pos = start_pos + j * (r // g) + lax.broadcasted_iota(jnp.int32, (r, 1), 0) // g
    wins = []
    if lvl:
        s2, s4, s8 = lvl
        s2[...] = pbuf[pl.ds(hrp - 24, r + 24), :] + pbuf[pl.ds(hrp - 25, r + 24), :]
        s4[...] = s2[pl.ds(8, r + 16), GROUP:C] + s2[pl.ds(6, r + 16), GROUP:C]
        s8[...] = s4[pl.ds(8, r + 8), GROUP:3 * GROUP] + s4[pl.ds(4, r + 8), GROUP:3 * GROUP]
        wins = [s2[pl.ds(24, r), 0:GROUP], s4[pl.ds(16, r), 0:GROUP], s8[pl.ds(8, r), 0:GROUP],
                s8[pl.ds(8, r), GROUP:2 * GROUP] + s8[pl.ds(0, r), GROUP:2 * GROUP]]
    else:
        acc = xp
        for grp in range(N_GROUPS):
            lo = grp * GROUP
            wdw = 2 ** (grp + 1)
            for back in range(wdw // 2, wdw):
                sh = pbuf[pl.ds(hrp - back * g, r), lo:C]
                acc = jnp.concatenate([acc[:, 0:lo], acc[:, lo:C] + sh], axis=1) if lo else acc + sh
            wins.append(acc[:, lo:lo + GROUP])
    yg = []
    for grp in range(N_GROUPS):
        lo = grp * GROUP
        inv_cnt = 1.0 / jnp.minimum(2 ** (grp + 1), pos + 1).astype(_f32)
        diff = wins[grp] * inv_cnt - xp[:, lo:lo + GROUP]
        yg.append(_dot(diff.astype(_bf16), wpool_ref[grp]))
    y_b = _dot((jnp.concatenate(yg, axis=1) * ls_ref[...]).astype(_bf16), wob_ref[...])

    g_a = _dot(u, win_ref[:, 4 * C:4 * C + D])
    g_b = _dot(u, win_ref[:, 4 * C + D:4 * C + 2 * D])
    merged = _sigmoid(g_a) * y_a + _sigmoid(g_b) * y_b
    o = _dot(merged.astype(_bf16), wo_ref[...])
    h = _layer_norm(ALPHA * x + _per_seq(o, gate1, jnp.multiply), g1_ref[...], b1_ref[...])
    v = _per_seq(_per_seq(h, 1.0 + scale2, jnp.multiply), shift2, jnp.add)
    h_ref[...] = h
    v_ref[...] = _pack_rows(v)
    lg_ref[...] = _dot(v.astype(_bf16), wr_ref[...].astype(_bf16)) + br_ref[...]
    nc_ref[...] = zbuf[pl.ds(r, hrc), :]
    np_ref[...] = pbuf[pl.ds(r, hrp), :]


def _mixer(x2, mod3, hc, hp, weights, row_tile, start_pos):
    n = x2.shape[0]
    n_mod, g, _ = mod3.shape
    hrc, hrp = hc.shape[0] // n_mod, hp.shape[0] // n_mod
    tiles_per_seq = n // n_mod // row_tile
    once = dict(pipeline_mode=pl.Buffered(1)) if n_mod == 1 else {}

    def full(a):
        nd = a.ndim
        return pl.BlockSpec(a.shape, lambda i: (0,) * nd)

    def seq_block(rows, **kw):
        return pl.BlockSpec((rows, C), lambda i: (i // tiles_per_seq, 0), **kw)

    def row_block(cols):
        return pl.BlockSpec((row_tile, cols), lambda i: (i, 0))

    levels = []
    if g == 1:
        assert hrp >= DOUBLING_HIST
        levels = [pltpu.VMEM((row_tile + 24, C), _f32), pltpu.VMEM((row_tile + 16, 3 * GROUP), _f32),
                  pltpu.VMEM((row_tile + 8, 2 * GROUP), _f32)]

    return pl.pallas_call(
        functools.partial(_mixer_kernel, g, tiles_per_seq, start_pos),
        out_shape=[
            jax.ShapeDtypeStruct((n, D), _f32),
            jax.ShapeDtypeStruct((n, D // 2), jnp.int32),
            jax.ShapeDtypeStruct((n, LANES), _f32),
            jax.ShapeDtypeStruct(hc.shape, _f32),
            jax.ShapeDtypeStruct(hp.shape, _f32),
        ],
        grid=(n // row_tile,),
        in_specs=[row_block(D),
                  pl.BlockSpec((None, g, 6 * D), lambda i: (i // tiles_per_seq, 0, 0), **once),
                  seq_block(hrc, **once), seq_block(hrp, **once)] + [full(a) for a in weights],
        out_specs=[row_block(D), row_block(D // 2), row_block(LANES), seq_block(hrc), seq_block(hrp)],
        scratch_shapes=[pltpu.VMEM((hrc + row_tile, C), _f32), pltpu.VMEM((hrp + row_tile, C), _f32)] + levels,
        compiler_params=pltpu.CompilerParams(vmem_limit_bytes=VMEM_LIMIT),
        name="mixer",
    )(x2, mod3, hc, hp, *weights)


def _plan_kernel(lgp_ref, lgs_ref, dest_ref, gate_ref, meta_ref, idx_s, rank_s):
    t = ROW_TILE
    e_iota = lax.broadcasted_iota(jnp.int32, (E, t), 0)
    tri = (lax.broadcasted_iota(jnp.int32, (t, t), 0)
           < lax.broadcasted_iota(jnp.int32, (t, t), 1)).astype(_f32).astype(_bf16)
    zeros_rest = jnp.zeros((LANES - TOP_K, t), _f32)

    def tile_body(lg_ref, off, i, carry):
        lt = lg_ref[i].T[0:E, :]
        vals, idxs = [], []
        for _ in range(TOP_K):
            mx = jnp.max(lt, axis=0, keepdims=True)
            ix = jnp.min(jnp.where(lt == mx, e_iota, E), axis=0, keepdims=True)
            vals.append(mx)
            idxs.append(ix)
            lt = jnp.where(e_iota == ix, -jnp.inf, lt)
        ex = [jnp.exp(vk - vals[0]) for vk in vals]
        den = ex[0] + ex[1] + ex[2] + ex[3]
        gates = [ek / den for ek in ex]
        gate_ref[off + i] = jnp.concatenate(gates + [zeros_rest], axis=0).T

        ohs = [(e_iota == ix) for ix in idxs]
        oh = (ohs[0] | ohs[1] | ohs[2] | ohs[3]).astype(_f32)
        before = _dot(oh.astype(_bf16), tri) + carry
        ranks = [jnp.sum(jnp.where(o, before, 0.0), axis=0, keepdims=True) for o in ohs]
        idx_s[off + i] = jnp.concatenate(idxs + idxs, axis=0)
        rank_s[off + i] = jnp.concatenate(ranks + ranks, axis=0).astype(jnp.int32)
        return carry + jnp.sum(oh, axis=1, keepdims=True)

    n_p, n_s = lgp_ref.shape[0], lgs_ref.shape[0]
    counts = lax.fori_loop(0, n_p, functools.partial(tile_body, lgp_ref, 0), jnp.zeros((E, 1), _f32))
    counts = lax.fori_loop(0, n_s, functools.partial(tile_body, lgs_ref, n_p), counts)
    padded = jnp.ceil(counts / SLOT_PAD) * SLOT_PAD
    low = (lax.broadcasted_iota(jnp.int32, (E, E), 1)
           <= lax.broadcasted_iota(jnp.int32, (E, E), 0)).astype(_f32)
    pad_end = _dot_exact(low, jnp.broadcast_to(padded, (E, LANES)))[:, 0:1]
    pad_start = pad_end - padded

    def dest_body(i, c):
        ix = idx_s[i]
        rk = rank_s[i]
        rows = []
        for k in range(TOP_K):
            st = jnp.sum(jnp.where(e_iota == ix[k:k + 1], pad_start, 0.0), axis=0, keepdims=True)
            rows.append(st.astype(jnp.int32) + rk[k:k + 1])
        dest_ref[i] = jnp.concatenate(rows + rows, axis=0)
        return c

    lax.fori_loop(0, n_p + n_s, dest_body, 0)

    sub = lax.broadcasted_iota(jnp.int32, (E, LANES), 0)
    lane = lax.broadcasted_iota(jnp.int32, (E, LANES), 1)

    def to_lanes(col):
        return jnp.sum(jnp.where(sub == lane, col, 0.0), axis=0, keepdims=True).astype(jnp.int32)

    later = jnp.min(jnp.where((sub > lane) & (counts > 0.0), sub, E), axis=0, keepdims=True)
    meta_ref[...] = jnp.concatenate(
        [to_lanes(counts), to_lanes(pad_start), to_lanes(padded), later,
         jnp.zeros((SUBLANES - 4, LANES), jnp.int32)], axis=0)


def _plan(logits_p, logits_s):
    n_tiles = (logits_p.shape[0] + logits_s.shape[0]) // ROW_TILE
    return pl.pallas_call(
        _plan_kernel,
        out_shape=[
            jax.ShapeDtypeStruct((n_tiles, 2 * TOP_K, ROW_TILE), jnp.int32),
            jax.ShapeDtypeStruct((n_tiles, ROW_TILE, LANES), _f32),
            jax.ShapeDtypeStruct((SUBLANES, LANES), jnp.int32),
        ],
        scratch_shapes=[pltpu.VMEM((n_tiles, 2 * TOP_K, ROW_TILE), jnp.int32),
                        pltpu.VMEM((n_tiles, 2 * TOP_K, ROW_TILE), jnp.int32)],
        compiler_params=pltpu.CompilerParams(vmem_limit_bytes=VMEM_LIMIT),
        name="plan",
    )(logits_p.reshape(-1, ROW_TILE, LANES), logits_s.reshape(-1, ROW_TILE, LANES))


def _dispatch(v_p, v_s, dests, n_rows_out):
    n_p, n_s = v_p.shape[0], v_s.shape[0]
    width = v_p.shape[1]
    w = DISPATCH_WINDOW
    n_pw, n_windows = n_p // w, (n_p + n_s) // w
    mesh = plsc.VectorSubcoreMesh(core_axis_name="core", subcore_axis_name="subcore")
    n_workers = mesh.num_cores * mesh.num_subcores
    n_rounds = -(-n_windows // n_workers)

    @functools.partial(
        pl.kernel, mesh=mesh, name="dispatch",
        out_type=jax.ShapeDtypeStruct((n_rows_out, width), jnp.int32),
        scratch_types=[pltpu.VMEM((w, width), jnp.int32)] * 2 + [pltpu.VMEM((w,), jnp.int32)] * (2 * TOP_K)
        + [pltpu.SemaphoreType.DMA] * 3)
    def scatter_rows(vp_hbm, vs_hbm, d0_hbm, d1_hbm, d2_hbm, d3_hbm, o_hbm, rows0, rows1, *rest):
        worker = lax.axis_index("subcore") * mesh.num_cores + lax.axis_index("core")
        rows = (rows0, rows1)
        idx = (rest[:TOP_K], rest[TOP_K:2 * TOP_K])
        lsem, ssem = rest[2 * TOP_K:2 * TOP_K + 2], rest[2 * TOP_K + 2]
        d_hbm = (d0_hbm, d1_hbm, d2_hbm, d3_hbm)

        def load(c, slot, act):
            def index_reads():
                t0 = pl.multiple_of(c * w, w)
                for k in range(TOP_K):
                    act(pltpu.make_async_copy(d_hbm[k].at[pl.ds(t0, w)], idx[slot][k], lsem[slot]))

            @pl.when(c < n_pw)
            def _():
                src = vp_hbm.at[pl.ds(pl.multiple_of(c * w, w), w)]
                act(pltpu.make_async_copy(src, rows[slot], lsem[slot]))
                index_reads()

            @pl.when((c >= n_pw) & (c < n_windows))
            def _():
                src = vs_hbm.at[pl.ds(pl.multiple_of((c - n_pw) * w, w), w)]
                act(pltpu.make_async_copy(src, rows[slot], lsem[slot]))
                index_reads()

        load(worker, 0, lambda cp: cp.start())
        for j in range(n_rounds):
            c = j * n_workers + worker
            slot = j % 2
            load(c, slot, lambda cp: cp.wait())
            if j + 1 < n_rounds:
                load(c + n_workers, 1 - slot, lambda cp: cp.start())

            @pl.when(c < n_windows)
            def _():
                copies = [pltpu.async_copy(rows[slot], o_hbm.at[iv], ssem) for iv in idx[slot]]
                for cp in copies:
                    cp.wait()

    return scatter_rows(v_p, v_s, *dests)


def _experts_kernel(cnt_ref, row0_ref, pad_ref, nxt_ref, x_hbm, wu_hbm, bu_ref, wd_hbm, bd_ref, y_hbm,
                    xbuf, ybuf, wu_st, wd_st, wu_bf, wd_bf, ysz, xsem, ysem, wsem):
    def w_fetch(e):
        return (pltpu.make_async_copy(wu_hbm.at[e], wu_st, wsem.at[0]),
                pltpu.make_async_copy(wd_hbm.at[e], wd_st, wsem.at[1]))

    def x_fetch(row, size, slot):
        rows = pl.ds(pl.multiple_of(row, SLOT_PAD), size)
        return pltpu.make_async_copy(x_hbm.at[rows, :], xbuf.at[slot, pl.ds(0, size), :], xsem.at[slot])

    def y_store(row, size, slot):
        rows = pl.ds(pl.multiple_of(row, SLOT_PAD), size)
        return pltpu.make_async_copy(ybuf.at[slot, pl.ds(0, size), :], y_hbm.at[rows, :], ysem.at[slot])

    def y_wait(slot):
        for size in (BIG_BLOCK, SLOT_PAD):
            @pl.when(ysz[slot] == size)
            def _():
                y_store(0, size, slot).wait()

    def fetch_first(e, slot):
        @pl.when(pad_ref[e] >= BIG_BLOCK)
        def _():
            x_fetch(row0_ref[e], BIG_BLOCK, slot).start()

        @pl.when((pad_ref[e] > 0) & (pad_ref[e] < BIG_BLOCK))
        def _():
            x_fetch(row0_ref[e], SLOT_PAD, slot).start()

    ysz[0] = 0
    ysz[1] = 0
    for c in w_fetch(0):
        c.start()
    fetch_first(jnp.where(pad_ref[0] > 0, 0, nxt_ref[0]), 0)

    def expert_body(e, n_done):
        for c in w_fetch(e):
            c.wait()
        padded = pad_ref[e]

        @pl.when(padded > 0)
        def _():
            wu_bf[...] = wu_st[...].astype(_bf16)
            wd_bf[...] = wd_st[...].astype(_bf16)

        @pl.when(e + 1 < E)
        def _():
            for c in w_fetch(e + 1):
                c.start()

        b_up = bu_ref[e]
        b_down = bd_ref[e]
        row0 = row0_ref[e]
        n_big = padded // BIG_BLOCK
        n_small = (padded - n_big * BIG_BLOCK) // SLOT_PAD
        small0 = row0 + n_big * BIG_BLOCK
        nxt = nxt_ref[e]

        def pass_body(size, j, n_done):
            slot = n_done % 2
            big = size == BIG_BLOCK
            row = row0 + j * BIG_BLOCK if big else small0 + j * SLOT_PAD
            x_fetch(row, size, slot).wait()

            more = j + 1 < (n_big if big else n_small)
            tail = (n_small > 0) if big else False

            @pl.when(more)
            def _():
                x_fetch(row + size, size, 1 - slot).start()

            if big:
                @pl.when(jnp.logical_not(more) & tail)
                def _():
                    x_fetch(small0, SLOT_PAD, 1 - slot).start()

            @pl.when(jnp.logical_not(more) & jnp.logical_not(tail) & (nxt < E))
            def _():
                fetch_first(jnp.minimum(nxt, E - 1), 1 - slot)

            rows = lax.broadcasted_iota(jnp.int32, (size, 1), 0)
            x = _unpack_rows(jnp.where(rows < cnt_ref[e] - (row - row0), xbuf[slot, pl.ds(0, size), :], 0))
            hcat = _dot(x, wu_bf[...]) + b_up
            glu = jnp.minimum(hcat[:, 0:F], SWIGLU_LIMIT)
            lin = jnp.clip(hcat[:, F:2 * F], -SWIGLU_LIMIT, SWIGLU_LIMIT)
            act = glu * _sigmoid(SWIGLU_ALPHA * glu) * (lin + 1.0)
            y = _pack_rows(_dot(act.astype(_bf16), wd_bf[...]) + b_down)

            y_wait(slot)
            ybuf[slot, pl.ds(0, size), :] = y
            y_store(row, size, slot).start()
            ysz[slot] = size
            return n_done + 1

        n_done = lax.fori_loop(0, n_big, functools.partial(pass_body, BIG_BLOCK), n_done)
        return lax.fori_loop(0, n_small, functools.partial(pass_body, SLOT_PAD), n_done)

    lax.fori_loop(0, E, expert_body, 0)
    y_wait(0)
    y_wait(1)


def _experts(cnt, row0, padded, nxt, xpad, w_up, b_up, w_down, b_down):
    def full(a):
        nd = a.ndim
        return pl.BlockSpec(a.shape, lambda i, *_: (0,) * nd)

    return pl.pallas_call(
        _experts_kernel,
        out_shape=jax.ShapeDtypeStruct(xpad.shape, jnp.int32),
        grid_spec=pltpu.PrefetchScalarGridSpec(
            num_scalar_prefetch=4,
            grid=(1,),
            in_specs=[pl.BlockSpec(memory_space=pl.ANY),
                      pl.BlockSpec(memory_space=pl.ANY), full(b_up),
                      pl.BlockSpec(memory_space=pl.ANY), full(b_down)],
            out_specs=pl.BlockSpec(memory_space=pl.ANY),
            scratch_shapes=[pltpu.VMEM((2, BIG_BLOCK, D // 2), jnp.int32),
                            pltpu.VMEM((2, BIG_BLOCK, D // 2), jnp.int32),
                            pltpu.VMEM((D, 2 * F), _f32), pltpu.VMEM((F, D), _f32),
                            pltpu.VMEM((D, 2 * F), _bf16), pltpu.VMEM((F, D), _bf16),
                            pltpu.SMEM((2,), jnp.int32),
                            pltpu.SemaphoreType.DMA((2,)), pltpu.SemaphoreType.DMA((2,)),
                            pltpu.SemaphoreType.DMA((2,))],
        ),
        compiler_params=pltpu.CompilerParams(vmem_limit_bytes=VMEM_LIMIT),
        name="experts",
    )(cnt, row0, padded, nxt, xpad, w_up, b_up, w_down, b_down)


def _gather_rows(ypad, dest_all):
    n_out = dest_all.shape[0]
    width = ypad.shape[1]
    w = SC_WINDOW
    mesh = plsc.VectorSubcoreMesh(core_axis_name="core", subcore_axis_name="subcore")
    n_workers = mesh.num_cores * mesh.num_subcores
    n_windows = n_out // w

    @functools.partial(
        pl.kernel, mesh=mesh, name="gather_rows",
        out_type=jax.ShapeDtypeStruct((n_out, width), jnp.int32),
        scratch_types=[pltpu.VMEM((w, width), jnp.int32), pltpu.VMEM((w,), jnp.int32), pltpu.SemaphoreType.DMA])
    def gather_rows(y_hbm, d_hbm, o_hbm, rows, iv, sem):
        worker = lax.axis_index("subcore") * mesh.num_cores + lax.axis_index("core")

        @pl.loop(0, n_windows // n_workers)
        def _(j):
            r0 = pl.multiple_of((j * n_workers + worker) * w, w)
            pltpu.sync_copy(d_hbm.at[pl.ds(r0, w)], iv)
            pltpu.async_copy(y_hbm.at[iv], rows, sem).wait()
            pltpu.sync_copy(rows, o_hbm.at[pl.ds(r0, w)])

    assert n_windows % n_workers == 0
    return gather_rows(ypad, dest_all)


def _combine_kernel(h_ref, mod_ref, gate_ref, g2_ref, b2_ref, y_ref, *aliased_and_out):
    o_ref = aliased_and_out[-1]
    gates = gate_ref[...]
    half = D // 2
    f_hi = jnp.zeros((h_ref.shape[0], half), _f32)
    f_lo = jnp.zeros((h_ref.shape[0], half), _f32)
    for k in range(TOP_K):
        p = y_ref[k]
        gk = gates[:, k:k + 1]
        f_hi = f_hi + gk * lax.bitcast_convert_type(p & jnp.int32(-65536), _f32)
        f_lo = f_lo + gk * lax.bitcast_convert_type(lax.shift_left(p, 16), _f32)
    f = jnp.concatenate([f_hi, f_lo], axis=1)
    gate2 = mod_ref[...][:, 5 * D:6 * D]
    pre = ALPHA * h_ref[...] + _per_seq(f, gate2, jnp.multiply)
    o_ref[...] = _layer_norm(pre, g2_ref[...], b2_ref[...])


def _combine(h, mod3, gates, ln2_g, ln2_b, y4, token0, rows_per_mod, row0, out_so_far):
    t = COMBINE_TILE
    hoff = row0 // t
    goff = (token0 + row0) // t
    g = mod3.shape[1]
    tiles_per_mod = rows_per_mod // t
    in_specs = [pl.BlockSpec((t, D), lambda i: (i + hoff, 0)),
                pl.BlockSpec((None, g, 6 * D), lambda i: ((i + hoff) // tiles_per_mod, 0, 0)),
                pl.BlockSpec((t, LANES), lambda i: (i + goff, 0)),
                pl.BlockSpec((1, D), lambda i: (0, 0)),
                pl.BlockSpec((1, D), lambda i: (0, 0)),
                pl.BlockSpec((TOP_K, t, D // 2), lambda i: (0, i, 0))]
    args = [h, mod3, gates, ln2_g, ln2_b, y4]
    aliases = {}
    if out_so_far is not None:
        in_specs.append(pl.BlockSpec(memory_space=pl.ANY))
        args.append(out_so_far)
        aliases = {len(args) - 1: 0}
    return pl.pallas_call(
        _combine_kernel,
        out_shape=jax.ShapeDtypeStruct(h.shape, _f32),
        grid=(y4.shape[1] // t,),
        in_specs=in_specs,
        out_specs=pl.BlockSpec((t, D), lambda i: (i + hoff, 0)),
        input_output_aliases=aliases,
        compiler_params=pltpu.CompilerParams(vmem_limit_bytes=VMEM_LIMIT),
        name="combine",
    )(*args)


def _time_major(a):
    return a.transpose(1, 0, 2)


def kernel(x_prompt, x_sample, c_prompt, c_sample, state_conv, state_pool, w_ada, b_ada, w_in,
           conv_w, w_out_a, w_pool, ls_pool, w_out_b, w_o, ln1_g, ln1_b, w_router, b_router,
           w_up, b_up, w_down, b_down, ln2_g, ln2_b):
    n_seq_p, seq, _ = x_prompt.shape
    n_seq_s, dec_seq, _ = x_sample.shape
    n_p, n_s = n_seq_p * seq, n_seq_s * dec_seq
    n = n_p + n_s
    n_slots = TOP_K * n + E * SLOT_PAD
    l = 0

    mod = _ada(jnp.concatenate([c_prompt, c_sample], axis=0), w_ada[l], b_ada[l][None])
    mod_p = mod[:n_seq_p][:, None, :]
    mod_s = mod[n_seq_p:][None]

    weights = (
        w_in[l].astype(_bf16), conv_w[l], w_out_a[l].astype(_bf16), w_pool[l].astype(_bf16),
        ls_pool[l][None], w_out_b[l].astype(_bf16), w_o[l].astype(_bf16), ln1_g[l][None], ln1_b[l][None],
        jnp.pad(w_router[l], ((0, 0), (0, LANES - E))), jnp.pad(b_router[l], (0, LANES - E))[None],
    )
    hc_p, hp_p = _hist_steps(CONV_HIST, 1), DOUBLING_HIST
    h_p, v_p, lg_p, nc_p, np_p = _mixer(
        x_prompt.reshape(n_p, D), mod_p, jnp.zeros((n_seq_p * hc_p, C), _f32),
        jnp.zeros((n_seq_p * hp_p, C), _f32), weights, PROMPT_ROW_TILE, 0)
    hc_s, hp_s = _hist_steps(CONV_HIST, n_seq_s), _hist_steps(POOL_HIST, n_seq_s)
    hist_c = jnp.pad(_time_major(state_conv[l]), ((hc_s - CONV_HIST, 0), (0, 0), (0, 0)))
    hist_p = jnp.pad(_time_major(state_pool[l]), ((hp_s - POOL_HIST, 0), (0, 0), (0, 0)))
    h_s, v_s, lg_s, nc_s, np_s = _mixer(
        _time_major(x_sample).reshape(n_s, D), mod_s, hist_c.reshape(hc_s * n_seq_s, C),
        hist_p.reshape(hp_s * n_seq_s, C), weights, SAMPLE_ROW_TILE, PAST_LEN)

    dest8, gates3, meta = _plan(lg_p, lg_s)
    cnt, row0, padded, nxt = (meta[i, :E] for i in range(4))
    gates = gates3.reshape(n, LANES)

    dests = [dest8[:, k, :].reshape(n) for k in range(TOP_K)]
    xpad = _dispatch(v_p, v_s, dests, n_slots)
    ypad = _experts(cnt, row0, padded, nxt, xpad, w_up[l], b_up[l][:, None, :], w_down[l], b_down[l][:, None, :])

    def gathered(t0, rows):
        idx = jnp.concatenate([dk[t0:t0 + rows] for dk in dests])
        return _gather_rows(ypad, idx).reshape(TOP_K, rows, D // 2)

    chunk = n_p // COMBINE_CHUNKS
    y4_p = [gathered(c * chunk, chunk) for c in range(COMBINE_CHUNKS)]
    y4_s = gathered(n_p, n_s)
    g2, b2 = ln2_g[l][None], ln2_b[l][None]
    y_p = None
    for c in range(COMBINE_CHUNKS):
        y_p = _combine(h_p, mod_p, gates, g2, b2, y4_p[c], 0, seq, c * chunk, y_p)
    y_s = _combine(h_s, mod_s, gates, g2, b2, y4_s, n_p, n_s, 0, None)

    y_prompt = y_p.reshape(n_seq_p, seq, D)
    y_sample = _time_major(y_s.reshape(dec_seq, n_seq_s, D))
    new_conv_p = nc_p.reshape(n_seq_p, hc_p, C)[:, hc_p - CONV_HIST:][None]
    new_pool_p = np_p.reshape(n_seq_p, hp_p, C)[:, hp_p - POOL_HIST:][None]
    new_conv_s = _time_major(nc_s.reshape(hc_s, n_seq_s, C)[hc_s - CONV_HIST:])[None]
    new_pool_s = _time_major(np_s.reshape(hp_s, n_seq_s, C)[hp_s - POOL_HIST:])[None]
    return (y_prompt, y_sample, new_conv_p, new_pool_p, new_conv_s, new_pool_s)
```

```python
import functools

import jax
import jax.numpy as jnp
from jax import lax
from jax.experimental import pallas as pl
from jax.experimental.pallas import tpu as pltpu
from jax.experimental.pallas import tpu_sc as plsc

D = 1024
C = 512
N_GROUPS = 4
GROUP = C // N_GROUPS
CONV_HIST = 2
POOL_HIST = 15
DOUBLING_HIST = 32
E = 32
TOP_K = 4
F = 1024
SWIGLU_LIMIT = 7.0
SWIGLU_ALPHA = 1.702
LN_EPS = 1e-5
DEPTH = 1
ALPHA = (2 * DEPTH) ** 0.25
PAST_LEN = 16384

LANES = 128
SUBLANES = 8
ROW_TILE = 512
PROMPT_ROW_TILE = 1024
SAMPLE_ROW_TILE = 512
SLOT_PAD = 256
BIG_BLOCK = 1024
COMBINE_TILE = 1024
COMBINE_CHUNKS = 4
DISPATCH_WINDOW = 64
ADA_COLS = 1536
VMEM_LIMIT = 56 * 1024 * 1024

_f32 = jnp.float32
_bf16 = jnp.bfloat16


def _dot(a, b):
    return jnp.dot(a, b, preferred_element_type=_f32)


def _dot_exact(a, b):
    return lax.dot_general(a, b, (((1,), (0,)), ((), ())),
                           precision=lax.Precision.HIGHEST, preferred_element_type=_f32)


def _pack_rows(x):
    w = x.shape[1] // 2
    hi = lax.bitcast_convert_type(x[:, :w].astype(_bf16).astype(_f32), jnp.int32)
    lo = lax.bitcast_convert_type(x[:, w:].astype(_bf16).astype(_f32), jnp.int32)
    return hi | lax.shift_right_logical(lo, 16)


def _unpack_rows(p):
    hi = lax.bitcast_convert_type(p & jnp.int32(-65536), _f32)
    lo = lax.bitcast_convert_type(lax.shift_left(p, 16), _f32)
    return jnp.concatenate([hi, lo], axis=1).astype(_bf16)


def _sigmoid(x):
    return 0.5 * jnp.tanh(0.5 * x) + 0.5


def _per_seq(x, m, op):
    g = m.shape[0]
    if g == 1:
        return op(x, m)
    r, n = x.shape
    return op(x.reshape(r // g, g, n), m[None]).reshape(r, n)


def _layer_norm(x, g, b):
    mu = jnp.mean(x, axis=-1, keepdims=True)
    xc = x - mu
    var = jnp.mean(xc * xc, axis=-1, keepdims=True)
    return xc * lax.rsqrt(var + LN_EPS) * g + b


def _hist_steps(needed, g):
    return -(-needed * g // SUBLANES) * SUBLANES // g


def _ada_kernel(c_ref, w_ref, b_ref, o_ref):
    c = c_ref[...]
    o_ref[...] = _dot((c * _sigmoid(c)).astype(_bf16), w_ref[...].astype(_bf16)) + b_ref[...]


def _ada(c, w_ada, b_ada):
    rows = c.shape[0]
    cols = w_ada.shape[1]
    bn = ADA_COLS
    return pl.pallas_call(
        _ada_kernel,
        out_shape=jax.ShapeDtypeStruct((rows, cols), _f32),
        grid=(cols // bn,),
        in_specs=[pl.BlockSpec((rows, D), lambda j: (0, 0)),
                  pl.BlockSpec((D, bn), lambda j: (0, j)),
                  pl.BlockSpec((1, bn), lambda j: (0, j))],
        out_specs=pl.BlockSpec((rows, bn), lambda j: (0, j)),
        compiler_params=pltpu.CompilerParams(vmem_limit_bytes=VMEM_LIMIT),
        name="ada",
    )(c, w_ada, b_ada)


def _mixer_kernel(g, tiles_per_seq, start_pos,
                  x_ref, mod_ref, hc_ref, hp_ref,
                  win_ref, cw_ref, woa_ref, wpool_ref, ls_ref, wob_ref, wo_ref, g1_ref, b1_ref,
                  wr_ref, br_ref,
                  h_ref, v_ref, lg_ref, nc_ref, np_ref, zbuf, pbuf, *lvl):
    r = x_ref.shape[0]
    hrc = hc_ref.shape[0]
    hrp = hp_ref.shape[0]
    j = pl.program_id(0) % tiles_per_seq

    @pl.when(j == 0)
    def _():
        zbuf[pl.ds(0, hrc), :] = hc_ref[...]
        pbuf[pl.ds(0, hrp), :] = hp_ref[...]

    @pl.when(j != 0)
    def _():
        zt = zbuf[pl.ds(r, hrc), :]
        pt = pbuf[pl.ds(r, hrp), :]
        zbuf[pl.ds(0, hrc), :] = zt
        pbuf[pl.ds(0, hrp), :] = pt

    m = mod_ref[...]
    shift1, scale1, gate1 = m[:, 0:D], m[:, D:2 * D], m[:, 2 * D:3 * D]
    shift2, scale2 = m[:, 3 * D:4 * D], m[:, 4 * D:5 * D]

    x = x_ref[...]
    u = _per_seq(_per_seq(x, 1.0 + scale1, jnp.multiply), shift1, jnp.add).astype(_bf16)

    z = _dot(u, win_ref[:, C:2 * C]) * _dot(u, win_ref[:, 2 * C:3 * C])
    zbuf[pl.ds(hrc, r), :] = z
    cw = cw_ref[...]
    conv = (cw[0:1] * zbuf[pl.ds(hrc - 2 * g, r), :] + cw[1:2] * zbuf[pl.ds(hrc - g, r), :]
            + cw[2:3] * z)
    y_a = _dot((_dot(u, win_ref[:, 0:C]) * conv).astype(_bf16), woa_ref[...])

    xp = _dot(u, win_ref[:, 3 * C:4 * C])
    pbuf[pl.ds(hrp, r), :] = xp
    pos = start_pos + j * (r // g) + lax.broadcasted_iota(jnp.int32, (r, 1), 0) // g
    wins = []
    if lvl:
        s2, s4, s8 = lvl
        s2[...] = pbuf[pl.ds(hrp - 24, r + 24), :] + pbuf[pl.ds(hrp - 25, r + 24), :]
        s4[...] = s2[pl.ds(8, r + 16), GROUP:C] + s2[pl.ds(6, r + 16), GROUP:C]
        s8[...] = s4[pl.ds(8, r + 8), GROUP:3 * GROUP] + s4[pl.ds(4, r + 8), GROUP:3 * GROUP]
        wins = [s2[pl.ds(24, r), 0:GROUP], s4[pl.ds(16, r), 0:GROUP], s8[pl.ds(8, r), 0:GROUP],
                s8[pl.ds(8, r), GROUP:2 * GROUP] + s8[pl.ds(0, r), GROUP:2 * GROUP]]
    else:
        acc = xp
        for grp in range(N_GROUPS):
            lo = grp * GROUP
            wdw = 2 ** (grp + 1)
            for back in range(wdw // 2, wdw):
                sh = pbuf[pl.ds(hrp - back * g, r), lo:C]
                acc = jnp.concatenate([acc[:, 0:lo], acc[:, lo:C] + sh], axis=1) if lo else acc + sh
            wins.append(acc[:, lo:lo + GROUP])
    yg = []
    for grp in range(N_GROUPS):
        lo = grp * GROUP
        inv_cnt = 1.0 / jnp.minimum(2 ** (grp + 1), pos + 1).astype(_f32)
        diff = wins[grp] * inv_cnt - xp[:, lo:lo + GROUP]
        yg.append(_dot(diff.astype(_bf16), wpool_ref[grp]))
    y_b = _dot((jnp.concatenate(yg, axis=1) * ls_ref[...]).astype(_bf16), wob_ref[...])

    g_a = _dot(u, win_ref[:, 4 * C:4 * C + D])
    g_b = _dot(u, win_ref[:, 4 * C + D:4 * C + 2 * D])
    merged = _sigmoid(g_a) * y_a + _sigmoid(g_b) * y_b
    o = _dot(merged.astype(_bf16), wo_ref[...])
    h = _layer_norm(ALPHA * x + _per_seq(o, gate1, jnp.multiply), g1_ref[...], b1_ref[...])
    v = _per_seq(_per_seq(h, 1.0 + scale2, jnp.multiply), shift2, jnp.add)
    h_ref[...] = h
    v_ref[...] = _pack_rows(v)
    lg_ref[...] = _dot(v.astype(_bf16), wr_ref[...].astype(_bf16)) + br_ref[...]
    nc_ref[...] = zbuf[pl.ds(r, hrc), :]
    np_ref[...] = pbuf[pl.ds(r, hrp), :]


def _mixer(x2, mod3, hc, hp, weights, row_tile, start_pos):
    n = x2.shape[0]
    n_mod, g, _ = mod3.shape
    hrc, hrp = hc.shape[0] // n_mod, hp.shape[0] // n_mod
    tiles_per_seq = n // n_mod // row_tile
    once = dict(pipeline_mode=pl.Buffered(1)) if n_mod == 1 else {}

    def full(a):
        nd = a.ndim
        return pl.BlockSpec(a.shape, lambda i: (0,) * nd)

    def seq_block(rows, **kw):
        return pl.BlockSpec((rows, C), lambda i: (i // tiles_per_seq, 0), **kw)

    def row_block(cols):
        return pl.BlockSpec((row_tile, cols), lambda i: (i, 0))

    levels = []
    if g == 1:
        assert hrp >= DOUBLING_HIST
        levels = [pltpu.VMEM((row_tile + 24, C), _f32), pltpu.VMEM((row_tile + 16, 3 * GROUP), _f32),
                  pltpu.VMEM((row_tile + 8, 2 * GROUP), _f32)]

    return pl.pallas_call(
        functools.partial(_mixer_kernel, g, tiles_per_seq, start_pos),
        out_shape=[
            jax.ShapeDtypeStruct((n, D), _f32),
            jax.ShapeDtypeStruct((n, D // 2), jnp.int32),
            jax.ShapeDtypeStruct((n, LANES), _f32),
            jax.ShapeDtypeStruct(hc.shape, _f32),
            jax.ShapeDtypeStruct(hp.shape, _f32),
        ],
        grid=(n // row_tile,),
        in_specs=[row_block(D),
                  pl.BlockSpec((None, g, 6 * D), lambda i: (i // tiles_per_seq, 0, 0), **once),
                  seq_block(hrc, **once), seq_block(hrp, **once)] + [full(a) for a in weights],
        out_specs=[row_block(D), row_block(D // 2), row_block(LANES), seq_block(hrc), seq_block(hrp)],
        scratch_shapes=[pltpu.VMEM((hrc + row_tile, C), _f32), pltpu.VMEM((hrp + row_tile, C), _f32)] + levels,
        compiler_params=pltpu.CompilerParams(vmem_limit_bytes=VMEM_LIMIT),
        name="mixer",
    )(x2, mod3, hc, hp, *weights)


def _plan_kernel(lgp_ref, lgs_ref, dest_ref, gate_ref, meta_ref, idx_s, rank_s):
    t = ROW_TILE
    e_iota = lax.broadcasted_iota(jnp.int32, (E, t), 0)
    tri = (lax.broadcasted_iota(jnp.int32, (t, t), 0)
           < lax.broadcasted_iota(jnp.int32, (t, t), 1)).astype(_f32).astype(_bf16)
    zeros_rest = jnp.zeros((LANES - TOP_K, t), _f32)

    def tile_body(lg_ref, off, i, carry):
        lt = lg_ref[i].T[0:E, :]
        vals, idxs = [], []
        for _ in range(TOP_K):
            mx = jnp.max(lt, axis=0, keepdims=True)
            ix = jnp.min(jnp.where(lt == mx, e_iota, E), axis=0, keepdims=True)
            vals.append(mx)
            idxs.append(ix)
            lt = jnp.where(e_iota == ix, -jnp.inf, lt)
        ex = [jnp.exp(vk - vals[0]) for vk in vals]
        den = ex[0] + ex[1] + ex[2] + ex[3]
        gates = [ek / den for ek in ex]
        gate_ref[off + i] = jnp.concatenate(gates + [zeros_rest], axis=0).T

        ohs = [(e_iota == ix) for ix in idxs]
        oh = (ohs[0] | ohs[1] | ohs[2] | ohs[3]).astype(_f32)
        before = _dot(oh.astype(_bf16), tri) + carry
        ranks = [jnp.sum(jnp.where(o, before, 0.0), axis=0, keepdims=True) for o in ohs]
        idx_s[off + i] = jnp.concatenate(idxs + idxs, axis=0)
        rank_s[off + i] = jnp.concatenate(ranks + ranks, axis=0).astype(jnp.int32)
        return carry + jnp.sum(oh, axis=1, keepdims=True)

    n_p, n_s = lgp_ref.shape[0], lgs_ref.shape[0]
    counts = lax.fori_loop(0, n_p, functools.partial(tile_body, lgp_ref, 0), jnp.zeros((E, 1), _f32))
    counts = lax.fori_loop(0, n_s, functools.partial(tile_body, lgs_ref, n_p), counts)
    padded = jnp.ceil(counts / SLOT_PAD) * SLOT_PAD
    low = (lax.broadcasted_iota(jnp.int32, (E, E), 1)
           <= lax.broadcasted_iota(jnp.int32, (E, E), 0)).astype(_f32)
    pad_end = _dot_exact(low, jnp.broadcast_to(padded, (E, LANES)))[:, 0:1]
    pad_start = pad_end - padded

    def dest_body(i, c):
        ix = idx_s[i]
        rk = rank_s[i]
        rows = []
        for k in range(TOP_K):
            st = jnp.sum(jnp.where(e_iota == ix[k:k + 1], pad_start, 0.0), axis=0, keepdims=True)
            rows.append(st.astype(jnp.int32) + rk[k:k + 1])
        dest_ref[i] = jnp.concatenate(rows + rows, axis=0)
        return c

    lax.fori_loop(0, n_p + n_s, dest_body, 0)

    sub = lax.broadcasted_iota(jnp.int32, (E, LANES), 0)
    lane = lax.broadcasted_iota(jnp.int32, (E, LANES), 1)

    def to_lanes(col):
        return jnp.sum(jnp.where(sub == lane, col, 0.0), axis=0, keepdims=True).astype(jnp.int32)

    later = jnp.min(jnp.where((sub > lane) & (counts > 0.0), sub, E), axis=0, keepdims=True)
    meta_ref[...] = jnp.concatenate(
        [to_lanes(counts), to_lanes(pad_start), to_lanes(padded), later,
         jnp.zeros((SUBLANES - 4, LANES), jnp.int32)], axis=0)


def _plan(logits_p, logits_s):
    n_tiles = (logits_p.shape[0] + logits_s.shape[0]) // ROW_TILE
    return pl.pallas_call(
        _plan_kernel,
        out_shape=[
            jax.ShapeDtypeStruct((n_tiles, 2 * TOP_K, ROW_TILE), jnp.int32),
            jax.ShapeDtypeStruct((n_tiles, ROW_TILE, LANES), _f32),
            jax.ShapeDtypeStruct((SUBLANES, LANES), jnp.int32),
        ],
        scratch_shapes=[pltpu.VMEM((n_tiles, 2 * TOP_K, ROW_TILE), jnp.int32),
                        pltpu.VMEM((n_tiles, 2 * TOP_K, ROW_TILE), jnp.int32)],
        compiler_params=pltpu.CompilerParams(vmem_limit_bytes=VMEM_LIMIT),
        name="plan",
    )(logits_p.reshape(-1, ROW_TILE, LANES), logits_s.reshape(-1, ROW_TILE, LANES))


def _dispatch(v_p, v_s, dests, n_rows_out):
    n_p, n_s = v_p.shape[0], v_s.shape[0]
    width = v_p.shape[1]
    w = DISPATCH_WINDOW
    n_pw, n_windows = n_p // w, (n_p + n_s) // w
    mesh = plsc.VectorSubcoreMesh(core_axis_name="core", subcore_axis_name="subcore")
    n_workers = mesh.num_cores * mesh.num_subcores
    n_rounds = -(-n_windows // n_workers)

    @functools.partial(
        pl.kernel, mesh=mesh, name="dispatch",
        out_type=jax.ShapeDtypeStruct((n_rows_out, width), jnp.int32),
        scratch_types=[pltpu.VMEM((w, width), jnp.int32)] * 2 + [pltpu.VMEM((w,), jnp.int32)] * (2 * TOP_K)
        + [pltpu.SemaphoreType.DMA] * 3)
    def scatter_rows(vp_hbm, vs_hbm, d0_hbm, d1_hbm, d2_hbm, d3_hbm, o_hbm, rows0, rows1, *rest):
        worker = lax.axis_index("subcore") * mesh.num_cores + lax.axis_index("core")
        rows = (rows0, rows1)
        idx = (rest[:TOP_K], rest[TOP_K:2 * TOP_K])
        lsem, ssem = rest[2 * TOP_K:2 * TOP_K + 2], rest[2 * TOP_K + 2]
        d_hbm = (d0_hbm, d1_hbm, d2_hbm, d3_hbm)

        def load(c, slot, act):
            def index_reads():
                t0 = pl.multiple_of(c * w, w)
                for k in range(TOP_K):
                    act(pltpu.make_async_copy(d_hbm[k].at[pl.ds(t0, w)], idx[slot][k], lsem[slot]))

            @pl.when(c < n_pw)
            def _():
                src = vp_hbm.at[pl.ds(pl.multiple_of(c * w, w), w)]
                act(pltpu.make_async_copy(src, rows[slot], lsem[slot]))
                index_reads()

            @pl.when((c >= n_pw) & (c < n_windows))
            def _():
                src = vs_hbm.at[pl.ds(pl.multiple_of((c - n_pw) * w, w), w)]
                act(pltpu.make_async_copy(src, rows[slot], lsem[slot]))
                index_reads()

        load(worker, 0, lambda cp: cp.start())
        for j in range(n_rounds):
            c = j * n_workers + worker
            slot = j % 2
            load(c, slot, lambda cp: cp.wait())
            if j + 1 < n_rounds:
                load(c + n_workers, 1 - slot, lambda cp: cp.start())

            @pl.when(c < n_windows)
            def _():
                copies = [pltpu.async_copy(rows[slot], o_hbm.at[iv], ssem) for iv in idx[slot]]
                for cp in copies:
                    cp.wait()

    return scatter_rows(v_p, v_s, *dests)


def _experts_kernel(cnt_ref, row0_ref, pad_ref, nxt_ref, x_hbm, wu_hbm, bu_ref, wd_hbm, bd_ref, y_hbm,
                    xbuf, ybuf, wu_st, wd_st, wu_bf, wd_bf, ysz, xsem, ysem, wsem):
    def w_fetch(e):
        return (pltpu.make_async_copy(wu_hbm.at[e], wu_st, wsem.at[0]),
                pltpu.make_async_copy(wd_hbm.at[e], wd_st, wsem.at[1]))

    def x_fetch(row, size, slot):
        rows = pl.ds(pl.multiple_of(row, SLOT_PAD), size)
        return pltpu.make_async_copy(x_hbm.at[rows, :], xbuf.at[slot, pl.ds(0, size), :], xsem.at[slot])

    def y_store(row, size, slot):
        rows = pl.ds(pl.multiple_of(row, SLOT_PAD), size)
        return pltpu.make_async_copy(ybuf.at[slot, pl.ds(0, size), :], y_hbm.at[rows, :], ysem.at[slot])

    def y_wait(slot):
        for size in (BIG_BLOCK, SLOT_PAD):
            @pl.when(ysz[slot] == size)
            def _():
                y_store(0, size, slot).wait()

    def fetch_first(e, slot):
        @pl.when(pad_ref[e] >= BIG_BLOCK)
        def _():
            x_fetch(row0_ref[e], BIG_BLOCK, slot).start()

        @pl.when((pad_ref[e] > 0) & (pad_ref[e] < BIG_BLOCK))
        def _():
            x_fetch(row0_ref[e], SLOT_PAD, slot).start()

    ysz[0] = 0
    ysz[1] = 0
    for c in w_fetch(0):
        c.start()
    fetch_first(jnp.where(pad_ref[0] > 0, 0, nxt_ref[0]), 0)

    def expert_body(e, n_done):
        for c in w_fetch(e):
            c.wait()
        padded = pad_ref[e]

        @pl.when(padded > 0)
        def _():
            wu_bf[...] = wu_st[...].astype(_bf16)
            wd_bf[...] = wd_st[...].astype(_bf16)

        @pl.when(e + 1 < E)
        def _():
            for c in w_fetch(e + 1):
                c.start()

        b_up = bu_ref[e]
        b_down = bd_ref[e]
        row0 = row0_ref[e]
        n_big = padded // BIG_BLOCK
        n_small = (padded - n_big * BIG_BLOCK) // SLOT_PAD
        small0 = row0 + n_big * BIG_BLOCK
        nxt = nxt_ref[e]

        def pass_body(size, j, n_done):
            slot = n_done % 2
            big = size == BIG_BLOCK
            row = row0 + j * BIG_BLOCK if big else small0 + j * SLOT_PAD
            x_fetch(row, size, slot).wait()

            more = j + 1 < (n_big if big else n_small)
            tail = (n_small > 0) if big else False

            @pl.when(more)
            def _():
                x_fetch(row + size, size, 1 - slot).start()

            if big:
                @pl.when(jnp.logical_not(more) & tail)
                def _():
                    x_fetch(small0, SLOT_PAD, 1 - slot).start()

            @pl.when(jnp.logical_not(more) & jnp.logical_not(tail) & (nxt < E))
            def _():
                fetch_first(jnp.minimum(nxt, E - 1), 1 - slot)

            rows = lax.broadcasted_iota(jnp.int32, (size, 1), 0)
            x = _unpack_rows(jnp.where(rows < cnt_ref[e] - (row - row0), xbuf[slot, pl.ds(0, size), :], 0))
            hcat = _dot(x, wu_bf[...]) + b_up
            glu = jnp.minimum(hcat[:, 0:F], SWIGLU_LIMIT)
            lin = jnp.clip(hcat[:, F:2 * F], -SWIGLU_LIMIT, SWIGLU_LIMIT)
            act = glu * _sigmoid(SWIGLU_ALPHA * glu) * (lin + 1.0)
            y = _pack_rows(_dot(act.astype(_bf16), wd_bf[...]) + b_down)

            y_wait(slot)
            ybuf[slot, pl.ds(0, size), :] = y
            y_store(row, size, slot).start()
            ysz[slot] = size
            return n_done + 1

        n_done = lax.fori_loop(0, n_big, functools.partial(pass_body, BIG_BLOCK), n_done)
        return lax.fori_loop(0, n_small, functools.partial(pass_body, SLOT_PAD), n_done)

    lax.fori_loop(0, E, expert_body, 0)
    y_wait(0)
    y_wait(1)


def _experts(cnt, row0, padded, nxt, xpad, w_up, b_up, w_down, b_down):
    def full(a):
        nd = a.ndim
        return pl.BlockSpec(a.shape, lambda i, *_: (0,) * nd)

    return pl.pallas_call(
        _experts_kernel,
        out_shape=jax.ShapeDtypeStruct(xpad.shape, jnp.int32),
        grid_spec=pltpu.PrefetchScalarGridSpec(
            num_scalar_prefetch=4,
            grid=(1,),
            in_specs=[pl.BlockSpec(memory_space=pl.ANY),
                      pl.BlockSpec(memory_space=pl.ANY), full(b_up),
                      pl.BlockSpec(memory_space=pl.ANY), full(b_down)],
            out_specs=pl.BlockSpec(memory_space=pl.ANY),
            scratch_shapes=[pltpu.VMEM((2, BIG_BLOCK, D // 2), jnp.int32),
                            pltpu.VMEM((2, BIG_BLOCK, D // 2), jnp.int32),
                            pltpu.VMEM((D, 2 * F), _f32), pltpu.VMEM((F, D), _f32),
                            pltpu.VMEM((D, 2 * F), _bf16), pltpu.VMEM((F, D), _bf16),
                            pltpu.SMEM((2,), jnp.int32),
                            pltpu.SemaphoreType.DMA((2,)), pltpu.SemaphoreType.DMA((2,)),
                            pltpu.SemaphoreType.DMA((2,))],
        ),
        compiler_params=pltpu.CompilerParams(vmem_limit_bytes=VMEM_LIMIT),
        name="experts",
    )(cnt, row0, padded, nxt, xpad, w_up, b_up, w_down, b_down)


def _gather_rows(ypad, dest_all):
    n_out = dest_all.shape[0]
    width = ypad.shape[1]
    w = DISPATCH_WINDOW
    mesh = plsc.VectorSubcoreMesh(core_axis_name="core", subcore_axis_name="subcore")
    n_workers = mesh.num_cores * mesh.num_subcores
    n_windows = n_out // w
    n_rounds = n_windows // n_workers

    @functools.partial(
        pl.kernel, mesh=mesh, name="gather_rows",
        out_type=jax.ShapeDtypeStruct((n_out, width), jnp.int32),
        scratch_types=[pltpu.VMEM((w, width), jnp.int32)] * 2 + [pltpu.VMEM((w,), jnp.int32)] * 2
        + [pltpu.SemaphoreType.DMA] * 4)
    def gather_rows(y_hbm, d_hbm, o_hbm, rows0, rows1, iv0, iv1, gsem0, gsem1, wsem0, wsem1):
        worker = lax.axis_index("subcore") * mesh.num_cores + lax.axis_index("core")
        rows, iv, gsem, wsem = (rows0, rows1), (iv0, iv1), (gsem0, gsem1), (wsem0, wsem1)

        def first_row(j):
            return pl.multiple_of((j * n_workers + worker) * w, w)

        def gather(j, slot):
            return pltpu.make_async_copy(y_hbm.at[iv[slot]], rows[slot], gsem[slot])

        def write(j, slot):
            return pltpu.make_async_copy(rows[slot], o_hbm.at[pl.ds(first_row(j), w)], wsem[slot])

        pltpu.sync_copy(d_hbm.at[pl.ds(first_row(0), w)], iv[0])
        gather(0, 0).start()
        for j in range(n_rounds):
            slot = j % 2
            if j + 1 < n_rounds:
                if j >= 1:
                    write(j - 1, 1 - slot).wait()
                pltpu.sync_copy(d_hbm.at[pl.ds(first_row(j + 1), w)], iv[1 - slot])
                gather(j + 1, 1 - slot).start()
            gather(j, slot).wait()
            write(j, slot).start()
        for j in range(max(n_rounds - 2, 0), n_rounds):
            write(j, j % 2).wait()

    assert n_windows % n_workers == 0
    return gather_rows(ypad, dest_all)


def _combine_kernel(h_ref, mod_ref, gate_ref, g2_ref, b2_ref, y_ref, *aliased_and_out):
    o_ref = aliased_and_out[-1]
    gates = gate_ref[...]
    half = D // 2
    f_hi = jnp.zeros((h_ref.shape[0], half), _f32)
    f_lo = jnp.zeros((h_ref.shape[0], half), _f32)
    for k in range(TOP_K):
        p = y_ref[k]
        gk = gates[:, k:k + 1]
        f_hi = f_hi + gk * lax.bitcast_convert_type(p & jnp.int32(-65536), _f32)
        f_lo = f_lo + gk * lax.bitcast_convert_type(lax.shift_left(p, 16), _f32)
    f = jnp.concatenate([f_hi, f_lo], axis=1)
    gate2 = mod_ref[...][:, 5 * D:6 * D]
    pre = ALPHA * h_ref[...] + _per_seq(f, gate2, jnp.multiply)
    o_ref[...] = _layer_norm(pre, g2_ref[...], b2_ref[...])


def _combine(h, mod3, gates, ln2_g, ln2_b, y4, token0, rows_per_mod, row0, out_so_far):
    t = COMBINE_TILE
    hoff = row0 // t
    goff = (token0 + row0) // t
    g = mod3.shape[1]
    tiles_per_mod = rows_per_mod // t
    in_specs = [pl.BlockSpec((t, D), lambda i: (i + hoff, 0)),
                pl.BlockSpec((None, g, 6 * D), lambda i: ((i + hoff) // tiles_per_mod, 0, 0)),
                pl.BlockSpec((t, LANES), lambda i: (i + goff, 0)),
                pl.BlockSpec((1, D), lambda i: (0, 0)),
                pl.BlockSpec((1, D), lambda i: (0, 0)),
                pl.BlockSpec((TOP_K, t, D // 2), lambda i: (0, i, 0))]
    args = [h, mod3, gates, ln2_g, ln2_b, y4]
    aliases = {}
    if out_so_far is not None:
        in_specs.append(pl.BlockSpec(memory_space=pl.ANY))
        args.append(out_so_far)
        aliases = {len(args) - 1: 0}
    return pl.pallas_call(
        _combine_kernel,
        out_shape=jax.ShapeDtypeStruct(h.shape, _f32),
        grid=(y4.shape[1] // t,),
        in_specs=in_specs,
        out_specs=pl.BlockSpec((t, D), lambda i: (i + hoff, 0)),
        input_output_aliases=aliases,
        compiler_params=pltpu.CompilerParams(vmem_limit_bytes=VMEM_LIMIT),
        name="combine",
    )(*args)


def _time_major(a):
    return a.transpose(1, 0, 2)


def kernel(x_prompt, x_sample, c_prompt, c_sample, state_conv, state_pool, w_ada, b_ada, w_in,
           conv_w, w_out_a, w_pool, ls_pool, w_out_b, w_o, ln1_g, ln1_b, w_router, b_router,
           w_up, b_up, w_down, b_down, ln2_g, ln2_b):
    n_seq_p, seq, _ = x_prompt.shape
    n_seq_s, dec_seq, _ = x_sample.shape
    n_p, n_s = n_seq_p * seq, n_seq_s * dec_seq
    n = n_p + n_s
    n_slots = TOP_K * n + E * SLOT_PAD
    l = 0

    mod = _ada(jnp.concatenate([c_prompt, c_sample], axis=0), w_ada[l], b_ada[l][None])
    mod_p = mod[:n_seq_p][:, None, :]
    mod_s = mod[n_seq_p:][None]

    weights = (
        w_in[l].astype(_bf16), conv_w[l], w_out_a[l].astype(_bf16), w_pool[l].astype(_bf16),
        ls_pool[l][None], w_out_b[l].astype(_bf16), w_o[l].astype(_bf16), ln1_g[l][None], ln1_b[l][None],
        jnp.pad(w_router[l], ((0, 0), (0, LANES - E))), jnp.pad(b_router[l], (0, LANES - E))[None],
    )
    hc_p, hp_p = _hist_steps(CONV_HIST, 1), DOUBLING_HIST
    h_p, v_p, lg_p, nc_p, np_p = _mixer(
        x_prompt.reshape(n_p, D), mod_p, jnp.zeros((n_seq_p * hc_p, C), _f32),
        jnp.zeros((n_seq_p * hp_p, C), _f32), weights, PROMPT_ROW_TILE, 0)
    hc_s, hp_s = _hist_steps(CONV_HIST, n_seq_s), _hist_steps(POOL_HIST, n_seq_s)
    hist_c = jnp.pad(_time_major(state_conv[l]), ((hc_s - CONV_HIST, 0), (0, 0), (0, 0)))
    hist_p = jnp.pad(_time_major(state_pool[l]), ((hp_s - POOL_HIST, 0), (0, 0), (0, 0)))
    h_s, v_s, lg_s, nc_s, np_s = _mixer(
        _time_major(x_sample).reshape(n_s, D), mod_s, hist_c.reshape(hc_s * n_seq_s, C),
        hist_p.reshape(hp_s * n_seq_s, C), weights, SAMPLE_ROW_TILE, PAST_LEN)

    dest8, gates3, meta = _plan(lg_p, lg_s)
    cnt, row0, padded, nxt = (meta[i, :E] for i in range(4))
    gates = gates3.reshape(n, LANES)

    dests = [dest8[:, k, :].reshape(n) for k in range(TOP_K)]
    xpad = _dispatch(v_p, v_s, dests, n_slots)
    ypad = _experts(cnt, row0, padded, nxt, xpad, w_up[l], b_up[l][:, None, :], w_down[l], b_down[l][:, None, :])

    def gathered(t0, rows):
        idx = jnp.concatenate([dk[t0:t0 + rows] for dk in dests])
        return _gather_rows(ypad, idx).reshape(TOP_K, rows, D // 2)

    chunk = n_p // COMBINE_CHUNKS
    y4_p = [gathered(c * chunk, chunk) for c in range(COMBINE_CHUNKS)]
    y4_s = gathered(n_p, n_s)
    g2, b2 = ln2_g[l][None], ln2_b[l][None]
    y_p = None
    for c in range(COMBINE_CHUNKS):
        y_p = _combine(h_p, mod_p, gates, g2, b2, y4_p[c], 0, seq, c * chunk, y_p)
    y_s = _combine(h_s, mod_s, gates, g2, b2, y4_s, n_p, n_s, 0, None)

    y_prompt = y_p.reshape(n_seq_p, seq, D)
    y_sample = _time_major(y_s.reshape(dec_seq, n_seq_s, D))
    new_conv_p = nc_p.reshape(n_seq_p, hc_p, C)[:, hc_p - CONV_HIST:][None]
    new_pool_p = np_p.reshape(n_seq_p, hp_p, C)[:, hp_p - POOL_HIST:][None]
    new_conv_s = _time_major(nc_s.reshape(hc_s, n_seq_s, C)[hc_s - CONV_HIST:])[None]
    new_pool_s = _time_major(np_s.reshape(hp_s, n_seq_s, C)[hp_s - POOL_HIST:])[None]
    return (y_prompt, y_sample, new_conv_p, new_pool_p, new_conv_s, new_pool_s)
```
